```python
import jax, jax.numpy as jnp
from jax import lax
import numpy as np

D_MODEL = 1024
BATCH = 8
SEQ = 4096
DEPTH = 4

CHUNK = 64
N_MIXERS = 3
N_HEADS = 16
HEAD_DIM = D_MODEL // N_HEADS
Q_BLOCK = 128
SG_CHUNK = 128
SG_WIDTH = D_MODEL
SG_GROUPS = 8
SG_GROUP_DIM = SG_WIDTH // SG_GROUPS
CONV_WIDTH = 31
D_FF = 4 * D_MODEL
N_A = (DEPTH + 2) // N_MIXERS
N_B = (DEPTH + 1) // N_MIXERS
N_C = DEPTH // N_MIXERS
EPS = 1e-6

kernel_name = "chunk_causal_hybrid_fox_gmlp_conformer"


def rms_norm(x, g):
    xf = x.astype(jnp.float32)
    y = xf * lax.rsqrt(jnp.mean(xf * xf, axis=-1, keepdims=True) + EPS)
    return (y * g.astype(jnp.float32)).astype(x.dtype)


def layer_norm(x, g, b):
    xf = x.astype(jnp.float32)
    mu = jnp.mean(xf, axis=-1, keepdims=True)
    xc = xf - mu
    y = xc * lax.rsqrt(jnp.mean(xc * xc, axis=-1, keepdims=True) + EPS)
    return (y * g.astype(jnp.float32) + b.astype(jnp.float32)).astype(x.dtype)


def fox_mixer(h, w_in, b_f, q_g, k_g, w_out):
    B, S, D = h.shape
    proj = h @ w_in
    q, k, v, f_pre = jnp.split(proj, [D, 2 * D, 3 * D], axis=-1)
    q = rms_norm(q.reshape(B, S, N_HEADS, HEAD_DIM), q_g)
    k = rms_norm(k.reshape(B, S, N_HEADS, HEAD_DIM), k_g)
    v = v.reshape(B, S, N_HEADS, HEAD_DIM)
    log_f = jax.nn.log_sigmoid(f_pre.astype(jnp.float32) + b_f.astype(jnp.float32))
    F = jnp.cumsum(log_f, axis=1).transpose(0, 2, 1)
    nb = S // Q_BLOCK
    qb = q.reshape(B, nb, Q_BLOCK, N_HEADS, HEAD_DIM).swapaxes(0, 1)
    Fq = F.reshape(B, N_HEADS, nb, Q_BLOCK).transpose(2, 0, 1, 3)
    k_pos = jnp.arange(S)
    scale = HEAD_DIM ** -0.5

    def block(args):
        q_i, F_i, b_i = args
        logits = (jnp.einsum('bqhd,bkhd->bhqk', q_i, k).astype(jnp.float32) * scale
                  + (F_i[..., :, None] - F[..., None, :]))
        q_pos = b_i * Q_BLOCK + jnp.arange(Q_BLOCK)
        logits = jnp.where(k_pos[None, :] <= q_pos[:, None], logits, -jnp.inf)
        p = jax.nn.softmax(logits, axis=-1).astype(v.dtype)
        return jnp.einsum('bhqk,bkhd->bqhd', p, v)

    o = lax.map(block, (qb, Fq, jnp.arange(nb)))
    o = o.swapaxes(0, 1).reshape(B, S, D)
    return o @ w_out


def gmlp_mixer(h, w_in, ln_g, ln_b, w_s, b_s, w_out):
    B, S, _ = h.shape
    uv = jax.nn.gelu(h @ w_in)
    u, v = jnp.split(uv, 2, axis=-1)
    v = layer_norm(v, ln_g, ln_b)
    v = v.reshape(B, S // SG_CHUNK, SG_CHUNK, SG_GROUPS, SG_GROUP_DIM)
    cid = jnp.arange(SG_CHUNK) // CHUNK
    mask = cid[None, :] <= cid[:, None]
    ws = jnp.where(mask[None], w_s, jnp.zeros_like(w_s))
    v = jnp.einsum('gts,bnsgc->bntgc', ws, v) + b_s.T[:, :, None]
    v = v.reshape(B, S, SG_WIDTH)
    return (u * v) @ w_out


def conv_mixer(h, w_pw1, b_pw1, w_dw, b_dw, ln_g, ln_b, w_pw2, b_pw2):
    D = h.shape[-1]
    y = jax.nn.glu(h @ w_pw1 + b_pw1, axis=-1)
    y = lax.conv_general_dilated(y, w_dw[:, None, :], window_strides=(1,),
                                 padding=[(CONV_WIDTH - 1, 0)],
                                 dimension_numbers=('NWC', 'WIO', 'NWC'),
                                 feature_group_count=D) + b_dw
    y = jax.nn.silu(layer_norm(y, ln_g, ln_b))
    return y @ w_pw2 + b_pw2


def _fwd_setup_inputs(seed: int = 0) -> dict:
    key = jax.random.key(seed)
    ks = iter(jax.random.split(key, 40))
    D = D_MODEL

    def nrm(shape, scale):
        return jax.random.normal(next(ks), shape, jnp.float32) * scale

    def gain(shape):
        return 1.0 + nrm(shape, 0.05)

    return {
        "x": nrm((BATCH, SEQ, D), 1.0),
        "c": nrm((BATCH, D), 1.0),
        "norm_mix": gain((DEPTH, D)),
        "norm_mlp": gain((DEPTH, D)),
        "w_ada": nrm((DEPTH, D, 6 * D), 0.5 * D ** -0.5),
        "b_ada": nrm((DEPTH, 6 * D), 0.02),
        "w_mlp_in": nrm((DEPTH, D, D_FF), D ** -0.5),
        "w_mlp_out": nrm((DEPTH, D_FF, D), D_FF ** -0.5),
        "fox_w_in": nrm((N_A, D, 3 * D + N_HEADS), D ** -0.5),
        "fox_b_f": jax.random.uniform(next(ks), (N_A, N_HEADS), jnp.float32, 1.0, 6.0),
        "fox_q_norm": gain((N_A, HEAD_DIM)),
        "fox_k_norm": gain((N_A, HEAD_DIM)),
        "fox_w_out": nrm((N_A, D, D), D ** -0.5),
        "sg_w_in": nrm((N_B, D, 2 * SG_WIDTH), D ** -0.5),
        "sg_ln_g": gain((N_B, SG_WIDTH)),
        "sg_ln_b": nrm((N_B, SG_WIDTH), 0.02),
        "sg_w_s": nrm((N_B, SG_GROUPS, SG_CHUNK, SG_CHUNK), 0.5 * SG_CHUNK ** -0.5),
        "sg_b_s": 1.0 + nrm((N_B, SG_GROUPS, SG_CHUNK), 0.02),
        "sg_w_out": nrm((N_B, SG_WIDTH, D), SG_WIDTH ** -0.5),
        "cv_w_pw1": nrm((N_C, D, 2 * D), D ** -0.5),
        "cv_b_pw1": nrm((N_C, 2 * D), 0.02),
        "cv_w_dw": nrm((N_C, CONV_WIDTH, D), CONV_WIDTH ** -0.5),
        "cv_b_dw": nrm((N_C, D), 0.02),
        "cv_ln_g": gain((N_C, D)),
        "cv_ln_b": nrm((N_C, D), 0.02),
        "cv_w_pw2": nrm((N_C, D, D), D ** -0.5),
        "cv_b_pw2": nrm((N_C, D), 0.02),
    }


def _fwd_reference(x, c, norm_mix, norm_mlp, w_ada, b_ada, w_mlp_in, w_mlp_out,
              fox_w_in, fox_b_f, fox_q_norm, fox_k_norm, fox_w_out,
              sg_w_in, sg_ln_g, sg_ln_b, sg_w_s, sg_b_s, sg_w_out,
              cv_w_pw1, cv_b_pw1, cv_w_dw, cv_b_dw, cv_ln_g, cv_ln_b, cv_w_pw2, cv_b_pw2):
    c_act = jax.nn.silu(c)
    for i in range(DEPTH):
        kind = i % N_MIXERS
        j = i // N_MIXERS
        mod = c_act @ w_ada[i] + b_ada[i]
        sh_m, sc_m, g_m, sh_f, sc_f, g_f = [m[:, None, :] for m in jnp.split(mod, 6, axis=-1)]
        h = rms_norm(x, norm_mix[i]) * (1 + sc_m) + sh_m
        if kind == 0:
            y = fox_mixer(h, fox_w_in[j], fox_b_f[j], fox_q_norm[j], fox_k_norm[j], fox_w_out[j])
        elif kind == 1:
            y = gmlp_mixer(h, sg_w_in[j], sg_ln_g[j], sg_ln_b[j], sg_w_s[j], sg_b_s[j], sg_w_out[j])
        else:
            y = conv_mixer(h, cv_w_pw1[j], cv_b_pw1[j], cv_w_dw[j], cv_b_dw[j],
                           cv_ln_g[j], cv_ln_b[j], cv_w_pw2[j], cv_b_pw2[j])
        x = x + g_m * y
        h = rms_norm(x, norm_mlp[i]) * (1 + sc_f) + sh_f
        x = x + g_f * (jnp.square(jax.nn.relu(h @ w_mlp_in[i])) @ w_mlp_out[i])
    return x


import jax as _jax
import jax.numpy as _jnp

TWIN_FORMAT = 'train_step'
FWD_PARAMS = ['x', 'c', 'norm_mix', 'norm_mlp', 'w_ada', 'b_ada', 'w_mlp_in', 'w_mlp_out', 'fox_w_in', 'fox_b_f', 'fox_q_norm', 'fox_k_norm', 'fox_w_out', 'sg_w_in', 'sg_ln_g', 'sg_ln_b', 'sg_w_s', 'sg_b_s', 'sg_w_out', 'cv_w_pw1', 'cv_b_pw1', 'cv_w_dw', 'cv_b_dw', 'cv_ln_g', 'cv_ln_b', 'cv_w_pw2', 'cv_b_pw2']
TWIN_WEIGHTS = ['norm_mix', 'norm_mlp', 'w_ada', 'b_ada', 'w_mlp_in', 'w_mlp_out', 'fox_w_in', 'fox_b_f', 'fox_q_norm', 'fox_k_norm', 'fox_w_out', 'sg_w_in', 'sg_ln_g', 'sg_ln_b', 'sg_w_s', 'sg_b_s', 'sg_w_out', 'cv_w_pw1', 'cv_b_pw1', 'cv_w_dw', 'cv_b_dw', 'cv_ln_g', 'cv_ln_b', 'cv_w_pw2', 'cv_b_pw2']
TWIN_DIFF_INPUT = 'x'
TWIN_INPUTS = ['x', 'c', 'norm_mix', 'norm_mlp', 'w_ada', 'b_ada', 'w_mlp_in', 'w_mlp_out', 'fox_w_in', 'fox_b_f', 'fox_q_norm', 'fox_k_norm', 'fox_w_out', 'sg_w_in', 'sg_ln_g', 'sg_ln_b', 'sg_w_s', 'sg_b_s', 'sg_w_out', 'cv_w_pw1', 'cv_b_pw1', 'cv_w_dw', 'cv_b_dw', 'cv_ln_g', 'cv_ln_b', 'cv_w_pw2', 'cv_b_pw2', 'loss_target', 'm_norm_mix', 'm_norm_mlp', 'm_w_ada', 'm_b_ada', 'm_w_mlp_in', 'm_w_mlp_out', 'm_fox_w_in', 'm_fox_b_f', 'm_fox_q_norm', 'm_fox_k_norm', 'm_fox_w_out', 'm_sg_w_in', 'm_sg_ln_g', 'm_sg_ln_b', 'm_sg_w_s', 'm_sg_b_s', 'm_sg_w_out', 'm_cv_w_pw1', 'm_cv_b_pw1', 'm_cv_w_dw', 'm_cv_b_dw', 'm_cv_ln_g', 'm_cv_ln_b', 'm_cv_w_pw2', 'm_cv_b_pw2', 'v_norm_mix', 'v_norm_mlp', 'v_w_ada', 'v_b_ada', 'v_w_mlp_in', 'v_w_mlp_out', 'v_fox_w_in', 'v_fox_b_f', 'v_fox_q_norm', 'v_fox_k_norm', 'v_fox_w_out', 'v_sg_w_in', 'v_sg_ln_g', 'v_sg_ln_b', 'v_sg_w_s', 'v_sg_b_s', 'v_sg_w_out', 'v_cv_w_pw1', 'v_cv_b_pw1', 'v_cv_w_dw', 'v_cv_b_dw', 'v_cv_ln_g', 'v_cv_ln_b', 'v_cv_w_pw2', 'v_cv_b_pw2']
TWIN_OUTPUTS = ['loss', 'grad_x', 'grad_norm_mix', 'grad_norm_mlp', 'grad_w_ada', 'grad_b_ada', 'grad_w_mlp_in', 'grad_w_mlp_out', 'grad_fox_w_in', 'grad_fox_b_f', 'grad_fox_q_norm', 'grad_fox_k_norm', 'grad_fox_w_out', 'grad_sg_w_in', 'grad_sg_ln_g', 'grad_sg_ln_b', 'grad_sg_w_s', 'grad_sg_b_s', 'grad_sg_w_out', 'grad_cv_w_pw1', 'grad_cv_b_pw1', 'grad_cv_w_dw', 'grad_cv_b_dw', 'grad_cv_ln_g', 'grad_cv_ln_b', 'grad_cv_w_pw2', 'grad_cv_b_pw2', 'delta_norm_mix', 'delta_norm_mlp', 'delta_w_ada', 'delta_b_ada', 'delta_w_mlp_in', 'delta_w_mlp_out', 'delta_fox_w_in', 'delta_fox_b_f', 'delta_fox_q_norm', 'delta_fox_k_norm', 'delta_fox_w_out', 'delta_sg_w_in', 'delta_sg_ln_g', 'delta_sg_ln_b', 'delta_sg_w_s', 'delta_sg_b_s', 'delta_sg_w_out', 'delta_cv_w_pw1', 'delta_cv_b_pw1', 'delta_cv_w_dw', 'delta_cv_b_dw', 'delta_cv_ln_g', 'delta_cv_ln_b', 'delta_cv_w_pw2', 'delta_cv_b_pw2', 'new_m_norm_mix', 'new_m_norm_mlp', 'new_m_w_ada', 'new_m_b_ada', 'new_m_w_mlp_in', 'new_m_w_mlp_out', 'new_m_fox_w_in', 'new_m_fox_b_f', 'new_m_fox_q_norm', 'new_m_fox_k_norm', 'new_m_fox_w_out', 'new_m_sg_w_in', 'new_m_sg_ln_g', 'new_m_sg_ln_b', 'new_m_sg_w_s', 'new_m_sg_b_s', 'new_m_sg_w_out', 'new_m_cv_w_pw1', 'new_m_cv_b_pw1', 'new_m_cv_w_dw', 'new_m_cv_b_dw', 'new_m_cv_ln_g', 'new_m_cv_ln_b', 'new_m_cv_w_pw2', 'new_m_cv_b_pw2', 'new_v_norm_mix', 'new_v_norm_mlp', 'new_v_w_ada', 'new_v_b_ada', 'new_v_w_mlp_in', 'new_v_w_mlp_out', 'new_v_fox_w_in', 'new_v_fox_b_f', 'new_v_fox_q_norm', 'new_v_fox_k_norm', 'new_v_fox_w_out', 'new_v_sg_w_in', 'new_v_sg_ln_g', 'new_v_sg_ln_b', 'new_v_sg_w_s', 'new_v_sg_b_s', 'new_v_sg_w_out', 'new_v_cv_w_pw1', 'new_v_cv_b_pw1', 'new_v_cv_w_dw', 'new_v_cv_b_dw', 'new_v_cv_ln_g', 'new_v_cv_ln_b', 'new_v_cv_w_pw2', 'new_v_cv_b_pw2']
TWIN_LEAF_KINDS = {'loss': 'loss', 'grad_x': 'grad_x', 'grad_norm_mix': 'grad_w', 'grad_norm_mlp': 'grad_w', 'grad_w_ada': 'grad_w', 'grad_b_ada': 'grad_w', 'grad_w_mlp_in': 'grad_w', 'grad_w_mlp_out': 'grad_w', 'grad_fox_w_in': 'grad_w', 'grad_fox_b_f': 'grad_w', 'grad_fox_q_norm': 'grad_w', 'grad_fox_k_norm': 'grad_w', 'grad_fox_w_out': 'grad_w', 'grad_sg_w_in': 'grad_w', 'grad_sg_ln_g': 'grad_w', 'grad_sg_ln_b': 'grad_w', 'grad_sg_w_s': 'grad_w', 'grad_sg_b_s': 'grad_w', 'grad_sg_w_out': 'grad_w', 'grad_cv_w_pw1': 'grad_w', 'grad_cv_b_pw1': 'grad_w', 'grad_cv_w_dw': 'grad_w', 'grad_cv_b_dw': 'grad_w', 'grad_cv_ln_g': 'grad_w', 'grad_cv_ln_b': 'grad_w', 'grad_cv_w_pw2': 'grad_w', 'grad_cv_b_pw2': 'grad_w', 'delta_norm_mix': 'delta_w', 'delta_norm_mlp': 'delta_w', 'delta_w_ada': 'delta_w', 'delta_b_ada': 'delta_w', 'delta_w_mlp_in': 'delta_w', 'delta_w_mlp_out': 'delta_w', 'delta_fox_w_in': 'delta_w', 'delta_fox_b_f': 'delta_w', 'delta_fox_q_norm': 'delta_w', 'delta_fox_k_norm': 'delta_w', 'delta_fox_w_out': 'delta_w', 'delta_sg_w_in': 'delta_w', 'delta_sg_ln_g': 'delta_w', 'delta_sg_ln_b': 'delta_w', 'delta_sg_w_s': 'delta_w', 'delta_sg_b_s': 'delta_w', 'delta_sg_w_out': 'delta_w', 'delta_cv_w_pw1': 'delta_w', 'delta_cv_b_pw1': 'delta_w', 'delta_cv_w_dw': 'delta_w', 'delta_cv_b_dw': 'delta_w', 'delta_cv_ln_g': 'delta_w', 'delta_cv_ln_b': 'delta_w', 'delta_cv_w_pw2': 'delta_w', 'delta_cv_b_pw2': 'delta_w', 'new_m_norm_mix': 'new_m', 'new_m_norm_mlp': 'new_m', 'new_m_w_ada': 'new_m', 'new_m_b_ada': 'new_m', 'new_m_w_mlp_in': 'new_m', 'new_m_w_mlp_out': 'new_m', 'new_m_fox_w_in': 'new_m', 'new_m_fox_b_f': 'new_m', 'new_m_fox_q_norm': 'new_m', 'new_m_fox_k_norm': 'new_m', 'new_m_fox_w_out': 'new_m', 'new_m_sg_w_in': 'new_m', 'new_m_sg_ln_g': 'new_m', 'new_m_sg_ln_b': 'new_m', 'new_m_sg_w_s': 'new_m', 'new_m_sg_b_s': 'new_m', 'new_m_sg_w_out': 'new_m', 'new_m_cv_w_pw1': 'new_m', 'new_m_cv_b_pw1': 'new_m', 'new_m_cv_w_dw': 'new_m', 'new_m_cv_b_dw': 'new_m', 'new_m_cv_ln_g': 'new_m', 'new_m_cv_ln_b': 'new_m', 'new_m_cv_w_pw2': 'new_m', 'new_m_cv_b_pw2': 'new_m', 'new_v_norm_mix': 'new_v', 'new_v_norm_mlp': 'new_v', 'new_v_w_ada': 'new_v', 'new_v_b_ada': 'new_v', 'new_v_w_mlp_in': 'new_v', 'new_v_w_mlp_out': 'new_v', 'new_v_fox_w_in': 'new_v', 'new_v_fox_b_f': 'new_v', 'new_v_fox_q_norm': 'new_v', 'new_v_fox_k_norm': 'new_v', 'new_v_fox_w_out': 'new_v', 'new_v_sg_w_in': 'new_v', 'new_v_sg_ln_g': 'new_v', 'new_v_sg_ln_b': 'new_v', 'new_v_sg_w_s': 'new_v', 'new_v_sg_b_s': 'new_v', 'new_v_sg_w_out': 'new_v', 'new_v_cv_w_pw1': 'new_v', 'new_v_cv_b_pw1': 'new_v', 'new_v_cv_w_dw': 'new_v', 'new_v_cv_b_dw': 'new_v', 'new_v_cv_ln_g': 'new_v', 'new_v_cv_ln_b': 'new_v', 'new_v_cv_w_pw2': 'new_v', 'new_v_cv_b_pw2': 'new_v'}


def _forward(args):
    return _fwd_reference(*[args[k] for k in FWD_PARAMS])


def _output_shape():
    def fwd():
        inp = _fwd_setup_inputs(0)
        return _fwd_reference(*[inp[k] for k in FWD_PARAMS])
    out = _jax.eval_shape(fwd)
    return out.shape, out.dtype

N_MICROBATCH = 1
ADAM_LR = 0.001
ADAM_B1 = 0.9
ADAM_B2 = 0.999
ADAM_EPS = 1e-08
ADAM_WD = 0.01
ADAM_STEP = 10
PER_EXAMPLE_BATCH_AXIS = {'x': 0, 'c': 0, 'loss_target': 0}
SHARED_INPUTS = []
_WEIGHT_DTYPES = {'norm_mix': _jnp.float32, 'norm_mlp': _jnp.float32, 'w_ada': _jnp.float32, 'b_ada': _jnp.float32, 'w_mlp_in': _jnp.float32, 'w_mlp_out': _jnp.float32, 'fox_w_in': _jnp.float32, 'fox_b_f': _jnp.float32, 'fox_q_norm': _jnp.float32, 'fox_k_norm': _jnp.float32, 'fox_w_out': _jnp.float32, 'sg_w_in': _jnp.float32, 'sg_ln_g': _jnp.float32, 'sg_ln_b': _jnp.float32, 'sg_w_s': _jnp.float32, 'sg_b_s': _jnp.float32, 'sg_w_out': _jnp.float32, 'cv_w_pw1': _jnp.float32, 'cv_b_pw1': _jnp.float32, 'cv_w_dw': _jnp.float32, 'cv_b_dw': _jnp.float32, 'cv_ln_g': _jnp.float32, 'cv_ln_b': _jnp.float32, 'cv_w_pw2': _jnp.float32, 'cv_b_pw2': _jnp.float32}
MOMENT_SCALE = {'norm_mix': 1.111367e+00, 'norm_mlp': 1.201067e+01, 'w_ada': 2.890784e+00, 'b_ada': 7.094164e+00, 'w_mlp_in': 5.373085e-01, 'w_mlp_out': 1.883050e+00, 'fox_w_in': 4.620550e-01, 'fox_b_f': 6.599568e+00, 'fox_q_norm': 1.344611e+00, 'fox_k_norm': 1.349150e+00, 'fox_w_out': 7.756925e-01, 'sg_w_in': 3.523776e-01, 'sg_ln_g': 2.778156e-01, 'sg_ln_b': 9.117266e-02, 'sg_w_s': 1.013795e-01, 'sg_b_s': 1.534917e+00, 'sg_w_out': 9.523307e-01, 'cv_w_pw1': 2.773463e-01, 'cv_b_pw1': 7.882201e-01, 'cv_w_dw': 4.138962e-01, 'cv_b_dw': 1.907910e+00, 'cv_ln_g': 1.689078e+00, 'cv_ln_b': 1.498871e+00, 'cv_w_pw2': 6.143388e-01, 'cv_b_pw2': 2.372046e+00}


def _to_microbatches(a, axis):
    t = _jnp.moveaxis(a, axis, 0)
    t = t.reshape((N_MICROBATCH, t.shape[0] // N_MICROBATCH) + t.shape[1:])
    return _jnp.moveaxis(t, 1, axis + 1)


def setup_inputs(seed: int = 0) -> dict:
    inp = _fwd_setup_inputs(seed)
    key = _jax.random.fold_in(_jax.random.key(seed), 7919)
    shape, _ = _output_shape()
    out = dict(inp)
    out["loss_target"] = _jax.random.normal(_jax.random.fold_in(key, 0), shape, _jnp.float32)
    for i, name in enumerate(TWIN_WEIGHTS):
        w = inp[name].astype(_jnp.float32)
        if MOMENT_SCALE is None:
            s = _jnp.sqrt(_jnp.mean(_jnp.square(w)) + 1e-30)
        else:
            s = MOMENT_SCALE[name]
        km, kv = _jax.random.split(_jax.random.fold_in(key, i + 1))
        out[name] = w
        out["m_" + name] = s * _jax.random.normal(km, w.shape, _jnp.float32)
        out["v_" + name] = (s * s) * _jax.random.uniform(kv, w.shape, _jnp.float32, 0.5, 1.5)
    if N_MICROBATCH > 1:
        for name, axis in PER_EXAMPLE_BATCH_AXIS.items():
            out[name] = _to_microbatches(out[name], axis)
    return {'x': out['x'], 'c': out['c'], 'norm_mix': out['norm_mix'], 'norm_mlp': out['norm_mlp'], 'w_ada': out['w_ada'], 'b_ada': out['b_ada'], 'w_mlp_in': out['w_mlp_in'], 'w_mlp_out': out['w_mlp_out'], 'fox_w_in': out['fox_w_in'], 'fox_b_f': out['fox_b_f'], 'fox_q_norm': out['fox_q_norm'], 'fox_k_norm': out['fox_k_norm'], 'fox_w_out': out['fox_w_out'], 'sg_w_in': out['sg_w_in'], 'sg_ln_g': out['sg_ln_g'], 'sg_ln_b': out['sg_ln_b'], 'sg_w_s': out['sg_w_s'], 'sg_b_s': out['sg_b_s'], 'sg_w_out': out['sg_w_out'], 'cv_w_pw1': out['cv_w_pw1'], 'cv_b_pw1': out['cv_b_pw1'], 'cv_w_dw': out['cv_w_dw'], 'cv_b_dw': out['cv_b_dw'], 'cv_ln_g': out['cv_ln_g'], 'cv_ln_b': out['cv_ln_b'], 'cv_w_pw2': out['cv_w_pw2'], 'cv_b_pw2': out['cv_b_pw2'], 'loss_target': out['loss_target'], 'm_norm_mix': out['m_norm_mix'], 'm_norm_mlp': out['m_norm_mlp'], 'm_w_ada': out['m_w_ada'], 'm_b_ada': out['m_b_ada'], 'm_w_mlp_in': out['m_w_mlp_in'], 'm_w_mlp_out': out['m_w_mlp_out'], 'm_fox_w_in': out['m_fox_w_in'], 'm_fox_b_f': out['m_fox_b_f'], 'm_fox_q_norm': out['m_fox_q_norm'], 'm_fox_k_norm': out['m_fox_k_norm'], 'm_fox_w_out': out['m_fox_w_out'], 'm_sg_w_in': out['m_sg_w_in'], 'm_sg_ln_g': out['m_sg_ln_g'], 'm_sg_ln_b': out['m_sg_ln_b'], 'm_sg_w_s': out['m_sg_w_s'], 'm_sg_b_s': out['m_sg_b_s'], 'm_sg_w_out': out['m_sg_w_out'], 'm_cv_w_pw1': out['m_cv_w_pw1'], 'm_cv_b_pw1': out['m_cv_b_pw1'], 'm_cv_w_dw': out['m_cv_w_dw'], 'm_cv_b_dw': out['m_cv_b_dw'], 'm_cv_ln_g': out['m_cv_ln_g'], 'm_cv_ln_b': out['m_cv_ln_b'], 'm_cv_w_pw2': out['m_cv_w_pw2'], 'm_cv_b_pw2': out['m_cv_b_pw2'], 'v_norm_mix': out['v_norm_mix'], 'v_norm_mlp': out['v_norm_mlp'], 'v_w_ada': out['v_w_ada'], 'v_b_ada': out['v_b_ada'], 'v_w_mlp_in': out['v_w_mlp_in'], 'v_w_mlp_out': out['v_w_mlp_out'], 'v_fox_w_in': out['v_fox_w_in'], 'v_fox_b_f': out['v_fox_b_f'], 'v_fox_q_norm': out['v_fox_q_norm'], 'v_fox_k_norm': out['v_fox_k_norm'], 'v_fox_w_out': out['v_fox_w_out'], 'v_sg_w_in': out['v_sg_w_in'], 'v_sg_ln_g': out['v_sg_ln_g'], 'v_sg_ln_b': out['v_sg_ln_b'], 'v_sg_w_s': out['v_sg_w_s'], 'v_sg_b_s': out['v_sg_b_s'], 'v_sg_w_out': out['v_sg_w_out'], 'v_cv_w_pw1': out['v_cv_w_pw1'], 'v_cv_b_pw1': out['v_cv_b_pw1'], 'v_cv_w_dw': out['v_cv_w_dw'], 'v_cv_b_dw': out['v_cv_b_dw'], 'v_cv_ln_g': out['v_cv_ln_g'], 'v_cv_ln_b': out['v_cv_ln_b'], 'v_cv_w_pw2': out['v_cv_w_pw2'], 'v_cv_b_pw2': out['v_cv_b_pw2']}


def _loss(weights, diff, rest, loss_target):
    with _jax.named_scope("forward"):
        args = {**rest, TWIN_DIFF_INPUT: diff, **{k: w.astype(_WEIGHT_DTYPES[k]) for k, w in weights.items()}}
        y = _forward(args)
    with _jax.named_scope("loss_head"):
        err = _jnp.square(y.astype(_jnp.float32) - loss_target)
        return 0.5 * _jnp.sum(_jnp.mean(err, axis=-1)) if err.ndim else 0.5 * err


def _adamw(w, g, m, v):
    m = ADAM_B1 * m + (1.0 - ADAM_B1) * g
    v = ADAM_B2 * v + (1.0 - ADAM_B2) * _jnp.square(g)
    m_hat = m / (1.0 - ADAM_B1 ** ADAM_STEP)
    v_hat = v / (1.0 - ADAM_B2 ** ADAM_STEP)
    delta = -ADAM_LR * (m_hat / (_jnp.sqrt(v_hat) + ADAM_EPS) + ADAM_WD * w)
    return delta, m, v


def reference(x, c, norm_mix, norm_mlp, w_ada, b_ada, w_mlp_in, w_mlp_out, fox_w_in, fox_b_f, fox_q_norm, fox_k_norm, fox_w_out, sg_w_in, sg_ln_g, sg_ln_b, sg_w_s, sg_b_s, sg_w_out, cv_w_pw1, cv_b_pw1, cv_w_dw, cv_b_dw, cv_ln_g, cv_ln_b, cv_w_pw2, cv_b_pw2, loss_target, m_norm_mix, m_norm_mlp, m_w_ada, m_b_ada, m_w_mlp_in, m_w_mlp_out, m_fox_w_in, m_fox_b_f, m_fox_q_norm, m_fox_k_norm, m_fox_w_out, m_sg_w_in, m_sg_ln_g, m_sg_ln_b, m_sg_w_s, m_sg_b_s, m_sg_w_out, m_cv_w_pw1, m_cv_b_pw1, m_cv_w_dw, m_cv_b_dw, m_cv_ln_g, m_cv_ln_b, m_cv_w_pw2, m_cv_b_pw2, v_norm_mix, v_norm_mlp, v_w_ada, v_b_ada, v_w_mlp_in, v_w_mlp_out, v_fox_w_in, v_fox_b_f, v_fox_q_norm, v_fox_k_norm, v_fox_w_out, v_sg_w_in, v_sg_ln_g, v_sg_ln_b, v_sg_w_s, v_sg_b_s, v_sg_w_out, v_cv_w_pw1, v_cv_b_pw1, v_cv_w_dw, v_cv_b_dw, v_cv_ln_g, v_cv_ln_b, v_cv_w_pw2, v_cv_b_pw2):
    given = dict(x=x, c=c, norm_mix=norm_mix, norm_mlp=norm_mlp, w_ada=w_ada, b_ada=b_ada, w_mlp_in=w_mlp_in, w_mlp_out=w_mlp_out, fox_w_in=fox_w_in, fox_b_f=fox_b_f, fox_q_norm=fox_q_norm, fox_k_norm=fox_k_norm, fox_w_out=fox_w_out, sg_w_in=sg_w_in, sg_ln_g=sg_ln_g, sg_ln_b=sg_ln_b, sg_w_s=sg_w_s, sg_b_s=sg_b_s, sg_w_out=sg_w_out, cv_w_pw1=cv_w_pw1, cv_b_pw1=cv_b_pw1, cv_w_dw=cv_w_dw, cv_b_dw=cv_b_dw, cv_ln_g=cv_ln_g, cv_ln_b=cv_ln_b, cv_w_pw2=cv_w_pw2, cv_b_pw2=cv_b_pw2, loss_target=loss_target, m_norm_mix=m_norm_mix, m_norm_mlp=m_norm_mlp, m_w_ada=m_w_ada, m_b_ada=m_b_ada, m_w_mlp_in=m_w_mlp_in, m_w_mlp_out=m_w_mlp_out, m_fox_w_in=m_fox_w_in, m_fox_b_f=m_fox_b_f, m_fox_q_norm=m_fox_q_norm, m_fox_k_norm=m_fox_k_norm, m_fox_w_out=m_fox_w_out, m_sg_w_in=m_sg_w_in, m_sg_ln_g=m_sg_ln_g, m_sg_ln_b=m_sg_ln_b, m_sg_w_s=m_sg_w_s, m_sg_b_s=m_sg_b_s, m_sg_w_out=m_sg_w_out, m_cv_w_pw1=m_cv_w_pw1, m_cv_b_pw1=m_cv_b_pw1, m_cv_w_dw=m_cv_w_dw, m_cv_b_dw=m_cv_b_dw, m_cv_ln_g=m_cv_ln_g, m_cv_ln_b=m_cv_ln_b, m_cv_w_pw2=m_cv_w_pw2, m_cv_b_pw2=m_cv_b_pw2, v_norm_mix=v_norm_mix, v_norm_mlp=v_norm_mlp, v_w_ada=v_w_ada, v_b_ada=v_b_ada, v_w_mlp_in=v_w_mlp_in, v_w_mlp_out=v_w_mlp_out, v_fox_w_in=v_fox_w_in, v_fox_b_f=v_fox_b_f, v_fox_q_norm=v_fox_q_norm, v_fox_k_norm=v_fox_k_norm, v_fox_w_out=v_fox_w_out, v_sg_w_in=v_sg_w_in, v_sg_ln_g=v_sg_ln_g, v_sg_ln_b=v_sg_ln_b, v_sg_w_s=v_sg_w_s, v_sg_b_s=v_sg_b_s, v_sg_w_out=v_sg_w_out, v_cv_w_pw1=v_cv_w_pw1, v_cv_b_pw1=v_cv_b_pw1, v_cv_w_dw=v_cv_w_dw, v_cv_b_dw=v_cv_b_dw, v_cv_ln_g=v_cv_ln_g, v_cv_ln_b=v_cv_ln_b, v_cv_w_pw2=v_cv_w_pw2, v_cv_b_pw2=v_cv_b_pw2)
    weights = {n: given[n] for n in TWIN_WEIGHTS}
    shared = {n: given[n] for n in SHARED_INPUTS}
    per_example = {n: given[n] for n in ['x', 'c']}
    grad_fn = _jax.value_and_grad(_loss, argnums=(0, 1))

    def one_microbatch(ex, loss_target):
        ex = dict(ex)
        diff = ex.pop(TWIN_DIFF_INPUT)
        return grad_fn(weights, diff, {**shared, **ex}, loss_target)

    if N_MICROBATCH == 1:
        loss, (grad_w, grad_x) = one_microbatch(per_example, given["loss_target"])
    else:
        def body(carry, xs):
            loss_sum, grad_sum = carry
            l_k, (gw_k, gx_k) = one_microbatch(xs[0], xs[1])
            with _jax.named_scope("update"):
                return (loss_sum + l_k, _jax.tree.map(_jnp.add, grad_sum, gw_k)), gx_k

        init = (_jnp.zeros((), _jnp.float32), _jax.tree.map(_jnp.zeros_like, weights))
        (loss, grad_w), grad_x = _jax.lax.scan(body, init, (per_example, given["loss_target"]))
    with _jax.named_scope("update"):
        delta_w, new_m, new_v = {}, {}, {}
        for n in TWIN_WEIGHTS:
            delta_w[n], new_m[n], new_v[n] = _adamw(weights[n], grad_w[n], given["m_" + n], given["v_" + n])
    return (loss, grad_x, *[grad_w[n] for n in TWIN_WEIGHTS], *[delta_w[n] for n in TWIN_WEIGHTS],
            *[new_m[n] for n in TWIN_WEIGHTS], *[new_v[n] for n in TWIN_WEIGHTS])
```

```python
import functools
import math

import jax
import jax.numpy as jnp
from jax import lax
from jax.experimental import pallas as pl
from jax.experimental.pallas import tpu as pltpu

F32 = jnp.float32
MXU_DTYPE = jnp.bfloat16
EPS = 1e-6
N_DEV = 8
HEAD_DIM = 64
LANES = 128
CONV_WIDTH = 31
CONV_PAD = 32
SG_CHUNK = 128
SG_BLOCK = 64
SG_GROUPS = 8
ATT_BLOCK = 256
SCAN_BLOCK = 256
VMEM_LIMIT = 48 * 1024 * 1024
ATT_BWD_VMEM_LIMIT = 56 * 1024 * 1024
ADAM_LR, ADAM_B1, ADAM_B2, ADAM_EPS, ADAM_WD, ADAM_STEP = 0.001, 0.9, 0.999, 1e-08, 0.01, 10
NEG = -1e30

_WEIGHTS = ['norm_mix', 'norm_mlp', 'w_ada', 'b_ada', 'w_mlp_in', 'w_mlp_out', 'fox_w_in', 'fox_b_f',
            'fox_q_norm', 'fox_k_norm', 'fox_w_out', 'sg_w_in', 'sg_ln_g', 'sg_ln_b', 'sg_w_s', 'sg_b_s',
            'sg_w_out', 'cv_w_pw1', 'cv_b_pw1', 'cv_w_dw', 'cv_b_dw', 'cv_ln_g', 'cv_ln_b', 'cv_w_pw2',
            'cv_b_pw2']
_ARGS = ['x', 'c'] + _WEIGHTS + ['loss_target'] + ['m_' + n for n in _WEIGHTS] + ['v_' + n for n in _WEIGHTS]


def _cparams(sem=None, vmem=VMEM_LIMIT):
    return pltpu.CompilerParams(dimension_semantics=sem, vmem_limit_bytes=vmem)


def _colsum(v):
    return jnp.sum(v, axis=0, keepdims=True)


def _sigmoid(v):
    return 1.0 / (1.0 + jnp.exp(-v))


def _rowwise(name, fn, rows, consts, row_out, red_out, tr):
    n_rows = rows[0].shape[0]
    tr = min(tr, n_rows)
    assert n_rows % tr == 0
    nr, nc, no = len(rows), len(consts), len(row_out)

    def body(*refs):
        ins = [r[...] for r in refs[:nr + nc]]
        outs, reds = fn(*ins)
        out_refs = refs[nr + nc:nr + nc + no]
        red_refs = refs[nr + nc + no:]
        for o_ref, o in zip(out_refs, outs):
            o_ref[...] = o.astype(o_ref.dtype)
        if red_refs:
            @pl.when(pl.program_id(0) == 0)
            def _():
                for r_ref in red_refs:
                    r_ref[...] = jnp.zeros_like(r_ref)
            for r_ref, r in zip(red_refs, reds):
                r_ref[...] += r

    def rspec(a):
        return pl.BlockSpec((tr,) + a.shape[1:], lambda i: (i,) + (0,) * (a.ndim - 1))

    def cspec(shape):
        return pl.BlockSpec(shape, lambda i: (0,) * len(shape))

    out_shape = [jax.ShapeDtypeStruct((n_rows, w), dt) for w, dt in row_out]
    out_shape += [jax.ShapeDtypeStruct(s, F32) for s in red_out]
    out_specs = [pl.BlockSpec((tr, w), lambda i: (i, 0)) for w, _ in row_out] + [cspec(s) for s in red_out]
    res = pl.pallas_call(
        body, name=name, grid=(n_rows // tr,),
        in_specs=[rspec(a) for a in rows] + [cspec(a.shape) for a in consts],
        out_specs=out_specs, out_shape=out_shape,
        compiler_params=_cparams(("arbitrary",)),
    )(*rows, *consts)
    return res[:no], res[no:]


def _mm(name, a, b, *, ta=False, tb=False, out_dtypes=(F32,), epi=None, tiles=(), vecs=(), tm=512, tn=512):
    m_dim, k_dim = (a.shape[1], a.shape[0]) if ta else a.shape
    n_dim = b.shape[0] if tb else b.shape[1]
    assert (b.shape[1] if tb else b.shape[0]) == k_dim
    tm, tn = min(tm, m_dim), min(tn, n_dim)
    assert m_dim % tm == 0 and n_dim % tn == 0, (name, m_dim, n_dim, tm, tn)
    dims = (((0 if ta else 1,), (1 if tb else 0,)), ((), ()))
    nx = len(tiles) + len(vecs)

    def body(a_ref, b_ref, *rest):
        acc = lax.dot_general(a_ref[...], b_ref[...], dims, preferred_element_type=F32)
        outs = epi(acc, *[r[...] for r in rest[:nx]]) if epi is not None else (acc,)
        for o_ref, o in zip(rest[nx:], outs):
            o_ref[...] = o.astype(o_ref.dtype)

    a_spec = pl.BlockSpec((k_dim, tm), lambda i, j: (0, i)) if ta else pl.BlockSpec((tm, k_dim), lambda i, j: (i, 0))
    b_spec = pl.BlockSpec((tn, k_dim), lambda i, j: (j, 0)) if tb else pl.BlockSpec((k_dim, tn), lambda i, j: (0, j))
    t_spec = pl.BlockSpec((tm, tn), lambda i, j: (i, j))
    v_spec = pl.BlockSpec((1, tn), lambda i, j: (0, j))
    res = pl.pallas_call(
        body, name=name, grid=(m_dim // tm, n_dim // tn),
        in_specs=[a_spec, b_spec] + [t_spec] * len(tiles) + [v_spec] * len(vecs),
        out_specs=[t_spec] * len(out_dtypes),
        out_shape=[jax.ShapeDtypeStruct((m_dim, n_dim), dt) for dt in out_dtypes],
        compiler_params=_cparams(("parallel", "parallel")),
    )(a, b, *tiles, *vecs)
    return res


def _exchange(name, arrays, scatter):
    n = len(arrays)
    out_shape = [jax.ShapeDtypeStruct(((N_DEV,) + a.shape[1:]) if scatter else ((N_DEV,) + a.shape), a.dtype)
                 for a in arrays]

    def body(*refs):
        in_refs, out_refs = refs[:n], refs[n:2 * n]
        send_sems, recv_sems, local_sems = refs[2 * n:]
        x, y, c = lax.axis_index("x"), lax.axis_index("y"), lax.axis_index("c")
        me = 4 * x + 2 * y + c
        copies = []
        for a in range(n):
            src_mine = in_refs[a].at[me] if scatter else in_refs[a]
            local = pltpu.make_async_copy(src_mine, out_refs[a].at[me], local_sems.at[a])
            local.start()
            copies.append(local)
        remote = []
        for k in range(1, N_DEV):
            px, py, pc = x ^ ((k >> 2) & 1), y ^ ((k >> 1) & 1), c ^ (k & 1)
            peer = 4 * px + 2 * py + pc
            for a in range(n):
                src = in_refs[a].at[peer] if scatter else in_refs[a]
                cp = pltpu.make_async_remote_copy(
                    src_ref=src, dst_ref=out_refs[a].at[me],
                    send_sem=send_sems.at[a * (N_DEV - 1) + k - 1],
                    recv_sem=recv_sems.at[a * (N_DEV - 1) + k - 1],
                    device_id=(px, py, pc), device_id_type=pl.DeviceIdType.MESH)
                cp.start()
                arrive = pltpu.make_async_remote_copy(
                    src_ref=src, dst_ref=out_refs[a].at[peer],
                    send_sem=send_sems.at[a * (N_DEV - 1) + k - 1],
                    recv_sem=recv_sems.at[a * (N_DEV - 1) + k - 1],
                    device_id=(px, py, pc), device_id_type=pl.DeviceIdType.MESH)
                remote.append((cp, arrive))
        for cp, arrive in remote:
            cp.wait_send()
            arrive.wait_recv()
        for local in copies:
            local.wait()

    any_spec = pl.BlockSpec(memory_space=pl.ANY)
    return pl.pallas_call(
        body, name=name,
        in_specs=[any_spec] * n, out_specs=[any_spec] * n, out_shape=out_shape,
        scratch_shapes=[pltpu.SemaphoreType.DMA((n * (N_DEV - 1),)),
                        pltpu.SemaphoreType.DMA((n * (N_DEV - 1),)),
                        pltpu.SemaphoreType.DMA((n,))],
        compiler_params=pltpu.CompilerParams(has_side_effects=True),
    )(*arrays)


def _norm_mod(x, g, sc, sh):
    r = lax.rsqrt(jnp.mean(x * x, axis=-1, keepdims=True) + EPS)
    return (x * r * g) * (1.0 + sc) + sh


def _first_norm(x, g, sc, sh):
    (h,), _ = _rowwise("first_norm", lambda x, g, sc, sh: ((_norm_mod(x, g, sc, sh),), ()),
                       [x], [g, sc, sh], [(x.shape[1], MXU_DTYPE)], [], 256)
    return h


def _res_norm(x, y, gate, g, sc, sh):
    def fn(x, y, gate, g, sc, sh):
        xn = x + gate * y
        return (xn, _norm_mod(xn, g, sc, sh)), ()
    (xn, h), _ = _rowwise("res_norm", fn, [x, y], [gate, g, sc, sh],
                          [(x.shape[1], F32), (x.shape[1], MXU_DTYPE)], [], 256)
    return xn, h


def _final_loss(x, y, gate, target):
    d = x.shape[1]

    def fn(x, y, target, gate):
        err = (x + gate * y) - target
        part = jnp.sum(jnp.sum(err * err, axis=-1, keepdims=True), axis=0, keepdims=True) * (0.5 / d)
        dx = err * (1.0 / d)
        return (dx, dx * gate), (jnp.broadcast_to(part, (1, LANES)), _colsum(dx * y))
    (dx, dy), (loss, dgate) = _rowwise("final_loss", fn, [x, y, target], [gate],
                                       [(d, F32), (d, MXU_DTYPE)], [(1, LANES), (1, d)], 256)
    return loss[0, 0], dx, dy, dgate


def _norm_bwd_core(dxo, dh, x, g, sc):
    r = lax.rsqrt(jnp.mean(x * x, axis=-1, keepdims=True) + EPS)
    xn = x * r
    dsh = _colsum(dh)
    dsc = _colsum(dh * (xn * g))
    dyy = dh * (1.0 + sc)
    dg = _colsum(dyy * xn)
    dxn = dyy * g
    dxi = dxo + r * (dxn - xn * jnp.mean(dxn * xn, axis=-1, keepdims=True))
    return dxi, dsh, dsc, dg


def _bwd_norm_gate(dxo, dh, x, y_prev, g, sc, gate_prev):
    d = x.shape[1]

    def fn(dxo, dh, x, y_prev, g, sc, gate_prev):
        dxi, dsh, dsc, dg = _norm_bwd_core(dxo, dh, x, g, sc)
        return (dxi, dxi * gate_prev), (dsh, dsc, dg, _colsum(dxi * y_prev))
    (dxi, dy), reds = _rowwise("bwd_norm_gate", fn, [dxo, dh, x, y_prev], [g, sc, gate_prev],
                               [(d, F32), (d, MXU_DTYPE)], [(1, d)] * 4, 256)
    return dxi, dy, reds


def _bwd_norm_first(dxo, dh, x, g, sc):
    d = x.shape[1]

    def fn(dxo, dh, x, g, sc):
        dxi, dsh, dsc, dg = _norm_bwd_core(dxo, dh, x, g, sc)
        return (dxi,), (dsh, dsc, dg)
    (dxi,), reds = _rowwise("bwd_norm_first", fn, [dxo, dh, x], [g, sc], [(d, F32)], [(1, d)] * 3, 256)
    return dxi, reds


def _mlp_fwd(h, w1, w2):
    def epi(acc):
        r = jnp.maximum(acc, 0.0)
        return acc, r * r
    a, z = _mm("mlp_in", h, w1, out_dtypes=(MXU_DTYPE, MXU_DTYPE), epi=epi, tm=1024, tn=512)
    (out,) = _mm("mlp_out", z, w2, tm=512, tn=512)
    return out, (a, z)


def _mlp_bwd(dy, h, w1, w2, saved):
    a, z = saved

    def epi(acc, a):
        return (acc * (2.0 * jnp.maximum(a.astype(F32), 0.0)),)
    (da,) = _mm("mlp_dz", dy, w2, tb=True, out_dtypes=(MXU_DTYPE,), epi=epi, tiles=(a,), tm=1024, tn=512)
    (dw2,) = _mm("mlp_dw2", z, dy, ta=True, out_dtypes=(MXU_DTYPE,))
    (dw1,) = _mm("mlp_dw1", h, da, ta=True, out_dtypes=(MXU_DTYPE,))
    (dh,) = _mm("mlp_dh", da, w1, tb=True)
    return dh, dw1, dw2


def _split3(v):
    hi = v.astype(jnp.bfloat16)
    r1 = v - hi.astype(F32)
    mid = r1.astype(jnp.bfloat16)
    lo = (r1 - mid.astype(F32)).astype(jnp.bfloat16)
    return hi, mid, lo


def _tri_matmul(tri, v):
    hi, mid, lo = _split3(v)
    dot = functools.partial(jnp.dot, preferred_element_type=F32)
    return dot(tri, hi) + dot(tri, mid) + dot(tri, lo)


def _log_sigmoid(v):
    return jnp.minimum(v, 0.0) - jnp.log(1.0 + jnp.exp(-jnp.abs(v)))


def _gate_fwd(proj, b_pad, col_block):
    s = proj.shape[0]
    tb = min(SCAN_BLOCK, s)
    nblk = s // tb

    def body(f_ref, b_ref, o_ref):
        row = lax.broadcasted_iota(jnp.int32, (tb, tb), 0)
        col = lax.broadcasted_iota(jnp.int32, (tb, tb), 1)
        tri = (col <= row).astype(jnp.bfloat16)

        def step(i, carry):
            rows = pl.ds(pl.multiple_of(i * tb, tb), tb)
            lf = _log_sigmoid(f_ref[rows, :] + b_ref[...])
            f = _tri_matmul(tri, lf) + carry
            o_ref[rows, :] = f
            return f[tb - 1:tb, :]
        lax.fori_loop(0, nblk, step, jnp.zeros((1, LANES), F32))

    return pl.pallas_call(
        body, name="gate_fwd", grid=(1,),
        in_specs=[pl.BlockSpec((s, LANES), lambda i: (0, col_block)), pl.BlockSpec((1, LANES), lambda i: (0, 0))],
        out_specs=pl.BlockSpec((s, LANES), lambda i: (0, 0)),
        out_shape=jax.ShapeDtypeStruct((s, LANES), F32),
        compiler_params=_cparams(("arbitrary",)),
    )(proj, b_pad)


def _gate_bwd(proj, b_pad, d_f, col_block):
    s = proj.shape[0]
    tb = min(SCAN_BLOCK, s)
    nblk = s // tb

    def body(f_ref, b_ref, d_ref, o_ref, db_ref):
        row = lax.broadcasted_iota(jnp.int32, (tb, tb), 0)
        col = lax.broadcasted_iota(jnp.int32, (tb, tb), 1)
        tri = (col >= row).astype(jnp.bfloat16)

        def step(j, carry):
            acc, db = carry
            i = nblk - 1 - j
            rows = pl.ds(pl.multiple_of(i * tb, tb), tb)
            dlf = _tri_matmul(tri, d_ref[rows, :]) + acc
            dpre = dlf * _sigmoid(-(f_ref[rows, :] + b_ref[...]))
            o_ref[rows, :] = dpre.astype(o_ref.dtype)
            return dlf[0:1, :], db + _colsum(dpre)
        _, db = lax.fori_loop(0, nblk, step, (jnp.zeros((1, LANES), F32), jnp.zeros((1, LANES), F32)))
        db_ref[...] = db

    return pl.pallas_call(
        body, name="gate_bwd", grid=(1,),
        in_specs=[pl.BlockSpec((s, LANES), lambda i: (0, col_block)), pl.BlockSpec((1, LANES), lambda i: (0, 0)),
                  pl.BlockSpec((s, LANES), lambda i: (0, 0))],
        out_specs=[pl.BlockSpec((s, LANES), lambda i: (0, 0)), pl.BlockSpec((1, LANES), lambda i: (0, 0))],
        out_shape=[jax.ShapeDtypeStruct((s, LANES), MXU_DTYPE), jax.ShapeDtypeStruct((1, LANES), F32)],
        compiler_params=_cparams(("arbitrary",)),
    )(proj, b_pad, d_f)


def _head_masks():
    lane = lax.broadcasted_iota(jnp.int32, (1, LANES), 1)
    return lane < HEAD_DIM


def _pair_norm(v, g, first):
    v2 = v * v
    ss0 = jnp.sum(jnp.where(first, v2, 0.0), axis=-1, keepdims=True)
    ss1 = jnp.sum(jnp.where(first, 0.0, v2), axis=-1, keepdims=True)
    r = jnp.where(first, lax.rsqrt(ss0 * (1.0 / HEAD_DIM) + EPS), lax.rsqrt(ss1 * (1.0 / HEAD_DIM) + EPS))
    vn = v * r
    return vn * g, vn, r


_NT = (((1,), (1,)), ((), ()))
_TN = (((0,), (0,)), ((), ()))


def _attn_fwd(proj, f_rep, f_rows, qg, kg, d_model):
    s = proj.shape[0]
    pairs = d_model // LANES
    tq = min(ATT_BLOCK, s)
    nq = s // tq
    scale = HEAD_DIM ** -0.5

    def body(q_ref, k_ref, v_ref, frep_ref, frow_ref, qg_ref, kg_ref, o_ref, lse_ref, q0_s, q1_s, kn_s, v_s):
        first = _head_masks()
        for ci in range(nq):
            rows = pl.ds(ci * tq, tq)
            qn, _, _ = _pair_norm(q_ref[rows, :], qg_ref[...], first)
            qn = qn * scale
            q0_s[rows, :] = jnp.where(first, qn, 0.0).astype(q0_s.dtype)
            q1_s[rows, :] = jnp.where(first, 0.0, qn).astype(q1_s.dtype)
            kn, _, _ = _pair_norm(k_ref[rows, :], kg_ref[...], first)
            kn_s[rows, :] = kn.astype(kn_s.dtype)
            v_s[rows, :] = v_ref[rows, :].astype(v_s.dtype)

        rid = lax.broadcasted_iota(jnp.int32, (tq, tq), 0)
        cid = lax.broadcasted_iota(jnp.int32, (tq, tq), 1)
        causal = cid <= rid

        def q_block(qi, _):
            qrows = pl.ds(pl.multiple_of(qi * tq, tq), tq)
            qs = (q0_s[qrows, :], q1_s[qrows, :])
            frep = frep_ref[qrows, :]
            fq = (frep[:, 0:1], frep[:, HEAD_DIM:HEAD_DIM + 1])

            def kv_step(kj, carry, diag):
                krows = pl.ds(pl.multiple_of(kj * tq, tq), tq)
                k = kn_s[krows, :]
                v = v_s[krows, :]
                fk = frow_ref[0, :, krows]
                new = []
                for hd in range(2):
                    m, l, acc = carry[hd]
                    sc = lax.dot_general(qs[hd], k, _NT, preferred_element_type=F32) + (fq[hd] - fk[hd:hd + 1, :])
                    if diag:
                        sc = jnp.where(causal, sc, NEG)
                    m_new = jnp.maximum(m, jnp.max(sc, axis=-1, keepdims=True))
                    p = jnp.exp(sc - m_new)
                    alpha = jnp.exp(m - m_new)
                    l = alpha * l + jnp.sum(p, axis=-1, keepdims=True)
                    acc = alpha * acc + jnp.dot(p.astype(v.dtype), v, preferred_element_type=F32)
                    new.append((m_new, l, acc))
                return tuple(new)

            init = tuple((jnp.full((tq, 1), NEG, F32), jnp.zeros((tq, 1), F32), jnp.zeros((tq, LANES), F32))
                         for _ in range(2))
            carry = lax.fori_loop(0, qi, lambda kj, cr: kv_step(kj, cr, False), init)
            (m0, l0, a0), (m1, l1, a1) = kv_step(qi, carry, True)
            o_ref[qrows, :] = jnp.where(first, a0 / l0, a1 / l1).astype(o_ref.dtype)
            lse_ref[qrows, :] = jnp.where(first, m0 + jnp.log(l0), m1 + jnp.log(l1))
            return 0
        lax.fori_loop(0, nq, q_block, 0)

    blk = lambda off: pl.BlockSpec((s, LANES), lambda h: (0, off + h))
    vec = pl.BlockSpec((1, LANES), lambda h: (0, 0))
    return pl.pallas_call(
        body, name="attn_fwd", grid=(pairs,),
        in_specs=[blk(0), blk(pairs), blk(2 * pairs), blk(0),
                  pl.BlockSpec((1, 2, s), lambda h: (h, 0, 0)), vec, vec],
        out_specs=[blk(0), blk(0)],
        out_shape=[jax.ShapeDtypeStruct((s, d_model), MXU_DTYPE), jax.ShapeDtypeStruct((s, d_model), F32)],
        scratch_shapes=[pltpu.VMEM((s, LANES), MXU_DTYPE)] * 4,
        compiler_params=_cparams(("arbitrary",)),
    )(proj, proj, proj, f_rep, f_rows, qg, kg)


def _attn_bwd(proj, do, o, lse, f_rep, f_rows, qg, kg, d_model):
    s = proj.shape[0]
    pairs = d_model // LANES
    tq = min(ATT_BLOCK, s)
    nq = s // tq
    scale = HEAD_DIM ** -0.5

    def norm_bwd(raw, g, dn, first):
        _, xn, r = _pair_norm(raw, g, first)
        dxn = dn * g
        t = dxn * xn
        mu0 = jnp.sum(jnp.where(first, t, 0.0), axis=-1, keepdims=True)
        mu1 = jnp.sum(jnp.where(first, 0.0, t), axis=-1, keepdims=True)
        mu = jnp.where(first, mu0, mu1) * (1.0 / HEAD_DIM)
        return r * (dxn - xn * mu), _colsum(dn * xn)

    def body(q_ref, k_ref, v_ref, do_ref, o_ref, lse_ref, frep_ref, frow_ref, qg_ref, kg_ref,
             dq_ref, dk_ref, dv_ref, dfq_ref, dfk_ref, dqg_ref, dkg_ref,
             q0_s, q1_s, kn_s, v_s, do0_s, do1_s, dk_s, dv_s):
        first = _head_masks()
        for ci in range(nq):
            rows = pl.ds(ci * tq, tq)
            qn, _, _ = _pair_norm(q_ref[rows, :], qg_ref[...], first)
            qn = qn * scale
            q0_s[rows, :] = jnp.where(first, qn, 0.0).astype(q0_s.dtype)
            q1_s[rows, :] = jnp.where(first, 0.0, qn).astype(q1_s.dtype)
            kn, _, _ = _pair_norm(k_ref[rows, :], kg_ref[...], first)
            kn_s[rows, :] = kn.astype(kn_s.dtype)
            v_s[rows, :] = v_ref[rows, :].astype(v_s.dtype)
            dov = do_ref[rows, :]
            do0_s[rows, :] = jnp.where(first, dov, jnp.zeros_like(dov))
            do1_s[rows, :] = jnp.where(first, jnp.zeros_like(dov), dov)
        dk_s[...] = jnp.zeros_like(dk_s)
        dv_s[...] = jnp.zeros_like(dv_s)
        dfk_ref[...] = jnp.zeros_like(dfk_ref)

        rid = lax.broadcasted_iota(jnp.int32, (tq, tq), 0)
        cid = lax.broadcasted_iota(jnp.int32, (tq, tq), 1)
        causal = cid <= rid

        def q_block(qi, dqg):
            qrows = pl.ds(pl.multiple_of(qi * tq, tq), tq)
            qs = (q0_s[qrows, :], q1_s[qrows, :])
            dos = (do0_s[qrows, :], do1_s[qrows, :])
            frep = frep_ref[qrows, :]
            lse = lse_ref[qrows, :]
            ov = o_ref[qrows, :].astype(F32)
            fq = (frep[:, 0:1], frep[:, HEAD_DIM:HEAD_DIM + 1])
            ls = (lse[:, 0:1], lse[:, HEAD_DIM:HEAD_DIM + 1])
            dls = tuple(jnp.sum(dos[hd].astype(F32) * ov, axis=-1, keepdims=True) for hd in range(2))

            def kv_step(kj, carry, diag):
                dq, rs = carry[0], list(carry[1:])
                krows = pl.ds(pl.multiple_of(kj * tq, tq), tq)
                k = kn_s[krows, :]
                v = v_s[krows, :]
                fk = frow_ref[0, :, krows]
                dk_add = jnp.zeros((tq, LANES), F32)
                dv_add = jnp.zeros((tq, LANES), F32)
                dfk = []
                for hd in range(2):
                    sc = lax.dot_general(qs[hd], k, _NT, preferred_element_type=F32) + (fq[hd] - fk[hd:hd + 1, :])
                    if diag:
                        sc = jnp.where(causal, sc, NEG)
                    p = jnp.exp(sc - ls[hd])
                    dp = lax.dot_general(dos[hd], v, _NT, preferred_element_type=F32)
                    ds = p * (dp - dls[hd])
                    dsb = ds.astype(k.dtype)
                    dqh = jnp.dot(dsb, k, preferred_element_type=F32)
                    dq = dq + (jnp.where(first, dqh, 0.0) if hd == 0 else jnp.where(first, 0.0, dqh))
                    dk_add = dk_add + lax.dot_general(dsb, qs[hd], _TN, preferred_element_type=F32)
                    dv_add = dv_add + lax.dot_general(p.astype(k.dtype), dos[hd], _TN, preferred_element_type=F32)
                    dfk.append(_colsum(ds))
                    rs[hd] = rs[hd] + jnp.sum(ds, axis=-1, keepdims=True)
                dk_s[krows, :] += dk_add
                dv_s[krows, :] += dv_add
                dfk_ref[0, :, krows] -= jnp.concatenate(dfk, axis=0)
                return (dq, rs[0], rs[1])

            init = (jnp.zeros((tq, LANES), F32), jnp.zeros((tq, 1), F32), jnp.zeros((tq, 1), F32))
            carry = lax.fori_loop(0, qi, lambda kj, cr: kv_step(kj, cr, False), init)
            dq, rs0, rs1 = kv_step(qi, carry, True)
            dfq_ref[qrows, :] = jnp.where(first, rs0, rs1)
            dq_raw, dg = norm_bwd(q_ref[qrows, :], qg_ref[...], dq * scale, first)
            dq_ref[qrows, :] = dq_raw.astype(dq_ref.dtype)
            return dqg + dg
        dqg_ref[0] = lax.fori_loop(0, nq, q_block, jnp.zeros((1, LANES), F32))

        dkg = jnp.zeros((1, LANES), F32)
        for ci in range(nq):
            rows = pl.ds(ci * tq, tq)
            dk_raw, dg = norm_bwd(k_ref[rows, :], kg_ref[...], dk_s[rows, :], first)
            dk_ref[rows, :] = dk_raw.astype(dk_ref.dtype)
            dkg = dkg + dg
            dv_ref[rows, :] = dv_s[rows, :].astype(dv_ref.dtype)
        dkg_ref[0] = dkg

    blk = lambda off: pl.BlockSpec((s, LANES), lambda h: (0, off + h))
    vec = pl.BlockSpec((1, LANES), lambda h: (0, 0))
    frow = pl.BlockSpec((1, 2, s), lambda h: (h, 0, 0))
    gout = pl.BlockSpec((1, 1, LANES), lambda h: (h, 0, 0))
    act = jax.ShapeDtypeStruct((s, d_model), MXU_DTYPE)
    gsh = jax.ShapeDtypeStruct((pairs, 1, LANES), F32)
    return pl.pallas_call(
        body, name="attn_bwd", grid=(pairs,),
        in_specs=[blk(0), blk(pairs), blk(2 * pairs), blk(0), blk(0), blk(0), blk(0), frow, vec, vec],
        out_specs=[blk(0), blk(0), blk(0), blk(0), frow, gout, gout],
        out_shape=[act, act, act, jax.ShapeDtypeStruct((s, d_model), F32),
                   jax.ShapeDtypeStruct((pairs, 2, s), F32), gsh, gsh],
        scratch_shapes=[pltpu.VMEM((s, LANES), MXU_DTYPE)] * 6 + [pltpu.VMEM((s, LANES), F32)] * 2,
        compiler_params=_cparams(("arbitrary",), ATT_BWD_VMEM_LIMIT),
    )(proj, proj, proj, do, o, lse, f_rep, f_rows, qg, kg)


def _pad_cols(a, n):
    return jnp.pad(a, ((0, 0), (0, n - a.shape[1])))


def _fox_fwd(h, w):
    d = h.shape[1]
    pairs = d // LANES
    (proj,) = _mm("fox_in", h, w["w_in"], tm=1024, tn=640)
    f_cum = _gate_fwd(proj, w["b_f"], 3 * pairs)
    f16 = f_cum[:, :d // HEAD_DIM]
    f_rep = jnp.repeat(f16, HEAD_DIM, axis=1)
    f_rows = f16.T.reshape(pairs, 2, h.shape[0])
    o, lse = _attn_fwd(proj, f_rep, f_rows, w["qg"], w["kg"], d)
    (y,) = _mm("fox_out", o, w["w_out"])
    return y, (proj, f_rep, f_rows, o, lse)


def _fox_bwd(dy, h, w, saved):
    proj, f_rep, f_rows, o, lse = saved
    s, d = h.shape
    pairs = d // LANES
    (do,) = _mm("fox_do", dy, w["w_out"], tb=True, out_dtypes=(MXU_DTYPE,))
    (dw_out,) = _mm("fox_dwout", o, dy, ta=True, out_dtypes=(MXU_DTYPE,))
    dq, dk, dv, dfq, dfk, dqg, dkg = _attn_bwd(proj, do, o, lse, f_rep, f_rows, w["qg"], w["kg"], d)
    d_f = _pad_cols(dfk.reshape(2 * pairs, s).T + dfq[:, ::HEAD_DIM], LANES)
    dfpre, db_f = _gate_bwd(proj, w["b_f"], d_f, 3 * pairs)
    dproj = jnp.concatenate([dq, dk, dv, dfpre], axis=1)
    (dw_in,) = _mm("fox_dwin", h, dproj, ta=True, out_dtypes=(MXU_DTYPE,), tn=640)
    (dh,) = _mm("fox_dh", dproj, w["w_in"], tb=True)
    fold = lambda g: jnp.sum(g, axis=(0, 1)).reshape(2, HEAD_DIM).sum(axis=0)
    return dh, dict(w_in=dw_in, w_out=dw_out, b_f=db_f[0, :d // HEAD_DIM], qg=fold(dqg), kg=fold(dkg))


_GELU_C = math.sqrt(2.0 / math.pi)


def _gelu(v):
    return 0.5 * v * (1.0 + jnp.tanh(_GELU_C * (v + 0.044715 * v * v * v)))


def _gelu_grad(v):
    t = jnp.tanh(_GELU_C * (v + 0.044715 * v * v * v))
    return 0.5 * (1.0 + t) + 0.5 * v * (1.0 - t * t) * (_GELU_C * (1.0 + 3.0 * 0.044715 * v * v))


def _ln_stats(v):
    mu = jnp.mean(v, axis=-1, keepdims=True)
    vc = v - mu
    r = lax.rsqrt(jnp.mean(vc * vc, axis=-1, keepdims=True) + EPS)
    return vc * r, r


def _sg_mask():
    t = lax.broadcasted_iota(jnp.int32, (SG_CHUNK, SG_CHUNK), 0) // SG_BLOCK
    sidx = lax.broadcasted_iota(jnp.int32, (SG_CHUNK, SG_CHUNK), 1) // SG_BLOCK
    return sidx <= t


def _sgu_fwd(uv_pre, ln_g, ln_b, w_s, b_st):
    s, w2 = uv_pre.shape
    wd = w2 // 2
    tr = min(256, s)

    def fn(uv_pre, ln_g, ln_b, w_s, b_st):
        uv = _gelu(uv_pre)
        u = uv[:, :wd]
        vh, _ = _ln_stats(uv[:, wd:])
        vl = (vh * ln_g + ln_b).astype(MXU_DTYPE)
        mask = _sg_mask()
        cols = []
        for g in range(SG_GROUPS):
            wg = jnp.where(mask, w_s[g * SG_CHUNK:(g + 1) * SG_CHUNK, :], 0.0).astype(MXU_DTYPE)
            parts = []
            for ci in range(tr // SG_CHUNK):
                vt = vl[ci * SG_CHUNK:(ci + 1) * SG_CHUNK, g * SG_CHUNK:(g + 1) * SG_CHUNK]
                parts.append(jnp.dot(wg, vt, preferred_element_type=F32) + b_st[:, g:g + 1])
            cols.append(jnp.concatenate(parts, axis=0) if len(parts) > 1 else parts[0])
        vout = jnp.concatenate(cols, axis=1)
        return (u * vout,), ()
    (m,), _ = _rowwise("sgu_fwd", fn, [uv_pre], [ln_g, ln_b, w_s, b_st], [(wd, MXU_DTYPE)], [], tr)
    return m


def _sgu_bwd(uv_pre, dm, ln_g, ln_b, w_s, b_st):
    s, w2 = uv_pre.shape
    wd = w2 // 2
    tr = min(256, s)

    def fn(uv_pre, dm, ln_g, ln_b, w_s, b_st):
        uv = _gelu(uv_pre)
        u = uv[:, :wd]
        vh, r = _ln_stats(uv[:, wd:])
        vl = (vh * ln_g + ln_b).astype(MXU_DTYPE)
        mask = _sg_mask()
        lane = lax.broadcasted_iota(jnp.int32, (1, LANES), 1)
        cols, dcols, dws, dbs = [], [], [], jnp.zeros((SG_CHUNK, LANES), F32)
        for g in range(SG_GROUPS):
            wg = jnp.where(mask, w_s[g * SG_CHUNK:(g + 1) * SG_CHUNK, :], 0.0).astype(MXU_DTYPE)
            parts, dparts = [], []
            dwg = jnp.zeros((SG_CHUNK, SG_CHUNK), F32)
            dbg = jnp.zeros((SG_CHUNK, 1), F32)
            for ci in range(tr // SG_CHUNK):
                rs = slice(ci * SG_CHUNK, (ci + 1) * SG_CHUNK)
                cs = slice(g * SG_CHUNK, (g + 1) * SG_CHUNK)
                vt = vl[rs, cs]
                parts.append(jnp.dot(wg, vt, preferred_element_type=F32) + b_st[:, g:g + 1])
                dvo = dm[rs, cs] * u[rs, cs]
                dvob = dvo.astype(MXU_DTYPE)
                dparts.append(lax.dot_general(wg, dvob, _TN, preferred_element_type=F32))
                dwg = dwg + lax.dot_general(dvob, vt, _NT, preferred_element_type=F32)
                dbg = dbg + jnp.sum(dvo, axis=-1, keepdims=True)
            cols.append(jnp.concatenate(parts, axis=0) if len(parts) > 1 else parts[0])
            dcols.append(jnp.concatenate(dparts, axis=0) if len(dparts) > 1 else dparts[0])
            dws.append(jnp.where(mask, dwg, 0.0))
            dbs = dbs + jnp.where(lane == g, dbg, 0.0)
        vout = jnp.concatenate(cols, axis=1)
        dvl = jnp.concatenate(dcols, axis=1)
        du = dm * vout
        dlg = _colsum(dvl * vh)
        dlb = _colsum(dvl)
        dvh = dvl * ln_g
        dv = r * (dvh - jnp.mean(dvh, axis=-1, keepdims=True) - vh * jnp.mean(dvh * vh, axis=-1, keepdims=True))
        dpre = jnp.concatenate([du, dv], axis=1) * _gelu_grad(uv_pre)
        return (dpre,), (dlg, dlb, jnp.concatenate(dws, axis=0), dbs)
    (dpre,), reds = _rowwise("sgu_bwd", fn, [uv_pre, dm], [ln_g, ln_b, w_s, b_st], [(w2, MXU_DTYPE)],
                             [(1, wd), (1, wd), (SG_GROUPS * SG_CHUNK, SG_CHUNK), (SG_CHUNK, LANES)], tr)
    return dpre, reds


def _sg_fwd(h, w):
    (uv_pre,) = _mm("sg_in", h, w["w_in"], tm=1024, tn=512)
    m = _sgu_fwd(uv_pre, w["ln_g"], w["ln_b"], w["w_s"], w["b_st"])
    (y,) = _mm("sg_out", m, w["w_out"])
    return y, (uv_pre, m)


def _sg_bwd(dy, h, w, saved):
    uv_pre, m = saved
    (dm,) = _mm("sg_dm", dy, w["w_out"], tb=True)
    (dw_out,) = _mm("sg_dwout", m, dy, ta=True, out_dtypes=(MXU_DTYPE,))
    dpre, (dlg, dlb, dws, dbs) = _sgu_bwd(uv_pre, dm, w["ln_g"], w["ln_b"], w["w_s"], w["b_st"])
    (dw_in,) = _mm("sg_dwin", h, dpre, ta=True, out_dtypes=(MXU_DTYPE,))
    (dh,) = _mm("sg_dh", dpre, w["w_in"], tb=True)
    return dh, dict(w_in=dw_in, w_out=dw_out, ln_g=dlg, ln_b=dlb, w_s=dws, b_s=dbs[:, :SG_GROUPS].T)


def _conv_fwd_kernel(ypad, w_dw, b_dw):
    s = ypad.shape[0] - CONV_PAD
    d = ypad.shape[1]
    tt = min(256, s)
    ext = tt + CONV_PAD

    def body(y_ref, w_ref, b_ref, o_ref):
        def chunk(ci, _):
            base = pl.multiple_of(ci * tt, tt)
            e = y_ref[pl.ds(base, ext), :]
            acc = jnp.zeros((tt, LANES), F32) + b_ref[...]
            for j in range(CONV_WIDTH):
                sh = pltpu.roll(e, ext - (CONV_PAD - CONV_WIDTH + 1 + j), 0)[:tt, :]
                acc = acc + w_ref[j:j + 1, :] * sh
            o_ref[pl.ds(base, tt), :] = acc
            return 0
        lax.fori_loop(0, s // tt, chunk, 0)

    return pl.pallas_call(
        body, name="conv_fwd", grid=(d // LANES,),
        in_specs=[pl.BlockSpec((s + CONV_PAD, LANES), lambda i: (0, i)),
                  pl.BlockSpec((CONV_PAD, LANES), lambda i: (0, i)), pl.BlockSpec((1, LANES), lambda i: (0, i))],
        out_specs=pl.BlockSpec((s, LANES), lambda i: (0, i)),
        out_shape=jax.ShapeDtypeStruct((s, d), F32),
        compiler_params=_cparams(("parallel",)),
    )(ypad, w_dw, b_dw)


def _conv_bwd_kernel(ypad, dpad, w_dw):
    s = ypad.shape[0] - CONV_PAD
    d = ypad.shape[1]
    tt = min(256, s)
    ext = tt + CONV_PAD

    def body(y_ref, d_ref, w_ref, o_ref, dw_ref):
        dw_ref[...] = jnp.zeros_like(dw_ref)

        def chunk(ci, _):
            base = pl.multiple_of(ci * tt, tt)
            ye = y_ref[pl.ds(base, ext), :]
            de = d_ref[pl.ds(base, ext), :]
            dcur = de[:tt, :]
            acc = jnp.zeros((tt, LANES), F32)
            for j in range(CONV_WIDTH):
                back = CONV_WIDTH - 1 - j
                dsh = dcur if back == 0 else pltpu.roll(de, ext - back, 0)[:tt, :]
                acc = acc + w_ref[j:j + 1, :] * dsh
                ysh = pltpu.roll(ye, ext - (CONV_PAD - CONV_WIDTH + 1 + j), 0)[:tt, :]
                dw_ref[j:j + 1, :] += _colsum(dcur * ysh)
            o_ref[pl.ds(base, tt), :] = acc
            return 0
        lax.fori_loop(0, s // tt, chunk, 0)

    return pl.pallas_call(
        body, name="conv_bwd", grid=(d // LANES,),
        in_specs=[pl.BlockSpec((s + CONV_PAD, LANES), lambda i: (0, i)),
                  pl.BlockSpec((s + CONV_PAD, LANES), lambda i: (0, i)),
                  pl.BlockSpec((CONV_PAD, LANES), lambda i: (0, i))],
        out_specs=[pl.BlockSpec((s, LANES), lambda i: (0, i)), pl.BlockSpec((CONV_PAD, LANES), lambda i: (0, i))],
        out_shape=[jax.ShapeDtypeStruct((s, d), F32), jax.ShapeDtypeStruct((CONV_PAD, d), F32)],
        compiler_params=_cparams(("parallel",)),
    )(ypad, dpad, w_dw)


def _cv_fwd(h, w):
    d = h.shape[1]
    (y1,) = _mm("cv_pw1", h, w["w_pw1"], tm=1024, tn=512)

    def glu(y1, b1):
        t = y1 + b1
        return (t[:, :d] * _sigmoid(t[:, d:]),), ()
    (y2,), _ = _rowwise("cv_glu", glu, [y1], [w["b_pw1"]], [(d, F32)], [], 256)
    y3 = _conv_fwd_kernel(jnp.pad(y2, ((CONV_PAD, 0), (0, 0))), w["w_dw"], w["b_dw"])

    def lnsilu(y3, g, b):
        vh, _ = _ln_stats(y3)
        y4 = vh * g + b
        return (y4 * _sigmoid(y4),), ()
    (y5,), _ = _rowwise("cv_lnsilu", lnsilu, [y3], [w["ln_g"], w["ln_b"]], [(d, MXU_DTYPE)], [], 256)
    (y,) = _mm("cv_pw2", y5, w["w_pw2"], epi=lambda acc, b: (acc + b,), vecs=(w["b_pw2"],))
    return y, (y1, y2, y3, y5)


def _cv_bwd(dy, h, w, saved):
    y1, y2, y3, y5 = saved
    d = h.shape[1]
    (dy5,) = _mm("cv_dy5", dy, w["w_pw2"], tb=True)
    (dw_pw2,) = _mm("cv_dwpw2", y5, dy, ta=True, out_dtypes=(MXU_DTYPE,))

    def ln_bwd(dy5, y3, dyb, g, b):
        vh, r = _ln_stats(y3)
        y4 = vh * g + b
        sg = _sigmoid(y4)
        dy4 = dy5 * (sg * (1.0 + y4 * (1.0 - sg)))
        dvh = dy4 * g
        dy3 = r * (dvh - jnp.mean(dvh, axis=-1, keepdims=True) - vh * jnp.mean(dvh * vh, axis=-1, keepdims=True))
        return (dy3,), (_colsum(dy4 * vh), _colsum(dy4), _colsum(dy3), _colsum(dyb.astype(F32)))
    (dy3,), (dlg, dlb, db_dw, db_pw2) = _rowwise("cv_ln_bwd", ln_bwd, [dy5, y3, dy], [w["ln_g"], w["ln_b"]],
                                                 [(d, F32)], [(1, d)] * 4, 256)
    dy2, dw_dw = _conv_bwd_kernel(jnp.pad(y2, ((CONV_PAD, 0), (0, 0))), jnp.pad(dy3, ((0, CONV_PAD), (0, 0))),
                                  w["w_dw"])

    def glu_bwd(y1, dy2, b1):
        t = y1 + b1
        a, sg = t[:, :d], _sigmoid(t[:, d:])
        dy1 = jnp.concatenate([dy2 * sg, dy2 * a * sg * (1.0 - sg)], axis=1)
        return (dy1,), (_colsum(dy1),)
    (dy1,), (db_pw1,) = _rowwise("cv_glu_bwd", glu_bwd, [y1, dy2], [w["b_pw1"]], [(2 * d, MXU_DTYPE)],
                                 [(1, 2 * d)], 256)
    (dw_pw1,) = _mm("cv_dwpw1", h, dy1, ta=True, out_dtypes=(MXU_DTYPE,))
    (dh,) = _mm("cv_dh", dy1, w["w_pw1"], tb=True)
    return dh, dict(w_pw1=dw_pw1, w_pw2=dw_pw2, b_pw1=db_pw1, b_pw2=db_pw2, w_dw=dw_dw[:CONV_WIDTH],
                    b_dw=db_dw, ln_g=dlg, ln_b=dlb)


def _ada_outer(c_t, dmod):
    def fn(c_t, dmod):
        acc = c_t[:, 0:1] * dmod[0:1, :]
        for b in range(1, N_DEV):
            acc = acc + c_t[:, b:b + 1] * dmod[b:b + 1, :]
        return (acc,), ()
    (g,), _ = _rowwise("ada_outer", fn, [c_t], [dmod], [(dmod.shape[1], F32)], [], 256)
    return g


def _adamw(name, parts, w, m, v, tr):
    npart = parts.shape[0]
    cols = w.shape[1]

    def fn(parts, w, m, v):
        g = parts[0].astype(F32)
        for q in range(1, npart):
            g = g + parts[q].astype(F32)
        m_new = ADAM_B1 * m + (1.0 - ADAM_B1) * g
        v_new = ADAM_B2 * v + (1.0 - ADAM_B2) * (g * g)
        m_hat = m_new / (1.0 - ADAM_B1 ** ADAM_STEP)
        v_hat = v_new / (1.0 - ADAM_B2 ** ADAM_STEP)
        delta = -ADAM_LR * (m_hat / (jnp.sqrt(v_hat) + ADAM_EPS) + ADAM_WD * w)
        return (g, delta, m_new, v_new), ()
    rows = w.shape[0]
    tr = min(tr, rows)

    def body(p_ref, w_ref, m_ref, v_ref, g_o, d_o, m_o, v_o):
        outs, _ = fn(p_ref[...], w_ref[...], m_ref[...], v_ref[...])
        for o_ref, o in zip((g_o, d_o, m_o, v_o), outs):
            o_ref[...] = o

    spec = pl.BlockSpec((tr, cols), lambda i: (i, 0))
    return pl.pallas_call(
        body, name=name, grid=(rows // tr,),
        in_specs=[pl.BlockSpec((npart, tr, cols), lambda i: (0, i, 0)), spec, spec, spec],
        out_specs=[spec] * 4, out_shape=[jax.ShapeDtypeStruct((rows, cols), F32)] * 4,
        compiler_params=_cparams(("parallel",)),
    )(parts, w, m, v)


def _pack(arrays):
    flat = jnp.concatenate([a.reshape(-1).astype(F32) for a in arrays])
    n = flat.shape[0]
    rows = -(-n // (8 * LANES)) * 8
    return jnp.pad(flat, (0, rows * LANES - n)).reshape(rows, LANES)


def _unpack(buf, shapes, lead=()):
    flat = buf.reshape(lead + (-1,))
    out, off = [], 0
    for shp in shapes:
        n = math.prod(shp)
        out.append(flat[..., off:off + n].reshape(lead + tuple(shp)))
        off += n
    return out


ADAM_TILE_ELEMS = 1 << 17


def _row_tile(rows, cols):
    want = max(8, ADAM_TILE_ELEMS // max(cols, LANES))
    if rows <= want:
        return rows
    best = None
    for t in range(8, want + 1, 8):
        if rows % t == 0:
            best = t
    assert best is not None, (rows, cols)
    return best


def _local_step(xs, tgt, mods, mw, w1, w2, norm_mix, norm_mlp):
    depth = len(mods)
    mixer_fwd = (_fox_fwd, _sg_fwd, _cv_fwd)
    mixer_bwd = (_fox_bwd, _sg_bwd, _cv_bwd)

    nsub = 2 * depth
    sub = []
    x_in = xs
    y_prev = gate_prev = None
    for k in range(nsub):
        i, is_mlp = k // 2, k % 2
        sh, sc = mods[i][3 * is_mlp], mods[i][3 * is_mlp + 1]
        g = (norm_mlp if is_mlp else norm_mix)[i:i + 1]
        if k == 0:
            h = _first_norm(x_in, g, sc, sh)
        else:
            x_in, h = _res_norm(x_in, y_prev, gate_prev, g, sc, sh)
        if is_mlp:
            y, saved = _mlp_fwd(h, w1[i], w2[i])
        else:
            y, saved = mixer_fwd[i % 3](h, mw[i])
        sub.append((x_in, h, y, saved))
        y_prev, gate_prev = y, mods[i][3 * is_mlp + 2]

    loss_part, dxo, dy, dgate = _final_loss(x_in, y_prev, gate_prev, tgt)

    dmods = [[None] * 6 for _ in range(depth)]
    g_norm = {'norm_mix': [None] * depth, 'norm_mlp': [None] * depth}
    g_mix = [None] * depth
    g_w1, g_w2 = [None] * depth, [None] * depth
    for k in reversed(range(nsub)):
        i, is_mlp = k // 2, k % 2
        x_k, h_k, _, saved = sub[k]
        dmods[i][3 * is_mlp + 2] = dgate
        if is_mlp:
            dh, g_w1[i], g_w2[i] = _mlp_bwd(dy, h_k, w1[i], w2[i], saved)
        else:
            dh, g_mix[i] = mixer_bwd[i % 3](dy, h_k, mw[i], saved)
        sc = mods[i][3 * is_mlp + 1]
        g = (norm_mlp if is_mlp else norm_mix)[i:i + 1]
        if k > 0:
            ip, mp = (k - 1) // 2, (k - 1) % 2
            dxo, dy, (dsh, dsc, dg, dgate) = _bwd_norm_gate(dxo, dh, x_k, sub[k - 1][2], g, sc, mods[ip][3 * mp + 2])
        else:
            dxo, (dsh, dsc, dg) = _bwd_norm_first(dxo, dh, x_k, g, sc)
        dmods[i][3 * is_mlp], dmods[i][3 * is_mlp + 1] = dsh, dsc
        g_norm['norm_mlp' if is_mlp else 'norm_mix'][i] = dg
    return loss_part, dxo, dmods, g_norm, g_mix, g_w1, g_w2


def kernel(x, c, norm_mix, norm_mlp, w_ada, b_ada, w_mlp_in, w_mlp_out, fox_w_in, fox_b_f, fox_q_norm, fox_k_norm, fox_w_out, sg_w_in, sg_ln_g, sg_ln_b, sg_w_s, sg_b_s, sg_w_out, cv_w_pw1, cv_b_pw1, cv_w_dw, cv_b_dw, cv_ln_g, cv_ln_b, cv_w_pw2, cv_b_pw2, loss_target, m_norm_mix, m_norm_mlp, m_w_ada, m_b_ada, m_w_mlp_in, m_w_mlp_out, m_fox_w_in, m_fox_b_f, m_fox_q_norm, m_fox_k_norm, m_fox_w_out, m_sg_w_in, m_sg_ln_g, m_sg_ln_b, m_sg_w_s, m_sg_b_s, m_sg_w_out, m_cv_w_pw1, m_cv_b_pw1, m_cv_w_dw, m_cv_b_dw, m_cv_ln_g, m_cv_ln_b, m_cv_w_pw2, m_cv_b_pw2, v_norm_mix, v_norm_mlp, v_w_ada, v_b_ada, v_w_mlp_in, v_w_mlp_out, v_fox_w_in, v_fox_b_f, v_fox_q_norm, v_fox_k_norm, v_fox_w_out, v_sg_w_in, v_sg_ln_g, v_sg_ln_b, v_sg_w_s, v_sg_b_s, v_sg_w_out, v_cv_w_pw1, v_cv_b_pw1, v_cv_w_dw, v_cv_b_dw, v_cv_ln_g, v_cv_ln_b, v_cv_w_pw2, v_cv_b_pw2):
    P = dict(zip(_ARGS, (x, c, norm_mix, norm_mlp, w_ada, b_ada, w_mlp_in, w_mlp_out, fox_w_in, fox_b_f, fox_q_norm, fox_k_norm, fox_w_out, sg_w_in, sg_ln_g, sg_ln_b, sg_w_s, sg_b_s, sg_w_out, cv_w_pw1, cv_b_pw1, cv_w_dw, cv_b_dw, cv_ln_g, cv_ln_b, cv_w_pw2, cv_b_pw2, loss_target, m_norm_mix, m_norm_mlp, m_w_ada, m_b_ada, m_w_mlp_in, m_w_mlp_out, m_fox_w_in, m_fox_b_f, m_fox_q_norm, m_fox_k_norm, m_fox_w_out, m_sg_w_in, m_sg_ln_g, m_sg_ln_b, m_sg_w_s, m_sg_b_s, m_sg_w_out, m_cv_w_pw1, m_cv_b_pw1, m_cv_w_dw, m_cv_b_dw, m_cv_ln_g, m_cv_ln_b, m_cv_w_pw2, m_cv_b_pw2, v_norm_mix, v_norm_mlp, v_w_ada, v_b_ada, v_w_mlp_in, v_w_mlp_out, v_fox_w_in, v_fox_b_f, v_fox_q_norm, v_fox_k_norm, v_fox_w_out, v_sg_w_in, v_sg_ln_g, v_sg_ln_b, v_sg_w_s, v_sg_b_s, v_sg_w_out, v_cv_w_pw1, v_cv_b_pw1, v_cv_w_dw, v_cv_b_dw, v_cv_ln_g, v_cv_ln_b, v_cv_w_pw2, v_cv_b_pw2)))
    me = 4 * lax.axis_index("x") + 2 * lax.axis_index("y") + lax.axis_index("c")
    xs = x[0]
    tgt = loss_target[0]
    s_len, d = xs.shape
    depth = norm_mix.shape[0]
    n_fox, n_sg, n_cv = fox_w_in.shape[0], sg_w_in.shape[0], cv_w_pw1.shape[0]
    heads = d // HEAD_DIM
    bf = lambda a: a.astype(MXU_DTYPE)

    cv_small = ['cv_b_pw1', 'cv_w_dw', 'cv_b_dw', 'cv_ln_g', 'cv_ln_b', 'cv_b_pw2']
    small_shapes = [c.shape] + [P[n].shape for n in cv_small]
    (small_all,) = _exchange("gather_small", [_pack([c] + [P[n] for n in cv_small])], scatter=False)
    sm = dict(zip(['c'] + cv_small, _unpack(small_all, small_shapes, lead=(N_DEV,))))
    c_all = sm['c'][:, 0, :]
    cat_last = lambda a: jnp.moveaxis(a, 0, -2).reshape(a.shape[1:-1] + (-1,))
    cvf = {n: cat_last(sm[n]) for n in cv_small}

    big = ['w_mlp_in', 'w_mlp_out', 'fox_w_in', 'fox_w_out', 'sg_w_in', 'sg_w_out', 'cv_w_pw1', 'cv_w_pw2']
    col_sharded = {'w_mlp_in', 'fox_w_in', 'sg_w_in', 'cv_w_pw1'}
    gathered = dict(zip(big, _exchange("gather_weights", [bf(P[n]) for n in big], scatter=False)))

    def full_weight(name, j):
        g = gathered[name][:, j]
        if name in col_sharded:
            return jnp.transpose(g, (1, 0, 2)).reshape(g.shape[1], -1)
        return g.reshape(-1, g.shape[2])

    c_act = c_all * _sigmoid(c_all)
    c_pad = bf(jnp.pad(c_act, ((0, 16 - N_DEV), (0, 0))))
    n_ada = w_ada.shape[2]
    (mod_part,) = _mm("ada_mod", c_pad, bf(jnp.transpose(w_ada, (1, 0, 2)).reshape(d, depth * n_ada)),
                      epi=lambda acc, b: (acc + b,),
                      vecs=(lax.dynamic_slice_in_dim(b_ada, me * n_ada, n_ada, axis=1).reshape(1, depth * n_ada),),
                      tn=n_ada)
    (mod_all,) = _exchange("gather_mod", [mod_part], scatter=False)
    mod_me = lax.dynamic_index_in_dim(mod_all, me, axis=1, keepdims=False)
    mod = jnp.transpose(mod_me.reshape(N_DEV, depth, n_ada), (1, 0, 2)).reshape(depth, 6 * d)
    mods = [[mod[i:i + 1, k * d:(k + 1) * d] for k in range(6)] for i in range(depth)]

    def mixer_weights(i):
        kind, j = i % 3, i // 3
        if kind == 0:
            w_in = full_weight('fox_w_in', j)
            n_pad = -(-w_in.shape[1] // (5 * LANES)) * (5 * LANES)
            return dict(w_in=_pad_cols(w_in, n_pad), w_out=full_weight('fox_w_out', j),
                        b_f=_pad_cols(fox_b_f[j:j + 1], LANES),
                        qg=jnp.tile(fox_q_norm[j:j + 1], (1, 2)), kg=jnp.tile(fox_k_norm[j:j + 1], (1, 2)))
        if kind == 1:
            return dict(w_in=full_weight('sg_w_in', j), w_out=full_weight('sg_w_out', j),
                        ln_g=sg_ln_g[j:j + 1], ln_b=sg_ln_b[j:j + 1],
                        w_s=sg_w_s[j].reshape(SG_GROUPS * SG_CHUNK, SG_CHUNK), b_st=_pad_cols(sg_b_s[j].T, LANES))
        return dict(w_pw1=full_weight('cv_w_pw1', j), w_pw2=full_weight('cv_w_pw2', j),
                    b_pw1=cvf['cv_b_pw1'][j:j + 1], b_pw2=cvf['cv_b_pw2'][j:j + 1],
                    w_dw=jnp.pad(cvf['cv_w_dw'][j], ((0, CONV_PAD - CONV_WIDTH), (0, 0))),
                    b_dw=cvf['cv_b_dw'][j:j + 1], ln_g=cvf['cv_ln_g'][j:j + 1], ln_b=cvf['cv_ln_b'][j:j + 1])

    mw = [mixer_weights(i) for i in range(depth)]
    w1 = [full_weight('w_mlp_in', i) for i in range(depth)]
    w2 = [full_weight('w_mlp_out', i) for i in range(depth)]
    loss_part, dxo, dmods, g_norm, g_mix, g_w1, g_w2 = _local_step(xs, tgt, mods, mw, w1, w2, norm_mix, norm_mlp)
    loss = lax.psum(loss_part, ("x", "y", "c"))
    grad_x = dxo[None]

    stack = lambda key, kind: jnp.stack([g_mix[i][key].reshape(P[name_of[(kind, key)]].shape[1:])
                                         for i in range(depth) if i % 3 == kind])
    name_of = {(0, 'b_f'): 'fox_b_f', (0, 'qg'): 'fox_q_norm', (0, 'kg'): 'fox_k_norm',
               (1, 'ln_g'): 'sg_ln_g', (1, 'ln_b'): 'sg_ln_b', (1, 'w_s'): 'sg_w_s', (1, 'b_s'): 'sg_b_s'}
    dmod_me = jnp.concatenate([jnp.concatenate(r, axis=1) for r in dmods], axis=0)
    small_g = {'dmod': dmod_me,
               'norm_mix': jnp.concatenate(g_norm['norm_mix'], axis=0),
               'norm_mlp': jnp.concatenate(g_norm['norm_mlp'], axis=0)}
    for (kind, key), nm in name_of.items():
        small_g[nm] = stack(key, kind)
    cv_keys = {'cv_b_pw1': 'b_pw1', 'cv_w_dw': 'w_dw', 'cv_b_dw': 'b_dw', 'cv_ln_g': 'ln_g', 'cv_ln_b': 'ln_b',
               'cv_b_pw2': 'b_pw2'}
    for nm, key in cv_keys.items():
        small_g[nm] = jnp.stack([g_mix[i][key].reshape(cvf[nm].shape[1:]) for i in range(depth) if i % 3 == 2])
    sg_names = list(small_g)
    sg_shapes = [small_g[n].shape for n in sg_names]
    (sg_all,) = _exchange("gather_small_grads", [_pack([small_g[n] for n in sg_names])], scatter=False)

    dmod_all = _unpack(sg_all, sg_shapes, lead=(N_DEV,))[0]
    out = {}

    def finish(name, parts, shard_of=None):
        w, m, v = P[name], P['m_' + name], P['v_' + name]
        cols = w.shape[-1]
        r2 = lambda a: a.reshape(-1, cols)
        rows = r2(w).shape[0]
        res = _adamw("adamw_" + name, parts.reshape(parts.shape[0], rows, cols), r2(w), r2(m), r2(v),
                     _row_tile(rows, cols))
        out[name] = tuple(r.reshape(w.shape) for r in res)

    c_t = c_act.T
    ada_g = []
    for i in range(depth):
        blk = lax.dynamic_slice_in_dim(dmod_all[:, i, :], me * n_ada, n_ada, axis=1)
        ada_g.append(_ada_outer(c_t, blk))
    finish('w_ada', jnp.stack(ada_g)[None])
    finish('b_ada', dmod_all)

    sm_names = [n for n in sg_names if n != 'dmod']
    sm_parts = jnp.stack([_pack([_unpack(sg_all[q], sg_shapes)[sg_names.index(n)] for n in sm_names])
                          for q in range(N_DEV)])

    def local_block(nm, a):
        if nm in cv_keys:
            n_loc = P[nm].shape[-1]
            return lax.dynamic_slice_in_dim(a, me * n_loc, n_loc, axis=a.ndim - 1)
        return a
    full_shapes = [small_g[n].shape for n in sm_names]

    def pack_full(prefix):
        arrs = []
        for nm in sm_names:
            a = P[prefix + nm]
            if nm in cv_keys:
                full = jnp.zeros(small_g[nm].shape, F32)
                a = lax.dynamic_update_slice_in_dim(full, a, me * a.shape[-1], axis=a.ndim - 1)
            arrs.append(a)
        return _pack(arrs)
    res = _adamw("adamw_small", sm_parts, pack_full(''), pack_full('m_'), pack_full('v_'),
                 _row_tile(sm_parts.shape[1], LANES))
    unp = [_unpack(r, full_shapes) for r in res]
    for idx, nm in enumerate(sm_names):
        out[nm] = tuple(local_block(nm, unp[t][idx]) for t in range(4))

    def to_slots(name, g2d):
        if name in col_sharded:
            r = g2d.shape[0]
            return jnp.transpose(g2d.reshape(r, N_DEV, -1), (1, 0, 2))
        return g2d.reshape(N_DEV, -1, g2d.shape[1])
    layer_grads = {
        'w_mlp_in': g_w1, 'w_mlp_out': g_w2,
        'fox_w_in': [g_mix[i]['w_in'][:, :fox_w_in.shape[2] * N_DEV] for i in range(depth) if i % 3 == 0],
        'fox_w_out': [g_mix[i]['w_out'] for i in range(depth) if i % 3 == 0],
        'sg_w_in': [g_mix[i]['w_in'] for i in range(depth) if i % 3 == 1],
        'sg_w_out': [g_mix[i]['w_out'] for i in range(depth) if i % 3 == 1],
        'cv_w_pw1': [g_mix[i]['w_pw1'] for i in range(depth) if i % 3 == 2],
        'cv_w_pw2': [g_mix[i]['w_pw2'] for i in range(depth) if i % 3 == 2],
    }
    slots = [jnp.stack([to_slots(n, g) for g in layer_grads[n]], axis=1) for n in big]
    recv = _exchange("scatter_grads", slots, scatter=True)
    for n, parts in zip(big, recv):
        finish(n, parts)

    outs = [loss, grad_x]
    for t in range(4):
        outs += [out[n][t] for n in _WEIGHTS]
    return tuple(outs)
```

```python
import functools
import math

import jax
import jax.numpy as jnp
from jax import lax
from jax.experimental import pallas as pl
from jax.experimental.pallas import tpu as pltpu

F32 = jnp.float32
MXU_DTYPE = jnp.bfloat16
EPS = 1e-6
N_DEV = 8
HEAD_DIM = 64
LANES = 128
CONV_WIDTH = 31
CONV_PAD = 32
SG_CHUNK = 128
SG_BLOCK = 64
SG_GROUPS = 8
ATT_BLOCK = 256
SCAN_BLOCK = 256
VMEM_LIMIT = 48 * 1024 * 1024
ATT_BWD_VMEM_LIMIT = 56 * 1024 * 1024
ADAM_LR, ADAM_B1, ADAM_B2, ADAM_EPS, ADAM_WD, ADAM_STEP = 0.001, 0.9, 0.999, 1e-08, 0.01, 10
NEG = -1e30

_WEIGHTS = ['norm_mix', 'norm_mlp', 'w_ada', 'b_ada', 'w_mlp_in', 'w_mlp_out', 'fox_w_in', 'fox_b_f',
            'fox_q_norm', 'fox_k_norm', 'fox_w_out', 'sg_w_in', 'sg_ln_g', 'sg_ln_b', 'sg_w_s', 'sg_b_s',
            'sg_w_out', 'cv_w_pw1', 'cv_b_pw1', 'cv_w_dw', 'cv_b_dw', 'cv_ln_g', 'cv_ln_b', 'cv_w_pw2',
            'cv_b_pw2']
_ARGS = ['x', 'c'] + _WEIGHTS + ['loss_target'] + ['m_' + n for n in _WEIGHTS] + ['v_' + n for n in _WEIGHTS]


def _cparams(sem=None, vmem=VMEM_LIMIT):
    return pltpu.CompilerParams(dimension_semantics=sem, vmem_limit_bytes=vmem)


def _colsum(v):
    return jnp.sum(v, axis=0, keepdims=True)


def _sigmoid(v):
    return 1.0 / (1.0 + jnp.exp(-v))


def _rowwise(name, fn, rows, consts, row_out, red_out, tr):
    n_rows = rows[0].shape[0]
    tr = min(tr, n_rows)
    assert n_rows % tr == 0
    nr, nc, no = len(rows), len(consts), len(row_out)

    def body(*refs):
        ins = [r[...] for r in refs[:nr + nc]]
        outs, reds = fn(*ins)
        out_refs = refs[nr + nc:nr + nc + no]
        red_refs = refs[nr + nc + no:]
        for o_ref, o in zip(out_refs, outs):
            o_ref[...] = o.astype(o_ref.dtype)
        if red_refs:
            @pl.when(pl.program_id(0) == 0)
            def _():
                for r_ref in red_refs:
                    r_ref[...] = jnp.zeros_like(r_ref)
            for r_ref, r in zip(red_refs, reds):
                r_ref[...] += r

    def rspec(a):
        return pl.BlockSpec((tr,) + a.shape[1:], lambda i: (i,) + (0,) * (a.ndim - 1))

    def cspec(shape):
        return pl.BlockSpec(shape, lambda i: (0,) * len(shape))

    out_shape = [jax.ShapeDtypeStruct((n_rows, w), dt) for w, dt in row_out]
    out_shape += [jax.ShapeDtypeStruct(s, F32) for s in red_out]
    out_specs = [pl.BlockSpec((tr, w), lambda i: (i, 0)) for w, _ in row_out] + [cspec(s) for s in red_out]
    res = pl.pallas_call(
        body, name=name, grid=(n_rows // tr,),
        in_specs=[rspec(a) for a in rows] + [cspec(a.shape) for a in consts],
        out_specs=out_specs, out_shape=out_shape,
        compiler_params=_cparams(("arbitrary",)),
    )(*rows, *consts)
    return res[:no], res[no:]


def _mm(name, a, b, *, ta=False, tb=False, out_dtypes=(F32,), epi=None, tiles=(), vecs=(), tm=512, tn=512):
    m_dim, k_dim = (a.shape[1], a.shape[0]) if ta else a.shape
    n_dim = b.shape[0] if tb else b.shape[1]
    assert (b.shape[1] if tb else b.shape[0]) == k_dim
    tm, tn = min(tm, m_dim), min(tn, n_dim)
    assert m_dim % tm == 0 and n_dim % tn == 0, (name, m_dim, n_dim, tm, tn)
    dims = (((0 if ta else 1,), (1 if tb else 0,)), ((), ()))
    nx = len(tiles) + len(vecs)

    def body(a_ref, b_ref, *rest):
        acc = lax.dot_general(a_ref[...], b_ref[...], dims, preferred_element_type=F32)
        outs = epi(acc, *[r[...] for r in rest[:nx]]) if epi is not None else (acc,)
        for o_ref, o in zip(rest[nx:], outs):
            o_ref[...] = o.astype(o_ref.dtype)

    a_spec = pl.BlockSpec((k_dim, tm), lambda i, j: (0, i)) if ta else pl.BlockSpec((tm, k_dim), lambda i, j: (i, 0))
    b_spec = pl.BlockSpec((tn, k_dim), lambda i, j: (j, 0)) if tb else pl.BlockSpec((k_dim, tn), lambda i, j: (0, j))
    t_spec = pl.BlockSpec((tm, tn), lambda i, j: (i, j))
    v_spec = pl.BlockSpec((1, tn), lambda i, j: (0, j))
    res = pl.pallas_call(
        body, name=name, grid=(m_dim // tm, n_dim // tn),
        in_specs=[a_spec, b_spec] + [t_spec] * len(tiles) + [v_spec] * len(vecs),
        out_specs=[t_spec] * len(out_dtypes),
        out_shape=[jax.ShapeDtypeStruct((m_dim, n_dim), dt) for dt in out_dtypes],
        compiler_params=_cparams(("parallel", "parallel")),
    )(a, b, *tiles, *vecs)
    return res


def _exchange(name, arrays, scatter):
    n = len(arrays)
    out_shape = [jax.ShapeDtypeStruct(((N_DEV,) + a.shape[1:]) if scatter else ((N_DEV,) + a.shape), a.dtype)
                 for a in arrays]

    def body(*refs):
        in_refs, out_refs = refs[:n], refs[n:2 * n]
        send_sems, recv_sems, local_sems = refs[2 * n:]
        x, y, c = lax.axis_index("x"), lax.axis_index("y"), lax.axis_index("c")
        me = 4 * x + 2 * y + c
        copies = []
        for a in range(n):
            src_mine = in_refs[a].at[me] if scatter else in_refs[a]
            local = pltpu.make_async_copy(src_mine, out_refs[a].at[me], local_sems.at[a])
            local.start()
            copies.append(local)
        remote = []
        for k in range(1, N_DEV):
            px, py, pc = x ^ ((k >> 2) & 1), y ^ ((k >> 1) & 1), c ^ (k & 1)
            peer = 4 * px + 2 * py + pc
            for a in range(n):
                src = in_refs[a].at[peer] if scatter else in_refs[a]
                cp = pltpu.make_async_remote_copy(
                    src_ref=src, dst_ref=out_refs[a].at[me],
                    send_sem=send_sems.at[a * (N_DEV - 1) + k - 1],
                    recv_sem=recv_sems.at[a * (N_DEV - 1) + k - 1],
                    device_id=(px, py, pc), device_id_type=pl.DeviceIdType.MESH)
                cp.start()
                arrive = pltpu.make_async_remote_copy(
                    src_ref=src, dst_ref=out_refs[a].at[peer],
                    send_sem=send_sems.at[a * (N_DEV - 1) + k - 1],
                    recv_sem=recv_sems.at[a * (N_DEV - 1) + k - 1],
                    device_id=(px, py, pc), device_id_type=pl.DeviceIdType.MESH)
                remote.append((cp, arrive))
        for cp, arrive in remote:
            cp.wait_send()
            arrive.wait_recv()
        for local in copies:
            local.wait()

    any_spec = pl.BlockSpec(memory_space=pl.ANY)
    return pl.pallas_call(
        body, name=name,
        in_specs=[any_spec] * n, out_specs=[any_spec] * n, out_shape=out_shape,
        scratch_shapes=[pltpu.SemaphoreType.DMA((n * (N_DEV - 1),)),
                        pltpu.SemaphoreType.DMA((n * (N_DEV - 1),)),
                        pltpu.SemaphoreType.DMA((n,))],
        compiler_params=pltpu.CompilerParams(has_side_effects=True),
    )(*arrays)


def _norm_mod(x, g, sc, sh):
    r = lax.rsqrt(jnp.mean(x * x, axis=-1, keepdims=True) + EPS)
    return (x * r * g) * (1.0 + sc) + sh


def _first_norm(x, g, sc, sh):
    (h,), _ = _rowwise("first_norm", lambda x, g, sc, sh: ((_norm_mod(x, g, sc, sh),), ()),
                       [x], [g, sc, sh], [(x.shape[1], MXU_DTYPE)], [], 256)
    return h


def _res_norm(x, y, gate, g, sc, sh):
    def fn(x, y, gate, g, sc, sh):
        xn = x + gate * y
        return (xn, _norm_mod(xn, g, sc, sh)), ()
    (xn, h), _ = _rowwise("res_norm", fn, [x, y], [gate, g, sc, sh],
                          [(x.shape[1], F32), (x.shape[1], MXU_DTYPE)], [], 256)
    return xn, h


def _final_loss(x, y, gate, target):
    d = x.shape[1]

    def fn(x, y, target, gate):
        err = (x + gate * y) - target
        part = jnp.sum(jnp.sum(err * err, axis=-1, keepdims=True), axis=0, keepdims=True) * (0.5 / d)
        dx = err * (1.0 / d)
        return (dx, dx * gate), (jnp.broadcast_to(part, (1, LANES)), _colsum(dx * y))
    (dx, dy), (loss, dgate) = _rowwise("final_loss", fn, [x, y, target], [gate],
                                       [(d, F32), (d, MXU_DTYPE)], [(1, LANES), (1, d)], 256)
    return loss[0, 0], dx, dy, dgate


def _norm_bwd_core(dxo, dh, x, g, sc):
    r = lax.rsqrt(jnp.mean(x * x, axis=-1, keepdims=True) + EPS)
    xn = x * r
    dsh = _colsum(dh)
    dsc = _colsum(dh * (xn * g))
    dyy = dh * (1.0 + sc)
    dg = _colsum(dyy * xn)
    dxn = dyy * g
    dxi = dxo + r * (dxn - xn * jnp.mean(dxn * xn, axis=-1, keepdims=True))
    return dxi, dsh, dsc, dg


def _bwd_norm_gate(dxo, dh, x, y_prev, g, sc, gate_prev):
    d = x.shape[1]

    def fn(dxo, dh, x, y_prev, g, sc, gate_prev):
        dxi, dsh, dsc, dg = _norm_bwd_core(dxo, dh, x, g, sc)
        return (dxi, dxi * gate_prev), (dsh, dsc, dg, _colsum(dxi * y_prev))
    (dxi, dy), reds = _rowwise("bwd_norm_gate", fn, [dxo, dh, x, y_prev], [g, sc, gate_prev],
                               [(d, F32), (d, MXU_DTYPE)], [(1, d)] * 4, 256)
    return dxi, dy, reds


def _bwd_norm_first(dxo, dh, x, g, sc):
    d = x.shape[1]

    def fn(dxo, dh, x, g, sc):
        dxi, dsh, dsc, dg = _norm_bwd_core(dxo, dh, x, g, sc)
        return (dxi,), (dsh, dsc, dg)
    (dxi,), reds = _rowwise("bwd_norm_first", fn, [dxo, dh, x], [g, sc], [(d, F32)], [(1, d)] * 3, 256)
    return dxi, reds


def _mlp_fwd(h, w1, w2):
    def epi(acc):
        r = jnp.maximum(acc, 0.0)
        return acc, r * r
    a, z = _mm("mlp_in", h, w1, out_dtypes=(MXU_DTYPE, MXU_DTYPE), epi=epi, tm=1024, tn=512)
    (out,) = _mm("mlp_out", z, w2, tm=512, tn=512)
    return out, (a, z)


def _mlp_bwd(dy, h, w1, w2, saved):
    a, z = saved

    def epi(acc, a):
        return (acc * (2.0 * jnp.maximum(a.astype(F32), 0.0)),)
    (da,) = _mm("mlp_dz", dy, w2, tb=True, out_dtypes=(MXU_DTYPE,), epi=epi, tiles=(a,), tm=1024, tn=512)
    (dw2,) = _mm("mlp_dw2", z, dy, ta=True, out_dtypes=(MXU_DTYPE,))
    (dw1,) = _mm("mlp_dw1", h, da, ta=True, out_dtypes=(MXU_DTYPE,))
    (dh,) = _mm("mlp_dh", da, w1, tb=True)
    return dh, dw1, dw2


def _split3(v):
    hi = v.astype(jnp.bfloat16)
    r1 = v - hi.astype(F32)
    mid = r1.astype(jnp.bfloat16)
    lo = (r1 - mid.astype(F32)).astype(jnp.bfloat16)
    return hi, mid, lo


def _tri_matmul(tri, v):
    hi, mid, lo = _split3(v)
    dot = functools.partial(jnp.dot, preferred_element_type=F32)
    return dot(tri, hi) + dot(tri, mid) + dot(tri, lo)


def _log_sigmoid(v):
    return jnp.minimum(v, 0.0) - jnp.log(1.0 + jnp.exp(-jnp.abs(v)))


def _gate_fwd(proj, b_pad, col_block):
    s = proj.shape[0]
    tb = min(SCAN_BLOCK, s)
    nblk = s // tb

    def body(f_ref, b_ref, o_ref):
        row = lax.broadcasted_iota(jnp.int32, (tb, tb), 0)
        col = lax.broadcasted_iota(jnp.int32, (tb, tb), 1)
        tri = (col <= row).astype(jnp.bfloat16)

        def step(i, carry):
            rows = pl.ds(pl.multiple_of(i * tb, tb), tb)
            lf = _log_sigmoid(f_ref[rows, :] + b_ref[...])
            f = _tri_matmul(tri, lf) + carry
            o_ref[rows, :] = f
            return f[tb - 1:tb, :]
        lax.fori_loop(0, nblk, step, jnp.zeros((1, LANES), F32))

    return pl.pallas_call(
        body, name="gate_fwd", grid=(1,),
        in_specs=[pl.BlockSpec((s, LANES), lambda i: (0, col_block)), pl.BlockSpec((1, LANES), lambda i: (0, 0))],
        out_specs=pl.BlockSpec((s, LANES), lambda i: (0, 0)),
        out_shape=jax.ShapeDtypeStruct((s, LANES), F32),
        compiler_params=_cparams(("arbitrary",)),
    )(proj, b_pad)


def _gate_bwd(proj, b_pad, d_f, col_block):
    s = proj.shape[0]
    tb = min(SCAN_BLOCK, s)
    nblk = s // tb

    def body(f_ref, b_ref, d_ref, o_ref, db_ref):
        row = lax.broadcasted_iota(jnp.int32, (tb, tb), 0)
        col = lax.broadcasted_iota(jnp.int32, (tb, tb), 1)
        tri = (col >= row).astype(jnp.bfloat16)

        def step(j, carry):
            acc, db = carry
            i = nblk - 1 - j
            rows = pl.ds(pl.multiple_of(i * tb, tb), tb)
            dlf = _tri_matmul(tri, d_ref[rows, :]) + acc
            dpre = dlf * _sigmoid(-(f_ref[rows, :] + b_ref[...]))
            o_ref[rows, :] = dpre.astype(o_ref.dtype)
            return dlf[0:1, :], db + _colsum(dpre)
        _, db = lax.fori_loop(0, nblk, step, (jnp.zeros((1, LANES), F32), jnp.zeros((1, LANES), F32)))
        db_ref[...] = db

    return pl.pallas_call(
        body, name="gate_bwd", grid=(1,),
        in_specs=[pl.BlockSpec((s, LANES), lambda i: (0, col_block)), pl.BlockSpec((1, LANES), lambda i: (0, 0)),
                  pl.BlockSpec((s, LANES), lambda i: (0, 0))],
        out_specs=[pl.BlockSpec((s, LANES), lambda i: (0, 0)), pl.BlockSpec((1, LANES), lambda i: (0, 0))],
        out_shape=[jax.ShapeDtypeStruct((s, LANES), MXU_DTYPE), jax.ShapeDtypeStruct((1, LANES), F32)],
        compiler_params=_cparams(("arbitrary",)),
    )(proj, b_pad, d_f)


def _head_masks():
    lane = lax.broadcasted_iota(jnp.int32, (1, LANES), 1)
    return lane < HEAD_DIM


def _pair_norm(v, g, first):
    v2 = v * v
    ss0 = jnp.sum(jnp.where(first, v2, 0.0), axis=-1, keepdims=True)
    ss1 = jnp.sum(jnp.where(first, 0.0, v2), axis=-1, keepdims=True)
    r = jnp.where(first, lax.rsqrt(ss0 * (1.0 / HEAD_DIM) + EPS), lax.rsqrt(ss1 * (1.0 / HEAD_DIM) + EPS))
    vn = v * r
    return vn * g, vn, r


_NT = (((1,), (1,)), ((), ()))
_TN = (((0,), (0,)), ((), ()))

ATT_TQ = 256
ATT_TK = 256
AUG_F, AUG_ONE = 0, 3


def _own_lanes(hd):
    lane = lax.broadcasted_iota(jnp.int32, (1, LANES), 1)
    return (lane < HEAD_DIM) if hd == 0 else (lane >= HEAD_DIM)


def _aug_lanes(hd, f_other):
    lane = lax.broadcasted_iota(jnp.int32, (1, LANES), 1) - (HEAD_DIM if hd == 0 else 0)
    hi, mid, lo = [t.astype(F32) for t in _split3(f_other)]
    zero = jnp.zeros_like(f_other)
    f_terms = jnp.where(lane == 0, hi, jnp.where(lane == 1, mid, jnp.where(lane == 2, lo, zero)))
    f_shift = jnp.where(lane == 3, hi, jnp.where(lane == 4, mid, jnp.where(lane == 5, lo, zero)))
    ones_lo = jnp.where(lane < 3, 1.0, 0.0) * jnp.where(lane >= 0, 1.0, 0.0)
    ones_hi = jnp.where(lane < 6, 1.0, 0.0) * jnp.where(lane >= 3, 1.0, 0.0)
    return f_terms + ones_hi, ones_lo - f_shift


def _attn_operands(q_raw, k_raw, f_rep, qg, kg, scale):
    first = _head_masks()
    qn, _, _ = _pair_norm(q_raw, qg, first)
    kn, _, _ = _pair_norm(k_raw, kg, first)
    f_other = pltpu.roll(f_rep, HEAD_DIM, 1)
    out = []
    for hd in range(2):
        own = _own_lanes(hd)
        q_x, k_x = _aug_lanes(hd, f_other)
        out.append((jnp.where(own, qn * scale, q_x), jnp.where(own, kn, k_x)))
    return out


def _causal_t(tk, tq, off):
    r = lax.broadcasted_iota(jnp.int32, (tk, tq), 0)
    c = lax.broadcasted_iota(jnp.int32, (tk, tq), 1)
    return (r - c) <= off


def _big(shape, index_map):
    return pl.BlockSpec(shape, index_map, pipeline_mode=pl.Buffered(1))


def _attn_fwd_t(proj, f_rep, qg, kg, d_model):
    s = proj.shape[0]
    pairs = d_model // LANES
    tq, tk = min(ATT_TQ, s), min(ATT_TK, s)
    assert tk % tq == 0 and s % tk == 0
    nq = s // tq
    scale = HEAD_DIM ** -0.5
    ch = tk

    def body(q_ref, k_ref, v_ref, frep_ref, qg_ref, kg_ref, o_ref, lse_ref, qt_s, k_s, vt_s):
        for ci in range(s // ch):
            rows = pl.ds(ci * ch, ch)
            ops = _attn_operands(q_ref[rows, :], k_ref[rows, :], frep_ref[rows, :], qg_ref[...], kg_ref[...], scale)
            vv = v_ref[rows, :]
            for hd in range(2):
                own = _own_lanes(hd)
                lane = lax.broadcasted_iota(jnp.int32, (1, LANES), 1)
                one_lane = lane == (HEAD_DIM if hd == 0 else 0)
                qt_s[hd, :, rows] = ops[hd][0].T.astype(qt_s.dtype)
                k_s[hd, rows, :] = ops[hd][1].astype(k_s.dtype)
                vt_s[hd, :, rows] = jnp.where(own, vv, jnp.where(one_lane, 1.0, 0.0)).T.astype(vt_s.dtype)

        def q_block(qi, _):
            q0 = pl.multiple_of(qi * tq, tq)
            qcols = pl.ds(q0, tq)
            nfull = q0 // tk
            qts = [qt_s[hd, :, qcols] for hd in range(2)]

            def krows(kj):
                return pl.ds(pl.multiple_of(kj * tk, tk), tk)

            def scores(hd, kj):
                return jnp.dot(k_s[hd, krows(kj), :], qts[hd], preferred_element_type=F32)

            def kv_step(kj, carry, last):
                new = []
                for hd in range(2):
                    m, acc, st, p_prev = carry[hd]
                    if not last:
                        st_next = scores(hd, kj + 1)
                    pv = jnp.dot(vt_s[hd, :, krows(jnp.maximum(kj - 1, 0))], p_prev, preferred_element_type=F32)
                    if last:
                        st = jnp.where(_causal_t(tk, tq, q0 - kj * tk), st, NEG)
                    m_new = jnp.maximum(m, jnp.max(st, axis=0, keepdims=True))
                    p = jnp.exp(st - m_new).astype(vt_s.dtype)
                    acc = jnp.exp(m - m_new) * (acc + pv)
                    if last:
                        acc = acc + jnp.dot(vt_s[hd, :, krows(kj)], p, preferred_element_type=F32)
                        new.append((m_new, acc))
                    else:
                        new.append((m_new, acc, st_next, p))
                return tuple(new)

            init = tuple((jnp.full((1, tq), NEG, F32), jnp.zeros((LANES, tq), F32), scores(hd, 0),
                          jnp.zeros((tk, tq), vt_s.dtype)) for hd in range(2))
            carry = lax.fori_loop(0, nfull, lambda kj, cr: kv_step(kj, cr, False), init)
            o_parts, lse_parts = [], []
            for hd, (m, acc) in enumerate(kv_step(nfull, carry, True)):
                e0 = HEAD_DIM if hd == 0 else 0
                l = acc[e0:e0 + 1, :]
                o_parts.append((acc / l).T)
                lse_parts.append(m + jnp.log(l))
            o_ref[pl.ds(q0, tq), :] = jnp.where(_head_masks(), o_parts[0], o_parts[1]).astype(o_ref.dtype)
            lse_ref[0, :, qcols] = jnp.concatenate(lse_parts, axis=0)
            return 0
        lax.fori_loop(0, nq, q_block, 0)

    blk = lambda off: _big((s, LANES), lambda h: (0, off + h))
    vec = pl.BlockSpec((1, LANES), lambda h: (0, 0))
    return pl.pallas_call(
        body, name="attn_fwd", grid=(pairs,),
        in_specs=[blk(0), blk(pairs), blk(2 * pairs), blk(0), vec, vec],
        out_specs=[pl.BlockSpec((s, LANES), lambda h: (0, h)), pl.BlockSpec((1, 2, s), lambda h: (h, 0, 0))],
        out_shape=[jax.ShapeDtypeStruct((s, d_model), MXU_DTYPE), jax.ShapeDtypeStruct((pairs, 2, s), F32)],
        scratch_shapes=[pltpu.VMEM((2, LANES, s), MXU_DTYPE), pltpu.VMEM((2, s, LANES), MXU_DTYPE),
                        pltpu.VMEM((2, LANES, s), MXU_DTYPE)],
        compiler_params=_cparams(("arbitrary",)),
    )(proj, proj, proj, f_rep, qg, kg)


def _attn_bwd_t(proj, do, o, lse, f_rep, qg, kg, d_model):
    s = proj.shape[0]
    pairs = d_model // LANES
    tq, tk = min(ATT_TQ, s), min(ATT_TK, s)
    assert tk % tq == 0 and s % tk == 0
    nq = s // tq
    scale = HEAD_DIM ** -0.5
    ch = tk

    def norm_bwd(raw, g, dn, first):
        _, xn, r = _pair_norm(raw, g, first)
        dxn = dn * g
        t = dxn * xn
        mu0 = jnp.sum(jnp.where(first, t, 0.0), axis=-1, keepdims=True)
        mu1 = jnp.sum(jnp.where(first, 0.0, t), axis=-1, keepdims=True)
        mu = jnp.where(first, mu0, mu1) * (1.0 / HEAD_DIM)
        return r * (dxn - xn * mu), _colsum(dn * xn)

    def body(q_ref, k_ref, v_ref, do_ref, o_ref, lse_ref, frep_ref, qg_ref, kg_ref,
             dq_ref, dk_ref, dv_ref, df_ref, dqg_ref, dkg_ref,
             q_s, qt_s, k_s, kt_s, v_s, do_s, dot_s, dl_s, dk_s, dv_s):
        first = _head_masks()
        hp = pl.program_id(0)
        lane = lax.broadcasted_iota(jnp.int32, (1, LANES), 1)
        for ci in range(s // ch):
            rows = pl.ds(ci * ch, ch)
            ops = _attn_operands(q_ref[rows, :], k_ref[rows, :], frep_ref[rows, :], qg_ref[...], kg_ref[...], scale)
            v_s[rows, :] = v_ref[rows, :].astype(v_s.dtype)
            dov = do_ref[rows, :].astype(F32)
            ot = o_ref[rows, :].astype(F32).T
            for hd in range(2):
                own = _own_lanes(hd)
                q_s[hd, rows, :] = ops[hd][0].astype(q_s.dtype)
                qt_s[hd, :, rows] = ops[hd][0].T.astype(qt_s.dtype)
                k_s[hd, rows, :] = ops[hd][1].astype(k_s.dtype)
                kt_s[hd, :, rows] = ops[hd][1].T.astype(kt_s.dtype)
                doh = jnp.where(own, dov, 0.0)
                do_s[hd, rows, :] = doh.astype(do_s.dtype)
                doht = doh.T
                dot_s[hd, :, rows] = doht.astype(dot_s.dtype)
                dl_s[hd:hd + 1, rows] = jnp.sum(doht * ot, axis=0, keepdims=True)
        dk_s[...] = jnp.zeros_like(dk_s)
        dv_s[...] = jnp.zeros_like(dv_s)

        @pl.when(hp == 0)
        def _():
            df_ref[...] = jnp.zeros_like(df_ref)

        def q_block(qi, dqg):
            q0 = pl.multiple_of(qi * tq, tq)
            qcols = pl.ds(q0, tq)
            qrows = pl.ds(q0, tq)
            nfull = q0 // tk
            qts = [qt_s[hd, :, qcols] for hd in range(2)]
            dots = [dot_s[hd, :, qcols] for hd in range(2)]
            qns = [q_s[hd, qrows, :] for hd in range(2)]
            dons = [do_s[hd, qrows, :] for hd in range(2)]
            lse_r = [lse_ref[0, hd:hd + 1, qcols] for hd in range(2)]
            dl_r = [dl_s[hd:hd + 1, qcols] for hd in range(2)]
            bdt = qt_s.dtype

            def krows(kj):
                return pl.ds(pl.multiple_of(kj * tk, tk), tk)

            def scores(hd, kj):
                return (jnp.dot(k_s[hd, krows(kj), :], qts[hd], preferred_element_type=F32),
                        jnp.dot(v_s[krows(kj), :], dots[hd], preferred_element_type=F32))

            def products(hd, rows, ds, p, dqt):
                dk_s[hd, rows, :] += jnp.dot(ds, qns[hd], preferred_element_type=F32)
                dv_s[rows, :] += jnp.dot(p, dons[hd], preferred_element_type=F32)
                return dqt + jnp.dot(kt_s[hd, :, rows], ds, preferred_element_type=F32)

            def kv_step(kj, carry, last):
                new = []
                for hd in range(2):
                    dqt, ds_prev, p_prev = carry[hd]
                    st, dp = scores(hd, kj)
                    dqt = products(hd, krows(jnp.maximum(kj - 1, 0)), ds_prev, p_prev, dqt)
                    if last:
                        st = jnp.where(_causal_t(tk, tq, q0 - kj * tk), st, NEG)
                    p = jnp.exp(st - lse_r[hd])
                    ds = (p * (dp - dl_r[hd])).astype(bdt)
                    if last:
                        new.append(products(hd, krows(kj), ds, p.astype(bdt), dqt))
                    else:
                        new.append((dqt, ds, p.astype(bdt)))
                return tuple(new)

            init = tuple((jnp.zeros((LANES, tq), F32), jnp.zeros((tk, tq), bdt), jnp.zeros((tk, tq), bdt))
                         for hd in range(2))
            carry = lax.fori_loop(0, nfull, lambda kj, cr: kv_step(kj, cr, False), init)
            dq_parts = [dqt.T for dqt in kv_step(nfull, carry, True)]
            rs0 = dq_parts[0][:, HEAD_DIM + AUG_F:HEAD_DIM + AUG_F + 1]
            rs1 = dq_parts[1][:, AUG_F:AUG_F + 1]
            df_ref[qrows, :] += jnp.where(lane == 2 * hp, rs0, 0.0) + jnp.where(lane == 2 * hp + 1, rs1, 0.0)
            dqn = jnp.where(first, dq_parts[0], dq_parts[1]) * scale
            dq_raw, dg = norm_bwd(q_ref[qrows, :], qg_ref[...], dqn, first)
            dq_ref[qrows, :] = dq_raw.astype(dq_ref.dtype)
            return dqg + dg
        dqg_ref[0] = lax.fori_loop(0, nq, q_block, jnp.zeros((1, LANES), F32))

        dkg = jnp.zeros((1, LANES), F32)
        for ci in range(s // ch):
            rows = pl.ds(ci * ch, ch)
            dk0, dk1 = dk_s[0, rows, :], dk_s[1, rows, :]
            cs0 = dk0[:, HEAD_DIM + AUG_ONE:HEAD_DIM + AUG_ONE + 1]
            cs1 = dk1[:, AUG_ONE:AUG_ONE + 1]
            df_ref[rows, :] -= jnp.where(lane == 2 * hp, cs0, 0.0) + jnp.where(lane == 2 * hp + 1, cs1, 0.0)
            dk_raw, dg = norm_bwd(k_ref[rows, :], kg_ref[...], jnp.where(first, dk0, dk1), first)
            dk_ref[rows, :] = dk_raw.astype(dk_ref.dtype)
            dkg = dkg + dg
            dv_ref[rows, :] = dv_s[rows, :].astype(dv_ref.dtype)
        dkg_ref[0] = dkg

    blk = lambda off: _big((s, LANES), lambda h: (0, off + h))
    outb = pl.BlockSpec((s, LANES), lambda h: (0, h))
    vec = pl.BlockSpec((1, LANES), lambda h: (0, 0))
    gout = pl.BlockSpec((1, 1, LANES), lambda h: (h, 0, 0))
    act = jax.ShapeDtypeStruct((s, d_model), MXU_DTYPE)
    gsh = jax.ShapeDtypeStruct((pairs, 1, LANES), F32)
    pair_rows = pltpu.VMEM((2, s, LANES), MXU_DTYPE)
    pair_cols = pltpu.VMEM((2, LANES, s), MXU_DTYPE)
    return pl.pallas_call(
        body, name="attn_bwd", grid=(pairs,),
        in_specs=[blk(0), blk(pairs), blk(2 * pairs), blk(0), blk(0),
                  pl.BlockSpec((1, 2, s), lambda h: (h, 0, 0)), blk(0), vec, vec],
        out_specs=[outb, outb, outb, pl.BlockSpec((s, LANES), lambda h: (0, 0)), gout, gout],
        out_shape=[act, act, act, jax.ShapeDtypeStruct((s, LANES), F32), gsh, gsh],
        scratch_shapes=[pair_rows, pair_cols, pair_rows, pair_cols, pltpu.VMEM((s, LANES), MXU_DTYPE),
                        pair_rows, pair_cols, pltpu.VMEM((8, s), F32),
                        pltpu.VMEM((2, s, LANES), F32), pltpu.VMEM((s, LANES), F32)],
        compiler_params=_cparams(("arbitrary",), ATT_BWD_VMEM_LIMIT),
    )(proj, proj, proj, do, o, lse, f_rep, qg, kg)


def _attn_fwd(proj, f_rep, f_rows, qg, kg, d_model):
    s = proj.shape[0]
    pairs = d_model // LANES
    tq = min(ATT_BLOCK, s)
    nq = s // tq
    scale = HEAD_DIM ** -0.5

    def body(q_ref, k_ref, v_ref, frep_ref, frow_ref, qg_ref, kg_ref, o_ref, lse_ref, q0_s, q1_s, kn_s, v_s):
        first = _head_masks()
        for ci in range(nq):
            rows = pl.ds(ci * tq, tq)
            qn, _, _ = _pair_norm(q_ref[rows, :], qg_ref[...], first)
            qn = qn * scale
            q0_s[rows, :] = jnp.where(first, qn, 0.0).astype(q0_s.dtype)
            q1_s[rows, :] = jnp.where(first, 0.0, qn).astype(q1_s.dtype)
            kn, _, _ = _pair_norm(k_ref[rows, :], kg_ref[...], first)
            kn_s[rows, :] = kn.astype(kn_s.dtype)
            v_s[rows, :] = v_ref[rows, :].astype(v_s.dtype)

        rid = lax.broadcasted_iota(jnp.int32, (tq, tq), 0)
        cid = lax.broadcasted_iota(jnp.int32, (tq, tq), 1)
        causal = cid <= rid

        def q_block(qi, _):
            qrows = pl.ds(pl.multiple_of(qi * tq, tq), tq)
            qs = (q0_s[qrows, :], q1_s[qrows, :])
            frep = frep_ref[qrows, :]
            fq = (frep[:, 0:1], frep[:, HEAD_DIM:HEAD_DIM + 1])

            def kv_step(kj, carry, diag):
                krows = pl.ds(pl.multiple_of(kj * tq, tq), tq)
                k = kn_s[krows, :]
                v = v_s[krows, :]
                fk = frow_ref[0, :, krows]
                new = []
                for hd in range(2):
                    m, l, acc = carry[hd]
                    sc = lax.dot_general(qs[hd], k, _NT, preferred_element_type=F32) + (fq[hd] - fk[hd:hd + 1, :])
                    if diag:
                        sc = jnp.where(causal, sc, NEG)
                    m_new = jnp.maximum(m, jnp.max(sc, axis=-1, keepdims=True))
                    p = jnp.exp(sc - m_new)
                    alpha = jnp.exp(m - m_new)
                    l = alpha * l + jnp.sum(p, axis=-1, keepdims=True)
                    acc = alpha * acc + jnp.dot(p.astype(v.dtype), v, preferred_element_type=F32)
                    new.append((m_new, l, acc))
                return tuple(new)

            init = tuple((jnp.full((tq, 1), NEG, F32), jnp.zeros((tq, 1), F32), jnp.zeros((tq, LANES), F32))
                         for _ in range(2))
            carry = lax.fori_loop(0, qi, lambda kj, cr: kv_step(kj, cr, False), init)
            (m0, l0, a0), (m1, l1, a1) = kv_step(qi, carry, True)
            o_ref[qrows, :] = jnp.where(first, a0 / l0, a1 / l1).astype(o_ref.dtype)
            lse_ref[qrows, :] = jnp.where(first, m0 + jnp.log(l0), m1 + jnp.log(l1))
            return 0
        lax.fori_loop(0, nq, q_block, 0)

    blk = lambda off: pl.BlockSpec((s, LANES), lambda h: (0, off + h))
    vec = pl.BlockSpec((1, LANES), lambda h: (0, 0))
    return pl.pallas_call(
        body, name="attn_fwd", grid=(pairs,),
        in_specs=[blk(0), blk(pairs), blk(2 * pairs), blk(0),
                  pl.BlockSpec((1, 2, s), lambda h: (h, 0, 0)), vec, vec],
        out_specs=[blk(0), blk(0)],
        out_shape=[jax.ShapeDtypeStruct((s, d_model), MXU_DTYPE), jax.ShapeDtypeStruct((s, d_model), F32)],
        scratch_shapes=[pltpu.VMEM((s, LANES), MXU_DTYPE)] * 4,
        compiler_params=_cparams(("arbitrary",)),
    )(proj, proj, proj, f_rep, f_rows, qg, kg)


def _attn_bwd(proj, do, o, lse, f_rep, f_rows, qg, kg, d_model):
    s = proj.shape[0]
    pairs = d_model // LANES
    tq = min(ATT_BLOCK, s)
    nq = s // tq
    scale = HEAD_DIM ** -0.5

    def norm_bwd(raw, g, dn, first):
        _, xn, r = _pair_norm(raw, g, first)
        dxn = dn * g
        t = dxn * xn
        mu0 = jnp.sum(jnp.where(first, t, 0.0), axis=-1, keepdims=True)
        mu1 = jnp.sum(jnp.where(first, 0.0, t), axis=-1, keepdims=True)
        mu = jnp.where(first, mu0, mu1) * (1.0 / HEAD_DIM)
        return r * (dxn - xn * mu), _colsum(dn * xn)

    def body(q_ref, k_ref, v_ref, do_ref, o_ref, lse_ref, frep_ref, frow_ref, qg_ref, kg_ref,
             dq_ref, dk_ref, dv_ref, dfq_ref, dfk_ref, dqg_ref, dkg_ref,
             q0_s, q1_s, kn_s, v_s, do0_s, do1_s, dk_s, dv_s):
        first = _head_masks()
        for ci in range(nq):
            rows = pl.ds(ci * tq, tq)
            qn, _, _ = _pair_norm(q_ref[rows, :], qg_ref[...], first)
            qn = qn * scale
            q0_s[rows, :] = jnp.where(first, qn, 0.0).astype(q0_s.dtype)
            q1_s[rows, :] = jnp.where(first, 0.0, qn).astype(q1_s.dtype)
            kn, _, _ = _pair_norm(k_ref[rows, :], kg_ref[...], first)
            kn_s[rows, :] = kn.astype(kn_s.dtype)
            v_s[rows, :] = v_ref[rows, :].astype(v_s.dtype)
            dov = do_ref[rows, :]
            do0_s[rows, :] = jnp.where(first, dov, jnp.zeros_like(dov))
            do1_s[rows, :] = jnp.where(first, jnp.zeros_like(dov), dov)
        dk_s[...] = jnp.zeros_like(dk_s)
        dv_s[...] = jnp.zeros_like(dv_s)
        dfk_ref[...] = jnp.zeros_like(dfk_ref)

        rid = lax.broadcasted_iota(jnp.int32, (tq, tq), 0)
        cid = lax.broadcasted_iota(jnp.int32, (tq, tq), 1)
        causal = cid <= rid

        def q_block(qi, dqg):
            qrows = pl.ds(pl.multiple_of(qi * tq, tq), tq)
            qs = (q0_s[qrows, :], q1_s[qrows, :])
            dos = (do0_s[qrows, :], do1_s[qrows, :])
            frep = frep_ref[qrows, :]
            lse = lse_ref[qrows, :]
            ov = o_ref[qrows, :].astype(F32)
            fq = (frep[:, 0:1], frep[:, HEAD_DIM:HEAD_DIM + 1])
            ls = (lse[:, 0:1], lse[:, HEAD_DIM:HEAD_DIM + 1])
            dls = tuple(jnp.sum(dos[hd].astype(F32) * ov, axis=-1, keepdims=True) for hd in range(2))

            def kv_step(kj, carry, diag):
                dq, rs = carry[0], list(carry[1:])
                krows = pl.ds(pl.multiple_of(kj * tq, tq), tq)
                k = kn_s[krows, :]
                v = v_s[krows, :]
                fk = frow_ref[0, :, krows]
                dk_add = jnp.zeros((tq, LANES), F32)
                dv_add = jnp.zeros((tq, LANES), F32)
                dfk = []
                for hd in range(2):
                    sc = lax.dot_general(qs[hd], k, _NT, preferred_element_type=F32) + (fq[hd] - fk[hd:hd + 1, :])
                    if diag:
                        sc = jnp.where(causal, sc, NEG)
                    p = jnp.exp(sc - ls[hd])
                    dp = lax.dot_general(dos[hd], v, _NT, preferred_element_type=F32)
                    ds = p * (dp - dls[hd])
                    dsb = ds.astype(k.dtype)
                    dqh = jnp.dot(dsb, k, preferred_element_type=F32)
                    dq = dq + (jnp.where(first, dqh, 0.0) if hd == 0 else jnp.where(first, 0.0, dqh))
                    dk_add = dk_add + lax.dot_general(dsb, qs[hd], _TN, preferred_element_type=F32)
                    dv_add = dv_add + lax.dot_general(p.astype(k.dtype), dos[hd], _TN, preferred_element_type=F32)
                    dfk.append(_colsum(ds))
                    rs[hd] = rs[hd] + jnp.sum(ds, axis=-1, keepdims=True)
                dk_s[krows, :] += dk_add
                dv_s[krows, :] += dv_add
                dfk_ref[0, :, krows] -= jnp.concatenate(dfk, axis=0)
                return (dq, rs[0], rs[1])

            init = (jnp.zeros((tq, LANES), F32), jnp.zeros((tq, 1), F32), jnp.zeros((tq, 1), F32))
            carry = lax.fori_loop(0, qi, lambda kj, cr: kv_step(kj, cr, False), init)
            dq, rs0, rs1 = kv_step(qi, carry, True)
            dfq_ref[qrows, :] = jnp.where(first, rs0, rs1)
            dq_raw, dg = norm_bwd(q_ref[qrows, :], qg_ref[...], dq * scale, first)
            dq_ref[qrows, :] = dq_raw.astype(dq_ref.dtype)
            return dqg + dg
        dqg_ref[0] = lax.fori_loop(0, nq, q_block, jnp.zeros((1, LANES), F32))

        dkg = jnp.zeros((1, LANES), F32)
        for ci in range(nq):
            rows = pl.ds(ci * tq, tq)
            dk_raw, dg = norm_bwd(k_ref[rows, :], kg_ref[...], dk_s[rows, :], first)
            dk_ref[rows, :] = dk_raw.astype(dk_ref.dtype)
            dkg = dkg + dg
            dv_ref[rows, :] = dv_s[rows, :].astype(dv_ref.dtype)
        dkg_ref[0] = dkg

    blk = lambda off: pl.BlockSpec((s, LANES), lambda h: (0, off + h))
    vec = pl.BlockSpec((1, LANES), lambda h: (0, 0))
    frow = pl.BlockSpec((1, 2, s), lambda h: (h, 0, 0))
    gout = pl.BlockSpec((1, 1, LANES), lambda h: (h, 0, 0))
    act = jax.ShapeDtypeStruct((s, d_model), MXU_DTYPE)
    gsh = jax.ShapeDtypeStruct((pairs, 1, LANES), F32)
    return pl.pallas_call(
        body, name="attn_bwd", grid=(pairs,),
        in_specs=[blk(0), blk(pairs), blk(2 * pairs), blk(0), blk(0), blk(0), blk(0), frow, vec, vec],
        out_specs=[blk(0), blk(0), blk(0), blk(0), frow, gout, gout],
        out_shape=[act, act, act, jax.ShapeDtypeStruct((s, d_model), F32),
                   jax.ShapeDtypeStruct((pairs, 2, s), F32), gsh, gsh],
        scratch_shapes=[pltpu.VMEM((s, LANES), MXU_DTYPE)] * 6 + [pltpu.VMEM((s, LANES), F32)] * 2,
        compiler_params=_cparams(("arbitrary",), ATT_BWD_VMEM_LIMIT),
    )(proj, proj, proj, do, o, lse, f_rep, f_rows, qg, kg)


def _pad_cols(a, n):
    return jnp.pad(a, ((0, 0), (0, n - a.shape[1])))


def _fox_fwd(h, w):
    d = h.shape[1]
    pairs = d // LANES
    (proj,) = _mm("fox_in", h, w["w_in"], tm=1024, tn=640)
    f_cum = _gate_fwd(proj, w["b_f"], 3 * pairs)
    f16 = f_cum[:, :d // HEAD_DIM]
    f_rep = jnp.repeat(f16, HEAD_DIM, axis=1)
    o, lse = _attn_fwd_t(proj, f_rep, w["qg"], w["kg"], d)
    (y,) = _mm("fox_out", o, w["w_out"])
    return y, (proj, f_rep, o, lse)


def _fox_bwd(dy, h, w, saved):
    proj, f_rep, o, lse = saved
    s, d = h.shape
    pairs = d // LANES
    (do,) = _mm("fox_do", dy, w["w_out"], tb=True, out_dtypes=(MXU_DTYPE,))
    (dw_out,) = _mm("fox_dwout", o, dy, ta=True, out_dtypes=(MXU_DTYPE,))
    dq, dk, dv, d_f, dqg, dkg = _attn_bwd_t(proj, do, o, lse, f_rep, w["qg"], w["kg"], d)
    dfpre, db_f = _gate_bwd(proj, w["b_f"], d_f, 3 * pairs)
    dproj = jnp.concatenate([dq, dk, dv, dfpre], axis=1)
    (dw_in,) = _mm("fox_dwin", h, dproj, ta=True, out_dtypes=(MXU_DTYPE,), tn=640)
    (dh,) = _mm("fox_dh", dproj, w["w_in"], tb=True)
    fold = lambda g: jnp.sum(g, axis=(0, 1)).reshape(2, HEAD_DIM).sum(axis=0)
    return dh, dict(w_in=dw_in, w_out=dw_out, b_f=db_f[0, :d // HEAD_DIM], qg=fold(dqg), kg=fold(dkg))


_GELU_C = math.sqrt(2.0 / math.pi)


def _gelu(v):
    return 0.5 * v * (1.0 + jnp.tanh(_GELU_C * (v + 0.044715 * v * v * v)))


def _gelu_grad(v):
    t = jnp.tanh(_GELU_C * (v + 0.044715 * v * v * v))
    return 0.5 * (1.0 + t) + 0.5 * v * (1.0 - t * t) * (_GELU_C * (1.0 + 3.0 * 0.044715 * v * v))


def _ln_stats(v):
    mu = jnp.mean(v, axis=-1, keepdims=True)
    vc = v - mu
    r = lax.rsqrt(jnp.mean(vc * vc, axis=-1, keepdims=True) + EPS)
    return vc * r, r


def _sg_mask():
    t = lax.broadcasted_iota(jnp.int32, (SG_CHUNK, SG_CHUNK), 0) // SG_BLOCK
    sidx = lax.broadcasted_iota(jnp.int32, (SG_CHUNK, SG_CHUNK), 1) // SG_BLOCK
    return sidx <= t


def _sgu_fwd(uv_pre, ln_g, ln_b, w_s, b_st):
    s, w2 = uv_pre.shape
    wd = w2 // 2
    tr = min(256, s)

    def fn(uv_pre, ln_g, ln_b, w_s, b_st):
        uv = _gelu(uv_pre)
        u = uv[:, :wd]
        vh, _ = _ln_stats(uv[:, wd:])
        vl = (vh * ln_g + ln_b).astype(MXU_DTYPE)
        mask = _sg_mask()
        cols = []
        for g in range(SG_GROUPS):
            wg = jnp.where(mask, w_s[g * SG_CHUNK:(g + 1) * SG_CHUNK, :], 0.0).astype(MXU_DTYPE)
            parts = []
            for ci in range(tr // SG_CHUNK):
                vt = vl[ci * SG_CHUNK:(ci + 1) * SG_CHUNK, g * SG_CHUNK:(g + 1) * SG_CHUNK]
                parts.append(jnp.dot(wg, vt, preferred_element_type=F32) + b_st[:, g:g + 1])
            cols.append(jnp.concatenate(parts, axis=0) if len(parts) > 1 else parts[0])
        vout = jnp.concatenate(cols, axis=1)
        return (u * vout,), ()
    (m,), _ = _rowwise("sgu_fwd", fn, [uv_pre], [ln_g, ln_b, w_s, b_st], [(wd, MXU_DTYPE)], [], tr)
    return m


def _sgu_bwd(uv_pre, dm, ln_g, ln_b, w_s, b_st):
    s, w2 = uv_pre.shape
    wd = w2 // 2
    tr = min(256, s)

    def fn(uv_pre, dm, ln_g, ln_b, w_s, b_st):
        uv = _gelu(uv_pre)
        u = uv[:, :wd]
        vh, r = _ln_stats(uv[:, wd:])
        vl = (vh * ln_g + ln_b).astype(MXU_DTYPE)
        mask = _sg_mask()
        lane = lax.broadcasted_iota(jnp.int32, (1, LANES), 1)
        cols, dcols, dws, dbs = [], [], [], jnp.zeros((SG_CHUNK, LANES), F32)
        for g in range(SG_GROUPS):
            wg = jnp.where(mask, w_s[g * SG_CHUNK:(g + 1) * SG_CHUNK, :], 0.0).astype(MXU_DTYPE)
            parts, dparts = [], []
            dwg = jnp.zeros((SG_CHUNK, SG_CHUNK), F32)
            dbg = jnp.zeros((SG_CHUNK, 1), F32)
            for ci in range(tr // SG_CHUNK):
                rs = slice(ci * SG_CHUNK, (ci + 1) * SG_CHUNK)
                cs = slice(g * SG_CHUNK, (g + 1) * SG_CHUNK)
                vt = vl[rs, cs]
                parts.append(jnp.dot(wg, vt, preferred_element_type=F32) + b_st[:, g:g + 1])
                dvo = dm[rs, cs] * u[rs, cs]
                dvob = dvo.astype(MXU_DTYPE)
                dparts.append(lax.dot_general(wg, dvob, _TN, preferred_element_type=F32))
                dwg = dwg + lax.dot_general(dvob, vt, _NT, preferred_element_type=F32)
                dbg = dbg + jnp.sum(dvo, axis=-1, keepdims=True)
            cols.append(jnp.concatenate(parts, axis=0) if len(parts) > 1 else parts[0])
            dcols.append(jnp.concatenate(dparts, axis=0) if len(dparts) > 1 else dparts[0])
            dws.append(jnp.where(mask, dwg, 0.0))
            dbs = dbs + jnp.where(lane == g, dbg, 0.0)
        vout = jnp.concatenate(cols, axis=1)
        dvl = jnp.concatenate(dcols, axis=1)
        du = dm * vout
        dlg = _colsum(dvl * vh)
        dlb = _colsum(dvl)
        dvh = dvl * ln_g
        dv = r * (dvh - jnp.mean(dvh, axis=-1, keepdims=True) - vh * jnp.mean(dvh * vh, axis=-1, keepdims=True))
        dpre = jnp.concatenate([du, dv], axis=1) * _gelu_grad(uv_pre)
        return (dpre,), (dlg, dlb, jnp.concatenate(dws, axis=0), dbs)
    (dpre,), reds = _rowwise("sgu_bwd", fn, [uv_pre, dm], [ln_g, ln_b, w_s, b_st], [(w2, MXU_DTYPE)],
                             [(1, wd), (1, wd), (SG_GROUPS * SG_CHUNK, SG_CHUNK), (SG_CHUNK, LANES)], tr)
    return dpre, reds


def _sg_fwd(h, w):
    (uv_pre,) = _mm("sg_in", h, w["w_in"], tm=1024, tn=512)
    m = _sgu_fwd(uv_pre, w["ln_g"], w["ln_b"], w["w_s"], w["b_st"])
    (y,) = _mm("sg_out", m, w["w_out"])
    return y, (uv_pre, m)


def _sg_bwd(dy, h, w, saved):
    uv_pre, m = saved
    (dm,) = _mm("sg_dm", dy, w["w_out"], tb=True)
    (dw_out,) = _mm("sg_dwout", m, dy, ta=True, out_dtypes=(MXU_DTYPE,))
    dpre, (dlg, dlb, dws, dbs) = _sgu_bwd(uv_pre, dm, w["ln_g"], w["ln_b"], w["w_s"], w["b_st"])
    (dw_in,) = _mm("sg_dwin", h, dpre, ta=True, out_dtypes=(MXU_DTYPE,))
    (dh,) = _mm("sg_dh", dpre, w["w_in"], tb=True)
    return dh, dict(w_in=dw_in, w_out=dw_out, ln_g=dlg, ln_b=dlb, w_s=dws, b_s=dbs[:, :SG_GROUPS].T)


def _conv_fwd_kernel(ypad, w_dw, b_dw):
    s = ypad.shape[0] - CONV_PAD
    d = ypad.shape[1]
    tt = min(256, s)
    ext = tt + CONV_PAD

    def body(y_ref, w_ref, b_ref, o_ref):
        def chunk(ci, _):
            base = pl.multiple_of(ci * tt, tt)
            e = y_ref[pl.ds(base, ext), :]
            acc = jnp.zeros((tt, LANES), F32) + b_ref[...]
            for j in range(CONV_WIDTH):
                sh = pltpu.roll(e, ext - (CONV_PAD - CONV_WIDTH + 1 + j), 0)[:tt, :]
                acc = acc + w_ref[j:j + 1, :] * sh
            o_ref[pl.ds(base, tt), :] = acc
            return 0
        lax.fori_loop(0, s // tt, chunk, 0)

    return pl.pallas_call(
        body, name="conv_fwd", grid=(d // LANES,),
        in_specs=[pl.BlockSpec((s + CONV_PAD, LANES), lambda i: (0, i)),
                  pl.BlockSpec((CONV_PAD, LANES), lambda i: (0, i)), pl.BlockSpec((1, LANES), lambda i: (0, i))],
        out_specs=pl.BlockSpec((s, LANES), lambda i: (0, i)),
        out_shape=jax.ShapeDtypeStruct((s, d), F32),
        compiler_params=_cparams(("parallel",)),
    )(ypad, w_dw, b_dw)


def _conv_bwd_kernel(ypad, dpad, w_dw):
    s = ypad.shape[0] - CONV_PAD
    d = ypad.shape[1]
    tt = min(256, s)
    ext = tt + CONV_PAD

    def body(y_ref, d_ref, w_ref, o_ref, dw_ref):
        dw_ref[...] = jnp.zeros_like(dw_ref)

        def chunk(ci, _):
            base = pl.multiple_of(ci * tt, tt)
            ye = y_ref[pl.ds(base, ext), :]
            de = d_ref[pl.ds(base, ext), :]
            dcur = de[:tt, :]
            acc = jnp.zeros((tt, LANES), F32)
            for j in range(CONV_WIDTH):
                back = CONV_WIDTH - 1 - j
                dsh = dcur if back == 0 else pltpu.roll(de, ext - back, 0)[:tt, :]
                acc = acc + w_ref[j:j + 1, :] * dsh
                ysh = pltpu.roll(ye, ext - (CONV_PAD - CONV_WIDTH + 1 + j), 0)[:tt, :]
                dw_ref[j:j + 1, :] += _colsum(dcur * ysh)
            o_ref[pl.ds(base, tt), :] = acc
            return 0
        lax.fori_loop(0, s // tt, chunk, 0)

    return pl.pallas_call(
        body, name="conv_bwd", grid=(d // LANES,),
        in_specs=[pl.BlockSpec((s + CONV_PAD, LANES), lambda i: (0, i)),
                  pl.BlockSpec((s + CONV_PAD, LANES), lambda i: (0, i)),
                  pl.BlockSpec((CONV_PAD, LANES), lambda i: (0, i))],
        out_specs=[pl.BlockSpec((s, LANES), lambda i: (0, i)), pl.BlockSpec((CONV_PAD, LANES), lambda i: (0, i))],
        out_shape=[jax.ShapeDtypeStruct((s, d), F32), jax.ShapeDtypeStruct((CONV_PAD, d), F32)],
        compiler_params=_cparams(("parallel",)),
    )(ypad, dpad, w_dw)


def _cv_fwd(h, w):
    d = h.shape[1]
    (y1,) = _mm("cv_pw1", h, w["w_pw1"], tm=1024, tn=512)

    def glu(y1, b1):
        t = y1 + b1
        return (t[:, :d] * _sigmoid(t[:, d:]),), ()
    (y2,), _ = _rowwise("cv_glu", glu, [y1], [w["b_pw1"]], [(d, F32)], [], 256)
    y3 = _conv_fwd_kernel(jnp.pad(y2, ((CONV_PAD, 0), (0, 0))), w["w_dw"], w["b_dw"])

    def lnsilu(y3, g, b):
        vh, _ = _ln_stats(y3)
        y4 = vh * g + b
        return (y4 * _sigmoid(y4),), ()
    (y5,), _ = _rowwise("cv_lnsilu", lnsilu, [y3], [w["ln_g"], w["ln_b"]], [(d, MXU_DTYPE)], [], 256)
    (y,) = _mm("cv_pw2", y5, w["w_pw2"], epi=lambda acc, b: (acc + b,), vecs=(w["b_pw2"],))
    return y, (y1, y2, y3, y5)


def _cv_bwd(dy, h, w, saved):
    y1, y2, y3, y5 = saved
    d = h.shape[1]
    (dy5,) = _mm("cv_dy5", dy, w["w_pw2"], tb=True)
    (dw_pw2,) = _mm("cv_dwpw2", y5, dy, ta=True, out_dtypes=(MXU_DTYPE,))

    def ln_bwd(dy5, y3, dyb, g, b):
        vh, r = _ln_stats(y3)
        y4 = vh * g + b
        sg = _sigmoid(y4)
        dy4 = dy5 * (sg * (1.0 + y4 * (1.0 - sg)))
        dvh = dy4 * g
        dy3 = r * (dvh - jnp.mean(dvh, axis=-1, keepdims=True) - vh * jnp.mean(dvh * vh, axis=-1, keepdims=True))
        return (dy3,), (_colsum(dy4 * vh), _colsum(dy4), _colsum(dy3), _colsum(dyb.astype(F32)))
    (dy3,), (dlg, dlb, db_dw, db_pw2) = _rowwise("cv_ln_bwd", ln_bwd, [dy5, y3, dy], [w["ln_g"], w["ln_b"]],
                                                 [(d, F32)], [(1, d)] * 4, 256)
    dy2, dw_dw = _conv_bwd_kernel(jnp.pad(y2, ((CONV_PAD, 0), (0, 0))), jnp.pad(dy3, ((0, CONV_PAD), (0, 0))),
                                  w["w_dw"])

    def glu_bwd(y1, dy2, b1):
        t = y1 + b1
        a, sg = t[:, :d], _sigmoid(t[:, d:])
        dy1 = jnp.concatenate([dy2 * sg, dy2 * a * sg * (1.0 - sg)], axis=1)
        return (dy1,), (_colsum(dy1),)
    (dy1,), (db_pw1,) = _rowwise("cv_glu_bwd", glu_bwd, [y1, dy2], [w["b_pw1"]], [(2 * d, MXU_DTYPE)],
                                 [(1, 2 * d)], 256)
    (dw_pw1,) = _mm("cv_dwpw1", h, dy1, ta=True, out_dtypes=(MXU_DTYPE,))
    (dh,) = _mm("cv_dh", dy1, w["w_pw1"], tb=True)
    return dh, dict(w_pw1=dw_pw1, w_pw2=dw_pw2, b_pw1=db_pw1, b_pw2=db_pw2, w_dw=dw_dw[:CONV_WIDTH],
                    b_dw=db_dw, ln_g=dlg, ln_b=dlb)


def _ada_outer(c_t, dmod):
    def fn(c_t, dmod):
        acc = c_t[:, 0:1] * dmod[0:1, :]
        for b in range(1, N_DEV):
            acc = acc + c_t[:, b:b + 1] * dmod[b:b + 1, :]
        return (acc,), ()
    (g,), _ = _rowwise("ada_outer", fn, [c_t], [dmod], [(dmod.shape[1], F32)], [], 256)
    return g


def _adamw(name, parts, w, m, v, tr):
    npart = parts.shape[0]
    cols = w.shape[1]

    def fn(parts, w, m, v):
        g = parts[0].astype(F32)
        for q in range(1, npart):
            g = g + parts[q].astype(F32)
        m_new = ADAM_B1 * m + (1.0 - ADAM_B1) * g
        v_new = ADAM_B2 * v + (1.0 - ADAM_B2) * (g * g)
        m_hat = m_new / (1.0 - ADAM_B1 ** ADAM_STEP)
        v_hat = v_new / (1.0 - ADAM_B2 ** ADAM_STEP)
        delta = -ADAM_LR * (m_hat / (jnp.sqrt(v_hat) + ADAM_EPS) + ADAM_WD * w)
        return (g, delta, m_new, v_new), ()
    rows = w.shape[0]
    tr = min(tr, rows)

    def body(p_ref, w_ref, m_ref, v_ref, g_o, d_o, m_o, v_o):
        outs, _ = fn(p_ref[...], w_ref[...], m_ref[...], v_ref[...])
        for o_ref, o in zip((g_o, d_o, m_o, v_o), outs):
            o_ref[...] = o

    spec = pl.BlockSpec((tr, cols), lambda i: (i, 0))
    return pl.pallas_call(
        body, name=name, grid=(rows // tr,),
        in_specs=[pl.BlockSpec((npart, tr, cols), lambda i: (0, i, 0)), spec, spec, spec],
        out_specs=[spec] * 4, out_shape=[jax.ShapeDtypeStruct((rows, cols), F32)] * 4,
        compiler_params=_cparams(("parallel",)),
    )(parts, w, m, v)


def _pack(arrays):
    flat = jnp.concatenate([a.reshape(-1).astype(F32) for a in arrays])
    n = flat.shape[0]
    rows = -(-n // (8 * LANES)) * 8
    return jnp.pad(flat, (0, rows * LANES - n)).reshape(rows, LANES)


def _unpack(buf, shapes, lead=()):
    flat = buf.reshape(lead + (-1,))
    out, off = [], 0
    for shp in shapes:
        n = math.prod(shp)
        out.append(flat[..., off:off + n].reshape(lead + tuple(shp)))
        off += n
    return out


ADAM_TILE_ELEMS = 1 << 17


def _row_tile(rows, cols):
    want = max(8, ADAM_TILE_ELEMS // max(cols, LANES))
    if rows <= want:
        return rows
    best = None
    for t in range(8, want + 1, 8):
        if rows % t == 0:
            best = t
    assert best is not None, (rows, cols)
    return best


def _local_step(xs, tgt, mods, mw, w1, w2, norm_mix, norm_mlp):
    depth = len(mods)
    mixer_fwd = (_fox_fwd, _sg_fwd, _cv_fwd)
    mixer_bwd = (_fox_bwd, _sg_bwd, _cv_bwd)

    nsub = 2 * depth
    sub = []
    x_in = xs
    y_prev = gate_prev = None
    for k in range(nsub):
        i, is_mlp = k // 2, k % 2
        sh, sc = mods[i][3 * is_mlp], mods[i][3 * is_mlp + 1]
        g = (norm_mlp if is_mlp else norm_mix)[i:i + 1]
        if k == 0:
            h = _first_norm(x_in, g, sc, sh)
        else:
            x_in, h = _res_norm(x_in, y_prev, gate_prev, g, sc, sh)
        if is_mlp:
            y, saved = _mlp_fwd(h, w1[i], w2[i])
        else:
            y, saved = mixer_fwd[i % 3](h, mw[i])
        sub.append((x_in, h, y, saved))
        y_prev, gate_prev = y, mods[i][3 * is_mlp + 2]

    loss_part, dxo, dy, dgate = _final_loss(x_in, y_prev, gate_prev, tgt)

    dmods = [[None] * 6 for _ in range(depth)]
    g_norm = {'norm_mix': [None] * depth, 'norm_mlp': [None] * depth}
    g_mix = [None] * depth
    g_w1, g_w2 = [None] * depth, [None] * depth
    for k in reversed(range(nsub)):
        i, is_mlp = k // 2, k % 2
        x_k, h_k, _, saved = sub[k]
        dmods[i][3 * is_mlp + 2] = dgate
        if is_mlp:
            dh, g_w1[i], g_w2[i] = _mlp_bwd(dy, h_k, w1[i], w2[i], saved)
        else:
            dh, g_mix[i] = mixer_bwd[i % 3](dy, h_k, mw[i], saved)
        sc = mods[i][3 * is_mlp + 1]
        g = (norm_mlp if is_mlp else norm_mix)[i:i + 1]
        if k > 0:
            ip, mp = (k - 1) // 2, (k - 1) % 2
            dxo, dy, (dsh, dsc, dg, dgate) = _bwd_norm_gate(dxo, dh, x_k, sub[k - 1][2], g, sc, mods[ip][3 * mp + 2])
        else:
            dxo, (dsh, dsc, dg) = _bwd_norm_first(dxo, dh, x_k, g, sc)
        dmods[i][3 * is_mlp], dmods[i][3 * is_mlp + 1] = dsh, dsc
        g_norm['norm_mlp' if is_mlp else 'norm_mix'][i] = dg
    return loss_part, dxo, dmods, g_norm, g_mix, g_w1, g_w2


def kernel(x, c, norm_mix, norm_mlp, w_ada, b_ada, w_mlp_in, w_mlp_out, fox_w_in, fox_b_f, fox_q_norm, fox_k_norm, fox_w_out, sg_w_in, sg_ln_g, sg_ln_b, sg_w_s, sg_b_s, sg_w_out, cv_w_pw1, cv_b_pw1, cv_w_dw, cv_b_dw, cv_ln_g, cv_ln_b, cv_w_pw2, cv_b_pw2, loss_target, m_norm_mix, m_norm_mlp, m_w_ada, m_b_ada, m_w_mlp_in, m_w_mlp_out, m_fox_w_in, m_fox_b_f, m_fox_q_norm, m_fox_k_norm, m_fox_w_out, m_sg_w_in, m_sg_ln_g, m_sg_ln_b, m_sg_w_s, m_sg_b_s, m_sg_w_out, m_cv_w_pw1, m_cv_b_pw1, m_cv_w_dw, m_cv_b_dw, m_cv_ln_g, m_cv_ln_b, m_cv_w_pw2, m_cv_b_pw2, v_norm_mix, v_norm_mlp, v_w_ada, v_b_ada, v_w_mlp_in, v_w_mlp_out, v_fox_w_in, v_fox_b_f, v_fox_q_norm, v_fox_k_norm, v_fox_w_out, v_sg_w_in, v_sg_ln_g, v_sg_ln_b, v_sg_w_s, v_sg_b_s, v_sg_w_out, v_cv_w_pw1, v_cv_b_pw1, v_cv_w_dw, v_cv_b_dw, v_cv_ln_g, v_cv_ln_b, v_cv_w_pw2, v_cv_b_pw2):
    P = dict(zip(_ARGS, (x, c, norm_mix, norm_mlp, w_ada, b_ada, w_mlp_in, w_mlp_out, fox_w_in, fox_b_f, fox_q_norm, fox_k_norm, fox_w_out, sg_w_in, sg_ln_g, sg_ln_b, sg_w_s, sg_b_s, sg_w_out, cv_w_pw1, cv_b_pw1, cv_w_dw, cv_b_dw, cv_ln_g, cv_ln_b, cv_w_pw2, cv_b_pw2, loss_target, m_norm_mix, m_norm_mlp, m_w_ada, m_b_ada, m_w_mlp_in, m_w_mlp_out, m_fox_w_in, m_fox_b_f, m_fox_q_norm, m_fox_k_norm, m_fox_w_out, m_sg_w_in, m_sg_ln_g, m_sg_ln_b, m_sg_w_s, m_sg_b_s, m_sg_w_out, m_cv_w_pw1, m_cv_b_pw1, m_cv_w_dw, m_cv_b_dw, m_cv_ln_g, m_cv_ln_b, m_cv_w_pw2, m_cv_b_pw2, v_norm_mix, v_norm_mlp, v_w_ada, v_b_ada, v_w_mlp_in, v_w_mlp_out, v_fox_w_in, v_fox_b_f, v_fox_q_norm, v_fox_k_norm, v_fox_w_out, v_sg_w_in, v_sg_ln_g, v_sg_ln_b, v_sg_w_s, v_sg_b_s, v_sg_w_out, v_cv_w_pw1, v_cv_b_pw1, v_cv_w_dw, v_cv_b_dw, v_cv_ln_g, v_cv_ln_b, v_cv_w_pw2, v_cv_b_pw2)))
    me = 4 * lax.axis_index("x") + 2 * lax.axis_index("y") + lax.axis_index("c")
    xs = x[0]
    tgt = loss_target[0]
    s_len, d = xs.shape
    depth = norm_mix.shape[0]
    n_fox, n_sg, n_cv = fox_w_in.shape[0], sg_w_in.shape[0], cv_w_pw1.shape[0]
    heads = d // HEAD_DIM
    bf = lambda a: a.astype(MXU_DTYPE)

    cv_small = ['cv_b_pw1', 'cv_w_dw', 'cv_b_dw', 'cv_ln_g', 'cv_ln_b', 'cv_b_pw2']
    small_shapes = [c.shape] + [P[n].shape for n in cv_small]
    (small_all,) = _exchange("gather_small", [_pack([c] + [P[n] for n in cv_small])], scatter=False)
    sm = dict(zip(['c'] + cv_small, _unpack(small_all, small_shapes, lead=(N_DEV,))))
    c_all = sm['c'][:, 0, :]
    cat_last = lambda a: jnp.moveaxis(a, 0, -2).reshape(a.shape[1:-1] + (-1,))
    cvf = {n: cat_last(sm[n]) for n in cv_small}

    big = ['w_mlp_in', 'w_mlp_out', 'fox_w_in', 'fox_w_out', 'sg_w_in', 'sg_w_out', 'cv_w_pw1', 'cv_w_pw2']
    col_sharded = {'w_mlp_in', 'fox_w_in', 'sg_w_in', 'cv_w_pw1'}
    gathered = dict(zip(big, _exchange("gather_weights", [bf(P[n]) for n in big], scatter=False)))

    def full_weight(name, j):
        g = gathered[name][:, j]
        if name in col_sharded:
            return jnp.transpose(g, (1, 0, 2)).reshape(g.shape[1], -1)
        return g.reshape(-1, g.shape[2])

    c_act = c_all * _sigmoid(c_all)
    c_pad = bf(jnp.pad(c_act, ((0, 16 - N_DEV), (0, 0))))
    n_ada = w_ada.shape[2]
    (mod_part,) = _mm("ada_mod", c_pad, bf(jnp.transpose(w_ada, (1, 0, 2)).reshape(d, depth * n_ada)),
                      epi=lambda acc, b: (acc + b,),
                      vecs=(lax.dynamic_slice_in_dim(b_ada, me * n_ada, n_ada, axis=1).reshape(1, depth * n_ada),),
                      tn=n_ada)
    (mod_all,) = _exchange("gather_mod", [mod_part], scatter=False)
    mod_me = lax.dynamic_index_in_dim(mod_all, me, axis=1, keepdims=False)
    mod = jnp.transpose(mod_me.reshape(N_DEV, depth, n_ada), (1, 0, 2)).reshape(depth, 6 * d)
    mods = [[mod[i:i + 1, k * d:(k + 1) * d] for k in range(6)] for i in range(depth)]

    def mixer_weights(i):
        kind, j = i % 3, i // 3
        if kind == 0:
            w_in = full_weight('fox_w_in', j)
            n_pad = -(-w_in.shape[1] // (5 * LANES)) * (5 * LANES)
            return dict(w_in=_pad_cols(w_in, n_pad), w_out=full_weight('fox_w_out', j),
                        b_f=_pad_cols(fox_b_f[j:j + 1], LANES),
                        qg=jnp.tile(fox_q_norm[j:j + 1], (1, 2)), kg=jnp.tile(fox_k_norm[j:j + 1], (1, 2)))
        if kind == 1:
            return dict(w_in=full_weight('sg_w_in', j), w_out=full_weight('sg_w_out', j),
                        ln_g=sg_ln_g[j:j + 1], ln_b=sg_ln_b[j:j + 1],
                        w_s=sg_w_s[j].reshape(SG_GROUPS * SG_CHUNK, SG_CHUNK), b_st=_pad_cols(sg_b_s[j].T, LANES))
        return dict(w_pw1=full_weight('cv_w_pw1', j), w_pw2=full_weight('cv_w_pw2', j),
                    b_pw1=cvf['cv_b_pw1'][j:j + 1], b_pw2=cvf['cv_b_pw2'][j:j + 1],
                    w_dw=jnp.pad(cvf['cv_w_dw'][j], ((0, CONV_PAD - CONV_WIDTH), (0, 0))),
                    b_dw=cvf['cv_b_dw'][j:j + 1], ln_g=cvf['cv_ln_g'][j:j + 1], ln_b=cvf['cv_ln_b'][j:j + 1])

    mw = [mixer_weights(i) for i in range(depth)]
    w1 = [full_weight('w_mlp_in', i) for i in range(depth)]
    w2 = [full_weight('w_mlp_out', i) for i in range(depth)]
    loss_part, dxo, dmods, g_norm, g_mix, g_w1, g_w2 = _local_step(xs, tgt, mods, mw, w1, w2, norm_mix, norm_mlp)
    loss = lax.psum(loss_part, ("x", "y", "c"))
    grad_x = dxo[None]

    stack = lambda key, kind: jnp.stack([g_mix[i][key].reshape(P[name_of[(kind, key)]].shape[1:])
                                         for i in range(depth) if i % 3 == kind])
    name_of = {(0, 'b_f'): 'fox_b_f', (0, 'qg'): 'fox_q_norm', (0, 'kg'): 'fox_k_norm',
               (1, 'ln_g'): 'sg_ln_g', (1, 'ln_b'): 'sg_ln_b', (1, 'w_s'): 'sg_w_s', (1, 'b_s'): 'sg_b_s'}
    dmod_me = jnp.concatenate([jnp.concatenate(r, axis=1) for r in dmods], axis=0)
    small_g = {'dmod': dmod_me,
               'norm_mix': jnp.concatenate(g_norm['norm_mix'], axis=0),
               'norm_mlp': jnp.concatenate(g_norm['norm_mlp'], axis=0)}
    for (kind, key), nm in name_of.items():
        small_g[nm] = stack(key, kind)
    cv_keys = {'cv_b_pw1': 'b_pw1', 'cv_w_dw': 'w_dw', 'cv_b_dw': 'b_dw', 'cv_ln_g': 'ln_g', 'cv_ln_b': 'ln_b',
               'cv_b_pw2': 'b_pw2'}
    for nm, key in cv_keys.items():
        small_g[nm] = jnp.stack([g_mix[i][key].reshape(cvf[nm].shape[1:]) for i in range(depth) if i % 3 == 2])
    sg_names = list(small_g)
    sg_shapes = [small_g[n].shape for n in sg_names]
    (sg_all,) = _exchange("gather_small_grads", [_pack([small_g[n] for n in sg_names])], scatter=False)

    dmod_all = _unpack(sg_all, sg_shapes, lead=(N_DEV,))[0]
    out = {}

    def finish(name, parts, shard_of=None):
        w, m, v = P[name], P['m_' + name], P['v_' + name]
        cols = w.shape[-1]
        r2 = lambda a: a.reshape(-1, cols)
        rows = r2(w).shape[0]
        res = _adamw("adamw_" + name, parts.reshape(parts.shape[0], rows, cols), r2(w), r2(m), r2(v),
                     _row_tile(rows, cols))
        out[name] = tuple(r.reshape(w.shape) for r in res)

    c_t = c_act.T
    ada_g = []
    for i in range(depth):
        blk = lax.dynamic_slice_in_dim(dmod_all[:, i, :], me * n_ada, n_ada, axis=1)
        ada_g.append(_ada_outer(c_t, blk))
    finish('w_ada', jnp.stack(ada_g)[None])
    finish('b_ada', dmod_all)

    sm_names = [n for n in sg_names if n != 'dmod']
    sm_parts = jnp.stack([_pack([_unpack(sg_all[q], sg_shapes)[sg_names.index(n)] for n in sm_names])
                          for q in range(N_DEV)])

    def local_block(nm, a):
        if nm in cv_keys:
            n_loc = P[nm].shape[-1]
            return lax.dynamic_slice_in_dim(a, me * n_loc, n_loc, axis=a.ndim - 1)
        return a
    full_shapes = [small_g[n].shape for n in sm_names]

    def pack_full(prefix):
        arrs = []
        for nm in sm_names:
            a = P[prefix + nm]
            if nm in cv_keys:
                full = jnp.zeros(small_g[nm].shape, F32)
                a = lax.dynamic_update_slice_in_dim(full, a, me * a.shape[-1], axis=a.ndim - 1)
            arrs.append(a)
        return _pack(arrs)
    res = _adamw("adamw_small", sm_parts, pack_full(''), pack_full('m_'), pack_full('v_'),
                 _row_tile(sm_parts.shape[1], LANES))
    unp = [_unpack(r, full_shapes) for r in res]
    for idx, nm in enumerate(sm_names):
        out[nm] = tuple(local_block(nm, unp[t][idx]) for t in range(4))

    def to_slots(name, g2d):
        if name in col_sharded:
            r = g2d.shape[0]
            return jnp.transpose(g2d.reshape(r, N_DEV, -1), (1, 0, 2))
        return g2d.reshape(N_DEV, -1, g2d.shape[1])
    layer_grads = {
        'w_mlp_in': g_w1, 'w_mlp_out': g_w2,
        'fox_w_in': [g_mix[i]['w_in'][:, :fox_w_in.shape[2] * N_DEV] for i in range(depth) if i % 3 == 0],
        'fox_w_out': [g_mix[i]['w_out'] for i in range(depth) if i % 3 == 0],
        'sg_w_in': [g_mix[i]['w_in'] for i in range(depth) if i % 3 == 1],
        'sg_w_out': [g_mix[i]['w_out'] for i in range(depth) if i % 3 == 1],
        'cv_w_pw1': [g_mix[i]['w_pw1'] for i in range(depth) if i % 3 == 2],
        'cv_w_pw2': [g_mix[i]['w_pw2'] for i in range(depth) if i % 3 == 2],
    }
    slots = [jnp.stack([to_slots(n, g) for g in layer_grads[n]], axis=1) for n in big]
    recv = _exchange("scatter_grads", slots, scatter=True)
    for n, parts in zip(big, recv):
        finish(n, parts)

    outs = [loss, grad_x]
    for t in range(4):
        outs += [out[n][t] for n in _WEIGHTS]
    return tuple(outs)
```

```python
import functools
import math

import jax
import jax.numpy as jnp
from jax import lax
from jax.experimental import pallas as pl
from jax.experimental.pallas import tpu as pltpu

F32 = jnp.float32
MXU_DTYPE = jnp.bfloat16
EPS = 1e-6
N_DEV = 8
HEAD_DIM = 64
LANES = 128
CONV_WIDTH = 31
CONV_PAD = 32
SG_CHUNK = 128
SG_BLOCK = 64
SG_GROUPS = 8
ATT_BLOCK = 256
SCAN_BLOCK = 256
VMEM_LIMIT = 48 * 1024 * 1024
ATT_BWD_VMEM_LIMIT = 56 * 1024 * 1024
ADAM_LR, ADAM_B1, ADAM_B2, ADAM_EPS, ADAM_WD, ADAM_STEP = 0.001, 0.9, 0.999, 1e-08, 0.01, 10
NEG = -1e30

_WEIGHTS = ['norm_mix', 'norm_mlp', 'w_ada', 'b_ada', 'w_mlp_in', 'w_mlp_out', 'fox_w_in', 'fox_b_f',
            'fox_q_norm', 'fox_k_norm', 'fox_w_out', 'sg_w_in', 'sg_ln_g', 'sg_ln_b', 'sg_w_s', 'sg_b_s',
            'sg_w_out', 'cv_w_pw1', 'cv_b_pw1', 'cv_w_dw', 'cv_b_dw', 'cv_ln_g', 'cv_ln_b', 'cv_w_pw2',
            'cv_b_pw2']
_ARGS = ['x', 'c'] + _WEIGHTS + ['loss_target'] + ['m_' + n for n in _WEIGHTS] + ['v_' + n for n in _WEIGHTS]


def _cparams(sem=None, vmem=VMEM_LIMIT):
    return pltpu.CompilerParams(dimension_semantics=sem, vmem_limit_bytes=vmem)


def _colsum(v):
    return jnp.sum(v, axis=0, keepdims=True)


def _sigmoid(v):
    return 1.0 / (1.0 + jnp.exp(-v))


def _rowwise(name, fn, rows, consts, row_out, red_out, tr):
    n_rows = rows[0].shape[0]
    tr = min(tr, n_rows)
    assert n_rows % tr == 0
    nr, nc, no = len(rows), len(consts), len(row_out)

    def body(*refs):
        ins = [r[...] for r in refs[:nr + nc]]
        outs, reds = fn(*ins)
        out_refs = refs[nr + nc:nr + nc + no]
        red_refs = refs[nr + nc + no:]
        for o_ref, o in zip(out_refs, outs):
            o_ref[...] = o.astype(o_ref.dtype)
        if red_refs:
            @pl.when(pl.program_id(0) == 0)
            def _():
                for r_ref in red_refs:
                    r_ref[...] = jnp.zeros_like(r_ref)
            for r_ref, r in zip(red_refs, reds):
                r_ref[...] += r

    def rspec(a):
        return pl.BlockSpec((tr,) + a.shape[1:], lambda i: (i,) + (0,) * (a.ndim - 1))

    def cspec(shape):
        return pl.BlockSpec(shape, lambda i: (0,) * len(shape))

    out_shape = [jax.ShapeDtypeStruct((n_rows, w), dt) for w, dt in row_out]
    out_shape += [jax.ShapeDtypeStruct(s, F32) for s in red_out]
    out_specs = [pl.BlockSpec((tr, w), lambda i: (i, 0)) for w, _ in row_out] + [cspec(s) for s in red_out]
    res = pl.pallas_call(
        body, name=name, grid=(n_rows // tr,),
        in_specs=[rspec(a) for a in rows] + [cspec(a.shape) for a in consts],
        out_specs=out_specs, out_shape=out_shape,
        compiler_params=_cparams(("arbitrary",)),
    )(*rows, *consts)
    return res[:no], res[no:]


def _mm(name, a, b, *, ta=False, tb=False, out_dtypes=(F32,), epi=None, tiles=(), vecs=(), tm=512, tn=512):
    m_dim, k_dim = (a.shape[1], a.shape[0]) if ta else a.shape
    n_dim = b.shape[0] if tb else b.shape[1]
    assert (b.shape[1] if tb else b.shape[0]) == k_dim
    tm, tn = min(tm, m_dim), min(tn, n_dim)
    assert m_dim % tm == 0 and n_dim % tn == 0, (name, m_dim, n_dim, tm, tn)
    dims = (((0 if ta else 1,), (1 if tb else 0,)), ((), ()))
    nx = len(tiles) + len(vecs)

    def body(a_ref, b_ref, *rest):
        acc = lax.dot_general(a_ref[...], b_ref[...], dims, preferred_element_type=F32)
        outs = epi(acc, *[r[...] for r in rest[:nx]]) if epi is not None else (acc,)
        for o_ref, o in zip(rest[nx:], outs):
            o_ref[...] = o.astype(o_ref.dtype)

    a_spec = pl.BlockSpec((k_dim, tm), lambda i, j: (0, i)) if ta else pl.BlockSpec((tm, k_dim), lambda i, j: (i, 0))
    b_spec = pl.BlockSpec((tn, k_dim), lambda i, j: (j, 0)) if tb else pl.BlockSpec((k_dim, tn), lambda i, j: (0, j))
    t_spec = pl.BlockSpec((tm, tn), lambda i, j: (i, j))
    v_spec = pl.BlockSpec((1, tn), lambda i, j: (0, j))
    res = pl.pallas_call(
        body, name=name, grid=(m_dim // tm, n_dim // tn),
        in_specs=[a_spec, b_spec] + [t_spec] * len(tiles) + [v_spec] * len(vecs),
        out_specs=[t_spec] * len(out_dtypes),
        out_shape=[jax.ShapeDtypeStruct((m_dim, n_dim), dt) for dt in out_dtypes],
        compiler_params=_cparams(("parallel", "parallel")),
    )(a, b, *tiles, *vecs)
    return res


def _exchange_copies(scatter, in_refs, land_refs, send_sems, recv_sems, local_sems):
    n = len(in_refs)
    x, y, c = lax.axis_index("x"), lax.axis_index("y"), lax.axis_index("c")
    me = 4 * x + 2 * y + c
    local = [pltpu.make_async_copy(in_refs[a].at[me] if scatter else in_refs[a], land_refs[a].at[me],
                                   local_sems.at[a]) for a in range(n)]
    send, arrive = [], []
    for k in range(1, N_DEV):
        px, py, pc = x ^ ((k >> 2) & 1), y ^ ((k >> 1) & 1), c ^ (k & 1)
        peer = 4 * px + 2 * py + pc
        for a in range(n):
            src = in_refs[a].at[peer] if scatter else in_refs[a]
            sems = dict(send_sem=send_sems.at[a * (N_DEV - 1) + k - 1], recv_sem=recv_sems.at[a * (N_DEV - 1) + k - 1],
                        device_id=(px, py, pc), device_id_type=pl.DeviceIdType.MESH)
            send.append(pltpu.make_async_remote_copy(src_ref=src, dst_ref=land_refs[a].at[me], **sems))
            arrive.append(pltpu.make_async_remote_copy(src_ref=src, dst_ref=land_refs[a].at[peer], **sems))
    return local, send, arrive


def _land_shape(a, scatter):
    return ((N_DEV,) + a.shape[1:]) if scatter else ((N_DEV,) + a.shape)


def _exchange(name, arrays, scatter):
    n = len(arrays)

    def body(*refs):
        local, send, arrive = _exchange_copies(scatter, refs[:n], refs[n:2 * n], *refs[2 * n:])
        for cp in local + send:
            cp.start()
        for cp, arr in zip(send, arrive):
            cp.wait_send()
            arr.wait_recv()
        for cp in local:
            cp.wait()

    any_spec = pl.BlockSpec(memory_space=pl.ANY)
    return pl.pallas_call(
        body, name=name,
        in_specs=[any_spec] * n, out_specs=[any_spec] * n,
        out_shape=[jax.ShapeDtypeStruct(_land_shape(a, scatter), a.dtype) for a in arrays],
        scratch_shapes=[pltpu.SemaphoreType.DMA((n * (N_DEV - 1),)),
                        pltpu.SemaphoreType.DMA((n * (N_DEV - 1),)),
                        pltpu.SemaphoreType.DMA((n,))],
        compiler_params=pltpu.CompilerParams(has_side_effects=True),
    )(*arrays)


_HBM = pl.BlockSpec(memory_space=pltpu.HBM)
_SEM = pl.BlockSpec(memory_space=pltpu.SEMAPHORE)
_EFFECT = pltpu.SideEffectType.DATAFLOW_SIDE_EFFECTING


def _exchange_start(name, arrays, scatter, dep):
    n = len(arrays)
    nsem = n * (N_DEV - 1)
    srcs = [pltpu.with_memory_space_constraint(a, pltpu.HBM) for a in arrays]
    lands = [pltpu.with_memory_space_constraint(lax.empty(_land_shape(a, scatter), a.dtype), pltpu.HBM) for a in arrays]

    def body(*refs):
        sems = refs[2 * n + 1:2 * n + 4]
        local, send, _ = _exchange_copies(scatter, refs[:n], refs[n:2 * n], *sems)
        for cp in local + send:
            cp.start()
        token = refs[-1]
        token[...] = jnp.zeros_like(token)

    res = pl.pallas_call(
        body, name=name,
        in_specs=[_HBM] * (2 * n) + [pl.BlockSpec(memory_space=pl.ANY)],
        out_specs=[_SEM] * 3 + [_HBM] * (2 * n) + [pl.BlockSpec(memory_space=pltpu.VMEM)],
        out_shape=[pltpu.SemaphoreType.DMA((nsem,)), pltpu.SemaphoreType.DMA((nsem,)), pltpu.SemaphoreType.DMA((n,))]
        + [pltpu.HBM(a.shape, a.dtype) for a in arrays]
        + [pltpu.HBM(_land_shape(a, scatter), a.dtype) for a in arrays]
        + [jax.ShapeDtypeStruct((8, LANES), F32)],
        input_output_aliases={i: 3 + i for i in range(2 * n)},
        compiler_params=pltpu.CompilerParams(has_side_effects=_EFFECT),
    )(*srcs, *lands, dep)
    return res[:-1], res[-1][0:1, 0:1]


def _exchange_wait(name, handles, scatter, after):
    n = (len(handles) - 3) // 2
    sems, thru = handles[:3], handles[3:]

    def body(*refs):
        local, send, arrive = _exchange_copies(scatter, refs[:n], refs[n:2 * n], *refs[2 * n:2 * n + 3])
        for cp, arr in zip(send, arrive):
            cp.wait_send()
            arr.wait_recv()
        for cp in local:
            cp.wait()

    res = pl.pallas_call(
        body, name=name,
        in_specs=[_HBM] * (2 * n) + [_SEM] * 3 + [pl.BlockSpec(memory_space=pl.ANY)],
        out_specs=[_HBM] * (2 * n),
        out_shape=[pltpu.HBM(t.shape, t.dtype) for t in thru],
        input_output_aliases={i: i for i in range(2 * n)},
        compiler_params=pltpu.CompilerParams(has_side_effects=_EFFECT),
    )(*thru, *sems, after)
    return res[n:]


def _norm_mod(x, g, sc, sh):
    r = lax.rsqrt(jnp.mean(x * x, axis=-1, keepdims=True) + EPS)
    return (x * r * g) * (1.0 + sc) + sh


def _first_norm(x, g, sc, sh):
    (h,), _ = _rowwise("first_norm", lambda x, g, sc, sh: ((_norm_mod(x, g, sc, sh),), ()),
                       [x], [g, sc, sh], [(x.shape[1], MXU_DTYPE)], [], 256)
    return h


def _res_norm(x, y, gate, g, sc, sh):
    def fn(x, y, gate, g, sc, sh):
        xn = x + gate * y
        return (xn, _norm_mod(xn, g, sc, sh)), ()
    (xn, h), _ = _rowwise("res_norm", fn, [x, y], [gate, g, sc, sh],
                          [(x.shape[1], F32), (x.shape[1], MXU_DTYPE)], [], 256)
    return xn, h


def _final_loss(x, y, gate, target):
    d = x.shape[1]

    def fn(x, y, target, gate):
        err = (x + gate * y) - target
        part = jnp.sum(jnp.sum(err * err, axis=-1, keepdims=True), axis=0, keepdims=True) * (0.5 / d)
        dx = err * (1.0 / d)
        return (dx, dx * gate), (jnp.broadcast_to(part, (1, LANES)), _colsum(dx * y))
    (dx, dy), (loss, dgate) = _rowwise("final_loss", fn, [x, y, target], [gate],
                                       [(d, F32), (d, MXU_DTYPE)], [(1, LANES), (1, d)], 256)
    return loss[0, 0], dx, dy, dgate


def _norm_bwd_core(dxo, dh, x, g, sc):
    r = lax.rsqrt(jnp.mean(x * x, axis=-1, keepdims=True) + EPS)
    xn = x * r
    dsh = _colsum(dh)
    dsc = _colsum(dh * (xn * g))
    dyy = dh * (1.0 + sc)
    dg = _colsum(dyy * xn)
    dxn = dyy * g
    dxi = dxo + r * (dxn - xn * jnp.mean(dxn * xn, axis=-1, keepdims=True))
    return dxi, dsh, dsc, dg


def _bwd_norm_gate(dxo, dh, x, y_prev, g, sc, gate_prev):
    d = x.shape[1]

    def fn(dxo, dh, x, y_prev, g, sc, gate_prev):
        dxi, dsh, dsc, dg = _norm_bwd_core(dxo, dh, x, g, sc)
        return (dxi, dxi * gate_prev), (dsh, dsc, dg, _colsum(dxi * y_prev))
    (dxi, dy), reds = _rowwise("bwd_norm_gate", fn, [dxo, dh, x, y_prev], [g, sc, gate_prev],
                               [(d, F32), (d, MXU_DTYPE)], [(1, d)] * 4, 256)
    return dxi, dy, reds


def _bwd_norm_first(dxo, dh, x, g, sc):
    d = x.shape[1]

    def fn(dxo, dh, x, g, sc):
        dxi, dsh, dsc, dg = _norm_bwd_core(dxo, dh, x, g, sc)
        return (dxi,), (dsh, dsc, dg)
    (dxi,), reds = _rowwise("bwd_norm_first", fn, [dxo, dh, x], [g, sc], [(d, F32)], [(1, d)] * 3, 256)
    return dxi, reds


def _mlp_fwd(h, w1, w2):
    def epi(acc):
        r = jnp.maximum(acc, 0.0)
        return acc, r * r
    a, z = _mm("mlp_in", h, w1, out_dtypes=(MXU_DTYPE, MXU_DTYPE), epi=epi, tm=1024, tn=512)
    (out,) = _mm("mlp_out", z, w2, tm=512, tn=512)
    return out, (a, z)


def _mlp_bwd(dy, h, w1, w2, saved):
    a, z = saved

    def epi(acc, a):
        return (acc * (2.0 * jnp.maximum(a.astype(F32), 0.0)),)
    (da,) = _mm("mlp_dz", dy, w2, tb=True, out_dtypes=(MXU_DTYPE,), epi=epi, tiles=(a,), tm=1024, tn=512)
    (dw2,) = _mm("mlp_dw2", z, dy, ta=True, out_dtypes=(MXU_DTYPE,))
    (dw1,) = _mm("mlp_dw1", h, da, ta=True, out_dtypes=(MXU_DTYPE,))
    (dh,) = _mm("mlp_dh", da, w1, tb=True)
    return dh, dw1, dw2


def _split3(v):
    hi = v.astype(jnp.bfloat16)
    r1 = v - hi.astype(F32)
    mid = r1.astype(jnp.bfloat16)
    lo = (r1 - mid.astype(F32)).astype(jnp.bfloat16)
    return hi, mid, lo


def _tri_matmul(tri, v):
    hi, mid, lo = _split3(v)
    dot = functools.partial(jnp.dot, preferred_element_type=F32)
    return dot(tri, hi) + dot(tri, mid) + dot(tri, lo)


def _log_sigmoid(v):
    return jnp.minimum(v, 0.0) - jnp.log(1.0 + jnp.exp(-jnp.abs(v)))


def _gate_fwd(proj, b_pad, col_block):
    s = proj.shape[0]
    tb = min(SCAN_BLOCK, s)
    nblk = s // tb

    def body(f_ref, b_ref, o_ref):
        row = lax.broadcasted_iota(jnp.int32, (tb, tb), 0)
        col = lax.broadcasted_iota(jnp.int32, (tb, tb), 1)
        tri = (col <= row).astype(jnp.bfloat16)

        def step(i, carry):
            rows = pl.ds(pl.multiple_of(i * tb, tb), tb)
            lf = _log_sigmoid(f_ref[rows, :] + b_ref[...])
            f = _tri_matmul(tri, lf) + carry
            o_ref[rows, :] = f
            return f[tb - 1:tb, :]
        lax.fori_loop(0, nblk, step, jnp.zeros((1, LANES), F32))

    return pl.pallas_call(
        body, name="gate_fwd", grid=(1,),
        in_specs=[pl.BlockSpec((s, LANES), lambda i: (0, col_block)), pl.BlockSpec((1, LANES), lambda i: (0, 0))],
        out_specs=pl.BlockSpec((s, LANES), lambda i: (0, 0)),
        out_shape=jax.ShapeDtypeStruct((s, LANES), F32),
        compiler_params=_cparams(("arbitrary",)),
    )(proj, b_pad)


def _gate_bwd(proj, b_pad, d_f, col_block):
    s = proj.shape[0]
    tb = min(SCAN_BLOCK, s)
    nblk = s // tb

    def body(f_ref, b_ref, d_ref, o_ref, db_ref):
        row = lax.broadcasted_iota(jnp.int32, (tb, tb), 0)
        col = lax.broadcasted_iota(jnp.int32, (tb, tb), 1)
        tri = (col >= row).astype(jnp.bfloat16)

        def step(j, carry):
            acc, db = carry
            i = nblk - 1 - j
            rows = pl.ds(pl.multiple_of(i * tb, tb), tb)
            dlf = _tri_matmul(tri, d_ref[rows, :]) + acc
            dpre = dlf * _sigmoid(-(f_ref[rows, :] + b_ref[...]))
            o_ref[rows, :] = dpre.astype(o_ref.dtype)
            return dlf[0:1, :], db + _colsum(dpre)
        _, db = lax.fori_loop(0, nblk, step, (jnp.zeros((1, LANES), F32), jnp.zeros((1, LANES), F32)))
        db_ref[...] = db

    return pl.pallas_call(
        body, name="gate_bwd", grid=(1,),
        in_specs=[pl.BlockSpec((s, LANES), lambda i: (0, col_block)), pl.BlockSpec((1, LANES), lambda i: (0, 0)),
                  pl.BlockSpec((s, LANES), lambda i: (0, 0))],
        out_specs=[pl.BlockSpec((s, LANES), lambda i: (0, 0)), pl.BlockSpec((1, LANES), lambda i: (0, 0))],
        out_shape=[jax.ShapeDtypeStruct((s, LANES), MXU_DTYPE), jax.ShapeDtypeStruct((1, LANES), F32)],
        compiler_params=_cparams(("arbitrary",)),
    )(proj, b_pad, d_f)


def _head_masks():
    lane = lax.broadcasted_iota(jnp.int32, (1, LANES), 1)
    return lane < HEAD_DIM


def _pair_norm(v, g, first):
    v2 = v * v
    ss0 = jnp.sum(jnp.where(first, v2, 0.0), axis=-1, keepdims=True)
    ss1 = jnp.sum(jnp.where(first, 0.0, v2), axis=-1, keepdims=True)
    r = jnp.where(first, lax.rsqrt(ss0 * (1.0 / HEAD_DIM) + EPS), lax.rsqrt(ss1 * (1.0 / HEAD_DIM) + EPS))
    vn = v * r
    return vn * g, vn, r


_NT = (((1,), (1,)), ((), ()))
_TN = (((0,), (0,)), ((), ()))

ATT_TQ = 256
ATT_TK = 256
AUG_F, AUG_ONE = 0, 3


def _own_lanes(hd):
    lane = lax.broadcasted_iota(jnp.int32, (1, LANES), 1)
    return (lane < HEAD_DIM) if hd == 0 else (lane >= HEAD_DIM)


def _aug_lanes(hd, f_other):
    lane = lax.broadcasted_iota(jnp.int32, (1, LANES), 1) - (HEAD_DIM if hd == 0 else 0)
    hi, mid, lo = [t.astype(F32) for t in _split3(f_other)]
    zero = jnp.zeros_like(f_other)
    f_terms = jnp.where(lane == 0, hi, jnp.where(lane == 1, mid, jnp.where(lane == 2, lo, zero)))
    f_shift = jnp.where(lane == 3, hi, jnp.where(lane == 4, mid, jnp.where(lane == 5, lo, zero)))
    ones_lo = jnp.where(lane < 3, 1.0, 0.0) * jnp.where(lane >= 0, 1.0, 0.0)
    ones_hi = jnp.where(lane < 6, 1.0, 0.0) * jnp.where(lane >= 3, 1.0, 0.0)
    return f_terms + ones_hi, ones_lo - f_shift


def _attn_operands(q_raw, k_raw, f_rep, qg, kg, scale):
    first = _head_masks()
    qn, _, _ = _pair_norm(q_raw, qg, first)
    kn, _, _ = _pair_norm(k_raw, kg, first)
    f_other = pltpu.roll(f_rep, HEAD_DIM, 1)
    out = []
    for hd in range(2):
        own = _own_lanes(hd)
        q_x, k_x = _aug_lanes(hd, f_other)
        out.append((jnp.where(own, qn * scale, q_x), jnp.where(own, kn, k_x)))
    return out


def _causal_t(tk, tq, off):
    r = lax.broadcasted_iota(jnp.int32, (tk, tq), 0)
    c = lax.broadcasted_iota(jnp.int32, (tk, tq), 1)
    return (r - c) <= off


def _big(shape, index_map):
    return pl.BlockSpec(shape, index_map, pipeline_mode=pl.Buffered(1))


def _attn_fwd_t(proj, f_rep, qg, kg, d_model):
    s = proj.shape[0]
    pairs = d_model // LANES
    tq, tk = min(ATT_TQ, s), min(ATT_TK, s)
    assert tk % tq == 0 and s % tk == 0
    nq = s // tq
    scale = HEAD_DIM ** -0.5
    ch = tk

    def body(q_ref, k_ref, v_ref, frep_ref, qg_ref, kg_ref, o_ref, lse_ref, qt_s, k_s, vt_s):
        for ci in range(s // ch):
            rows = pl.ds(ci * ch, ch)
            ops = _attn_operands(q_ref[rows, :], k_ref[rows, :], frep_ref[rows, :], qg_ref[...], kg_ref[...], scale)
            vv = v_ref[rows, :]
            for hd in range(2):
                own = _own_lanes(hd)
                lane = lax.broadcasted_iota(jnp.int32, (1, LANES), 1)
                one_lane = lane == (HEAD_DIM if hd == 0 else 0)
                qt_s[hd, :, rows] = ops[hd][0].T.astype(qt_s.dtype)
                k_s[hd, rows, :] = ops[hd][1].astype(k_s.dtype)
                vt_s[hd, :, rows] = jnp.where(own, vv, jnp.where(one_lane, 1.0, 0.0)).T.astype(vt_s.dtype)

        def q_block(qi, _):
            q0 = pl.multiple_of(qi * tq, tq)
            qcols = pl.ds(q0, tq)
            nfull = q0 // tk
            qts = [qt_s[hd, :, qcols] for hd in range(2)]

            def krows(kj):
                return pl.ds(pl.multiple_of(kj * tk, tk), tk)

            def scores(hd, kj):
                return jnp.dot(k_s[hd, krows(kj), :], qts[hd], preferred_element_type=F32)

            def kv_step(kj, carry, last):
                new = []
                for hd in range(2):
                    m, acc, st, p_prev = carry[hd]
                    if not last:
                        st_next = scores(hd, kj + 1)
                    pv = jnp.dot(vt_s[hd, :, krows(jnp.maximum(kj - 1, 0))], p_prev, preferred_element_type=F32)
                    if last:
                        st = jnp.where(_causal_t(tk, tq, q0 - kj * tk), st, NEG)
                    m_new = jnp.maximum(m, jnp.max(st, axis=0, keepdims=True))
                    p = jnp.exp(st - m_new).astype(vt_s.dtype)
                    acc = jnp.exp(m - m_new) * (acc + pv)
                    if last:
                        acc = acc + jnp.dot(vt_s[hd, :, krows(kj)], p, preferred_element_type=F32)
                        new.append((m_new, acc))
                    else:
                        new.append((m_new, acc, st_next, p))
                return tuple(new)

            init = tuple((jnp.full((1, tq), NEG, F32), jnp.zeros((LANES, tq), F32), scores(hd, 0),
                          jnp.zeros((tk, tq), vt_s.dtype)) for hd in range(2))
            carry = lax.fori_loop(0, nfull, lambda kj, cr: kv_step(kj, cr, False), init)
            o_parts, lse_parts = [], []
            for hd, (m, acc) in enumerate(kv_step(nfull, carry, True)):
                e0 = HEAD_DIM if hd == 0 else 0
                l = acc[e0:e0 + 1, :]
                o_parts.append((acc / l).T)
                lse_parts.append(m + jnp.log(l))
            o_ref[pl.ds(q0, tq), :] = jnp.where(_head_masks(), o_parts[0], o_parts[1]).astype(o_ref.dtype)
            lse_ref[0, :, qcols] = jnp.concatenate(lse_parts, axis=0)
            return 0
        lax.fori_loop(0, nq, q_block, 0)

    blk = lambda off: _big((s, LANES), lambda h: (0, off + h))
    vec = pl.BlockSpec((1, LANES), lambda h: (0, 0))
    return pl.pallas_call(
        body, name="attn_fwd", grid=(pairs,),
        in_specs=[blk(0), blk(pairs), blk(2 * pairs), blk(0), vec, vec],
        out_specs=[pl.BlockSpec((s, LANES), lambda h: (0, h)), pl.BlockSpec((1, 2, s), lambda h: (h, 0, 0))],
        out_shape=[jax.ShapeDtypeStruct((s, d_model), MXU_DTYPE), jax.ShapeDtypeStruct((pairs, 2, s), F32)],
        scratch_shapes=[pltpu.VMEM((2, LANES, s), MXU_DTYPE), pltpu.VMEM((2, s, LANES), MXU_DTYPE),
                        pltpu.VMEM((2, LANES, s), MXU_DTYPE)],
        compiler_params=_cparams(("arbitrary",)),
    )(proj, proj, proj, f_rep, qg, kg)


def _attn_bwd_t(proj, do, o, lse, f_rep, qg, kg, d_model):
    s = proj.shape[0]
    pairs = d_model // LANES
    tq, tk = min(ATT_TQ, s), min(ATT_TK, s)
    assert tk % tq == 0 and s % tk == 0
    nq = s // tq
    scale = HEAD_DIM ** -0.5
    ch = tk

    def norm_bwd(raw, g, dn, first):
        _, xn, r = _pair_norm(raw, g, first)
        dxn = dn * g
        t = dxn * xn
        mu0 = jnp.sum(jnp.where(first, t, 0.0), axis=-1, keepdims=True)
        mu1 = jnp.sum(jnp.where(first, 0.0, t), axis=-1, keepdims=True)
        mu = jnp.where(first, mu0, mu1) * (1.0 / HEAD_DIM)
        return r * (dxn - xn * mu), _colsum(dn * xn)

    def body(q_ref, k_ref, v_ref, do_ref, o_ref, lse_ref, frep_ref, qg_ref, kg_ref,
             dq_ref, dk_ref, dv_ref, df_ref, dqg_ref, dkg_ref,
             q_s, qt_s, k_s, kt_s, v_s, do_s, dot_s, dl_s, dk_s, dv_s):
        first = _head_masks()
        hp = pl.program_id(0)
        lane = lax.broadcasted_iota(jnp.int32, (1, LANES), 1)
        for ci in range(s // ch):
            rows = pl.ds(ci * ch, ch)
            ops = _attn_operands(q_ref[rows, :], k_ref[rows, :], frep_ref[rows, :], qg_ref[...], kg_ref[...], scale)
            v_s[rows, :] = v_ref[rows, :].astype(v_s.dtype)
            dov = do_ref[rows, :].astype(F32)
            ot = o_ref[rows, :].astype(F32).T
            for hd in range(2):
                own = _own_lanes(hd)
                q_s[hd, rows, :] = ops[hd][0].astype(q_s.dtype)
                qt_s[hd, :, rows] = ops[hd][0].T.astype(qt_s.dtype)
                k_s[hd, rows, :] = ops[hd][1].astype(k_s.dtype)
                kt_s[hd, :, rows] = ops[hd][1].T.astype(kt_s.dtype)
                doh = jnp.where(own, dov, 0.0)
                do_s[hd, rows, :] = doh.astype(do_s.dtype)
                doht = doh.T
                dot_s[hd, :, rows] = doht.astype(dot_s.dtype)
                dl_s[hd:hd + 1, rows] = jnp.sum(doht * ot, axis=0, keepdims=True)
        dk_s[...] = jnp.zeros_like(dk_s)
        dv_s[...] = jnp.zeros_like(dv_s)

        @pl.when(hp == 0)
        def _():
            df_ref[...] = jnp.zeros_like(df_ref)

        def q_block(qi, dqg):
            q0 = pl.multiple_of(qi * tq, tq)
            qcols = pl.ds(q0, tq)
            qrows = pl.ds(q0, tq)
            nfull = q0 // tk
            qts = [qt_s[hd, :, qcols] for hd in range(2)]
            dots = [dot_s[hd, :, qcols] for hd in range(2)]
            qns = [q_s[hd, qrows, :] for hd in range(2)]
            dons = [do_s[hd, qrows, :] for hd in range(2)]
            lse_r = [lse_ref[0, hd:hd + 1, qcols] for hd in range(2)]
            dl_r = [dl_s[hd:hd + 1, qcols] for hd in range(2)]
            bdt = qt_s.dtype

            def krows(kj):
                return pl.ds(pl.multiple_of(kj * tk, tk), tk)

            def scores(hd, kj):
                return (jnp.dot(k_s[hd, krows(kj), :], qts[hd], preferred_element_type=F32),
                        jnp.dot(v_s[krows(kj), :], dots[hd], preferred_element_type=F32))

            def products(hd, rows, ds, p, dqt):
                dk_s[hd, rows, :] += jnp.dot(ds, qns[hd], preferred_element_type=F32)
                dv_s[rows, :] += jnp.dot(p, dons[hd], preferred_element_type=F32)
                return dqt + jnp.dot(kt_s[hd, :, rows], ds, preferred_element_type=F32)

            def kv_step(kj, carry, last):
                new = []
                for hd in range(2):
                    dqt, ds_prev, p_prev = carry[hd]
                    st, dp = scores(hd, kj)
                    dqt = products(hd, krows(jnp.maximum(kj - 1, 0)), ds_prev, p_prev, dqt)
                    if last:
                        st = jnp.where(_causal_t(tk, tq, q0 - kj * tk), st, NEG)
                    p = jnp.exp(st - lse_r[hd])
                    ds = (p * (dp - dl_r[hd])).astype(bdt)
                    if last:
                        new.append(products(hd, krows(kj), ds, p.astype(bdt), dqt))
                    else:
                        new.append((dqt, ds, p.astype(bdt)))
                return tuple(new)

            init = tuple((jnp.zeros((LANES, tq), F32), jnp.zeros((tk, tq), bdt), jnp.zeros((tk, tq), bdt))
                         for hd in range(2))
            carry = lax.fori_loop(0, nfull, lambda kj, cr: kv_step(kj, cr, False), init)
            dq_parts = [dqt.T for dqt in kv_step(nfull, carry, True)]
            rs0 = dq_parts[0][:, HEAD_DIM + AUG_F:HEAD_DIM + AUG_F + 1]
            rs1 = dq_parts[1][:, AUG_F:AUG_F + 1]
            df_ref[qrows, :] += jnp.where(lane == 2 * hp, rs0, 0.0) + jnp.where(lane == 2 * hp + 1, rs1, 0.0)
            dqn = jnp.where(first, dq_parts[0], dq_parts[1]) * scale
            dq_raw, dg = norm_bwd(q_ref[qrows, :], qg_ref[...], dqn, first)
            dq_ref[qrows, :] = dq_raw.astype(dq_ref.dtype)
            return dqg + dg
        dqg_ref[0] = lax.fori_loop(0, nq, q_block, jnp.zeros((1, LANES), F32))

        dkg = jnp.zeros((1, LANES), F32)
        for ci in range(s // ch):
            rows = pl.ds(ci * ch, ch)
            dk0, dk1 = dk_s[0, rows, :], dk_s[1, rows, :]
            cs0 = dk0[:, HEAD_DIM + AUG_ONE:HEAD_DIM + AUG_ONE + 1]
            cs1 = dk1[:, AUG_ONE:AUG_ONE + 1]
            df_ref[rows, :] -= jnp.where(lane == 2 * hp, cs0, 0.0) + jnp.where(lane == 2 * hp + 1, cs1, 0.0)
            dk_raw, dg = norm_bwd(k_ref[rows, :], kg_ref[...], jnp.where(first, dk0, dk1), first)
            dk_ref[rows, :] = dk_raw.astype(dk_ref.dtype)
            dkg = dkg + dg
            dv_ref[rows, :] = dv_s[rows, :].astype(dv_ref.dtype)
        dkg_ref[0] = dkg

    blk = lambda off: _big((s, LANES), lambda h: (0, off + h))
    outb = pl.BlockSpec((s, LANES), lambda h: (0, h))
    vec = pl.BlockSpec((1, LANES), lambda h: (0, 0))
    gout = pl.BlockSpec((1, 1, LANES), lambda h: (h, 0, 0))
    act = jax.ShapeDtypeStruct((s, d_model), MXU_DTYPE)
    gsh = jax.ShapeDtypeStruct((pairs, 1, LANES), F32)
    pair_rows = pltpu.VMEM((2, s, LANES), MXU_DTYPE)
    pair_cols = pltpu.VMEM((2, LANES, s), MXU_DTYPE)
    return pl.pallas_call(
        body, name="attn_bwd", grid=(pairs,),
        in_specs=[blk(0), blk(pairs), blk(2 * pairs), blk(0), blk(0),
                  pl.BlockSpec((1, 2, s), lambda h: (h, 0, 0)), blk(0), vec, vec],
        out_specs=[outb, outb, outb, pl.BlockSpec((s, LANES), lambda h: (0, 0)), gout, gout],
        out_shape=[act, act, act, jax.ShapeDtypeStruct((s, LANES), F32), gsh, gsh],
        scratch_shapes=[pair_rows, pair_cols, pair_rows, pair_cols, pltpu.VMEM((s, LANES), MXU_DTYPE),
                        pair_rows, pair_cols, pltpu.VMEM((8, s), F32),
                        pltpu.VMEM((2, s, LANES), F32), pltpu.VMEM((s, LANES), F32)],
        compiler_params=_cparams(("arbitrary",), ATT_BWD_VMEM_LIMIT),
    )(proj, proj, proj, do, o, lse, f_rep, qg, kg)


def _attn_fwd(proj, f_rep, f_rows, qg, kg, d_model):
    s = proj.shape[0]
    pairs = d_model // LANES
    tq = min(ATT_BLOCK, s)
    nq = s // tq
    scale = HEAD_DIM ** -0.5

    def body(q_ref, k_ref, v_ref, frep_ref, frow_ref, qg_ref, kg_ref, o_ref, lse_ref, q0_s, q1_s, kn_s, v_s):
        first = _head_masks()
        for ci in range(nq):
            rows = pl.ds(ci * tq, tq)
            qn, _, _ = _pair_norm(q_ref[rows, :], qg_ref[...], first)
            qn = qn * scale
            q0_s[rows, :] = jnp.where(first, qn, 0.0).astype(q0_s.dtype)
            q1_s[rows, :] = jnp.where(first, 0.0, qn).astype(q1_s.dtype)
            kn, _, _ = _pair_norm(k_ref[rows, :], kg_ref[...], first)
            kn_s[rows, :] = kn.astype(kn_s.dtype)
            v_s[rows, :] = v_ref[rows, :].astype(v_s.dtype)

        rid = lax.broadcasted_iota(jnp.int32, (tq, tq), 0)
        cid = lax.broadcasted_iota(jnp.int32, (tq, tq), 1)
        causal = cid <= rid

        def q_block(qi, _):
            qrows = pl.ds(pl.multiple_of(qi * tq, tq), tq)
            qs = (q0_s[qrows, :], q1_s[qrows, :])
            frep = frep_ref[qrows, :]
            fq = (frep[:, 0:1], frep[:, HEAD_DIM:HEAD_DIM + 1])

            def kv_step(kj, carry, diag):
                krows = pl.ds(pl.multiple_of(kj * tq, tq), tq)
                k = kn_s[krows, :]
                v = v_s[krows, :]
                fk = frow_ref[0, :, krows]
                new = []
                for hd in range(2):
                    m, l, acc = carry[hd]
                    sc = lax.dot_general(qs[hd], k, _NT, preferred_element_type=F32) + (fq[hd] - fk[hd:hd + 1, :])
                    if diag:
                        sc = jnp.where(causal, sc, NEG)
                    m_new = jnp.maximum(m, jnp.max(sc, axis=-1, keepdims=True))
                    p = jnp.exp(sc - m_new)
                    alpha = jnp.exp(m - m_new)
                    l = alpha * l + jnp.sum(p, axis=-1, keepdims=True)
                    acc = alpha * acc + jnp.dot(p.astype(v.dtype), v, preferred_element_type=F32)
                    new.append((m_new, l, acc))
                return tuple(new)

            init = tuple((jnp.full((tq, 1), NEG, F32), jnp.zeros((tq, 1), F32), jnp.zeros((tq, LANES), F32))
                         for _ in range(2))
            carry = lax.fori_loop(0, qi, lambda kj, cr: kv_step(kj, cr, False), init)
            (m0, l0, a0), (m1, l1, a1) = kv_step(qi, carry, True)
            o_ref[qrows, :] = jnp.where(first, a0 / l0, a1 / l1).astype(o_ref.dtype)
            lse_ref[qrows, :] = jnp.where(first, m0 + jnp.log(l0), m1 + jnp.log(l1))
            return 0
        lax.fori_loop(0, nq, q_block, 0)

    blk = lambda off: pl.BlockSpec((s, LANES), lambda h: (0, off + h))
    vec = pl.BlockSpec((1, LANES), lambda h: (0, 0))
    return pl.pallas_call(
        body, name="attn_fwd", grid=(pairs,),
        in_specs=[blk(0), blk(pairs), blk(2 * pairs), blk(0),
                  pl.BlockSpec((1, 2, s), lambda h: (h, 0, 0)), vec, vec],
        out_specs=[blk(0), blk(0)],
        out_shape=[jax.ShapeDtypeStruct((s, d_model), MXU_DTYPE), jax.ShapeDtypeStruct((s, d_model), F32)],
        scratch_shapes=[pltpu.VMEM((s, LANES), MXU_DTYPE)] * 4,
        compiler_params=_cparams(("arbitrary",)),
    )(proj, proj, proj, f_rep, f_rows, qg, kg)


def _attn_bwd(proj, do, o, lse, f_rep, f_rows, qg, kg, d_model):
    s = proj.shape[0]
    pairs = d_model // LANES
    tq = min(ATT_BLOCK, s)
    nq = s // tq
    scale = HEAD_DIM ** -0.5

    def norm_bwd(raw, g, dn, first):
        _, xn, r = _pair_norm(raw, g, first)
        dxn = dn * g
        t = dxn * xn
        mu0 = jnp.sum(jnp.where(first, t, 0.0), axis=-1, keepdims=True)
        mu1 = jnp.sum(jnp.where(first, 0.0, t), axis=-1, keepdims=True)
        mu = jnp.where(first, mu0, mu1) * (1.0 / HEAD_DIM)
        return r * (dxn - xn * mu), _colsum(dn * xn)

    def body(q_ref, k_ref, v_ref, do_ref, o_ref, lse_ref, frep_ref, frow_ref, qg_ref, kg_ref,
             dq_ref, dk_ref, dv_ref, dfq_ref, dfk_ref, dqg_ref, dkg_ref,
             q0_s, q1_s, kn_s, v_s, do0_s, do1_s, dk_s, dv_s):
        first = _head_masks()
        for ci in range(nq):
            rows = pl.ds(ci * tq, tq)
            qn, _, _ = _pair_norm(q_ref[rows, :], qg_ref[...], first)
            qn = qn * scale
            q0_s[rows, :] = jnp.where(first, qn, 0.0).astype(q0_s.dtype)
            q1_s[rows, :] = jnp.where(first, 0.0, qn).astype(q1_s.dtype)
            kn, _, _ = _pair_norm(k_ref[rows, :], kg_ref[...], first)
            kn_s[rows, :] = kn.astype(kn_s.dtype)
            v_s[rows, :] = v_ref[rows, :].astype(v_s.dtype)
            dov = do_ref[rows, :]
            do0_s[rows, :] = jnp.where(first, dov, jnp.zeros_like(dov))
            do1_s[rows, :] = jnp.where(first, jnp.zeros_like(dov), dov)
        dk_s[...] = jnp.zeros_like(dk_s)
        dv_s[...] = jnp.zeros_like(dv_s)
        dfk_ref[...] = jnp.zeros_like(dfk_ref)

        rid = lax.broadcasted_iota(jnp.int32, (tq, tq), 0)
        cid = lax.broadcasted_iota(jnp.int32, (tq, tq), 1)
        causal = cid <= rid

        def q_block(qi, dqg):
            qrows = pl.ds(pl.multiple_of(qi * tq, tq), tq)
            qs = (q0_s[qrows, :], q1_s[qrows, :])
            dos = (do0_s[qrows, :], do1_s[qrows, :])
            frep = frep_ref[qrows, :]
            lse = lse_ref[qrows, :]
            ov = o_ref[qrows, :].astype(F32)
            fq = (frep[:, 0:1], frep[:, HEAD_DIM:HEAD_DIM + 1])
            ls = (lse[:, 0:1], lse[:, HEAD_DIM:HEAD_DIM + 1])
            dls = tuple(jnp.sum(dos[hd].astype(F32) * ov, axis=-1, keepdims=True) for hd in range(2))

            def kv_step(kj, carry, diag):
                dq, rs = carry[0], list(carry[1:])
                krows = pl.ds(pl.multiple_of(kj * tq, tq), tq)
                k = kn_s[krows, :]
                v = v_s[krows, :]
                fk = frow_ref[0, :, krows]
                dk_add = jnp.zeros((tq, LANES), F32)
                dv_add = jnp.zeros((tq, LANES), F32)
                dfk = []
                for hd in range(2):
                    sc = lax.dot_general(qs[hd], k, _NT, preferred_element_type=F32) + (fq[hd] - fk[hd:hd + 1, :])
                    if diag:
                        sc = jnp.where(causal, sc, NEG)
                    p = jnp.exp(sc - ls[hd])
                    dp = lax.dot_general(dos[hd], v, _NT, preferred_element_type=F32)
                    ds = p * (dp - dls[hd])
                    dsb = ds.astype(k.dtype)
                    dqh = jnp.dot(dsb, k, preferred_element_type=F32)
                    dq = dq + (jnp.where(first, dqh, 0.0) if hd == 0 else jnp.where(first, 0.0, dqh))
                    dk_add = dk_add + lax.dot_general(dsb, qs[hd], _TN, preferred_element_type=F32)
                    dv_add = dv_add + lax.dot_general(p.astype(k.dtype), dos[hd], _TN, preferred_element_type=F32)
                    dfk.append(_colsum(ds))
                    rs[hd] = rs[hd] + jnp.sum(ds, axis=-1, keepdims=True)
                dk_s[krows, :] += dk_add
                dv_s[krows, :] += dv_add
                dfk_ref[0, :, krows] -= jnp.concatenate(dfk, axis=0)
                return (dq, rs[0], rs[1])

            init = (jnp.zeros((tq, LANES), F32), jnp.zeros((tq, 1), F32), jnp.zeros((tq, 1), F32))
            carry = lax.fori_loop(0, qi, lambda kj, cr: kv_step(kj, cr, False), init)
            dq, rs0, rs1 = kv_step(qi, carry, True)
            dfq_ref[qrows, :] = jnp.where(first, rs0, rs1)
            dq_raw, dg = norm_bwd(q_ref[qrows, :], qg_ref[...], dq * scale, first)
            dq_ref[qrows, :] = dq_raw.astype(dq_ref.dtype)
            return dqg + dg
        dqg_ref[0] = lax.fori_loop(0, nq, q_block, jnp.zeros((1, LANES), F32))

        dkg = jnp.zeros((1, LANES), F32)
        for ci in range(nq):
            rows = pl.ds(ci * tq, tq)
            dk_raw, dg = norm_bwd(k_ref[rows, :], kg_ref[...], dk_s[rows, :], first)
            dk_ref[rows, :] = dk_raw.astype(dk_ref.dtype)
            dkg = dkg + dg
            dv_ref[rows, :] = dv_s[rows, :].astype(dv_ref.dtype)
        dkg_ref[0] = dkg

    blk = lambda off: pl.BlockSpec((s, LANES), lambda h: (0, off + h))
    vec = pl.BlockSpec((1, LANES), lambda h: (0, 0))
    frow = pl.BlockSpec((1, 2, s), lambda h: (h, 0, 0))
    gout = pl.BlockSpec((1, 1, LANES), lambda h: (h, 0, 0))
    act = jax.ShapeDtypeStruct((s, d_model), MXU_DTYPE)
    gsh = jax.ShapeDtypeStruct((pairs, 1, LANES), F32)
    return pl.pallas_call(
        body, name="attn_bwd", grid=(pairs,),
        in_specs=[blk(0), blk(pairs), blk(2 * pairs), blk(0), blk(0), blk(0), blk(0), frow, vec, vec],
        out_specs=[blk(0), blk(0), blk(0), blk(0), frow, gout, gout],
        out_shape=[act, act, act, jax.ShapeDtypeStruct((s, d_model), F32),
                   jax.ShapeDtypeStruct((pairs, 2, s), F32), gsh, gsh],
        scratch_shapes=[pltpu.VMEM((s, LANES), MXU_DTYPE)] * 6 + [pltpu.VMEM((s, LANES), F32)] * 2,
        compiler_params=_cparams(("arbitrary",), ATT_BWD_VMEM_LIMIT),
    )(proj, proj, proj, do, o, lse, f_rep, f_rows, qg, kg)


def _pad_cols(a, n):
    return jnp.pad(a, ((0, 0), (0, n - a.shape[1])))


def _fox_fwd(h, w):
    d = h.shape[1]
    pairs = d // LANES
    (proj,) = _mm("fox_in", h, w["w_in"], tm=1024, tn=640)
    f_cum = _gate_fwd(proj, w["b_f"], 3 * pairs)
    f16 = f_cum[:, :d // HEAD_DIM]
    f_rep = jnp.repeat(f16, HEAD_DIM, axis=1)
    o, lse = _attn_fwd_t(proj, f_rep, w["qg"], w["kg"], d)
    (y,) = _mm("fox_out", o, w["w_out"])
    return y, (proj, f_rep, o, lse)


def _fox_bwd(dy, h, w, saved):
    proj, f_rep, o, lse = saved
    s, d = h.shape
    pairs = d // LANES
    (do,) = _mm("fox_do", dy, w["w_out"], tb=True, out_dtypes=(MXU_DTYPE,))
    (dw_out,) = _mm("fox_dwout", o, dy, ta=True, out_dtypes=(MXU_DTYPE,))
    dq, dk, dv, d_f, dqg, dkg = _attn_bwd_t(proj, do, o, lse, f_rep, w["qg"], w["kg"], d)
    dfpre, db_f = _gate_bwd(proj, w["b_f"], d_f, 3 * pairs)
    dproj = jnp.concatenate([dq, dk, dv, dfpre], axis=1)
    (dw_in,) = _mm("fox_dwin", h, dproj, ta=True, out_dtypes=(MXU_DTYPE,), tn=640)
    (dh,) = _mm("fox_dh", dproj, w["w_in"], tb=True)
    fold = lambda g: jnp.sum(g, axis=(0, 1)).reshape(2, HEAD_DIM).sum(axis=0)
    return dh, dict(w_in=dw_in, w_out=dw_out, b_f=db_f[0, :d // HEAD_DIM], qg=fold(dqg), kg=fold(dkg))


_GELU_C = math.sqrt(2.0 / math.pi)


def _gelu(v):
    return 0.5 * v * (1.0 + jnp.tanh(_GELU_C * (v + 0.044715 * v * v * v)))


def _gelu_grad(v):
    t = jnp.tanh(_GELU_C * (v + 0.044715 * v * v * v))
    return 0.5 * (1.0 + t) + 0.5 * v * (1.0 - t * t) * (_GELU_C * (1.0 + 3.0 * 0.044715 * v * v))


def _ln_stats(v):
    mu = jnp.mean(v, axis=-1, keepdims=True)
    vc = v - mu
    r = lax.rsqrt(jnp.mean(vc * vc, axis=-1, keepdims=True) + EPS)
    return vc * r, r


def _sg_mask():
    t = lax.broadcasted_iota(jnp.int32, (SG_CHUNK, SG_CHUNK), 0) // SG_BLOCK
    sidx = lax.broadcasted_iota(jnp.int32, (SG_CHUNK, SG_CHUNK), 1) // SG_BLOCK
    return sidx <= t


def _sgu_fwd(uv_pre, ln_g, ln_b, w_s, b_st):
    s, w2 = uv_pre.shape
    wd = w2 // 2
    tr = min(256, s)

    def fn(uv_pre, ln_g, ln_b, w_s, b_st):
        uv = _gelu(uv_pre)
        u = uv[:, :wd]
        vh, _ = _ln_stats(uv[:, wd:])
        vl = (vh * ln_g + ln_b).astype(MXU_DTYPE)
        mask = _sg_mask()
        cols = []
        for g in range(SG_GROUPS):
            wg = jnp.where(mask, w_s[g * SG_CHUNK:(g + 1) * SG_CHUNK, :], 0.0).astype(MXU_DTYPE)
            parts = []
            for ci in range(tr // SG_CHUNK):
                vt = vl[ci * SG_CHUNK:(ci + 1) * SG_CHUNK, g * SG_CHUNK:(g + 1) * SG_CHUNK]
                parts.append(jnp.dot(wg, vt, preferred_element_type=F32) + b_st[:, g:g + 1])
            cols.append(jnp.concatenate(parts, axis=0) if len(parts) > 1 else parts[0])
        vout = jnp.concatenate(cols, axis=1)
        return (u * vout,), ()
    (m,), _ = _rowwise("sgu_fwd", fn, [uv_pre], [ln_g, ln_b, w_s, b_st], [(wd, MXU_DTYPE)], [], tr)
    return m


def _sgu_bwd(uv_pre, dm, ln_g, ln_b, w_s, b_st):
    s, w2 = uv_pre.shape
    wd = w2 // 2
    tr = min(256, s)

    def fn(uv_pre, dm, ln_g, ln_b, w_s, b_st):
        uv = _gelu(uv_pre)
        u = uv[:, :wd]
        vh, r = _ln_stats(uv[:, wd:])
        vl = (vh * ln_g + ln_b).astype(MXU_DTYPE)
        mask = _sg_mask()
        lane = lax.broadcasted_iota(jnp.int32, (1, LANES), 1)
        cols, dcols, dws, dbs = [], [], [], jnp.zeros((SG_CHUNK, LANES), F32)
        for g in range(SG_GROUPS):
            wg = jnp.where(mask, w_s[g * SG_CHUNK:(g + 1) * SG_CHUNK, :], 0.0).astype(MXU_DTYPE)
            parts, dparts = [], []
            dwg = jnp.zeros((SG_CHUNK, SG_CHUNK), F32)
            dbg = jnp.zeros((SG_CHUNK, 1), F32)
            for ci in range(tr // SG_CHUNK):
                rs = slice(ci * SG_CHUNK, (ci + 1) * SG_CHUNK)
                cs = slice(g * SG_CHUNK, (g + 1) * SG_CHUNK)
                vt = vl[rs, cs]
                parts.append(jnp.dot(wg, vt, preferred_element_type=F32) + b_st[:, g:g + 1])
                dvo = dm[rs, cs] * u[rs, cs]
                dvob = dvo.astype(MXU_DTYPE)
                dparts.append(lax.dot_general(wg, dvob, _TN, preferred_element_type=F32))
                dwg = dwg + lax.dot_general(dvob, vt, _NT, preferred_element_type=F32)
                dbg = dbg + jnp.sum(dvo, axis=-1, keepdims=True)
            cols.append(jnp.concatenate(parts, axis=0) if len(parts) > 1 else parts[0])
            dcols.append(jnp.concatenate(dparts, axis=0) if len(dparts) > 1 else dparts[0])
            dws.append(jnp.where(mask, dwg, 0.0))
            dbs = dbs + jnp.where(lane == g, dbg, 0.0)
        vout = jnp.concatenate(cols, axis=1)
        dvl = jnp.concatenate(dcols, axis=1)
        du = dm * vout
        dlg = _colsum(dvl * vh)
        dlb = _colsum(dvl)
        dvh = dvl * ln_g
        dv = r * (dvh - jnp.mean(dvh, axis=-1, keepdims=True) - vh * jnp.mean(dvh * vh, axis=-1, keepdims=True))
        dpre = jnp.concatenate([du, dv], axis=1) * _gelu_grad(uv_pre)
        return (dpre,), (dlg, dlb, jnp.concatenate(dws, axis=0), dbs)
    (dpre,), reds = _rowwise("sgu_bwd", fn, [uv_pre, dm], [ln_g, ln_b, w_s, b_st], [(w2, MXU_DTYPE)],
                             [(1, wd), (1, wd), (SG_GROUPS * SG_CHUNK, SG_CHUNK), (SG_CHUNK, LANES)], tr)
    return dpre, reds


def _sg_fwd(h, w):
    (uv_pre,) = _mm("sg_in", h, w["w_in"], tm=1024, tn=512)
    m = _sgu_fwd(uv_pre, w["ln_g"], w["ln_b"], w["w_s"], w["b_st"])
    (y,) = _mm("sg_out", m, w["w_out"])
    return y, (uv_pre, m)


def _sg_bwd(dy, h, w, saved):
    uv_pre, m = saved
    (dm,) = _mm("sg_dm", dy, w["w_out"], tb=True)
    (dw_out,) = _mm("sg_dwout", m, dy, ta=True, out_dtypes=(MXU_DTYPE,))
    dpre, (dlg, dlb, dws, dbs) = _sgu_bwd(uv_pre, dm, w["ln_g"], w["ln_b"], w["w_s"], w["b_st"])
    (dw_in,) = _mm("sg_dwin", h, dpre, ta=True, out_dtypes=(MXU_DTYPE,))
    (dh,) = _mm("sg_dh", dpre, w["w_in"], tb=True)
    return dh, dict(w_in=dw_in, w_out=dw_out, ln_g=dlg, ln_b=dlb, w_s=dws, b_s=dbs[:, :SG_GROUPS].T)


def _conv_fwd_kernel(ypad, w_dw, b_dw):
    s = ypad.shape[0] - CONV_PAD
    d = ypad.shape[1]
    tt = min(256, s)
    ext = tt + CONV_PAD

    def body(y_ref, w_ref, b_ref, o_ref):
        def chunk(ci, _):
            base = pl.multiple_of(ci * tt, tt)
            e = y_ref[pl.ds(base, ext), :]
            acc = jnp.zeros((tt, LANES), F32) + b_ref[...]
            for j in range(CONV_WIDTH):
                sh = pltpu.roll(e, ext - (CONV_PAD - CONV_WIDTH + 1 + j), 0)[:tt, :]
                acc = acc + w_ref[j:j + 1, :] * sh
            o_ref[pl.ds(base, tt), :] = acc
            return 0
        lax.fori_loop(0, s // tt, chunk, 0)

    return pl.pallas_call(
        body, name="conv_fwd", grid=(d // LANES,),
        in_specs=[pl.BlockSpec((s + CONV_PAD, LANES), lambda i: (0, i)),
                  pl.BlockSpec((CONV_PAD, LANES), lambda i: (0, i)), pl.BlockSpec((1, LANES), lambda i: (0, i))],
        out_specs=pl.BlockSpec((s, LANES), lambda i: (0, i)),
        out_shape=jax.ShapeDtypeStruct((s, d), F32),
        compiler_params=_cparams(("parallel",)),
    )(ypad, w_dw, b_dw)


def _conv_bwd_kernel(ypad, dpad, w_dw):
    s = ypad.shape[0] - CONV_PAD
    d = ypad.shape[1]
    tt = min(256, s)
    ext = tt + CONV_PAD

    def body(y_ref, d_ref, w_ref, o_ref, dw_ref):
        dw_ref[...] = jnp.zeros_like(dw_ref)

        def chunk(ci, _):
            base = pl.multiple_of(ci * tt, tt)
            ye = y_ref[pl.ds(base, ext), :]
            de = d_ref[pl.ds(base, ext), :]
            dcur = de[:tt, :]
            acc = jnp.zeros((tt, LANES), F32)
            for j in range(CONV_WIDTH):
                back = CONV_WIDTH - 1 - j
                dsh = dcur if back == 0 else pltpu.roll(de, ext - back, 0)[:tt, :]
                acc = acc + w_ref[j:j + 1, :] * dsh
                ysh = pltpu.roll(ye, ext - (CONV_PAD - CONV_WIDTH + 1 + j), 0)[:tt, :]
                dw_ref[j:j + 1, :] += _colsum(dcur * ysh)
            o_ref[pl.ds(base, tt), :] = acc
            return 0
        lax.fori_loop(0, s // tt, chunk, 0)

    return pl.pallas_call(
        body, name="conv_bwd", grid=(d // LANES,),
        in_specs=[pl.BlockSpec((s + CONV_PAD, LANES), lambda i: (0, i)),
                  pl.BlockSpec((s + CONV_PAD, LANES), lambda i: (0, i)),
                  pl.BlockSpec((CONV_PAD, LANES), lambda i: (0, i))],
        out_specs=[pl.BlockSpec((s, LANES), lambda i: (0, i)), pl.BlockSpec((CONV_PAD, LANES), lambda i: (0, i))],
        out_shape=[jax.ShapeDtypeStruct((s, d), F32), jax.ShapeDtypeStruct((CONV_PAD, d), F32)],
        compiler_params=_cparams(("parallel",)),
    )(ypad, dpad, w_dw)


def _cv_fwd(h, w):
    d = h.shape[1]
    (y1,) = _mm("cv_pw1", h, w["w_pw1"], tm=1024, tn=512)

    def glu(y1, b1):
        t = y1 + b1
        return (t[:, :d] * _sigmoid(t[:, d:]),), ()
    (y2,), _ = _rowwise("cv_glu", glu, [y1], [w["b_pw1"]], [(d, F32)], [], 256)
    y3 = _conv_fwd_kernel(jnp.pad(y2, ((CONV_PAD, 0), (0, 0))), w["w_dw"], w["b_dw"])

    def lnsilu(y3, g, b):
        vh, _ = _ln_stats(y3)
        y4 = vh * g + b
        return (y4 * _sigmoid(y4),), ()
    (y5,), _ = _rowwise("cv_lnsilu", lnsilu, [y3], [w["ln_g"], w["ln_b"]], [(d, MXU_DTYPE)], [], 256)
    (y,) = _mm("cv_pw2", y5, w["w_pw2"], epi=lambda acc, b: (acc + b,), vecs=(w["b_pw2"],))
    return y, (y1, y2, y3, y5)


def _cv_bwd(dy, h, w, saved):
    y1, y2, y3, y5 = saved
    d = h.shape[1]
    (dy5,) = _mm("cv_dy5", dy, w["w_pw2"], tb=True)
    (dw_pw2,) = _mm("cv_dwpw2", y5, dy, ta=True, out_dtypes=(MXU_DTYPE,))

    def ln_bwd(dy5, y3, dyb, g, b):
        vh, r = _ln_stats(y3)
        y4 = vh * g + b
        sg = _sigmoid(y4)
        dy4 = dy5 * (sg * (1.0 + y4 * (1.0 - sg)))
        dvh = dy4 * g
        dy3 = r * (dvh - jnp.mean(dvh, axis=-1, keepdims=True) - vh * jnp.mean(dvh * vh, axis=-1, keepdims=True))
        return (dy3,), (_colsum(dy4 * vh), _colsum(dy4), _colsum(dy3), _colsum(dyb.astype(F32)))
    (dy3,), (dlg, dlb, db_dw, db_pw2) = _rowwise("cv_ln_bwd", ln_bwd, [dy5, y3, dy], [w["ln_g"], w["ln_b"]],
                                                 [(d, F32)], [(1, d)] * 4, 256)
    dy2, dw_dw = _conv_bwd_kernel(jnp.pad(y2, ((CONV_PAD, 0), (0, 0))), jnp.pad(dy3, ((0, CONV_PAD), (0, 0))),
                                  w["w_dw"])

    def glu_bwd(y1, dy2, b1):
        t = y1 + b1
        a, sg = t[:, :d], _sigmoid(t[:, d:])
        dy1 = jnp.concatenate([dy2 * sg, dy2 * a * sg * (1.0 - sg)], axis=1)
        return (dy1,), (_colsum(dy1),)
    (dy1,), (db_pw1,) = _rowwise("cv_glu_bwd", glu_bwd, [y1, dy2], [w["b_pw1"]], [(2 * d, MXU_DTYPE)],
                                 [(1, 2 * d)], 256)
    (dw_pw1,) = _mm("cv_dwpw1", h, dy1, ta=True, out_dtypes=(MXU_DTYPE,))
    (dh,) = _mm("cv_dh", dy1, w["w_pw1"], tb=True)
    return dh, dict(w_pw1=dw_pw1, w_pw2=dw_pw2, b_pw1=db_pw1, b_pw2=db_pw2, w_dw=dw_dw[:CONV_WIDTH],
                    b_dw=db_dw, ln_g=dlg, ln_b=dlb)


def _ada_outer(c_t, dmod):
    def fn(c_t, dmod):
        acc = c_t[:, 0:1] * dmod[0:1, :]
        for b in range(1, N_DEV):
            acc = acc + c_t[:, b:b + 1] * dmod[b:b + 1, :]
        return (acc,), ()
    (g,), _ = _rowwise("ada_outer", fn, [c_t], [dmod], [(dmod.shape[1], F32)], [], 256)
    return g


def _adamw(name, parts, w, m, v, tr):
    npart = parts.shape[0]
    cols = w.shape[1]

    def fn(parts, w, m, v):
        g = parts[0].astype(F32)
        for q in range(1, npart):
            g = g + parts[q].astype(F32)
        m_new = ADAM_B1 * m + (1.0 - ADAM_B1) * g
        v_new = ADAM_B2 * v + (1.0 - ADAM_B2) * (g * g)
        m_hat = m_new / (1.0 - ADAM_B1 ** ADAM_STEP)
        v_hat = v_new / (1.0 - ADAM_B2 ** ADAM_STEP)
        delta = -ADAM_LR * (m_hat / (jnp.sqrt(v_hat) + ADAM_EPS) + ADAM_WD * w)
        return (g, delta, m_new, v_new), ()
    rows = w.shape[0]
    tr = min(tr, rows)

    def body(p_ref, w_ref, m_ref, v_ref, g_o, d_o, m_o, v_o):
        outs, _ = fn(p_ref[...], w_ref[...], m_ref[...], v_ref[...])
        for o_ref, o in zip((g_o, d_o, m_o, v_o), outs):
            o_ref[...] = o

    spec = pl.BlockSpec((tr, cols), lambda i: (i, 0))
    return pl.pallas_call(
        body, name=name, grid=(rows // tr,),
        in_specs=[pl.BlockSpec((npart, tr, cols), lambda i: (0, i, 0)), spec, spec, spec],
        out_specs=[spec] * 4, out_shape=[jax.ShapeDtypeStruct((rows, cols), F32)] * 4,
        compiler_params=_cparams(("parallel",)),
    )(parts, w, m, v)


def _pack(arrays):
    flat = jnp.concatenate([a.reshape(-1).astype(F32) for a in arrays])
    n = flat.shape[0]
    rows = -(-n // (8 * LANES)) * 8
    return jnp.pad(flat, (0, rows * LANES - n)).reshape(rows, LANES)


def _unpack(buf, shapes, lead=()):
    flat = buf.reshape(lead + (-1,))
    out, off = [], 0
    for shp in shapes:
        n = math.prod(shp)
        out.append(flat[..., off:off + n].reshape(lead + tuple(shp)))
        off += n
    return out


ADAM_TILE_ELEMS = 1 << 17


def _row_tile(rows, cols):
    want = max(8, ADAM_TILE_ELEMS // max(cols, LANES))
    if rows <= want:
        return rows
    best = None
    for t in range(8, want + 1, 8):
        if rows % t == 0:
            best = t
    assert best is not None, (rows, cols)
    return best


def _local_step(xs, tgt, mods, norm_mix, norm_mlp, fetch, send):
    depth = len(mods)
    mixer_fwd = (_fox_fwd, _sg_fwd, _cv_fwd)
    mixer_bwd = (_fox_bwd, _sg_bwd, _cv_bwd)
    mw, w1, w2 = [None] * depth, [None] * depth, [None] * depth

    nsub = 2 * depth
    sub = []
    x_in = xs
    y_prev = gate_prev = None
    for k in range(nsub):
        i, is_mlp = k // 2, k % 2
        sh, sc = mods[i][3 * is_mlp], mods[i][3 * is_mlp + 1]
        g = (norm_mlp if is_mlp else norm_mix)[i:i + 1]
        if not is_mlp:
            (mw[i], w1[i], w2[i]), token = fetch(i, xs if k == 0 else y_prev)
            g = g + token
        if k == 0:
            h = _first_norm(x_in, g, sc, sh)
        else:
            x_in, h = _res_norm(x_in, y_prev, gate_prev, g, sc, sh)
        if is_mlp:
            y, saved = _mlp_fwd(h, w1[i], w2[i])
        else:
            y, saved = mixer_fwd[i % 3](h, mw[i])
        sub.append((x_in, h, y, saved))
        y_prev, gate_prev = y, mods[i][3 * is_mlp + 2]

    loss_part, dxo, dy, dgate = _final_loss(x_in, y_prev, gate_prev, tgt)

    dmods = [[None] * 6 for _ in range(depth)]
    g_norm = {'norm_mix': [None] * depth, 'norm_mlp': [None] * depth}
    g_mix = [None] * depth
    g_w1, g_w2 = [None] * depth, [None] * depth
    for k in reversed(range(nsub)):
        i, is_mlp = k // 2, k % 2
        x_k, h_k, _, saved = sub[k]
        dmods[i][3 * is_mlp + 2] = dgate
        if is_mlp:
            dh, g_w1[i], g_w2[i] = _mlp_bwd(dy, h_k, w1[i], w2[i], saved)
        else:
            dh, g_mix[i] = mixer_bwd[i % 3](dy, h_k, mw[i], saved)
        sc = mods[i][3 * is_mlp + 1]
        g = (norm_mlp if is_mlp else norm_mix)[i:i + 1]
        if not is_mlp:
            g = g + send(i, g_w1[i], g_w2[i], g_mix[i])
        if k > 0:
            ip, mp = (k - 1) // 2, (k - 1) % 2
            dxo, dy, (dsh, dsc, dg, dgate) = _bwd_norm_gate(dxo, dh, x_k, sub[k - 1][2], g, sc, mods[ip][3 * mp + 2])
        else:
            dxo, (dsh, dsc, dg) = _bwd_norm_first(dxo, dh, x_k, g, sc)
        dmods[i][3 * is_mlp], dmods[i][3 * is_mlp + 1] = dsh, dsc
        g_norm['norm_mlp' if is_mlp else 'norm_mix'][i] = dg
    return loss_part, dxo, dmods, g_norm, g_mix, g_w1, g_w2


def kernel(x, c, norm_mix, norm_mlp, w_ada, b_ada, w_mlp_in, w_mlp_out, fox_w_in, fox_b_f, fox_q_norm, fox_k_norm, fox_w_out, sg_w_in, sg_ln_g, sg_ln_b, sg_w_s, sg_b_s, sg_w_out, cv_w_pw1, cv_b_pw1, cv_w_dw, cv_b_dw, cv_ln_g, cv_ln_b, cv_w_pw2, cv_b_pw2, loss_target, m_norm_mix, m_norm_mlp, m_w_ada, m_b_ada, m_w_mlp_in, m_w_mlp_out, m_fox_w_in, m_fox_b_f, m_fox_q_norm, m_fox_k_norm, m_fox_w_out, m_sg_w_in, m_sg_ln_g, m_sg_ln_b, m_sg_w_s, m_sg_b_s, m_sg_w_out, m_cv_w_pw1, m_cv_b_pw1, m_cv_w_dw, m_cv_b_dw, m_cv_ln_g, m_cv_ln_b, m_cv_w_pw2, m_cv_b_pw2, v_norm_mix, v_norm_mlp, v_w_ada, v_b_ada, v_w_mlp_in, v_w_mlp_out, v_fox_w_in, v_fox_b_f, v_fox_q_norm, v_fox_k_norm, v_fox_w_out, v_sg_w_in, v_sg_ln_g, v_sg_ln_b, v_sg_w_s, v_sg_b_s, v_sg_w_out, v_cv_w_pw1, v_cv_b_pw1, v_cv_w_dw, v_cv_b_dw, v_cv_ln_g, v_cv_ln_b, v_cv_w_pw2, v_cv_b_pw2):
    P = dict(zip(_ARGS, (x, c, norm_mix, norm_mlp, w_ada, b_ada, w_mlp_in, w_mlp_out, fox_w_in, fox_b_f, fox_q_norm, fox_k_norm, fox_w_out, sg_w_in, sg_ln_g, sg_ln_b, sg_w_s, sg_b_s, sg_w_out, cv_w_pw1, cv_b_pw1, cv_w_dw, cv_b_dw, cv_ln_g, cv_ln_b, cv_w_pw2, cv_b_pw2, loss_target, m_norm_mix, m_norm_mlp, m_w_ada, m_b_ada, m_w_mlp_in, m_w_mlp_out, m_fox_w_in, m_fox_b_f, m_fox_q_norm, m_fox_k_norm, m_fox_w_out, m_sg_w_in, m_sg_ln_g, m_sg_ln_b, m_sg_w_s, m_sg_b_s, m_sg_w_out, m_cv_w_pw1, m_cv_b_pw1, m_cv_w_dw, m_cv_b_dw, m_cv_ln_g, m_cv_ln_b, m_cv_w_pw2, m_cv_b_pw2, v_norm_mix, v_norm_mlp, v_w_ada, v_b_ada, v_w_mlp_in, v_w_mlp_out, v_fox_w_in, v_fox_b_f, v_fox_q_norm, v_fox_k_norm, v_fox_w_out, v_sg_w_in, v_sg_ln_g, v_sg_ln_b, v_sg_w_s, v_sg_b_s, v_sg_w_out, v_cv_w_pw1, v_cv_b_pw1, v_cv_w_dw, v_cv_b_dw, v_cv_ln_g, v_cv_ln_b, v_cv_w_pw2, v_cv_b_pw2)))
    me = 4 * lax.axis_index("x") + 2 * lax.axis_index("y") + lax.axis_index("c")
    xs = x[0]
    tgt = loss_target[0]
    s_len, d = xs.shape
    depth = norm_mix.shape[0]
    n_fox, n_sg, n_cv = fox_w_in.shape[0], sg_w_in.shape[0], cv_w_pw1.shape[0]
    heads = d // HEAD_DIM
    bf = lambda a: a.astype(MXU_DTYPE)

    cv_small = ['cv_b_pw1', 'cv_w_dw', 'cv_b_dw', 'cv_ln_g', 'cv_ln_b', 'cv_b_pw2']
    small_shapes = [c.shape] + [P[n].shape for n in cv_small]
    (small_all,) = _exchange("gather_small", [_pack([c] + [P[n] for n in cv_small])], scatter=False)
    sm = dict(zip(['c'] + cv_small, _unpack(small_all, small_shapes, lead=(N_DEV,))))
    c_all = sm['c'][:, 0, :]
    cat_last = lambda a: jnp.moveaxis(a, 0, -2).reshape(a.shape[1:-1] + (-1,))
    cvf = {n: cat_last(sm[n]) for n in cv_small}

    big = ['w_mlp_in', 'w_mlp_out', 'fox_w_in', 'fox_w_out', 'sg_w_in', 'sg_w_out', 'cv_w_pw1', 'cv_w_pw2']
    col_sharded = {'w_mlp_in', 'fox_w_in', 'sg_w_in', 'cv_w_pw1'}
    mixer_names = (('fox_w_in', 'fox_w_out'), ('sg_w_in', 'sg_w_out'), ('cv_w_pw1', 'cv_w_pw2'))
    layer_parts = [[('w_mlp_in', i), ('w_mlp_out', i)] + [(nm, i // 3) for nm in mixer_names[i % 3]]
                   for i in range(depth)]
    gather_handles = [None] * depth
    scatter_handles = [None] * depth
    gather_handles[0], _ = _exchange_start("gather_start_0", [bf(P[nm][j]) for nm, j in layer_parts[0]], False, c)

    c_act = c_all * _sigmoid(c_all)
    c_pad = bf(jnp.pad(c_act, ((0, 16 - N_DEV), (0, 0))))
    n_ada = w_ada.shape[2]
    (mod_part,) = _mm("ada_mod", c_pad, bf(jnp.transpose(w_ada, (1, 0, 2)).reshape(d, depth * n_ada)),
                      epi=lambda acc, b: (acc + b,),
                      vecs=(lax.dynamic_slice_in_dim(b_ada, me * n_ada, n_ada, axis=1).reshape(1, depth * n_ada),),
                      tn=n_ada)
    (mod_all,) = _exchange("gather_mod", [mod_part], scatter=False)
    mod_me = lax.dynamic_index_in_dim(mod_all, me, axis=1, keepdims=False)
    mod = jnp.transpose(mod_me.reshape(N_DEV, depth, n_ada), (1, 0, 2)).reshape(depth, 6 * d)
    mods = [[mod[i:i + 1, k * d:(k + 1) * d] for k in range(6)] for i in range(depth)]

    def mixer_weights(i, full_weight):
        kind, j = i % 3, i // 3
        if kind == 0:
            w_in = full_weight('fox_w_in')
            n_pad = -(-w_in.shape[1] // (5 * LANES)) * (5 * LANES)
            return dict(w_in=_pad_cols(w_in, n_pad), w_out=full_weight('fox_w_out'),
                        b_f=_pad_cols(fox_b_f[j:j + 1], LANES),
                        qg=jnp.tile(fox_q_norm[j:j + 1], (1, 2)), kg=jnp.tile(fox_k_norm[j:j + 1], (1, 2)))
        if kind == 1:
            return dict(w_in=full_weight('sg_w_in'), w_out=full_weight('sg_w_out'),
                        ln_g=sg_ln_g[j:j + 1], ln_b=sg_ln_b[j:j + 1],
                        w_s=sg_w_s[j].reshape(SG_GROUPS * SG_CHUNK, SG_CHUNK), b_st=_pad_cols(sg_b_s[j].T, LANES))
        return dict(w_pw1=full_weight('cv_w_pw1'), w_pw2=full_weight('cv_w_pw2'),
                    b_pw1=cvf['cv_b_pw1'][j:j + 1], b_pw2=cvf['cv_b_pw2'][j:j + 1],
                    w_dw=jnp.pad(cvf['cv_w_dw'][j], ((0, CONV_PAD - CONV_WIDTH), (0, 0))),
                    b_dw=cvf['cv_b_dw'][j:j + 1], ln_g=cvf['cv_ln_g'][j:j + 1], ln_b=cvf['cv_ln_b'][j:j + 1])

    def fetch(i, dep):
        got = _exchange_wait(f"gather_wait_{i}", gather_handles[i], False, dep)
        token = jnp.zeros((1, 1), F32)
        if i + 1 < depth:
            gather_handles[i + 1], token = _exchange_start(
                f"gather_start_{i + 1}", [bf(P[nm][j]) for nm, j in layer_parts[i + 1]], False, got[0])
        by_name = {nm: g for (nm, _), g in zip(layer_parts[i], got)}

        def full_weight(name):
            g = by_name[name]
            if name in col_sharded:
                return jnp.transpose(g, (1, 0, 2)).reshape(g.shape[1], -1)
            return g.reshape(-1, g.shape[2])
        return (mixer_weights(i, full_weight), full_weight('w_mlp_in'), full_weight('w_mlp_out')), token

    def to_slots(name, g2d):
        if name in col_sharded:
            r = g2d.shape[0]
            return jnp.transpose(g2d.reshape(r, N_DEV, -1), (1, 0, 2))
        return g2d.reshape(N_DEV, -1, g2d.shape[1])

    def send(i, g_w1, g_w2, g_mix):
        key = {'fox_w_in': 'w_in', 'fox_w_out': 'w_out', 'sg_w_in': 'w_in', 'sg_w_out': 'w_out',
               'cv_w_pw1': 'w_pw1', 'cv_w_pw2': 'w_pw2'}
        slots = []
        for nm, _ in layer_parts[i]:
            g2d = g_w1 if nm == 'w_mlp_in' else g_w2 if nm == 'w_mlp_out' else g_mix[key[nm]]
            slots.append(to_slots(nm, g2d[:, :P[nm].shape[-1] * N_DEV] if nm in col_sharded else g2d))
        scatter_handles[i], token = _exchange_start(f"scatter_start_{i}", slots, True, slots[0])
        return token

    loss_part, dxo, dmods, g_norm, g_mix, _, _ = _local_step(xs, tgt, mods, norm_mix, norm_mlp, fetch, send)
    loss = lax.psum(loss_part, ("x", "y", "c"))
    grad_x = dxo[None]

    stack = lambda key, kind: jnp.stack([g_mix[i][key].reshape(P[name_of[(kind, key)]].shape[1:])
                                         for i in range(depth) if i % 3 == kind])
    name_of = {(0, 'b_f'): 'fox_b_f', (0, 'qg'): 'fox_q_norm', (0, 'kg'): 'fox_k_norm',
               (1, 'ln_g'): 'sg_ln_g', (1, 'ln_b'): 'sg_ln_b', (1, 'w_s'): 'sg_w_s', (1, 'b_s'): 'sg_b_s'}
    dmod_me = jnp.concatenate([jnp.concatenate(r, axis=1) for r in dmods], axis=0)
    small_g = {'dmod': dmod_me,
               'norm_mix': jnp.concatenate(g_norm['norm_mix'], axis=0),
               'norm_mlp': jnp.concatenate(g_norm['norm_mlp'], axis=0)}
    for (kind, key), nm in name_of.items():
        small_g[nm] = stack(key, kind)
    cv_keys = {'cv_b_pw1': 'b_pw1', 'cv_w_dw': 'w_dw', 'cv_b_dw': 'b_dw', 'cv_ln_g': 'ln_g', 'cv_ln_b': 'ln_b',
               'cv_b_pw2': 'b_pw2'}
    for nm, key in cv_keys.items():
        small_g[nm] = jnp.stack([g_mix[i][key].reshape(cvf[nm].shape[1:]) for i in range(depth) if i % 3 == 2])
    sg_names = list(small_g)
    sg_shapes = [small_g[n].shape for n in sg_names]
    (sg_all,) = _exchange("gather_small_grads", [_pack([small_g[n] for n in sg_names])], scatter=False)

    dmod_all = _unpack(sg_all, sg_shapes, lead=(N_DEV,))[0]
    out = {}

    def finish(name, parts, shard_of=None):
        w, m, v = P[name], P['m_' + name], P['v_' + name]
        cols = w.shape[-1]
        r2 = lambda a: a.reshape(-1, cols)
        rows = r2(w).shape[0]
        res = _adamw("adamw_" + name, parts.reshape(parts.shape[0], rows, cols), r2(w), r2(m), r2(v),
                     _row_tile(rows, cols))
        out[name] = tuple(r.reshape(w.shape) for r in res)

    c_t = c_act.T
    ada_g = []
    for i in range(depth):
        blk = lax.dynamic_slice_in_dim(dmod_all[:, i, :], me * n_ada, n_ada, axis=1)
        ada_g.append(_ada_outer(c_t, blk))
    finish('w_ada', jnp.stack(ada_g)[None])
    finish('b_ada', dmod_all)

    sm_names = [n for n in sg_names if n != 'dmod']
    sm_parts = jnp.stack([_pack([_unpack(sg_all[q], sg_shapes)[sg_names.index(n)] for n in sm_names])
                          for q in range(N_DEV)])

    def local_block(nm, a):
        if nm in cv_keys:
            n_loc = P[nm].shape[-1]
            return lax.dynamic_slice_in_dim(a, me * n_loc, n_loc, axis=a.ndim - 1)
        return a
    full_shapes = [small_g[n].shape for n in sm_names]

    def pack_full(prefix):
        arrs = []
        for nm in sm_names:
            a = P[prefix + nm]
            if nm in cv_keys:
                full = jnp.zeros(small_g[nm].shape, F32)
                a = lax.dynamic_update_slice_in_dim(full, a, me * a.shape[-1], axis=a.ndim - 1)
            arrs.append(a)
        return _pack(arrs)
    res = _adamw("adamw_small", sm_parts, pack_full(''), pack_full('m_'), pack_full('v_'),
                 _row_tile(sm_parts.shape[1], LANES))
    unp = [_unpack(r, full_shapes) for r in res]
    for idx, nm in enumerate(sm_names):
        out[nm] = tuple(local_block(nm, unp[t][idx]) for t in range(4))

    per_layer = {}
    last = dxo
    for i in reversed(range(depth)):
        recv = _exchange_wait(f"scatter_wait_{i}", scatter_handles[i], True, last)
        for (nm, j), parts in zip(layer_parts[i], recv):
            w, m, v = P[nm][j], P['m_' + nm][j], P['v_' + nm][j]
            per_layer[(nm, j)] = _adamw(f"adamw_{nm}_{j}", parts, w, m, v, _row_tile(*w.shape))
            last = per_layer[(nm, j)][0]
    for nm in big:
        out[nm] = tuple(jnp.stack([per_layer[(nm, j)][t] for j in range(P[nm].shape[0])]) for t in range(4))

    outs = [loss, grad_x]
    for t in range(4):
        outs += [out[n][t] for n in _WEIGHTS]
    return tuple(outs)
```

```python
import functools
import math

import jax
import jax.numpy as jnp
from jax import lax
from jax.experimental import pallas as pl
from jax.experimental.pallas import tpu as pltpu

F32 = jnp.float32
MXU_DTYPE = jnp.bfloat16
EPS = 1e-6
N_DEV = 8
HEAD_DIM = 64
LANES = 128
CONV_WIDTH = 31
CONV_PAD = 32
SG_CHUNK = 128
SG_BLOCK = 64
SG_GROUPS = 8
ATT_BLOCK = 256
SCAN_BLOCK = 256
VMEM_LIMIT = 48 * 1024 * 1024
ATT_BWD_VMEM_LIMIT = 56 * 1024 * 1024
ADAM_LR, ADAM_B1, ADAM_B2, ADAM_EPS, ADAM_WD, ADAM_STEP = 0.001, 0.9, 0.999, 1e-08, 0.01, 10
NEG = -1e30

_WEIGHTS = ['norm_mix', 'norm_mlp', 'w_ada', 'b_ada', 'w_mlp_in', 'w_mlp_out', 'fox_w_in', 'fox_b_f',
            'fox_q_norm', 'fox_k_norm', 'fox_w_out', 'sg_w_in', 'sg_ln_g', 'sg_ln_b', 'sg_w_s', 'sg_b_s',
            'sg_w_out', 'cv_w_pw1', 'cv_b_pw1', 'cv_w_dw', 'cv_b_dw', 'cv_ln_g', 'cv_ln_b', 'cv_w_pw2',
            'cv_b_pw2']
_ARGS = ['x', 'c'] + _WEIGHTS + ['loss_target'] + ['m_' + n for n in _WEIGHTS] + ['v_' + n for n in _WEIGHTS]


def _cparams(sem=None, vmem=VMEM_LIMIT):
    return pltpu.CompilerParams(dimension_semantics=sem, vmem_limit_bytes=vmem)


def _colsum(v):
    return jnp.sum(v, axis=0, keepdims=True)


def _sigmoid(v):
    return 1.0 / (1.0 + jnp.exp(-v))


def _rowwise(name, fn, rows, consts, row_out, red_out, tr):
    n_rows = rows[0].shape[0]
    tr = min(tr, n_rows)
    assert n_rows % tr == 0
    nr, nc, no = len(rows), len(consts), len(row_out)

    def body(*refs):
        ins = [r[...] for r in refs[:nr + nc]]
        outs, reds = fn(*ins)
        out_refs = refs[nr + nc:nr + nc + no]
        red_refs = refs[nr + nc + no:]
        for o_ref, o in zip(out_refs, outs):
            o_ref[...] = o.astype(o_ref.dtype)
        if red_refs:
            @pl.when(pl.program_id(0) == 0)
            def _():
                for r_ref in red_refs:
                    r_ref[...] = jnp.zeros_like(r_ref)
            for r_ref, r in zip(red_refs, reds):
                r_ref[...] += r

    def rspec(a):
        return pl.BlockSpec((tr,) + a.shape[1:], lambda i: (i,) + (0,) * (a.ndim - 1))

    def cspec(shape):
        return pl.BlockSpec(shape, lambda i: (0,) * len(shape))

    out_shape = [jax.ShapeDtypeStruct((n_rows, w), dt) for w, dt in row_out]
    out_shape += [jax.ShapeDtypeStruct(s, F32) for s in red_out]
    out_specs = [pl.BlockSpec((tr, w), lambda i: (i, 0)) for w, _ in row_out] + [cspec(s) for s in red_out]
    res = pl.pallas_call(
        body, name=name, grid=(n_rows // tr,),
        in_specs=[rspec(a) for a in rows] + [cspec(a.shape) for a in consts],
        out_specs=out_specs, out_shape=out_shape,
        compiler_params=_cparams(("arbitrary",)),
    )(*rows, *consts)
    return res[:no], res[no:]


def _mm(name, a, b, *, ta=False, tb=False, out_dtypes=(F32,), epi=None, tiles=(), vecs=(), tm=512, tn=512):
    m_dim, k_dim = (a.shape[1], a.shape[0]) if ta else a.shape
    n_dim = b.shape[0] if tb else b.shape[1]
    assert (b.shape[1] if tb else b.shape[0]) == k_dim
    tm, tn = min(tm, m_dim), min(tn, n_dim)
    assert m_dim % tm == 0 and n_dim % tn == 0, (name, m_dim, n_dim, tm, tn)
    dims = (((0 if ta else 1,), (1 if tb else 0,)), ((), ()))
    nx = len(tiles) + len(vecs)

    def body(a_ref, b_ref, *rest):
        acc = lax.dot_general(a_ref[...], b_ref[...], dims, preferred_element_type=F32)
        outs = epi(acc, *[r[...] for r in rest[:nx]]) if epi is not None else (acc,)
        for o_ref, o in zip(rest[nx:], outs):
            o_ref[...] = o.astype(o_ref.dtype)

    a_spec = pl.BlockSpec((k_dim, tm), lambda i, j: (0, i)) if ta else pl.BlockSpec((tm, k_dim), lambda i, j: (i, 0))
    b_spec = pl.BlockSpec((tn, k_dim), lambda i, j: (j, 0)) if tb else pl.BlockSpec((k_dim, tn), lambda i, j: (0, j))
    t_spec = pl.BlockSpec((tm, tn), lambda i, j: (i, j))
    v_spec = pl.BlockSpec((1, tn), lambda i, j: (0, j))
    res = pl.pallas_call(
        body, name=name, grid=(m_dim // tm, n_dim // tn),
        in_specs=[a_spec, b_spec] + [t_spec] * len(tiles) + [v_spec] * len(vecs),
        out_specs=[t_spec] * len(out_dtypes),
        out_shape=[jax.ShapeDtypeStruct((m_dim, n_dim), dt) for dt in out_dtypes],
        compiler_params=_cparams(("parallel", "parallel")),
    )(a, b, *tiles, *vecs)
    return res


def _exchange_copies(scatter, in_refs, land_refs, send_sems, recv_sems, local_sems):
    n = len(in_refs)
    x, y, c = lax.axis_index("x"), lax.axis_index("y"), lax.axis_index("c")
    me = 4 * x + 2 * y + c
    local = [pltpu.make_async_copy(in_refs[a].at[me] if scatter else in_refs[a], land_refs[a].at[me],
                                   local_sems.at[a]) for a in range(n)]
    send, arrive = [], []
    for k in range(1, N_DEV):
        px, py, pc = x ^ ((k >> 2) & 1), y ^ ((k >> 1) & 1), c ^ (k & 1)
        peer = 4 * px + 2 * py + pc
        for a in range(n):
            src = in_refs[a].at[peer] if scatter else in_refs[a]
            sems = dict(send_sem=send_sems.at[a * (N_DEV - 1) + k - 1], recv_sem=recv_sems.at[a * (N_DEV - 1) + k - 1],
                        device_id=(px, py, pc), device_id_type=pl.DeviceIdType.MESH)
            send.append(pltpu.make_async_remote_copy(src_ref=src, dst_ref=land_refs[a].at[me], **sems))
            arrive.append(pltpu.make_async_remote_copy(src_ref=src, dst_ref=land_refs[a].at[peer], **sems))
    return local, send, arrive


def _land_shape(a, scatter):
    return ((N_DEV,) + a.shape[1:]) if scatter else ((N_DEV,) + a.shape)


def _exchange(name, arrays, scatter):
    n = len(arrays)

    def body(*refs):
        local, send, arrive = _exchange_copies(scatter, refs[:n], refs[n:2 * n], *refs[2 * n:])
        for cp in local + send:
            cp.start()
        for cp, arr in zip(send, arrive):
            cp.wait_send()
            arr.wait_recv()
        for cp in local:
            cp.wait()

    any_spec = pl.BlockSpec(memory_space=pl.ANY)
    return pl.pallas_call(
        body, name=name,
        in_specs=[any_spec] * n, out_specs=[any_spec] * n,
        out_shape=[jax.ShapeDtypeStruct(_land_shape(a, scatter), a.dtype) for a in arrays],
        scratch_shapes=[pltpu.SemaphoreType.DMA((n * (N_DEV - 1),)),
                        pltpu.SemaphoreType.DMA((n * (N_DEV - 1),)),
                        pltpu.SemaphoreType.DMA((n,))],
        compiler_params=pltpu.CompilerParams(has_side_effects=True),
    )(*arrays)


_HBM = pl.BlockSpec(memory_space=pltpu.HBM)
_SEM = pl.BlockSpec(memory_space=pltpu.SEMAPHORE)
_EFFECT = pltpu.SideEffectType.DATAFLOW_SIDE_EFFECTING


def _exchange_start(name, arrays, scatter, dep):
    n = len(arrays)
    nsem = n * (N_DEV - 1)
    srcs = [pltpu.with_memory_space_constraint(a, pltpu.HBM) for a in arrays]
    lands = [pltpu.with_memory_space_constraint(lax.empty(_land_shape(a, scatter), a.dtype), pltpu.HBM) for a in arrays]

    def body(*refs):
        sems = refs[2 * n + 1:2 * n + 4]
        local, send, _ = _exchange_copies(scatter, refs[:n], refs[n:2 * n], *sems)
        for cp in local + send:
            cp.start()
        token = refs[-1]
        token[...] = jnp.zeros_like(token)

    res = pl.pallas_call(
        body, name=name,
        in_specs=[_HBM] * (2 * n) + [pl.BlockSpec(memory_space=pl.ANY)],
        out_specs=[_SEM] * 3 + [_HBM] * (2 * n) + [pl.BlockSpec(memory_space=pltpu.VMEM)],
        out_shape=[pltpu.SemaphoreType.DMA((nsem,)), pltpu.SemaphoreType.DMA((nsem,)), pltpu.SemaphoreType.DMA((n,))]
        + [pltpu.HBM(a.shape, a.dtype) for a in arrays]
        + [pltpu.HBM(_land_shape(a, scatter), a.dtype) for a in arrays]
        + [jax.ShapeDtypeStruct((8, LANES), F32)],
        input_output_aliases={i: 3 + i for i in range(2 * n)},
        compiler_params=pltpu.CompilerParams(has_side_effects=_EFFECT),
    )(*srcs, *lands, dep)
    return res[:-1], res[-1][0:1, 0:1]


def _exchange_wait(name, handles, scatter, after):
    n = (len(handles) - 3) // 2
    sems, thru = handles[:3], handles[3:]

    def body(*refs):
        local, send, arrive = _exchange_copies(scatter, refs[:n], refs[n:2 * n], *refs[2 * n:2 * n + 3])
        for cp, arr in zip(send, arrive):
            cp.wait_send()
            arr.wait_recv()
        for cp in local:
            cp.wait()

    res = pl.pallas_call(
        body, name=name,
        in_specs=[_HBM] * (2 * n) + [_SEM] * 3 + [pl.BlockSpec(memory_space=pl.ANY)],
        out_specs=[_HBM] * (2 * n),
        out_shape=[pltpu.HBM(t.shape, t.dtype) for t in thru],
        input_output_aliases={i: i for i in range(2 * n)},
        compiler_params=pltpu.CompilerParams(has_side_effects=_EFFECT),
    )(*thru, *sems, after)
    return res[n:]


def _norm_mod(x, g, sc, sh):
    r = lax.rsqrt(jnp.mean(x * x, axis=-1, keepdims=True) + EPS)
    return (x * r * g) * (1.0 + sc) + sh


def _first_norm(x, g, sc, sh):
    (h,), _ = _rowwise("first_norm", lambda x, g, sc, sh: ((_norm_mod(x, g, sc, sh),), ()),
                       [x], [g, sc, sh], [(x.shape[1], MXU_DTYPE)], [], 256)
    return h


def _res_norm(x, y, gate, g, sc, sh):
    def fn(x, y, gate, g, sc, sh):
        xn = x + gate * y
        return (xn, _norm_mod(xn, g, sc, sh)), ()
    (xn, h), _ = _rowwise("res_norm", fn, [x, y], [gate, g, sc, sh],
                          [(x.shape[1], F32), (x.shape[1], MXU_DTYPE)], [], 256)
    return xn, h


def _final_loss(x, y, gate, target):
    d = x.shape[1]

    def fn(x, y, target, gate):
        err = (x + gate * y) - target
        part = jnp.sum(jnp.sum(err * err, axis=-1, keepdims=True), axis=0, keepdims=True) * (0.5 / d)
        dx = err * (1.0 / d)
        return (dx, dx * gate), (jnp.broadcast_to(part, (1, LANES)), _colsum(dx * y))
    (dx, dy), (loss, dgate) = _rowwise("final_loss", fn, [x, y, target], [gate],
                                       [(d, F32), (d, MXU_DTYPE)], [(1, LANES), (1, d)], 256)
    return loss[0, 0], dx, dy, dgate


def _norm_bwd_core(dxo, dh, x, g, sc):
    r = lax.rsqrt(jnp.mean(x * x, axis=-1, keepdims=True) + EPS)
    xn = x * r
    dsh = _colsum(dh)
    dsc = _colsum(dh * (xn * g))
    dyy = dh * (1.0 + sc)
    dg = _colsum(dyy * xn)
    dxn = dyy * g
    dxi = dxo + r * (dxn - xn * jnp.mean(dxn * xn, axis=-1, keepdims=True))
    return dxi, dsh, dsc, dg


def _bwd_norm_gate(dxo, dh, x, y_prev, g, sc, gate_prev):
    d = x.shape[1]

    def fn(dxo, dh, x, y_prev, g, sc, gate_prev):
        dxi, dsh, dsc, dg = _norm_bwd_core(dxo, dh, x, g, sc)
        return (dxi, dxi * gate_prev), (dsh, dsc, dg, _colsum(dxi * y_prev))
    (dxi, dy), reds = _rowwise("bwd_norm_gate", fn, [dxo, dh, x, y_prev], [g, sc, gate_prev],
                               [(d, F32), (d, MXU_DTYPE)], [(1, d)] * 4, 256)
    return dxi, dy, reds


def _bwd_norm_first(dxo, dh, x, g, sc):
    d = x.shape[1]

    def fn(dxo, dh, x, g, sc):
        dxi, dsh, dsc, dg = _norm_bwd_core(dxo, dh, x, g, sc)
        return (dxi,), (dsh, dsc, dg)
    (dxi,), reds = _rowwise("bwd_norm_first", fn, [dxo, dh, x], [g, sc], [(d, F32)], [(1, d)] * 3, 256)
    return dxi, reds


def _mlp_fwd(h, w1, w2):
    def epi(acc):
        r = jnp.maximum(acc, 0.0)
        return acc, r * r
    a, z = _mm("mlp_in", h, w1, out_dtypes=(MXU_DTYPE, MXU_DTYPE), epi=epi, tm=1024, tn=512)
    (out,) = _mm("mlp_out", z, w2, tm=512, tn=512)
    return out, (a, z)


def _mlp_bwd(dy, h, w1, w2, saved):
    a, z = saved

    def epi(acc, a):
        return (acc * (2.0 * jnp.maximum(a.astype(F32), 0.0)),)
    (da,) = _mm("mlp_dz", dy, w2, tb=True, out_dtypes=(MXU_DTYPE,), epi=epi, tiles=(a,), tm=1024, tn=512)
    (dw2,) = _mm("mlp_dw2", z, dy, ta=True, out_dtypes=(MXU_DTYPE,))
    (dw1,) = _mm("mlp_dw1", h, da, ta=True, out_dtypes=(MXU_DTYPE,))
    (dh,) = _mm("mlp_dh", da, w1, tb=True)
    return dh, dw1, dw2


def _split3(v):
    hi = v.astype(jnp.bfloat16)
    r1 = v - hi.astype(F32)
    mid = r1.astype(jnp.bfloat16)
    lo = (r1 - mid.astype(F32)).astype(jnp.bfloat16)
    return hi, mid, lo


def _tri_matmul(tri, v):
    hi, mid, lo = _split3(v)
    dot = functools.partial(jnp.dot, preferred_element_type=F32)
    return dot(tri, hi) + dot(tri, mid) + dot(tri, lo)


def _log_sigmoid(v):
    return jnp.minimum(v, 0.0) - jnp.log(1.0 + jnp.exp(-jnp.abs(v)))


def _gate_fwd(proj, b_pad, col_block):
    s = proj.shape[0]
    tb = min(SCAN_BLOCK, s)
    nblk = s // tb

    def body(f_ref, b_ref, o_ref):
        row = lax.broadcasted_iota(jnp.int32, (tb, tb), 0)
        col = lax.broadcasted_iota(jnp.int32, (tb, tb), 1)
        tri = (col <= row).astype(jnp.bfloat16)

        def step(i, carry):
            rows = pl.ds(pl.multiple_of(i * tb, tb), tb)
            lf = _log_sigmoid(f_ref[rows, :] + b_ref[...])
            f = _tri_matmul(tri, lf) + carry
            o_ref[rows, :] = f
            return f[tb - 1:tb, :]
        lax.fori_loop(0, nblk, step, jnp.zeros((1, LANES), F32))

    return pl.pallas_call(
        body, name="gate_fwd", grid=(1,),
        in_specs=[pl.BlockSpec((s, LANES), lambda i: (0, col_block)), pl.BlockSpec((1, LANES), lambda i: (0, 0))],
        out_specs=pl.BlockSpec((s, LANES), lambda i: (0, 0)),
        out_shape=jax.ShapeDtypeStruct((s, LANES), F32),
        compiler_params=_cparams(("arbitrary",)),
    )(proj, b_pad)


def _gate_bwd(proj, b_pad, d_f, col_block):
    s = proj.shape[0]
    tb = min(SCAN_BLOCK, s)
    nblk = s // tb

    def body(f_ref, b_ref, d_ref, o_ref, db_ref):
        row = lax.broadcasted_iota(jnp.int32, (tb, tb), 0)
        col = lax.broadcasted_iota(jnp.int32, (tb, tb), 1)
        tri = (col >= row).astype(jnp.bfloat16)

        def step(j, carry):
            acc, db = carry
            i = nblk - 1 - j
            rows = pl.ds(pl.multiple_of(i * tb, tb), tb)
            dlf = _tri_matmul(tri, d_ref[rows, :]) + acc
            dpre = dlf * _sigmoid(-(f_ref[rows, :] + b_ref[...]))
            o_ref[rows, :] = dpre.astype(o_ref.dtype)
            return dlf[0:1, :], db + _colsum(dpre)
        _, db = lax.fori_loop(0, nblk, step, (jnp.zeros((1, LANES), F32), jnp.zeros((1, LANES), F32)))
        db_ref[...] = db

    return pl.pallas_call(
        body, name="gate_bwd", grid=(1,),
        in_specs=[pl.BlockSpec((s, LANES), lambda i: (0, col_block)), pl.BlockSpec((1, LANES), lambda i: (0, 0)),
                  pl.BlockSpec((s, LANES), lambda i: (0, 0))],
        out_specs=[pl.BlockSpec((s, LANES), lambda i: (0, 0)), pl.BlockSpec((1, LANES), lambda i: (0, 0))],
        out_shape=[jax.ShapeDtypeStruct((s, LANES), MXU_DTYPE), jax.ShapeDtypeStruct((1, LANES), F32)],
        compiler_params=_cparams(("arbitrary",)),
    )(proj, b_pad, d_f)


def _head_masks():
    lane = lax.broadcasted_iota(jnp.int32, (1, LANES), 1)
    return lane < HEAD_DIM


def _pair_norm(v, g, first):
    v2 = v * v
    ss0 = jnp.sum(jnp.where(first, v2, 0.0), axis=-1, keepdims=True)
    ss1 = jnp.sum(jnp.where(first, 0.0, v2), axis=-1, keepdims=True)
    r = jnp.where(first, lax.rsqrt(ss0 * (1.0 / HEAD_DIM) + EPS), lax.rsqrt(ss1 * (1.0 / HEAD_DIM) + EPS))
    vn = v * r
    return vn * g, vn, r


_NT = (((1,), (1,)), ((), ()))
_TN = (((0,), (0,)), ((), ()))

ATT_TQ = 256
ATT_TK = 256
AUG_F, AUG_ONE = 0, 3


def _own_lanes(hd):
    lane = lax.broadcasted_iota(jnp.int32, (1, LANES), 1)
    return (lane < HEAD_DIM) if hd == 0 else (lane >= HEAD_DIM)


def _aug_lanes(hd, f_other):
    lane = lax.broadcasted_iota(jnp.int32, (1, LANES), 1) - (HEAD_DIM if hd == 0 else 0)
    hi, mid, lo = [t.astype(F32) for t in _split3(f_other)]
    zero = jnp.zeros_like(f_other)
    f_terms = jnp.where(lane == 0, hi, jnp.where(lane == 1, mid, jnp.where(lane == 2, lo, zero)))
    f_shift = jnp.where(lane == 3, hi, jnp.where(lane == 4, mid, jnp.where(lane == 5, lo, zero)))
    ones_lo = jnp.where(lane < 3, 1.0, 0.0) * jnp.where(lane >= 0, 1.0, 0.0)
    ones_hi = jnp.where(lane < 6, 1.0, 0.0) * jnp.where(lane >= 3, 1.0, 0.0)
    return f_terms + ones_hi, ones_lo - f_shift


def _attn_operands(q_raw, k_raw, f_rep, qg, kg, scale):
    first = _head_masks()
    qn, _, _ = _pair_norm(q_raw, qg, first)
    kn, _, _ = _pair_norm(k_raw, kg, first)
    f_other = pltpu.roll(f_rep, HEAD_DIM, 1)
    out = []
    for hd in range(2):
        own = _own_lanes(hd)
        q_x, k_x = _aug_lanes(hd, f_other)
        out.append((jnp.where(own, qn * scale, q_x), jnp.where(own, kn, k_x)))
    return out


def _causal_t(tk, tq, off):
    r = lax.broadcasted_iota(jnp.int32, (tk, tq), 0)
    c = lax.broadcasted_iota(jnp.int32, (tk, tq), 1)
    return (r - c) <= off


def _big(shape, index_map):
    return pl.BlockSpec(shape, index_map, pipeline_mode=pl.Buffered(1))


def _attn_fwd_t(proj, f_rep, qg, kg, d_model):
    s = proj.shape[0]
    pairs = d_model // LANES
    tq, tk = min(ATT_TQ, s), min(ATT_TK, s)
    assert tk % tq == 0 and s % tk == 0
    nq = s // tq
    scale = HEAD_DIM ** -0.5
    ch = tk

    def body(q_ref, k_ref, v_ref, frep_ref, qg_ref, kg_ref, o_ref, lse_ref, qt_s, k_s, vt_s):
        for ci in range(s // ch):
            rows = pl.ds(ci * ch, ch)
            ops = _attn_operands(q_ref[rows, :], k_ref[rows, :], frep_ref[rows, :], qg_ref[...], kg_ref[...], scale)
            vv = v_ref[rows, :]
            for hd in range(2):
                own = _own_lanes(hd)
                lane = lax.broadcasted_iota(jnp.int32, (1, LANES), 1)
                one_lane = lane == (HEAD_DIM if hd == 0 else 0)
                qt_s[hd, :, rows] = ops[hd][0].T.astype(qt_s.dtype)
                k_s[hd, rows, :] = ops[hd][1].astype(k_s.dtype)
                vt_s[hd, :, rows] = jnp.where(own, vv, jnp.where(one_lane, 1.0, 0.0)).T.astype(vt_s.dtype)

        def q_block(qi, _):
            q0 = pl.multiple_of(qi * tq, tq)
            qcols = pl.ds(q0, tq)
            nfull = q0 // tk
            qts = [qt_s[hd, :, qcols] for hd in range(2)]

            def krows(kj):
                return pl.ds(pl.multiple_of(kj * tk, tk), tk)

            def scores(hd, kj):
                return jnp.dot(k_s[hd, krows(kj), :], qts[hd], preferred_element_type=F32)

            def kv_step(kj, carry, last):
                new = []
                for hd in range(2):
                    m, acc, st, p_prev = carry[hd]
                    if not last:
                        st_next = scores(hd, kj + 1)
                    pv = jnp.dot(vt_s[hd, :, krows(jnp.maximum(kj - 1, 0))], p_prev, preferred_element_type=F32)
                    if last:
                        st = jnp.where(_causal_t(tk, tq, q0 - kj * tk), st, NEG)
                    m_new = jnp.maximum(m, jnp.max(st, axis=0, keepdims=True))
                    p = jnp.exp(st - m_new).astype(vt_s.dtype)
                    acc = jnp.exp(m - m_new) * (acc + pv)
                    if last:
                        acc = acc + jnp.dot(vt_s[hd, :, krows(kj)], p, preferred_element_type=F32)
                        new.append((m_new, acc))
                    else:
                        new.append((m_new, acc, st_next, p))
                return tuple(new)

            init = tuple((jnp.full((1, tq), NEG, F32), jnp.zeros((LANES, tq), F32), scores(hd, 0),
                          jnp.zeros((tk, tq), vt_s.dtype)) for hd in range(2))
            carry = lax.fori_loop(0, nfull, lambda kj, cr: kv_step(kj, cr, False), init)
            o_parts, lse_parts = [], []
            for hd, (m, acc) in enumerate(kv_step(nfull, carry, True)):
                e0 = HEAD_DIM if hd == 0 else 0
                l = acc[e0:e0 + 1, :]
                o_parts.append((acc / l).T)
                lse_parts.append(m + jnp.log(l))
            o_ref[pl.ds(q0, tq), :] = jnp.where(_head_masks(), o_parts[0], o_parts[1]).astype(o_ref.dtype)
            lse_ref[0, :, qcols] = jnp.concatenate(lse_parts, axis=0)
            return 0
        lax.fori_loop(0, nq, q_block, 0)

    blk = lambda off: _big((s, LANES), lambda h: (0, off + h))
    vec = pl.BlockSpec((1, LANES), lambda h: (0, 0))
    return pl.pallas_call(
        body, name="attn_fwd", grid=(pairs,),
        in_specs=[blk(0), blk(pairs), blk(2 * pairs), blk(0), vec, vec],
        out_specs=[pl.BlockSpec((s, LANES), lambda h: (0, h)), pl.BlockSpec((1, 2, s), lambda h: (h, 0, 0))],
        out_shape=[jax.ShapeDtypeStruct((s, d_model), MXU_DTYPE), jax.ShapeDtypeStruct((pairs, 2, s), F32)],
        scratch_shapes=[pltpu.VMEM((2, LANES, s), MXU_DTYPE), pltpu.VMEM((2, s, LANES), MXU_DTYPE),
                        pltpu.VMEM((2, LANES, s), MXU_DTYPE)],
        compiler_params=_cparams(("arbitrary",)),
    )(proj, proj, proj, f_rep, qg, kg)


def _attn_bwd_t(proj, do, o, lse, f_rep, qg, kg, d_model):
    s = proj.shape[0]
    pairs = d_model // LANES
    tq, tk = min(ATT_TQ, s), min(ATT_TK, s)
    assert tk % tq == 0 and s % tk == 0
    nq = s // tq
    scale = HEAD_DIM ** -0.5
    ch = tk

    def norm_bwd(raw, g, dn, first):
        _, xn, r = _pair_norm(raw, g, first)
        dxn = dn * g
        t = dxn * xn
        mu0 = jnp.sum(jnp.where(first, t, 0.0), axis=-1, keepdims=True)
        mu1 = jnp.sum(jnp.where(first, 0.0, t), axis=-1, keepdims=True)
        mu = jnp.where(first, mu0, mu1) * (1.0 / HEAD_DIM)
        return r * (dxn - xn * mu), _colsum(dn * xn)

    def body(q_ref, k_ref, v_ref, do_ref, o_ref, lse_ref, frep_ref, qg_ref, kg_ref,
             dq_ref, dk_ref, dv_ref, df_ref, dqg_ref, dkg_ref,
             q_s, qt_s, k_s, kt_s, v_s, do_s, dot_s, dl_s, dk_s, dv_s):
        first = _head_masks()
        hp = pl.program_id(0)
        lane = lax.broadcasted_iota(jnp.int32, (1, LANES), 1)
        for ci in range(s // ch):
            rows = pl.ds(ci * ch, ch)
            ops = _attn_operands(q_ref[rows, :], k_ref[rows, :], frep_ref[rows, :], qg_ref[...], kg_ref[...], scale)
            v_s[rows, :] = v_ref[rows, :].astype(v_s.dtype)
            dov = do_ref[rows, :].astype(F32)
            ot = o_ref[rows, :].astype(F32).T
            for hd in range(2):
                own = _own_lanes(hd)
                q_s[hd, rows, :] = ops[hd][0].astype(q_s.dtype)
                qt_s[hd, :, rows] = ops[hd][0].T.astype(qt_s.dtype)
                k_s[hd, rows, :] = ops[hd][1].astype(k_s.dtype)
                kt_s[hd, :, rows] = ops[hd][1].T.astype(kt_s.dtype)
                doh = jnp.where(own, dov, 0.0)
                do_s[hd, rows, :] = doh.astype(do_s.dtype)
                doht = doh.T
                dot_s[hd, :, rows] = doht.astype(dot_s.dtype)
                dl_s[hd:hd + 1, rows] = jnp.sum(doht * ot, axis=0, keepdims=True)
        dk_s[...] = jnp.zeros_like(dk_s)
        dv_s[...] = jnp.zeros_like(dv_s)

        @pl.when(hp == 0)
        def _():
            df_ref[...] = jnp.zeros_like(df_ref)

        def q_block(qi, dqg):
            q0 = pl.multiple_of(qi * tq, tq)
            qcols = pl.ds(q0, tq)
            qrows = pl.ds(q0, tq)
            nfull = q0 // tk
            qts = [qt_s[hd, :, qcols] for hd in range(2)]
            dots = [dot_s[hd, :, qcols] for hd in range(2)]
            qns = [q_s[hd, qrows, :] for hd in range(2)]
            dons = [do_s[hd, qrows, :] for hd in range(2)]
            lse_r = [lse_ref[0, hd:hd + 1, qcols] for hd in range(2)]
            dl_r = [dl_s[hd:hd + 1, qcols] for hd in range(2)]
            bdt = qt_s.dtype

            def krows(kj):
                return pl.ds(pl.multiple_of(kj * tk, tk), tk)

            def scores(hd, kj):
                return (jnp.dot(k_s[hd, krows(kj), :], qts[hd], preferred_element_type=F32),
                        jnp.dot(v_s[krows(kj), :], dots[hd], preferred_element_type=F32))

            def products(hd, rows, ds, p, dqt):
                dk_s[hd, rows, :] += jnp.dot(ds, qns[hd], preferred_element_type=F32)
                dv_s[rows, :] += jnp.dot(p, dons[hd], preferred_element_type=F32)
                return dqt + jnp.dot(kt_s[hd, :, rows], ds, preferred_element_type=F32)

            def kv_step(kj, carry, last):
                new = []
                for hd in range(2):
                    dqt, ds_prev, p_prev = carry[hd]
                    st, dp = scores(hd, kj)
                    dqt = products(hd, krows(jnp.maximum(kj - 1, 0)), ds_prev, p_prev, dqt)
                    if last:
                        st = jnp.where(_causal_t(tk, tq, q0 - kj * tk), st, NEG)
                    p = jnp.exp(st - lse_r[hd])
                    ds = (p * (dp - dl_r[hd])).astype(bdt)
                    if last:
                        new.append(products(hd, krows(kj), ds, p.astype(bdt), dqt))
                    else:
                        new.append((dqt, ds, p.astype(bdt)))
                return tuple(new)

            init = tuple((jnp.zeros((LANES, tq), F32), jnp.zeros((tk, tq), bdt), jnp.zeros((tk, tq), bdt))
                         for hd in range(2))
            carry = lax.fori_loop(0, nfull, lambda kj, cr: kv_step(kj, cr, False), init)
            dq_parts = [dqt.T for dqt in kv_step(nfull, carry, True)]
            rs0 = dq_parts[0][:, HEAD_DIM + AUG_F:HEAD_DIM + AUG_F + 1]
            rs1 = dq_parts[1][:, AUG_F:AUG_F + 1]
            df_ref[qrows, :] += jnp.where(lane == 2 * hp, rs0, 0.0) + jnp.where(lane == 2 * hp + 1, rs1, 0.0)
            dqn = jnp.where(first, dq_parts[0], dq_parts[1]) * scale
            dq_raw, dg = norm_bwd(q_ref[qrows, :], qg_ref[...], dqn, first)
            dq_ref[qrows, :] = dq_raw.astype(dq_ref.dtype)
            return dqg + dg
        dqg_ref[0] = lax.fori_loop(0, nq, q_block, jnp.zeros((1, LANES), F32))

        dkg = jnp.zeros((1, LANES), F32)
        for ci in range(s // ch):
            rows = pl.ds(ci * ch, ch)
            dk0, dk1 = dk_s[0, rows, :], dk_s[1, rows, :]
            cs0 = dk0[:, HEAD_DIM + AUG_ONE:HEAD_DIM + AUG_ONE + 1]
            cs1 = dk1[:, AUG_ONE:AUG_ONE + 1]
            df_ref[rows, :] -= jnp.where(lane == 2 * hp, cs0, 0.0) + jnp.where(lane == 2 * hp + 1, cs1, 0.0)
            dk_raw, dg = norm_bwd(k_ref[rows, :], kg_ref[...], jnp.where(first, dk0, dk1), first)
            dk_ref[rows, :] = dk_raw.astype(dk_ref.dtype)
            dkg = dkg + dg
            dv_ref[rows, :] = dv_s[rows, :].astype(dv_ref.dtype)
        dkg_ref[0] = dkg

    blk = lambda off: _big((s, LANES), lambda h: (0, off + h))
    outb = pl.BlockSpec((s, LANES), lambda h: (0, h))
    vec = pl.BlockSpec((1, LANES), lambda h: (0, 0))
    gout = pl.BlockSpec((1, 1, LANES), lambda h: (h, 0, 0))
    act = jax.ShapeDtypeStruct((s, d_model), MXU_DTYPE)
    gsh = jax.ShapeDtypeStruct((pairs, 1, LANES), F32)
    pair_rows = pltpu.VMEM((2, s, LANES), MXU_DTYPE)
    pair_cols = pltpu.VMEM((2, LANES, s), MXU_DTYPE)
    return pl.pallas_call(
        body, name="attn_bwd", grid=(pairs,),
        in_specs=[blk(0), blk(pairs), blk(2 * pairs), blk(0), blk(0),
                  pl.BlockSpec((1, 2, s), lambda h: (h, 0, 0)), blk(0), vec, vec],
        out_specs=[outb, outb, outb, pl.BlockSpec((s, LANES), lambda h: (0, 0)), gout, gout],
        out_shape=[act, act, act, jax.ShapeDtypeStruct((s, LANES), F32), gsh, gsh],
        scratch_shapes=[pair_rows, pair_cols, pair_rows, pair_cols, pltpu.VMEM((s, LANES), MXU_DTYPE),
                        pair_rows, pair_cols, pltpu.VMEM((8, s), F32),
                        pltpu.VMEM((2, s, LANES), F32), pltpu.VMEM((s, LANES), F32)],
        compiler_params=_cparams(("arbitrary",), ATT_BWD_VMEM_LIMIT),
    )(proj, proj, proj, do, o, lse, f_rep, qg, kg)


def _attn_fwd(proj, f_rep, f_rows, qg, kg, d_model):
    s = proj.shape[0]
    pairs = d_model // LANES
    tq = min(ATT_BLOCK, s)
    nq = s // tq
    scale = HEAD_DIM ** -0.5

    def body(q_ref, k_ref, v_ref, frep_ref, frow_ref, qg_ref, kg_ref, o_ref, lse_ref, q0_s, q1_s, kn_s, v_s):
        first = _head_masks()
        for ci in range(nq):
            rows = pl.ds(ci * tq, tq)
            qn, _, _ = _pair_norm(q_ref[rows, :], qg_ref[...], first)
            qn = qn * scale
            q0_s[rows, :] = jnp.where(first, qn, 0.0).astype(q0_s.dtype)
            q1_s[rows, :] = jnp.where(first, 0.0, qn).astype(q1_s.dtype)
            kn, _, _ = _pair_norm(k_ref[rows, :], kg_ref[...], first)
            kn_s[rows, :] = kn.astype(kn_s.dtype)
            v_s[rows, :] = v_ref[rows, :].astype(v_s.dtype)

        rid = lax.broadcasted_iota(jnp.int32, (tq, tq), 0)
        cid = lax.broadcasted_iota(jnp.int32, (tq, tq), 1)
        causal = cid <= rid

        def q_block(qi, _):
            qrows = pl.ds(pl.multiple_of(qi * tq, tq), tq)
            qs = (q0_s[qrows, :], q1_s[qrows, :])
            frep = frep_ref[qrows, :]
            fq = (frep[:, 0:1], frep[:, HEAD_DIM:HEAD_DIM + 1])

            def kv_step(kj, carry, diag):
                krows = pl.ds(pl.multiple_of(kj * tq, tq), tq)
                k = kn_s[krows, :]
                v = v_s[krows, :]
                fk = frow_ref[0, :, krows]
                new = []
                for hd in range(2):
                    m, l, acc = carry[hd]
                    sc = lax.dot_general(qs[hd], k, _NT, preferred_element_type=F32) + (fq[hd] - fk[hd:hd + 1, :])
                    if diag:
                        sc = jnp.where(causal, sc, NEG)
                    m_new = jnp.maximum(m, jnp.max(sc, axis=-1, keepdims=True))
                    p = jnp.exp(sc - m_new)
                    alpha = jnp.exp(m - m_new)
                    l = alpha * l + jnp.sum(p, axis=-1, keepdims=True)
                    acc = alpha * acc + jnp.dot(p.astype(v.dtype), v, preferred_element_type=F32)
                    new.append((m_new, l, acc))
                return tuple(new)

            init = tuple((jnp.full((tq, 1), NEG, F32), jnp.zeros((tq, 1), F32), jnp.zeros((tq, LANES), F32))
                         for _ in range(2))
            carry = lax.fori_loop(0, qi, lambda kj, cr: kv_step(kj, cr, False), init)
            (m0, l0, a0), (m1, l1, a1) = kv_step(qi, carry, True)
            o_ref[qrows, :] = jnp.where(first, a0 / l0, a1 / l1).astype(o_ref.dtype)
            lse_ref[qrows, :] = jnp.where(first, m0 + jnp.log(l0), m1 + jnp.log(l1))
            return 0
        lax.fori_loop(0, nq, q_block, 0)

    blk = lambda off: pl.BlockSpec((s, LANES), lambda h: (0, off + h))
    vec = pl.BlockSpec((1, LANES), lambda h: (0, 0))
    return pl.pallas_call(
        body, name="attn_fwd", grid=(pairs,),
        in_specs=[blk(0), blk(pairs), blk(2 * pairs), blk(0),
                  pl.BlockSpec((1, 2, s), lambda h: (h, 0, 0)), vec, vec],
        out_specs=[blk(0), blk(0)],
        out_shape=[jax.ShapeDtypeStruct((s, d_model), MXU_DTYPE), jax.ShapeDtypeStruct((s, d_model), F32)],
        scratch_shapes=[pltpu.VMEM((s, LANES), MXU_DTYPE)] * 4,
        compiler_params=_cparams(("arbitrary",)),
    )(proj, proj, proj, f_rep, f_rows, qg, kg)


def _attn_bwd(proj, do, o, lse, f_rep, f_rows, qg, kg, d_model):
    s = proj.shape[0]
    pairs = d_model // LANES
    tq = min(ATT_BLOCK, s)
    nq = s // tq
    scale = HEAD_DIM ** -0.5

    def norm_bwd(raw, g, dn, first):
        _, xn, r = _pair_norm(raw, g, first)
        dxn = dn * g
        t = dxn * xn
        mu0 = jnp.sum(jnp.where(first, t, 0.0), axis=-1, keepdims=True)
        mu1 = jnp.sum(jnp.where(first, 0.0, t), axis=-1, keepdims=True)
        mu = jnp.where(first, mu0, mu1) * (1.0 / HEAD_DIM)
        return r * (dxn - xn * mu), _colsum(dn * xn)

    def body(q_ref, k_ref, v_ref, do_ref, o_ref, lse_ref, frep_ref, frow_ref, qg_ref, kg_ref,
             dq_ref, dk_ref, dv_ref, dfq_ref, dfk_ref, dqg_ref, dkg_ref,
             q0_s, q1_s, kn_s, v_s, do0_s, do1_s, dk_s, dv_s):
        first = _head_masks()
        for ci in range(nq):
            rows = pl.ds(ci * tq, tq)
            qn, _, _ = _pair_norm(q_ref[rows, :], qg_ref[...], first)
            qn = qn * scale
            q0_s[rows, :] = jnp.where(first, qn, 0.0).astype(q0_s.dtype)
            q1_s[rows, :] = jnp.where(first, 0.0, qn).astype(q1_s.dtype)
            kn, _, _ = _pair_norm(k_ref[rows, :], kg_ref[...], first)
            kn_s[rows, :] = kn.astype(kn_s.dtype)
            v_s[rows, :] = v_ref[rows, :].astype(v_s.dtype)
            dov = do_ref[rows, :]
            do0_s[rows, :] = jnp.where(first, dov, jnp.zeros_like(dov))
            do1_s[rows, :] = jnp.where(first, jnp.zeros_like(dov), dov)
        dk_s[...] = jnp.zeros_like(dk_s)
        dv_s[...] = jnp.zeros_like(dv_s)
        dfk_ref[...] = jnp.zeros_like(dfk_ref)

        rid = lax.broadcasted_iota(jnp.int32, (tq, tq), 0)
        cid = lax.broadcasted_iota(jnp.int32, (tq, tq), 1)
        causal = cid <= rid

        def q_block(qi, dqg):
            qrows = pl.ds(pl.multiple_of(qi * tq, tq), tq)
            qs = (q0_s[qrows, :], q1_s[qrows, :])
            dos = (do0_s[qrows, :], do1_s[qrows, :])
            frep = frep_ref[qrows, :]
            lse = lse_ref[qrows, :]
            ov = o_ref[qrows, :].astype(F32)
            fq = (frep[:, 0:1], frep[:, HEAD_DIM:HEAD_DIM + 1])
            ls = (lse[:, 0:1], lse[:, HEAD_DIM:HEAD_DIM + 1])
            dls = tuple(jnp.sum(dos[hd].astype(F32) * ov, axis=-1, keepdims=True) for hd in range(2))

            def kv_step(kj, carry, diag):
                dq, rs = carry[0], list(carry[1:])
                krows = pl.ds(pl.multiple_of(kj * tq, tq), tq)
                k = kn_s[krows, :]
                v = v_s[krows, :]
                fk = frow_ref[0, :, krows]
                dk_add = jnp.zeros((tq, LANES), F32)
                dv_add = jnp.zeros((tq, LANES), F32)
                dfk = []
                for hd in range(2):
                    sc = lax.dot_general(qs[hd], k, _NT, preferred_element_type=F32) + (fq[hd] - fk[hd:hd + 1, :])
                    if diag:
                        sc = jnp.where(causal, sc, NEG)
                    p = jnp.exp(sc - ls[hd])
                    dp = lax.dot_general(dos[hd], v, _NT, preferred_element_type=F32)
                    ds = p * (dp - dls[hd])
                    dsb = ds.astype(k.dtype)
                    dqh = jnp.dot(dsb, k, preferred_element_type=F32)
                    dq = dq + (jnp.where(first, dqh, 0.0) if hd == 0 else jnp.where(first, 0.0, dqh))
                    dk_add = dk_add + lax.dot_general(dsb, qs[hd], _TN, preferred_element_type=F32)
                    dv_add = dv_add + lax.dot_general(p.astype(k.dtype), dos[hd], _TN, preferred_element_type=F32)
                    dfk.append(_colsum(ds))
                    rs[hd] = rs[hd] + jnp.sum(ds, axis=-1, keepdims=True)
                dk_s[krows, :] += dk_add
                dv_s[krows, :] += dv_add
                dfk_ref[0, :, krows] -= jnp.concatenate(dfk, axis=0)
                return (dq, rs[0], rs[1])

            init = (jnp.zeros((tq, LANES), F32), jnp.zeros((tq, 1), F32), jnp.zeros((tq, 1), F32))
            carry = lax.fori_loop(0, qi, lambda kj, cr: kv_step(kj, cr, False), init)
            dq, rs0, rs1 = kv_step(qi, carry, True)
            dfq_ref[qrows, :] = jnp.where(first, rs0, rs1)
            dq_raw, dg = norm_bwd(q_ref[qrows, :], qg_ref[...], dq * scale, first)
            dq_ref[qrows, :] = dq_raw.astype(dq_ref.dtype)
            return dqg + dg
        dqg_ref[0] = lax.fori_loop(0, nq, q_block, jnp.zeros((1, LANES), F32))

        dkg = jnp.zeros((1, LANES), F32)
        for ci in range(nq):
            rows = pl.ds(ci * tq, tq)
            dk_raw, dg = norm_bwd(k_ref[rows, :], kg_ref[...], dk_s[rows, :], first)
            dk_ref[rows, :] = dk_raw.astype(dk_ref.dtype)
            dkg = dkg + dg
            dv_ref[rows, :] = dv_s[rows, :].astype(dv_ref.dtype)
        dkg_ref[0] = dkg

    blk = lambda off: pl.BlockSpec((s, LANES), lambda h: (0, off + h))
    vec = pl.BlockSpec((1, LANES), lambda h: (0, 0))
    frow = pl.BlockSpec((1, 2, s), lambda h: (h, 0, 0))
    gout = pl.BlockSpec((1, 1, LANES), lambda h: (h, 0, 0))
    act = jax.ShapeDtypeStruct((s, d_model), MXU_DTYPE)
    gsh = jax.ShapeDtypeStruct((pairs, 1, LANES), F32)
    return pl.pallas_call(
        body, name="attn_bwd", grid=(pairs,),
        in_specs=[blk(0), blk(pairs), blk(2 * pairs), blk(0), blk(0), blk(0), blk(0), frow, vec, vec],
        out_specs=[blk(0), blk(0), blk(0), blk(0), frow, gout, gout],
        out_shape=[act, act, act, jax.ShapeDtypeStruct((s, d_model), F32),
                   jax.ShapeDtypeStruct((pairs, 2, s), F32), gsh, gsh],
        scratch_shapes=[pltpu.VMEM((s, LANES), MXU_DTYPE)] * 6 + [pltpu.VMEM((s, LANES), F32)] * 2,
        compiler_params=_cparams(("arbitrary",), ATT_BWD_VMEM_LIMIT),
    )(proj, proj, proj, do, o, lse, f_rep, f_rows, qg, kg)


def _pad_cols(a, n):
    return jnp.pad(a, ((0, 0), (0, n - a.shape[1])))


def _fox_fwd(h, w):
    d = h.shape[1]
    pairs = d // LANES
    (proj,) = _mm("fox_in", h, w["w_in"], tm=1024, tn=640)
    f_cum = _gate_fwd(proj, w["b_f"], 3 * pairs)
    f16 = f_cum[:, :d // HEAD_DIM]
    f_rep = jnp.repeat(f16, HEAD_DIM, axis=1)
    o, lse = _attn_fwd_t(proj, f_rep, w["qg"], w["kg"], d)
    (y,) = _mm("fox_out", o, w["w_out"])
    return y, (proj, f_rep, o, lse)


def _fox_bwd(dy, h, w, saved):
    proj, f_rep, o, lse = saved
    s, d = h.shape
    pairs = d // LANES
    (do,) = _mm("fox_do", dy, w["w_out"], tb=True, out_dtypes=(MXU_DTYPE,))
    (dw_out,) = _mm("fox_dwout", o, dy, ta=True, out_dtypes=(MXU_DTYPE,))
    dq, dk, dv, d_f, dqg, dkg = _attn_bwd_t(proj, do, o, lse, f_rep, w["qg"], w["kg"], d)
    dfpre, db_f = _gate_bwd(proj, w["b_f"], d_f, 3 * pairs)
    dproj = jnp.concatenate([dq, dk, dv, dfpre], axis=1)
    (dw_in,) = _mm("fox_dwin", h, dproj, ta=True, out_dtypes=(MXU_DTYPE,), tn=640)
    (dh,) = _mm("fox_dh", dproj, w["w_in"], tb=True)
    fold = lambda g: jnp.sum(g, axis=(0, 1)).reshape(2, HEAD_DIM).sum(axis=0)
    return dh, dict(w_in=dw_in, w_out=dw_out, b_f=db_f[0, :d // HEAD_DIM], qg=fold(dqg), kg=fold(dkg))


_GELU_C = math.sqrt(2.0 / math.pi)


def _gelu(v):
    return 0.5 * v * (1.0 + jnp.tanh(_GELU_C * (v + 0.044715 * v * v * v)))


def _gelu_grad(v):
    t = jnp.tanh(_GELU_C * (v + 0.044715 * v * v * v))
    return 0.5 * (1.0 + t) + 0.5 * v * (1.0 - t * t) * (_GELU_C * (1.0 + 3.0 * 0.044715 * v * v))


def _ln_stats(v):
    mu = jnp.mean(v, axis=-1, keepdims=True)
    vc = v - mu
    r = lax.rsqrt(jnp.mean(vc * vc, axis=-1, keepdims=True) + EPS)
    return vc * r, r


def _sg_mask():
    t = lax.broadcasted_iota(jnp.int32, (SG_CHUNK, SG_CHUNK), 0) // SG_BLOCK
    sidx = lax.broadcasted_iota(jnp.int32, (SG_CHUNK, SG_CHUNK), 1) // SG_BLOCK
    return sidx <= t


def _sgu_fwd(uv_pre, ln_g, ln_b, w_s, b_st):
    s, w2 = uv_pre.shape
    wd = w2 // 2
    tr = min(256, s)

    def fn(uv_pre, ln_g, ln_b, w_s, b_st):
        uv = _gelu(uv_pre)
        u = uv[:, :wd]
        vh, _ = _ln_stats(uv[:, wd:])
        vl = (vh * ln_g + ln_b).astype(MXU_DTYPE)
        mask = _sg_mask()
        cols = []
        for g in range(SG_GROUPS):
            wg = jnp.where(mask, w_s[g * SG_CHUNK:(g + 1) * SG_CHUNK, :], 0.0).astype(MXU_DTYPE)
            parts = []
            for ci in range(tr // SG_CHUNK):
                vt = vl[ci * SG_CHUNK:(ci + 1) * SG_CHUNK, g * SG_CHUNK:(g + 1) * SG_CHUNK]
                parts.append(jnp.dot(wg, vt, preferred_element_type=F32) + b_st[:, g:g + 1])
            cols.append(jnp.concatenate(parts, axis=0) if len(parts) > 1 else parts[0])
        vout = jnp.concatenate(cols, axis=1)
        return (u * vout,), ()
    (m,), _ = _rowwise("sgu_fwd", fn, [uv_pre], [ln_g, ln_b, w_s, b_st], [(wd, MXU_DTYPE)], [], tr)
    return m


def _sgu_bwd(uv_pre, dm, ln_g, ln_b, w_s, b_st):
    s, w2 = uv_pre.shape
    wd = w2 // 2
    tr = min(256, s)

    def fn(uv_pre, dm, ln_g, ln_b, w_s, b_st):
        uv = _gelu(uv_pre)
        u = uv[:, :wd]
        vh, r = _ln_stats(uv[:, wd:])
        vl = (vh * ln_g + ln_b).astype(MXU_DTYPE)
        mask = _sg_mask()
        lane = lax.broadcasted_iota(jnp.int32, (1, LANES), 1)
        cols, dcols, dws, dbs = [], [], [], jnp.zeros((SG_CHUNK, LANES), F32)
        for g in range(SG_GROUPS):
            wg = jnp.where(mask, w_s[g * SG_CHUNK:(g + 1) * SG_CHUNK, :], 0.0).astype(MXU_DTYPE)
            parts, dparts = [], []
            dwg = jnp.zeros((SG_CHUNK, SG_CHUNK), F32)
            dbg = jnp.zeros((SG_CHUNK, 1), F32)
            for ci in range(tr // SG_CHUNK):
                rs = slice(ci * SG_CHUNK, (ci + 1) * SG_CHUNK)
                cs = slice(g * SG_CHUNK, (g + 1) * SG_CHUNK)
                vt = vl[rs, cs]
                parts.append(jnp.dot(wg, vt, preferred_element_type=F32) + b_st[:, g:g + 1])
                dvo = dm[rs, cs] * u[rs, cs]
                dvob = dvo.astype(MXU_DTYPE)
                dparts.append(lax.dot_general(wg, dvob, _TN, preferred_element_type=F32))
                dwg = dwg + lax.dot_general(dvob, vt, _NT, preferred_element_type=F32)
                dbg = dbg + jnp.sum(dvo, axis=-1, keepdims=True)
            cols.append(jnp.concatenate(parts, axis=0) if len(parts) > 1 else parts[0])
            dcols.append(jnp.concatenate(dparts, axis=0) if len(dparts) > 1 else dparts[0])
            dws.append(jnp.where(mask, dwg, 0.0))
            dbs = dbs + jnp.where(lane == g, dbg, 0.0)
        vout = jnp.concatenate(cols, axis=1)
        dvl = jnp.concatenate(dcols, axis=1)
        du = dm * vout
        dlg = _colsum(dvl * vh)
        dlb = _colsum(dvl)
        dvh = dvl * ln_g
        dv = r * (dvh - jnp.mean(dvh, axis=-1, keepdims=True) - vh * jnp.mean(dvh * vh, axis=-1, keepdims=True))
        dpre = jnp.concatenate([du, dv], axis=1) * _gelu_grad(uv_pre)
        return (dpre,), (dlg, dlb, jnp.concatenate(dws, axis=0), dbs)
    (dpre,), reds = _rowwise("sgu_bwd", fn, [uv_pre, dm], [ln_g, ln_b, w_s, b_st], [(w2, MXU_DTYPE)],
                             [(1, wd), (1, wd), (SG_GROUPS * SG_CHUNK, SG_CHUNK), (SG_CHUNK, LANES)], tr)
    return dpre, reds


def _sg_fwd(h, w):
    (uv_pre,) = _mm("sg_in", h, w["w_in"], tm=1024, tn=512)
    m = _sgu_fwd(uv_pre, w["ln_g"], w["ln_b"], w["w_s"], w["b_st"])
    (y,) = _mm("sg_out", m, w["w_out"])
    return y, (uv_pre, m)


def _sg_bwd(dy, h, w, saved):
    uv_pre, m = saved
    (dm,) = _mm("sg_dm", dy, w["w_out"], tb=True)
    (dw_out,) = _mm("sg_dwout", m, dy, ta=True, out_dtypes=(MXU_DTYPE,))
    dpre, (dlg, dlb, dws, dbs) = _sgu_bwd(uv_pre, dm, w["ln_g"], w["ln_b"], w["w_s"], w["b_st"])
    (dw_in,) = _mm("sg_dwin", h, dpre, ta=True, out_dtypes=(MXU_DTYPE,))
    (dh,) = _mm("sg_dh", dpre, w["w_in"], tb=True)
    return dh, dict(w_in=dw_in, w_out=dw_out, ln_g=dlg, ln_b=dlb, w_s=dws, b_s=dbs[:, :SG_GROUPS].T)


def _conv_fwd_kernel(ypad, w_dw, b_dw):
    s = ypad.shape[0] - CONV_PAD
    d = ypad.shape[1]
    tt = min(256, s)
    ext = tt + CONV_PAD

    def body(y_ref, w_ref, b_ref, o_ref):
        def chunk(ci, _):
            base = pl.multiple_of(ci * tt, tt)
            e = y_ref[pl.ds(base, ext), :]
            acc = jnp.zeros((tt, LANES), F32) + b_ref[...]
            for j in range(CONV_WIDTH):
                sh = pltpu.roll(e, ext - (CONV_PAD - CONV_WIDTH + 1 + j), 0)[:tt, :]
                acc = acc + w_ref[j:j + 1, :] * sh
            o_ref[pl.ds(base, tt), :] = acc
            return 0
        lax.fori_loop(0, s // tt, chunk, 0)

    return pl.pallas_call(
        body, name="conv_fwd", grid=(d // LANES,),
        in_specs=[pl.BlockSpec((s + CONV_PAD, LANES), lambda i: (0, i)),
                  pl.BlockSpec((CONV_PAD, LANES), lambda i: (0, i)), pl.BlockSpec((1, LANES), lambda i: (0, i))],
        out_specs=pl.BlockSpec((s, LANES), lambda i: (0, i)),
        out_shape=jax.ShapeDtypeStruct((s, d), F32),
        compiler_params=_cparams(("parallel",)),
    )(ypad, w_dw, b_dw)


def _conv_bwd_kernel(ypad, dpad, w_dw):
    s = ypad.shape[0] - CONV_PAD
    d = ypad.shape[1]
    tt = min(256, s)
    ext = tt + CONV_PAD

    def body(y_ref, d_ref, w_ref, o_ref, dw_ref):
        dw_ref[...] = jnp.zeros_like(dw_ref)

        def chunk(ci, _):
            base = pl.multiple_of(ci * tt, tt)
            ye = y_ref[pl.ds(base, ext), :]
            de = d_ref[pl.ds(base, ext), :]
            dcur = de[:tt, :]
            acc = jnp.zeros((tt, LANES), F32)
            for j in range(CONV_WIDTH):
                back = CONV_WIDTH - 1 - j
                dsh = dcur if back == 0 else pltpu.roll(de, ext - back, 0)[:tt, :]
                acc = acc + w_ref[j:j + 1, :] * dsh
                ysh = pltpu.roll(ye, ext - (CONV_PAD - CONV_WIDTH + 1 + j), 0)[:tt, :]
                dw_ref[j:j + 1, :] += _colsum(dcur * ysh)
            o_ref[pl.ds(base, tt), :] = acc
            return 0
        lax.fori_loop(0, s // tt, chunk, 0)

    return pl.pallas_call(
        body, name="conv_bwd", grid=(d // LANES,),
        in_specs=[pl.BlockSpec((s + CONV_PAD, LANES), lambda i: (0, i)),
                  pl.BlockSpec((s + CONV_PAD, LANES), lambda i: (0, i)),
                  pl.BlockSpec((CONV_PAD, LANES), lambda i: (0, i))],
        out_specs=[pl.BlockSpec((s, LANES), lambda i: (0, i)), pl.BlockSpec((CONV_PAD, LANES), lambda i: (0, i))],
        out_shape=[jax.ShapeDtypeStruct((s, d), F32), jax.ShapeDtypeStruct((CONV_PAD, d), F32)],
        compiler_params=_cparams(("parallel",)),
    )(ypad, dpad, w_dw)


def _cv_fwd(h, w):
    d = h.shape[1]
    (y1,) = _mm("cv_pw1", h, w["w_pw1"], tm=1024, tn=512)

    def glu(y1, b1):
        t = y1 + b1
        return (t[:, :d] * _sigmoid(t[:, d:]),), ()
    (y2,), _ = _rowwise("cv_glu", glu, [y1], [w["b_pw1"]], [(d, F32)], [], 256)
    y3 = _conv_fwd_kernel(jnp.pad(y2, ((CONV_PAD, 0), (0, 0))), w["w_dw"], w["b_dw"])

    def lnsilu(y3, g, b):
        vh, _ = _ln_stats(y3)
        y4 = vh * g + b
        return (y4 * _sigmoid(y4),), ()
    (y5,), _ = _rowwise("cv_lnsilu", lnsilu, [y3], [w["ln_g"], w["ln_b"]], [(d, MXU_DTYPE)], [], 256)
    (y,) = _mm("cv_pw2", y5, w["w_pw2"], epi=lambda acc, b: (acc + b,), vecs=(w["b_pw2"],))
    return y, (y1, y2, y3, y5)


def _cv_bwd(dy, h, w, saved):
    y1, y2, y3, y5 = saved
    d = h.shape[1]
    (dy5,) = _mm("cv_dy5", dy, w["w_pw2"], tb=True)
    (dw_pw2,) = _mm("cv_dwpw2", y5, dy, ta=True, out_dtypes=(MXU_DTYPE,))

    def ln_bwd(dy5, y3, dyb, g, b):
        vh, r = _ln_stats(y3)
        y4 = vh * g + b
        sg = _sigmoid(y4)
        dy4 = dy5 * (sg * (1.0 + y4 * (1.0 - sg)))
        dvh = dy4 * g
        dy3 = r * (dvh - jnp.mean(dvh, axis=-1, keepdims=True) - vh * jnp.mean(dvh * vh, axis=-1, keepdims=True))
        return (dy3,), (_colsum(dy4 * vh), _colsum(dy4), _colsum(dy3), _colsum(dyb.astype(F32)))
    (dy3,), (dlg, dlb, db_dw, db_pw2) = _rowwise("cv_ln_bwd", ln_bwd, [dy5, y3, dy], [w["ln_g"], w["ln_b"]],
                                                 [(d, F32)], [(1, d)] * 4, 256)
    dy2, dw_dw = _conv_bwd_kernel(jnp.pad(y2, ((CONV_PAD, 0), (0, 0))), jnp.pad(dy3, ((0, CONV_PAD), (0, 0))),
                                  w["w_dw"])

    def glu_bwd(y1, dy2, b1):
        t = y1 + b1
        a, sg = t[:, :d], _sigmoid(t[:, d:])
        dy1 = jnp.concatenate([dy2 * sg, dy2 * a * sg * (1.0 - sg)], axis=1)
        return (dy1,), (_colsum(dy1),)
    (dy1,), (db_pw1,) = _rowwise("cv_glu_bwd", glu_bwd, [y1, dy2], [w["b_pw1"]], [(2 * d, MXU_DTYPE)],
                                 [(1, 2 * d)], 256)
    (dw_pw1,) = _mm("cv_dwpw1", h, dy1, ta=True, out_dtypes=(MXU_DTYPE,))
    (dh,) = _mm("cv_dh", dy1, w["w_pw1"], tb=True)
    return dh, dict(w_pw1=dw_pw1, w_pw2=dw_pw2, b_pw1=db_pw1, b_pw2=db_pw2, w_dw=dw_dw[:CONV_WIDTH],
                    b_dw=db_dw, ln_g=dlg, ln_b=dlb)


def _ada_outer(c_t, dmod):
    def fn(c_t, dmod):
        acc = c_t[:, 0:1] * dmod[0:1, :]
        for b in range(1, N_DEV):
            acc = acc + c_t[:, b:b + 1] * dmod[b:b + 1, :]
        return (acc,), ()
    (g,), _ = _rowwise("ada_outer", fn, [c_t], [dmod], [(dmod.shape[1], F32)], [], 256)
    return g


def _adamw(name, parts, w, m, v, tr):
    npart = parts.shape[0]
    cols = w.shape[1]

    def fn(parts, w, m, v):
        g = parts[0].astype(F32)
        for q in range(1, npart):
            g = g + parts[q].astype(F32)
        m_new = ADAM_B1 * m + (1.0 - ADAM_B1) * g
        v_new = ADAM_B2 * v + (1.0 - ADAM_B2) * (g * g)
        m_hat = m_new / (1.0 - ADAM_B1 ** ADAM_STEP)
        v_hat = v_new / (1.0 - ADAM_B2 ** ADAM_STEP)
        delta = -ADAM_LR * (m_hat / (jnp.sqrt(v_hat) + ADAM_EPS) + ADAM_WD * w)
        return (g, delta, m_new, v_new), ()
    rows = w.shape[0]
    tr = min(tr, rows)

    def body(p_ref, w_ref, m_ref, v_ref, g_o, d_o, m_o, v_o):
        outs, _ = fn(p_ref[...], w_ref[...], m_ref[...], v_ref[...])
        for o_ref, o in zip((g_o, d_o, m_o, v_o), outs):
            o_ref[...] = o

    spec = pl.BlockSpec((tr, cols), lambda i: (i, 0))
    return pl.pallas_call(
        body, name=name, grid=(rows // tr,),
        in_specs=[pl.BlockSpec((npart, tr, cols), lambda i: (0, i, 0)), spec, spec, spec],
        out_specs=[spec] * 4, out_shape=[jax.ShapeDtypeStruct((rows, cols), F32)] * 4,
        compiler_params=_cparams(("parallel",)),
    )(parts, w, m, v)


def _pack(arrays):
    flat = jnp.concatenate([a.reshape(-1).astype(F32) for a in arrays])
    n = flat.shape[0]
    rows = -(-n // (8 * LANES)) * 8
    return jnp.pad(flat, (0, rows * LANES - n)).reshape(rows, LANES)


def _unpack(buf, shapes, lead=()):
    flat = buf.reshape(lead + (-1,))
    out, off = [], 0
    for shp in shapes:
        n = math.prod(shp)
        out.append(flat[..., off:off + n].reshape(lead + tuple(shp)))
        off += n
    return out


ADAM_TILE_ELEMS = 1 << 17


def _row_tile(rows, cols):
    want = max(8, ADAM_TILE_ELEMS // max(cols, LANES))
    if rows <= want:
        return rows
    best = None
    for t in range(8, want + 1, 8):
        if rows % t == 0:
            best = t
    assert best is not None, (rows, cols)
    return best


def _local_step(xs, tgt, mods, norm_mix, norm_mlp, fetch, send):
    depth = len(mods)
    mixer_fwd = (_fox_fwd, _sg_fwd, _cv_fwd)
    mixer_bwd = (_fox_bwd, _sg_bwd, _cv_bwd)
    mw, w1, w2 = [None] * depth, [None] * depth, [None] * depth

    nsub = 2 * depth
    sub = []
    x_in = xs
    y_prev = gate_prev = None
    for k in range(nsub):
        i, is_mlp = k // 2, k % 2
        sh, sc = mods[i][3 * is_mlp], mods[i][3 * is_mlp + 1]
        g = (norm_mlp if is_mlp else norm_mix)[i:i + 1]
        wts, token = fetch(k, xs if k == 0 else y_prev)
        g = g + token
        if is_mlp:
            w1[i], w2[i] = wts
        else:
            mw[i] = wts
        if k == 0:
            h = _first_norm(x_in, g, sc, sh)
        else:
            x_in, h = _res_norm(x_in, y_prev, gate_prev, g, sc, sh)
        if is_mlp:
            y, saved = _mlp_fwd(h, w1[i], w2[i])
        else:
            y, saved = mixer_fwd[i % 3](h, mw[i])
        sub.append((x_in, h, y, saved))
        y_prev, gate_prev = y, mods[i][3 * is_mlp + 2]

    loss_part, dxo, dy, dgate = _final_loss(x_in, y_prev, gate_prev, tgt)

    dmods = [[None] * 6 for _ in range(depth)]
    g_norm = {'norm_mix': [None] * depth, 'norm_mlp': [None] * depth}
    g_mix = [None] * depth
    g_w1, g_w2 = [None] * depth, [None] * depth
    for k in reversed(range(nsub)):
        i, is_mlp = k // 2, k % 2
        x_k, h_k, _, saved = sub[k]
        dmods[i][3 * is_mlp + 2] = dgate
        if is_mlp:
            dh, g_w1[i], g_w2[i] = _mlp_bwd(dy, h_k, w1[i], w2[i], saved)
        else:
            dh, g_mix[i] = mixer_bwd[i % 3](dy, h_k, mw[i], saved)
        sc = mods[i][3 * is_mlp + 1]
        g = (norm_mlp if is_mlp else norm_mix)[i:i + 1]
        g = g + send(k, (g_w1[i], g_w2[i]) if is_mlp else g_mix[i])
        if k > 0:
            ip, mp = (k - 1) // 2, (k - 1) % 2
            dxo, dy, (dsh, dsc, dg, dgate) = _bwd_norm_gate(dxo, dh, x_k, sub[k - 1][2], g, sc, mods[ip][3 * mp + 2])
        else:
            dxo, (dsh, dsc, dg) = _bwd_norm_first(dxo, dh, x_k, g, sc)
        dmods[i][3 * is_mlp], dmods[i][3 * is_mlp + 1] = dsh, dsc
        g_norm['norm_mlp' if is_mlp else 'norm_mix'][i] = dg
    return loss_part, dxo, dmods, g_norm, g_mix, g_w1, g_w2


def kernel(x, c, norm_mix, norm_mlp, w_ada, b_ada, w_mlp_in, w_mlp_out, fox_w_in, fox_b_f, fox_q_norm, fox_k_norm, fox_w_out, sg_w_in, sg_ln_g, sg_ln_b, sg_w_s, sg_b_s, sg_w_out, cv_w_pw1, cv_b_pw1, cv_w_dw, cv_b_dw, cv_ln_g, cv_ln_b, cv_w_pw2, cv_b_pw2, loss_target, m_norm_mix, m_norm_mlp, m_w_ada, m_b_ada, m_w_mlp_in, m_w_mlp_out, m_fox_w_in, m_fox_b_f, m_fox_q_norm, m_fox_k_norm, m_fox_w_out, m_sg_w_in, m_sg_ln_g, m_sg_ln_b, m_sg_w_s, m_sg_b_s, m_sg_w_out, m_cv_w_pw1, m_cv_b_pw1, m_cv_w_dw, m_cv_b_dw, m_cv_ln_g, m_cv_ln_b, m_cv_w_pw2, m_cv_b_pw2, v_norm_mix, v_norm_mlp, v_w_ada, v_b_ada, v_w_mlp_in, v_w_mlp_out, v_fox_w_in, v_fox_b_f, v_fox_q_norm, v_fox_k_norm, v_fox_w_out, v_sg_w_in, v_sg_ln_g, v_sg_ln_b, v_sg_w_s, v_sg_b_s, v_sg_w_out, v_cv_w_pw1, v_cv_b_pw1, v_cv_w_dw, v_cv_b_dw, v_cv_ln_g, v_cv_ln_b, v_cv_w_pw2, v_cv_b_pw2):
    P = dict(zip(_ARGS, (x, c, norm_mix, norm_mlp, w_ada, b_ada, w_mlp_in, w_mlp_out, fox_w_in, fox_b_f, fox_q_norm, fox_k_norm, fox_w_out, sg_w_in, sg_ln_g, sg_ln_b, sg_w_s, sg_b_s, sg_w_out, cv_w_pw1, cv_b_pw1, cv_w_dw, cv_b_dw, cv_ln_g, cv_ln_b, cv_w_pw2, cv_b_pw2, loss_target, m_norm_mix, m_norm_mlp, m_w_ada, m_b_ada, m_w_mlp_in, m_w_mlp_out, m_fox_w_in, m_fox_b_f, m_fox_q_norm, m_fox_k_norm, m_fox_w_out, m_sg_w_in, m_sg_ln_g, m_sg_ln_b, m_sg_w_s, m_sg_b_s, m_sg_w_out, m_cv_w_pw1, m_cv_b_pw1, m_cv_w_dw, m_cv_b_dw, m_cv_ln_g, m_cv_ln_b, m_cv_w_pw2, m_cv_b_pw2, v_norm_mix, v_norm_mlp, v_w_ada, v_b_ada, v_w_mlp_in, v_w_mlp_out, v_fox_w_in, v_fox_b_f, v_fox_q_norm, v_fox_k_norm, v_fox_w_out, v_sg_w_in, v_sg_ln_g, v_sg_ln_b, v_sg_w_s, v_sg_b_s, v_sg_w_out, v_cv_w_pw1, v_cv_b_pw1, v_cv_w_dw, v_cv_b_dw, v_cv_ln_g, v_cv_ln_b, v_cv_w_pw2, v_cv_b_pw2)))
    me = 4 * lax.axis_index("x") + 2 * lax.axis_index("y") + lax.axis_index("c")
    xs = x[0]
    tgt = loss_target[0]
    s_len, d = xs.shape
    depth = norm_mix.shape[0]
    n_fox, n_sg, n_cv = fox_w_in.shape[0], sg_w_in.shape[0], cv_w_pw1.shape[0]
    heads = d // HEAD_DIM
    bf = lambda a: a.astype(MXU_DTYPE)

    cv_small = ['cv_b_pw1', 'cv_w_dw', 'cv_b_dw', 'cv_ln_g', 'cv_ln_b', 'cv_b_pw2']
    small_shapes = [c.shape] + [P[n].shape for n in cv_small]
    (small_all,) = _exchange("gather_small", [_pack([c] + [P[n] for n in cv_small])], scatter=False)
    sm = dict(zip(['c'] + cv_small, _unpack(small_all, small_shapes, lead=(N_DEV,))))
    c_all = sm['c'][:, 0, :]
    cat_last = lambda a: jnp.moveaxis(a, 0, -2).reshape(a.shape[1:-1] + (-1,))
    cvf = {n: cat_last(sm[n]) for n in cv_small}

    big = ['w_mlp_in', 'w_mlp_out', 'fox_w_in', 'fox_w_out', 'sg_w_in', 'sg_w_out', 'cv_w_pw1', 'cv_w_pw2']
    col_sharded = {'w_mlp_in', 'fox_w_in', 'sg_w_in', 'cv_w_pw1'}
    mixer_names = (('fox_w_in', 'fox_w_out'), ('sg_w_in', 'sg_w_out'), ('cv_w_pw1', 'cv_w_pw2'))
    nsub = 2 * depth
    groups = [[('w_mlp_in', k // 2), ('w_mlp_out', k // 2)] if k % 2 else
              [(nm, k // 6) for nm in mixer_names[(k // 2) % 3]] for k in range(nsub)]
    gather_handles = [None] * nsub
    scatter_handles = [None] * nsub

    c_act = c_all * _sigmoid(c_all)
    c_pad = bf(jnp.pad(c_act, ((0, 16 - N_DEV), (0, 0))))
    n_ada = w_ada.shape[2]
    (mod_part,) = _mm("ada_mod", c_pad, bf(jnp.transpose(w_ada, (1, 0, 2)).reshape(d, depth * n_ada)),
                      epi=lambda acc, b: (acc + b,),
                      vecs=(lax.dynamic_slice_in_dim(b_ada, me * n_ada, n_ada, axis=1).reshape(1, depth * n_ada),),
                      tn=n_ada)
    (mod_all,) = _exchange("gather_mod", [mod_part], scatter=False)
    mod_me = lax.dynamic_index_in_dim(mod_all, me, axis=1, keepdims=False)
    mod = jnp.transpose(mod_me.reshape(N_DEV, depth, n_ada), (1, 0, 2)).reshape(depth, 6 * d)
    mods = [[mod[i:i + 1, k * d:(k + 1) * d] for k in range(6)] for i in range(depth)]

    def mixer_weights(i, full_weight):
        kind, j = i % 3, i // 3
        if kind == 0:
            w_in = full_weight('fox_w_in')
            n_pad = -(-w_in.shape[1] // (5 * LANES)) * (5 * LANES)
            return dict(w_in=_pad_cols(w_in, n_pad), w_out=full_weight('fox_w_out'),
                        b_f=_pad_cols(fox_b_f[j:j + 1], LANES),
                        qg=jnp.tile(fox_q_norm[j:j + 1], (1, 2)), kg=jnp.tile(fox_k_norm[j:j + 1], (1, 2)))
        if kind == 1:
            return dict(w_in=full_weight('sg_w_in'), w_out=full_weight('sg_w_out'),
                        ln_g=sg_ln_g[j:j + 1], ln_b=sg_ln_b[j:j + 1],
                        w_s=sg_w_s[j].reshape(SG_GROUPS * SG_CHUNK, SG_CHUNK), b_st=_pad_cols(sg_b_s[j].T, LANES))
        return dict(w_pw1=full_weight('cv_w_pw1'), w_pw2=full_weight('cv_w_pw2'),
                    b_pw1=cvf['cv_b_pw1'][j:j + 1], b_pw2=cvf['cv_b_pw2'][j:j + 1],
                    w_dw=jnp.pad(cvf['cv_w_dw'][j], ((0, CONV_PAD - CONV_WIDTH), (0, 0))),
                    b_dw=cvf['cv_b_dw'][j:j + 1], ln_g=cvf['cv_ln_g'][j:j + 1], ln_b=cvf['cv_ln_b'][j:j + 1])

    dep = mod_all
    for k in range(nsub):
        gather_handles[k], dep = _exchange_start(f"gather_start_{k}", [bf(P[nm][j]) for nm, j in groups[k]],
                                                 False, dep)
    first_token = dep

    def fetch(k, dep):
        got = _exchange_wait(f"gather_wait_{k}", gather_handles[k], False, dep)
        by_name = {nm: g for (nm, _), g in zip(groups[k], got)}

        def full_weight(name):
            g = by_name[name]
            if name in col_sharded:
                return jnp.transpose(g, (1, 0, 2)).reshape(g.shape[1], -1)
            return g.reshape(-1, g.shape[2])
        wts = (full_weight('w_mlp_in'), full_weight('w_mlp_out')) if k % 2 else mixer_weights(k // 2, full_weight)
        return wts, (first_token if k == 0 else jnp.zeros((1, 1), F32))

    def to_slots(name, g2d):
        if name in col_sharded:
            r = g2d.shape[0]
            return jnp.transpose(g2d.reshape(r, N_DEV, -1), (1, 0, 2))
        return g2d.reshape(N_DEV, -1, g2d.shape[1])

    def send(k, grads, dep=None):
        key = {'fox_w_in': 'w_in', 'fox_w_out': 'w_out', 'sg_w_in': 'w_in', 'sg_w_out': 'w_out',
               'cv_w_pw1': 'w_pw1', 'cv_w_pw2': 'w_pw2'}
        slots = []
        for nm, _ in groups[k]:
            g2d = grads[0] if nm == 'w_mlp_in' else grads[1] if nm == 'w_mlp_out' else grads[key[nm]]
            slots.append(to_slots(nm, g2d[:, :P[nm].shape[-1] * N_DEV] if nm in col_sharded else g2d))
        scatter_handles[k], token = _exchange_start(f"scatter_start_{k}", slots, True, slots[0] if dep is None else dep)
        return token

    send_later = lambda k, grads: jnp.zeros((1, 1), F32) if k == 0 else send(k, grads)
    loss_part, dxo, dmods, g_norm, g_mix, _, _ = _local_step(xs, tgt, mods, norm_mix, norm_mlp, fetch, send_later)
    loss = lax.psum(loss_part, ("x", "y", "c"))
    grad_x = dxo[None]

    stack = lambda key, kind: jnp.stack([g_mix[i][key].reshape(P[name_of[(kind, key)]].shape[1:])
                                         for i in range(depth) if i % 3 == kind])
    name_of = {(0, 'b_f'): 'fox_b_f', (0, 'qg'): 'fox_q_norm', (0, 'kg'): 'fox_k_norm',
               (1, 'ln_g'): 'sg_ln_g', (1, 'ln_b'): 'sg_ln_b', (1, 'w_s'): 'sg_w_s', (1, 'b_s'): 'sg_b_s'}
    dmod_me = jnp.concatenate([jnp.concatenate(r, axis=1) for r in dmods], axis=0)
    small_g = {'dmod': dmod_me,
               'norm_mix': jnp.concatenate(g_norm['norm_mix'], axis=0),
               'norm_mlp': jnp.concatenate(g_norm['norm_mlp'], axis=0)}
    for (kind, key), nm in name_of.items():
        small_g[nm] = stack(key, kind)
    cv_keys = {'cv_b_pw1': 'b_pw1', 'cv_w_dw': 'w_dw', 'cv_b_dw': 'b_dw', 'cv_ln_g': 'ln_g', 'cv_ln_b': 'ln_b',
               'cv_b_pw2': 'b_pw2'}
    for nm, key in cv_keys.items():
        small_g[nm] = jnp.stack([g_mix[i][key].reshape(cvf[nm].shape[1:]) for i in range(depth) if i % 3 == 2])
    sg_names = list(small_g)
    sg_shapes = [small_g[n].shape for n in sg_names]
    (sg_all,) = _exchange("gather_small_grads", [_pack([small_g[n] for n in sg_names])], scatter=False)
    send(0, g_mix[0], sg_all)

    dmod_all = _unpack(sg_all, sg_shapes, lead=(N_DEV,))[0]
    out = {}

    def finish(name, parts, shard_of=None):
        w, m, v = P[name], P['m_' + name], P['v_' + name]
        cols = w.shape[-1]
        r2 = lambda a: a.reshape(-1, cols)
        rows = r2(w).shape[0]
        res = _adamw("adamw_" + name, parts.reshape(parts.shape[0], rows, cols), r2(w), r2(m), r2(v),
                     _row_tile(rows, cols))
        out[name] = tuple(r.reshape(w.shape) for r in res)

    c_t = c_act.T
    ada_g = []
    for i in range(depth):
        blk = lax.dynamic_slice_in_dim(dmod_all[:, i, :], me * n_ada, n_ada, axis=1)
        ada_g.append(_ada_outer(c_t, blk))
    finish('w_ada', jnp.stack(ada_g)[None])
    finish('b_ada', dmod_all)

    sm_names = [n for n in sg_names if n != 'dmod']
    sm_parts = jnp.stack([_pack([_unpack(sg_all[q], sg_shapes)[sg_names.index(n)] for n in sm_names])
                          for q in range(N_DEV)])

    def local_block(nm, a):
        if nm in cv_keys:
            n_loc = P[nm].shape[-1]
            return lax.dynamic_slice_in_dim(a, me * n_loc, n_loc, axis=a.ndim - 1)
        return a
    full_shapes = [small_g[n].shape for n in sm_names]

    def pack_full(prefix):
        arrs = []
        for nm in sm_names:
            a = P[prefix + nm]
            if nm in cv_keys:
                full = jnp.zeros(small_g[nm].shape, F32)
                a = lax.dynamic_update_slice_in_dim(full, a, me * a.shape[-1], axis=a.ndim - 1)
            arrs.append(a)
        return _pack(arrs)
    res = _adamw("adamw_small", sm_parts, pack_full(''), pack_full('m_'), pack_full('v_'),
                 _row_tile(sm_parts.shape[1], LANES))
    unp = [_unpack(r, full_shapes) for r in res]
    for idx, nm in enumerate(sm_names):
        out[nm] = tuple(local_block(nm, unp[t][idx]) for t in range(4))

    per_layer = {}
    last = out['w_ada'][0]
    for k in reversed(range(nsub)):
        recv = _exchange_wait(f"scatter_wait_{k}", scatter_handles[k], True, last)
        for (nm, j), parts in zip(groups[k], recv):
            w, m, v = P[nm][j], P['m_' + nm][j], P['v_' + nm][j]
            per_layer[(nm, j)] = _adamw(f"adamw_{nm}_{j}", parts, w, m, v, _row_tile(*w.shape))
            last = per_layer[(nm, j)][0]
    for nm in big:
        out[nm] = tuple(jnp.stack([per_layer[(nm, j)][t] for j in range(P[nm].shape[0])]) for t in range(4))

    outs = [loss, grad_x]
    for t in range(4):
        outs += [out[n][t] for n in _WEIGHTS]
    return tuple(outs)
```

```python
import functools
import math

import jax
import jax.numpy as jnp
from jax import lax
from jax.experimental import pallas as pl
from jax.experimental.pallas import tpu as pltpu

F32 = jnp.float32
MXU_DTYPE = jnp.bfloat16
EPS = 1e-6
N_DEV = 8
HEAD_DIM = 64
LANES = 128
CONV_WIDTH = 31
CONV_PAD = 32
SG_CHUNK = 128
SG_BLOCK = 64
SG_GROUPS = 8
SCAN_BLOCK = 256
VMEM_LIMIT = 48 * 1024 * 1024
ATT_BWD_VMEM_LIMIT = 56 * 1024 * 1024
ADAM_LR, ADAM_B1, ADAM_B2, ADAM_EPS, ADAM_WD, ADAM_STEP = 0.001, 0.9, 0.999, 1e-08, 0.01, 10
NEG = -1e30

_WEIGHTS = ['norm_mix', 'norm_mlp', 'w_ada', 'b_ada', 'w_mlp_in', 'w_mlp_out', 'fox_w_in', 'fox_b_f',
            'fox_q_norm', 'fox_k_norm', 'fox_w_out', 'sg_w_in', 'sg_ln_g', 'sg_ln_b', 'sg_w_s', 'sg_b_s',
            'sg_w_out', 'cv_w_pw1', 'cv_b_pw1', 'cv_w_dw', 'cv_b_dw', 'cv_ln_g', 'cv_ln_b', 'cv_w_pw2',
            'cv_b_pw2']
_ARGS = ['x', 'c'] + _WEIGHTS + ['loss_target'] + ['m_' + n for n in _WEIGHTS] + ['v_' + n for n in _WEIGHTS]


def _cparams(sem=None, vmem=VMEM_LIMIT):
    return pltpu.CompilerParams(dimension_semantics=sem, vmem_limit_bytes=vmem)


def _colsum(v):
    return jnp.sum(v, axis=0, keepdims=True)


def _sigmoid(v):
    return 1.0 / (1.0 + jnp.exp(-v))


def _rowwise(name, fn, rows, consts, row_out, red_out, tr):
    n_rows = rows[0].shape[0]
    tr = min(tr, n_rows)
    assert n_rows % tr == 0
    nr, nc, no = len(rows), len(consts), len(row_out)

    def body(*refs):
        ins = [r[...] for r in refs[:nr + nc]]
        outs, reds = fn(*ins)
        out_refs = refs[nr + nc:nr + nc + no]
        red_refs = refs[nr + nc + no:]
        for o_ref, o in zip(out_refs, outs):
            o_ref[...] = o.astype(o_ref.dtype)
        if red_refs:
            @pl.when(pl.program_id(0) == 0)
            def _():
                for r_ref in red_refs:
                    r_ref[...] = jnp.zeros_like(r_ref)
            for r_ref, r in zip(red_refs, reds):
                r_ref[...] += r

    def rspec(a):
        return pl.BlockSpec((tr,) + a.shape[1:], lambda i: (i,) + (0,) * (a.ndim - 1))

    def cspec(shape):
        return pl.BlockSpec(shape, lambda i: (0,) * len(shape))

    out_shape = [jax.ShapeDtypeStruct((n_rows, w), dt) for w, dt in row_out]
    out_shape += [jax.ShapeDtypeStruct(s, F32) for s in red_out]
    out_specs = [pl.BlockSpec((tr, w), lambda i: (i, 0)) for w, _ in row_out] + [cspec(s) for s in red_out]
    res = pl.pallas_call(
        body, name=name, grid=(n_rows // tr,),
        in_specs=[rspec(a) for a in rows] + [cspec(a.shape) for a in consts],
        out_specs=out_specs, out_shape=out_shape,
        compiler_params=_cparams(("arbitrary",)),
    )(*rows, *consts)
    return res[:no], res[no:]


def _mm(name, a, b, *, ta=False, tb=False, out_dtypes=(F32,), epi=None, tiles=(), vecs=(), tm=512, tn=512):
    m_dim, k_dim = (a.shape[1], a.shape[0]) if ta else a.shape
    n_dim = b.shape[0] if tb else b.shape[1]
    assert (b.shape[1] if tb else b.shape[0]) == k_dim
    tm, tn = min(tm, m_dim), min(tn, n_dim)
    assert m_dim % tm == 0 and n_dim % tn == 0, (name, m_dim, n_dim, tm, tn)
    dims = (((0 if ta else 1,), (1 if tb else 0,)), ((), ()))
    nx = len(tiles) + len(vecs)

    def body(a_ref, b_ref, *rest):
        acc = lax.dot_general(a_ref[...], b_ref[...], dims, preferred_element_type=F32)
        outs = epi(acc, *[r[...] for r in rest[:nx]]) if epi is not None else (acc,)
        for o_ref, o in zip(rest[nx:], outs):
            o_ref[...] = o.astype(o_ref.dtype)

    a_spec = pl.BlockSpec((k_dim, tm), lambda i, j: (0, i)) if ta else pl.BlockSpec((tm, k_dim), lambda i, j: (i, 0))
    b_spec = pl.BlockSpec((tn, k_dim), lambda i, j: (j, 0)) if tb else pl.BlockSpec((k_dim, tn), lambda i, j: (0, j))
    t_spec = pl.BlockSpec((tm, tn), lambda i, j: (i, j))
    v_spec = pl.BlockSpec((1, tn), lambda i, j: (0, j))
    res = pl.pallas_call(
        body, name=name, grid=(m_dim // tm, n_dim // tn),
        in_specs=[a_spec, b_spec] + [t_spec] * len(tiles) + [v_spec] * len(vecs),
        out_specs=[t_spec] * len(out_dtypes),
        out_shape=[jax.ShapeDtypeStruct((m_dim, n_dim), dt) for dt in out_dtypes],
        compiler_params=_cparams(("parallel", "parallel")),
    )(a, b, *tiles, *vecs)
    return res


def _exchange_copies(scatter, in_refs, land_refs, send_sems, recv_sems, local_sems):
    n = len(in_refs)
    x, y, c = lax.axis_index("x"), lax.axis_index("y"), lax.axis_index("c")
    me = 4 * x + 2 * y + c
    local = [pltpu.make_async_copy(in_refs[a].at[me] if scatter else in_refs[a], land_refs[a].at[me],
                                   local_sems.at[a]) for a in range(n)]
    send, arrive = [], []
    for k in range(1, N_DEV):
        px, py, pc = x ^ ((k >> 2) & 1), y ^ ((k >> 1) & 1), c ^ (k & 1)
        peer = 4 * px + 2 * py + pc
        for a in range(n):
            src = in_refs[a].at[peer] if scatter else in_refs[a]
            sems = dict(send_sem=send_sems.at[a * (N_DEV - 1) + k - 1], recv_sem=recv_sems.at[a * (N_DEV - 1) + k - 1],
                        device_id=(px, py, pc), device_id_type=pl.DeviceIdType.MESH)
            send.append(pltpu.make_async_remote_copy(src_ref=src, dst_ref=land_refs[a].at[me], **sems))
            arrive.append(pltpu.make_async_remote_copy(src_ref=src, dst_ref=land_refs[a].at[peer], **sems))
    return local, send, arrive


def _land_shape(a, scatter):
    return ((N_DEV,) + a.shape[1:]) if scatter else ((N_DEV,) + a.shape)


def _exchange(name, arrays, scatter):
    n = len(arrays)

    def body(*refs):
        local, send, arrive = _exchange_copies(scatter, refs[:n], refs[n:2 * n], *refs[2 * n:])
        for cp in local + send:
            cp.start()
        for cp, arr in zip(send, arrive):
            cp.wait_send()
            arr.wait_recv()
        for cp in local:
            cp.wait()

    any_spec = pl.BlockSpec(memory_space=pl.ANY)
    return pl.pallas_call(
        body, name=name,
        in_specs=[any_spec] * n, out_specs=[any_spec] * n,
        out_shape=[jax.ShapeDtypeStruct(_land_shape(a, scatter), a.dtype) for a in arrays],
        scratch_shapes=[pltpu.SemaphoreType.DMA((n * (N_DEV - 1),)),
                        pltpu.SemaphoreType.DMA((n * (N_DEV - 1),)),
                        pltpu.SemaphoreType.DMA((n,))],
        compiler_params=pltpu.CompilerParams(has_side_effects=True),
    )(*arrays)


_HBM = pl.BlockSpec(memory_space=pltpu.HBM)
_SEM = pl.BlockSpec(memory_space=pltpu.SEMAPHORE)
_EFFECT = pltpu.SideEffectType.DATAFLOW_SIDE_EFFECTING


def _exchange_start(name, arrays, scatter, dep):
    n = len(arrays)
    nsem = n * (N_DEV - 1)
    srcs = [pltpu.with_memory_space_constraint(a, pltpu.HBM) for a in arrays]
    lands = [pltpu.with_memory_space_constraint(lax.empty(_land_shape(a, scatter), a.dtype), pltpu.HBM) for a in arrays]

    def body(*refs):
        sems = refs[2 * n + 1:2 * n + 4]
        local, send, _ = _exchange_copies(scatter, refs[:n], refs[n:2 * n], *sems)
        for cp in local + send:
            cp.start()
        token = refs[-1]
        token[...] = jnp.zeros_like(token)

    res = pl.pallas_call(
        body, name=name,
        in_specs=[_HBM] * (2 * n) + [pl.BlockSpec(memory_space=pl.ANY)],
        out_specs=[_SEM] * 3 + [_HBM] * (2 * n) + [pl.BlockSpec(memory_space=pltpu.VMEM)],
        out_shape=[pltpu.SemaphoreType.DMA((nsem,)), pltpu.SemaphoreType.DMA((nsem,)), pltpu.SemaphoreType.DMA((n,))]
        + [pltpu.HBM(a.shape, a.dtype) for a in arrays]
        + [pltpu.HBM(_land_shape(a, scatter), a.dtype) for a in arrays]
        + [jax.ShapeDtypeStruct((8, LANES), F32)],
        input_output_aliases={i: 3 + i for i in range(2 * n)},
        compiler_params=pltpu.CompilerParams(has_side_effects=_EFFECT),
    )(*srcs, *lands, dep)
    return res[:-1], res[-1][0:1, 0:1]


def _exchange_wait(name, handles, scatter, after):
    n = (len(handles) - 3) // 2
    sems, thru = handles[:3], handles[3:]

    def body(*refs):
        local, send, arrive = _exchange_copies(scatter, refs[:n], refs[n:2 * n], *refs[2 * n:2 * n + 3])
        for cp, arr in zip(send, arrive):
            cp.wait_send()
            arr.wait_recv()
        for cp in local:
            cp.wait()

    res = pl.pallas_call(
        body, name=name,
        in_specs=[_HBM] * (2 * n) + [_SEM] * 3 + [pl.BlockSpec(memory_space=pl.ANY)],
        out_specs=[_HBM] * (2 * n),
        out_shape=[pltpu.HBM(t.shape, t.dtype) for t in thru],
        input_output_aliases={i: i for i in range(2 * n)},
        compiler_params=pltpu.CompilerParams(has_side_effects=_EFFECT),
    )(*thru, *sems, after)
    return res[n:]


def _norm_mod(x, g, sc, sh):
    r = lax.rsqrt(jnp.mean(x * x, axis=-1, keepdims=True) + EPS)
    return (x * r * g) * (1.0 + sc) + sh


def _first_norm(x, g, sc, sh):
    (h,), _ = _rowwise("first_norm", lambda x, g, sc, sh: ((_norm_mod(x, g, sc, sh),), ()),
                       [x], [g, sc, sh], [(x.shape[1], MXU_DTYPE)], [], 256)
    return h


def _res_norm(x, y, gate, g, sc, sh):
    def fn(x, y, gate, g, sc, sh):
        xn = x + gate * y
        return (xn, _norm_mod(xn, g, sc, sh)), ()
    (xn, h), _ = _rowwise("res_norm", fn, [x, y], [gate, g, sc, sh],
                          [(x.shape[1], F32), (x.shape[1], MXU_DTYPE)], [], 256)
    return xn, h


def _final_loss(x, y, gate, target):
    d = x.shape[1]

    def fn(x, y, target, gate):
        err = (x + gate * y) - target
        part = jnp.sum(jnp.sum(err * err, axis=-1, keepdims=True), axis=0, keepdims=True) * (0.5 / d)
        dx = err * (1.0 / d)
        return (dx, dx * gate), (jnp.broadcast_to(part, (1, LANES)), _colsum(dx * y))
    (dx, dy), (loss, dgate) = _rowwise("final_loss", fn, [x, y, target], [gate],
                                       [(d, F32), (d, MXU_DTYPE)], [(1, LANES), (1, d)], 256)
    return loss[0, 0], dx, dy, dgate


def _norm_bwd_core(dxo, dh, x, g, sc):
    r = lax.rsqrt(jnp.mean(x * x, axis=-1, keepdims=True) + EPS)
    xn = x * r
    dsh = _colsum(dh)
    dsc = _colsum(dh * (xn * g))
    dyy = dh * (1.0 + sc)
    dg = _colsum(dyy * xn)
    dxn = dyy * g
    dxi = dxo + r * (dxn - xn * jnp.mean(dxn * xn, axis=-1, keepdims=True))
    return dxi, dsh, dsc, dg


def _bwd_norm_gate(dxo, dh, x, y_prev, g, sc, gate_prev):
    d = x.shape[1]

    def fn(dxo, dh, x, y_prev, g, sc, gate_prev):
        dxi, dsh, dsc, dg = _norm_bwd_core(dxo, dh, x, g, sc)
        return (dxi, dxi * gate_prev), (dsh, dsc, dg, _colsum(dxi * y_prev))
    (dxi, dy), reds = _rowwise("bwd_norm_gate", fn, [dxo, dh, x, y_prev], [g, sc, gate_prev],
                               [(d, F32), (d, MXU_DTYPE)], [(1, d)] * 4, 256)
    return dxi, dy, reds


def _bwd_norm_first(dxo, dh, x, g, sc):
    d = x.shape[1]

    def fn(dxo, dh, x, g, sc):
        dxi, dsh, dsc, dg = _norm_bwd_core(dxo, dh, x, g, sc)
        return (dxi,), (dsh, dsc, dg)
    (dxi,), reds = _rowwise("bwd_norm_first", fn, [dxo, dh, x], [g, sc], [(d, F32)], [(1, d)] * 3, 256)
    return dxi, reds


def _mlp_fwd(h, w1, w2):
    def epi(acc):
        r = jnp.maximum(acc, 0.0)
        return acc, r * r
    a, z = _mm("mlp_in", h, w1, out_dtypes=(MXU_DTYPE, MXU_DTYPE), epi=epi, tm=1024, tn=512)
    (out,) = _mm("mlp_out", z, w2, tm=512, tn=512)
    return out, (a, z)


def _mlp_bwd(dy, h, w1, w2, saved):
    a, z = saved

    def epi(acc, a):
        return (acc * (2.0 * jnp.maximum(a.astype(F32), 0.0)),)
    (da,) = _mm("mlp_dz", dy, w2, tb=True, out_dtypes=(MXU_DTYPE,), epi=epi, tiles=(a,), tm=1024, tn=512)
    (dw2,) = _mm("mlp_dw2", z, dy, ta=True, out_dtypes=(MXU_DTYPE,))
    (dw1,) = _mm("mlp_dw1", h, da, ta=True, out_dtypes=(MXU_DTYPE,))
    (dh,) = _mm("mlp_dh", da, w1, tb=True)
    return dh, dw1, dw2


def _split3(v):
    hi = v.astype(jnp.bfloat16)
    r1 = v - hi.astype(F32)
    mid = r1.astype(jnp.bfloat16)
    lo = (r1 - mid.astype(F32)).astype(jnp.bfloat16)
    return hi, mid, lo


def _tri_matmul(tri, v):
    hi, mid, lo = _split3(v)
    dot = functools.partial(jnp.dot, preferred_element_type=F32)
    return dot(tri, hi) + dot(tri, mid) + dot(tri, lo)


def _log_sigmoid(v):
    return jnp.minimum(v, 0.0) - jnp.log(1.0 + jnp.exp(-jnp.abs(v)))


def _gate_fwd(proj, b_pad, col_block):
    s = proj.shape[0]
    tb = min(SCAN_BLOCK, s)
    nblk = s // tb

    def body(f_ref, b_ref, o_ref):
        row = lax.broadcasted_iota(jnp.int32, (tb, tb), 0)
        col = lax.broadcasted_iota(jnp.int32, (tb, tb), 1)
        tri = (col <= row).astype(jnp.bfloat16)

        def step(i, carry):
            rows = pl.ds(pl.multiple_of(i * tb, tb), tb)
            lf = _log_sigmoid(f_ref[rows, :] + b_ref[...])
            f = _tri_matmul(tri, lf) + carry
            o_ref[rows, :] = f
            return f[tb - 1:tb, :]
        lax.fori_loop(0, nblk, step, jnp.zeros((1, LANES), F32))

    return pl.pallas_call(
        body, name="gate_fwd", grid=(1,),
        in_specs=[pl.BlockSpec((s, LANES), lambda i: (0, col_block)), pl.BlockSpec((1, LANES), lambda i: (0, 0))],
        out_specs=pl.BlockSpec((s, LANES), lambda i: (0, 0)),
        out_shape=jax.ShapeDtypeStruct((s, LANES), F32),
        compiler_params=_cparams(("arbitrary",)),
    )(proj, b_pad)


def _gate_bwd(proj, b_pad, d_f, col_block):
    s = proj.shape[0]
    tb = min(SCAN_BLOCK, s)
    nblk = s // tb

    def body(f_ref, b_ref, d_ref, o_ref, db_ref):
        row = lax.broadcasted_iota(jnp.int32, (tb, tb), 0)
        col = lax.broadcasted_iota(jnp.int32, (tb, tb), 1)
        tri = (col >= row).astype(jnp.bfloat16)

        def step(j, carry):
            acc, db = carry
            i = nblk - 1 - j
            rows = pl.ds(pl.multiple_of(i * tb, tb), tb)
            dlf = _tri_matmul(tri, d_ref[rows, :]) + acc
            dpre = dlf * _sigmoid(-(f_ref[rows, :] + b_ref[...]))
            o_ref[rows, :] = dpre.astype(o_ref.dtype)
            return dlf[0:1, :], db + _colsum(dpre)
        _, db = lax.fori_loop(0, nblk, step, (jnp.zeros((1, LANES), F32), jnp.zeros((1, LANES), F32)))
        db_ref[...] = db

    return pl.pallas_call(
        body, name="gate_bwd", grid=(1,),
        in_specs=[pl.BlockSpec((s, LANES), lambda i: (0, col_block)), pl.BlockSpec((1, LANES), lambda i: (0, 0)),
                  pl.BlockSpec((s, LANES), lambda i: (0, 0))],
        out_specs=[pl.BlockSpec((s, LANES), lambda i: (0, 0)), pl.BlockSpec((1, LANES), lambda i: (0, 0))],
        out_shape=[jax.ShapeDtypeStruct((s, LANES), MXU_DTYPE), jax.ShapeDtypeStruct((1, LANES), F32)],
        compiler_params=_cparams(("arbitrary",)),
    )(proj, b_pad, d_f)


def _head_masks():
    lane = lax.broadcasted_iota(jnp.int32, (1, LANES), 1)
    return lane < HEAD_DIM


def _pair_norm(v, g, first):
    v2 = v * v
    ss0 = jnp.sum(jnp.where(first, v2, 0.0), axis=-1, keepdims=True)
    ss1 = jnp.sum(jnp.where(first, 0.0, v2), axis=-1, keepdims=True)
    r = jnp.where(first, lax.rsqrt(ss0 * (1.0 / HEAD_DIM) + EPS), lax.rsqrt(ss1 * (1.0 / HEAD_DIM) + EPS))
    vn = v * r
    return vn * g, vn, r


_NT = (((1,), (1,)), ((), ()))
_TN = (((0,), (0,)), ((), ()))

ATT_TQ = 512
ATT_TK = 256
AUG_F, AUG_ONE = 0, 3


def _own_lanes(hd):
    lane = lax.broadcasted_iota(jnp.int32, (1, LANES), 1)
    return (lane < HEAD_DIM) if hd == 0 else (lane >= HEAD_DIM)


def _aug_lanes(hd, f_other):
    lane = lax.broadcasted_iota(jnp.int32, (1, LANES), 1) - (HEAD_DIM if hd == 0 else 0)
    hi, mid, lo = [t.astype(F32) for t in _split3(f_other)]
    zero = jnp.zeros_like(f_other)
    f_terms = jnp.where(lane == 0, hi, jnp.where(lane == 1, mid, jnp.where(lane == 2, lo, zero)))
    f_shift = jnp.where(lane == 3, hi, jnp.where(lane == 4, mid, jnp.where(lane == 5, lo, zero)))
    ones_lo = jnp.where(lane < 3, 1.0, 0.0) * jnp.where(lane >= 0, 1.0, 0.0)
    ones_hi = jnp.where(lane < 6, 1.0, 0.0) * jnp.where(lane >= 3, 1.0, 0.0)
    return f_terms + ones_hi, ones_lo - f_shift


def _attn_operands(q_raw, k_raw, f_rep, qg, kg, scale):
    first = _head_masks()
    qn, _, _ = _pair_norm(q_raw, qg, first)
    kn, _, _ = _pair_norm(k_raw, kg, first)
    f_other = pltpu.roll(f_rep, HEAD_DIM, 1)
    out = []
    for hd in range(2):
        own = _own_lanes(hd)
        q_x, k_x = _aug_lanes(hd, f_other)
        out.append((jnp.where(own, qn * scale, q_x), jnp.where(own, kn, k_x)))
    return out


def _causal_t(tk, tq, off):
    r = lax.broadcasted_iota(jnp.int32, (tk, tq), 0)
    c = lax.broadcasted_iota(jnp.int32, (tk, tq), 1)
    return (r - c) <= off


def _big(shape, index_map):
    return pl.BlockSpec(shape, index_map, pipeline_mode=pl.Buffered(1))


def _attn_fwd_t(proj, f_rep, qg, kg, d_model):
    s = proj.shape[0]
    pairs = d_model // LANES
    tq, tk = min(ATT_TQ, s), min(ATT_TK, s)
    assert (tk % tq == 0 or tq % tk == 0) and s % tk == 0 and s % tq == 0
    nq = s // tq
    n_diag = max(1, tq // tk)
    scale = HEAD_DIM ** -0.5
    ch = tk

    def body(q_ref, k_ref, v_ref, frep_ref, qg_ref, kg_ref, o_ref, lse_ref, qt_s, k_s, vt_s):
        for ci in range(s // ch):
            rows = pl.ds(ci * ch, ch)
            ops = _attn_operands(q_ref[rows, :], k_ref[rows, :], frep_ref[rows, :], qg_ref[...], kg_ref[...], scale)
            vv = v_ref[rows, :]
            for hd in range(2):
                own = _own_lanes(hd)
                lane = lax.broadcasted_iota(jnp.int32, (1, LANES), 1)
                one_lane = lane == (HEAD_DIM if hd == 0 else 0)
                qt_s[hd, :, rows] = ops[hd][0].T.astype(qt_s.dtype)
                k_s[hd, rows, :] = ops[hd][1].astype(k_s.dtype)
                vt_s[hd, :, rows] = jnp.where(own, vv, jnp.where(one_lane, 1.0, 0.0)).T.astype(vt_s.dtype)

        def q_block(qi, _):
            q0 = pl.multiple_of(qi * tq, tq)
            qcols = pl.ds(q0, tq)
            nfull = q0 // tk
            qts = [qt_s[hd, :, qcols] for hd in range(2)]

            def krows(kj):
                return pl.ds(pl.multiple_of(kj * tk, tk), tk)

            def scores(hd, kj):
                return jnp.dot(k_s[hd, krows(kj), :], qts[hd], preferred_element_type=F32)

            def kv_step(kj, carry, masked, last=False):
                new = []
                for hd in range(2):
                    m, acc, p_prev = carry[hd]
                    st = scores(hd, kj)
                    pv = jnp.dot(vt_s[hd, :, krows(jnp.maximum(kj - 1, 0))], p_prev, preferred_element_type=F32)
                    if masked:
                        st = jnp.where(_causal_t(tk, tq, q0 - kj * tk), st, NEG)
                    m_new = jnp.maximum(m, jnp.max(st, axis=0, keepdims=True))
                    p = jnp.exp(st - m_new).astype(vt_s.dtype)
                    acc = jnp.exp(m - m_new) * (acc + pv)
                    if last:
                        acc = acc + jnp.dot(vt_s[hd, :, krows(kj)], p, preferred_element_type=F32)
                        new.append((m_new, acc))
                    else:
                        new.append((m_new, acc, p))
                return tuple(new)

            init = tuple((jnp.full((1, tq), NEG, F32), jnp.zeros((LANES, tq), F32),
                          jnp.zeros((tk, tq), vt_s.dtype)) for hd in range(2))
            carry = lax.fori_loop(0, nfull, lambda kj, cr: kv_step(kj, cr, False), init)
            for t in range(n_diag):
                carry = kv_step(nfull + t, carry, True, t == n_diag - 1)
            o_parts, lse_parts = [], []
            for hd, (m, acc) in enumerate(carry):
                e0 = HEAD_DIM if hd == 0 else 0
                l = acc[e0:e0 + 1, :]
                o_parts.append((acc / l).T)
                lse_parts.append(m + jnp.log(l))
            o_ref[pl.ds(q0, tq), :] = jnp.where(_head_masks(), o_parts[0], o_parts[1]).astype(o_ref.dtype)
            lse_ref[0, :, qcols] = jnp.concatenate(lse_parts, axis=0)
            return 0
        lax.fori_loop(0, nq, q_block, 0)

    blk = lambda off: _big((s, LANES), lambda h: (0, off + h))
    vec = pl.BlockSpec((1, LANES), lambda h: (0, 0))
    return pl.pallas_call(
        body, name="attn_fwd", grid=(pairs,),
        in_specs=[blk(0), blk(pairs), blk(2 * pairs), blk(0), vec, vec],
        out_specs=[pl.BlockSpec((s, LANES), lambda h: (0, h)), pl.BlockSpec((1, 2, s), lambda h: (h, 0, 0))],
        out_shape=[jax.ShapeDtypeStruct((s, d_model), MXU_DTYPE), jax.ShapeDtypeStruct((pairs, 2, s), F32)],
        scratch_shapes=[pltpu.VMEM((2, LANES, s), MXU_DTYPE), pltpu.VMEM((2, s, LANES), MXU_DTYPE),
                        pltpu.VMEM((2, LANES, s), MXU_DTYPE)],
        compiler_params=_cparams(("arbitrary",)),
    )(proj, proj, proj, f_rep, qg, kg)


def _attn_bwd_t(proj, do, o, lse, f_rep, qg, kg, d_model):
    s = proj.shape[0]
    pairs = d_model // LANES
    tq, tk = min(ATT_TQ, s), min(ATT_TK, s)
    assert (tk % tq == 0 or tq % tk == 0) and s % tk == 0 and s % tq == 0
    nq = s // tq
    n_diag = max(1, tq // tk)
    scale = HEAD_DIM ** -0.5
    ch = tk

    def norm_bwd(raw, g, dn, first):
        _, xn, r = _pair_norm(raw, g, first)
        dxn = dn * g
        t = dxn * xn
        mu0 = jnp.sum(jnp.where(first, t, 0.0), axis=-1, keepdims=True)
        mu1 = jnp.sum(jnp.where(first, 0.0, t), axis=-1, keepdims=True)
        mu = jnp.where(first, mu0, mu1) * (1.0 / HEAD_DIM)
        return r * (dxn - xn * mu), _colsum(dn * xn)

    def body(q_ref, k_ref, v_ref, do_ref, o_ref, lse_ref, frep_ref, qg_ref, kg_ref,
             dq_ref, dk_ref, dv_ref, df_ref, dqg_ref, dkg_ref,
             q_s, qt_s, k_s, kt_s, v_s, do_s, dot_s, dl_s, dk_s, dv_s):
        first = _head_masks()
        hp = pl.program_id(0)
        lane = lax.broadcasted_iota(jnp.int32, (1, LANES), 1)
        for ci in range(s // ch):
            rows = pl.ds(ci * ch, ch)
            ops = _attn_operands(q_ref[rows, :], k_ref[rows, :], frep_ref[rows, :], qg_ref[...], kg_ref[...], scale)
            v_s[rows, :] = v_ref[rows, :].astype(v_s.dtype)
            dov = do_ref[rows, :].astype(F32)
            ot = o_ref[rows, :].astype(F32).T
            for hd in range(2):
                own = _own_lanes(hd)
                q_s[hd, rows, :] = ops[hd][0].astype(q_s.dtype)
                qt_s[hd, :, rows] = ops[hd][0].T.astype(qt_s.dtype)
                k_s[hd, rows, :] = ops[hd][1].astype(k_s.dtype)
                kt_s[hd, :, rows] = ops[hd][1].T.astype(kt_s.dtype)
                doh = jnp.where(own, dov, 0.0)
                do_s[hd, rows, :] = doh.astype(do_s.dtype)
                doht = doh.T
                dot_s[hd, :, rows] = doht.astype(dot_s.dtype)
                dl_s[hd:hd + 1, rows] = jnp.sum(doht * ot, axis=0, keepdims=True)
        dk_s[...] = jnp.zeros_like(dk_s)
        dv_s[...] = jnp.zeros_like(dv_s)

        @pl.when(hp == 0)
        def _():
            df_ref[...] = jnp.zeros_like(df_ref)

        def q_block(qi, dqg):
            q0 = pl.multiple_of(qi * tq, tq)
            qcols = pl.ds(q0, tq)
            qrows = pl.ds(q0, tq)
            nfull = q0 // tk
            qts = [qt_s[hd, :, qcols] for hd in range(2)]
            dots = [dot_s[hd, :, qcols] for hd in range(2)]
            qns = [q_s[hd, qrows, :] for hd in range(2)]
            dons = [do_s[hd, qrows, :] for hd in range(2)]
            lse_r = [lse_ref[0, hd:hd + 1, qcols] for hd in range(2)]
            dl_r = [dl_s[hd:hd + 1, qcols] for hd in range(2)]
            bdt = qt_s.dtype

            def krows(kj):
                return pl.ds(pl.multiple_of(kj * tk, tk), tk)

            def scores(hd, kj):
                return (jnp.dot(k_s[hd, krows(kj), :], qts[hd], preferred_element_type=F32),
                        jnp.dot(v_s[krows(kj), :], dots[hd], preferred_element_type=F32))

            def products(hd, rows, ds, p, dqt):
                dk_s[hd, rows, :] += jnp.dot(ds, qns[hd], preferred_element_type=F32)
                dv_s[rows, :] += jnp.dot(p, dons[hd], preferred_element_type=F32)
                return dqt + jnp.dot(kt_s[hd, :, rows], ds, preferred_element_type=F32)

            def kv_step(kj, carry, masked, last=False):
                new = []
                for hd in range(2):
                    dqt, ds_prev, p_prev = carry[hd]
                    st, dp = scores(hd, kj)
                    dqt = products(hd, krows(jnp.maximum(kj - 1, 0)), ds_prev, p_prev, dqt)
                    if masked:
                        st = jnp.where(_causal_t(tk, tq, q0 - kj * tk), st, NEG)
                    p = jnp.exp(st - lse_r[hd])
                    ds = (p * (dp - dl_r[hd])).astype(bdt)
                    if last:
                        new.append(products(hd, krows(kj), ds, p.astype(bdt), dqt))
                    else:
                        new.append((dqt, ds, p.astype(bdt)))
                return tuple(new)

            init = tuple((jnp.zeros((LANES, tq), F32), jnp.zeros((tk, tq), bdt), jnp.zeros((tk, tq), bdt))
                         for hd in range(2))
            carry = lax.fori_loop(0, nfull, lambda kj, cr: kv_step(kj, cr, False), init)
            for t in range(n_diag):
                carry = kv_step(nfull + t, carry, True, t == n_diag - 1)
            dq_parts = [dqt.T for dqt in carry]
            rs0 = dq_parts[0][:, HEAD_DIM + AUG_F:HEAD_DIM + AUG_F + 1]
            rs1 = dq_parts[1][:, AUG_F:AUG_F + 1]
            df_ref[qrows, :] += jnp.where(lane == 2 * hp, rs0, 0.0) + jnp.where(lane == 2 * hp + 1, rs1, 0.0)
            dqn = jnp.where(first, dq_parts[0], dq_parts[1]) * scale
            dq_raw, dg = norm_bwd(q_ref[qrows, :], qg_ref[...], dqn, first)
            dq_ref[qrows, :] = dq_raw.astype(dq_ref.dtype)
            return dqg + dg
        dqg_ref[0] = lax.fori_loop(0, nq, q_block, jnp.zeros((1, LANES), F32))

        dkg = jnp.zeros((1, LANES), F32)
        for ci in range(s // ch):
            rows = pl.ds(ci * ch, ch)
            dk0, dk1 = dk_s[0, rows, :], dk_s[1, rows, :]
            cs0 = dk0[:, HEAD_DIM + AUG_ONE:HEAD_DIM + AUG_ONE + 1]
            cs1 = dk1[:, AUG_ONE:AUG_ONE + 1]
            df_ref[rows, :] -= jnp.where(lane == 2 * hp, cs0, 0.0) + jnp.where(lane == 2 * hp + 1, cs1, 0.0)
            dk_raw, dg = norm_bwd(k_ref[rows, :], kg_ref[...], jnp.where(first, dk0, dk1), first)
            dk_ref[rows, :] = dk_raw.astype(dk_ref.dtype)
            dkg = dkg + dg
            dv_ref[rows, :] = dv_s[rows, :].astype(dv_ref.dtype)
        dkg_ref[0] = dkg

    blk = lambda off: _big((s, LANES), lambda h: (0, off + h))
    outb = pl.BlockSpec((s, LANES), lambda h: (0, h))
    vec = pl.BlockSpec((1, LANES), lambda h: (0, 0))
    gout = pl.BlockSpec((1, 1, LANES), lambda h: (h, 0, 0))
    act = jax.ShapeDtypeStruct((s, d_model), MXU_DTYPE)
    gsh = jax.ShapeDtypeStruct((pairs, 1, LANES), F32)
    pair_rows = pltpu.VMEM((2, s, LANES), MXU_DTYPE)
    pair_cols = pltpu.VMEM((2, LANES, s), MXU_DTYPE)
    return pl.pallas_call(
        body, name="attn_bwd", grid=(pairs,),
        in_specs=[blk(0), blk(pairs), blk(2 * pairs), blk(0), blk(0),
                  pl.BlockSpec((1, 2, s), lambda h: (h, 0, 0)), blk(0), vec, vec],
        out_specs=[outb, outb, outb, pl.BlockSpec((s, LANES), lambda h: (0, 0)), gout, gout],
        out_shape=[act, act, act, jax.ShapeDtypeStruct((s, LANES), F32), gsh, gsh],
        scratch_shapes=[pair_rows, pair_cols, pair_rows, pair_cols, pltpu.VMEM((s, LANES), MXU_DTYPE),
                        pair_rows, pair_cols, pltpu.VMEM((8, s), F32),
                        pltpu.VMEM((2, s, LANES), F32), pltpu.VMEM((s, LANES), F32)],
        compiler_params=_cparams(("arbitrary",), ATT_BWD_VMEM_LIMIT),
    )(proj, proj, proj, do, o, lse, f_rep, qg, kg)


def _pad_cols(a, n):
    return jnp.pad(a, ((0, 0), (0, n - a.shape[1])))


def _fox_fwd(h, w):
    d = h.shape[1]
    pairs = d // LANES
    (proj,) = _mm("fox_in", h, w["w_in"], tm=1024, tn=640)
    f_cum = _gate_fwd(proj, w["b_f"], 3 * pairs)
    f16 = f_cum[:, :d // HEAD_DIM]
    f_rep = jnp.repeat(f16, HEAD_DIM, axis=1)
    o, lse = _attn_fwd_t(proj, f_rep, w["qg"], w["kg"], d)
    (y,) = _mm("fox_out", o, w["w_out"])
    return y, (proj, f_rep, o, lse)


def _fox_bwd(dy, h, w, saved):
    proj, f_rep, o, lse = saved
    s, d = h.shape
    pairs = d // LANES
    (do,) = _mm("fox_do", dy, w["w_out"], tb=True, out_dtypes=(MXU_DTYPE,))
    (dw_out,) = _mm("fox_dwout", o, dy, ta=True, out_dtypes=(MXU_DTYPE,))
    dq, dk, dv, d_f, dqg, dkg = _attn_bwd_t(proj, do, o, lse, f_rep, w["qg"], w["kg"], d)
    dfpre, db_f = _gate_bwd(proj, w["b_f"], d_f, 3 * pairs)
    dproj = jnp.concatenate([dq, dk, dv, dfpre], axis=1)
    (dw_in,) = _mm("fox_dwin", h, dproj, ta=True, out_dtypes=(MXU_DTYPE,), tn=640)
    (dh,) = _mm("fox_dh", dproj, w["w_in"], tb=True)
    fold = lambda g: jnp.sum(g, axis=(0, 1)).reshape(2, HEAD_DIM).sum(axis=0)
    return dh, dict(w_in=dw_in, w_out=dw_out, b_f=db_f[0, :d // HEAD_DIM], qg=fold(dqg), kg=fold(dkg))


_GELU_C = math.sqrt(2.0 / math.pi)


def _gelu(v):
    return 0.5 * v * (1.0 + jnp.tanh(_GELU_C * (v + 0.044715 * v * v * v)))


def _gelu_grad(v):
    t = jnp.tanh(_GELU_C * (v + 0.044715 * v * v * v))
    return 0.5 * (1.0 + t) + 0.5 * v * (1.0 - t * t) * (_GELU_C * (1.0 + 3.0 * 0.044715 * v * v))


def _ln_stats(v):
    mu = jnp.mean(v, axis=-1, keepdims=True)
    vc = v - mu
    r = lax.rsqrt(jnp.mean(vc * vc, axis=-1, keepdims=True) + EPS)
    return vc * r, r


def _sg_mask():
    t = lax.broadcasted_iota(jnp.int32, (SG_CHUNK, SG_CHUNK), 0) // SG_BLOCK
    sidx = lax.broadcasted_iota(jnp.int32, (SG_CHUNK, SG_CHUNK), 1) // SG_BLOCK
    return sidx <= t


def _sgu_fwd(uv_pre, ln_g, ln_b, w_s, b_st):
    s, w2 = uv_pre.shape
    wd = w2 // 2
    tr = min(256, s)

    def fn(uv_pre, ln_g, ln_b, w_s, b_st):
        uv = _gelu(uv_pre)
        u = uv[:, :wd]
        vh, _ = _ln_stats(uv[:, wd:])
        vl = (vh * ln_g + ln_b).astype(MXU_DTYPE)
        mask = _sg_mask()
        cols = []
        for g in range(SG_GROUPS):
            wg = jnp.where(mask, w_s[g * SG_CHUNK:(g + 1) * SG_CHUNK, :], 0.0).astype(MXU_DTYPE)
            parts = []
            for ci in range(tr // SG_CHUNK):
                vt = vl[ci * SG_CHUNK:(ci + 1) * SG_CHUNK, g * SG_CHUNK:(g + 1) * SG_CHUNK]
                parts.append(jnp.dot(wg, vt, preferred_element_type=F32) + b_st[:, g:g + 1])
            cols.append(jnp.concatenate(parts, axis=0) if len(parts) > 1 else parts[0])
        vout = jnp.concatenate(cols, axis=1)
        return (u * vout,), ()
    (m,), _ = _rowwise("sgu_fwd", fn, [uv_pre], [ln_g, ln_b, w_s, b_st], [(wd, MXU_DTYPE)], [], tr)
    return m


def _sgu_bwd(uv_pre, dm, ln_g, ln_b, w_s, b_st):
    s, w2 = uv_pre.shape
    wd = w2 // 2
    tr = min(256, s)

    def fn(uv_pre, dm, ln_g, ln_b, w_s, b_st):
        uv = _gelu(uv_pre)
        u = uv[:, :wd]
        vh, r = _ln_stats(uv[:, wd:])
        vl = (vh * ln_g + ln_b).astype(MXU_DTYPE)
        mask = _sg_mask()
        lane = lax.broadcasted_iota(jnp.int32, (1, LANES), 1)
        cols, dcols, dws, dbs = [], [], [], jnp.zeros((SG_CHUNK, LANES), F32)
        for g in range(SG_GROUPS):
            wg = jnp.where(mask, w_s[g * SG_CHUNK:(g + 1) * SG_CHUNK, :], 0.0).astype(MXU_DTYPE)
            parts, dparts = [], []
            dwg = jnp.zeros((SG_CHUNK, SG_CHUNK), F32)
            dbg = jnp.zeros((SG_CHUNK, 1), F32)
            for ci in range(tr // SG_CHUNK):
                rs = slice(ci * SG_CHUNK, (ci + 1) * SG_CHUNK)
                cs = slice(g * SG_CHUNK, (g + 1) * SG_CHUNK)
                vt = vl[rs, cs]
                parts.append(jnp.dot(wg, vt, preferred_element_type=F32) + b_st[:, g:g + 1])
                dvo = dm[rs, cs] * u[rs, cs]
                dvob = dvo.astype(MXU_DTYPE)
                dparts.append(lax.dot_general(wg, dvob, _TN, preferred_element_type=F32))
                dwg = dwg + lax.dot_general(dvob, vt, _NT, preferred_element_type=F32)
                dbg = dbg + jnp.sum(dvo, axis=-1, keepdims=True)
            cols.append(jnp.concatenate(parts, axis=0) if len(parts) > 1 else parts[0])
            dcols.append(jnp.concatenate(dparts, axis=0) if len(dparts) > 1 else dparts[0])
            dws.append(jnp.where(mask, dwg, 0.0))
            dbs = dbs + jnp.where(lane == g, dbg, 0.0)
        vout = jnp.concatenate(cols, axis=1)
        dvl = jnp.concatenate(dcols, axis=1)
        du = dm * vout
        dlg = _colsum(dvl * vh)
        dlb = _colsum(dvl)
        dvh = dvl * ln_g
        dv = r * (dvh - jnp.mean(dvh, axis=-1, keepdims=True) - vh * jnp.mean(dvh * vh, axis=-1, keepdims=True))
        dpre = jnp.concatenate([du, dv], axis=1) * _gelu_grad(uv_pre)
        return (dpre,), (dlg, dlb, jnp.concatenate(dws, axis=0), dbs)
    (dpre,), reds = _rowwise("sgu_bwd", fn, [uv_pre, dm], [ln_g, ln_b, w_s, b_st], [(w2, MXU_DTYPE)],
                             [(1, wd), (1, wd), (SG_GROUPS * SG_CHUNK, SG_CHUNK), (SG_CHUNK, LANES)], tr)
    return dpre, reds


def _sg_fwd(h, w):
    (uv_pre,) = _mm("sg_in", h, w["w_in"], tm=1024, tn=512)
    m = _sgu_fwd(uv_pre, w["ln_g"], w["ln_b"], w["w_s"], w["b_st"])
    (y,) = _mm("sg_out", m, w["w_out"])
    return y, (uv_pre, m)


def _sg_bwd(dy, h, w, saved):
    uv_pre, m = saved
    (dm,) = _mm("sg_dm", dy, w["w_out"], tb=True)
    (dw_out,) = _mm("sg_dwout", m, dy, ta=True, out_dtypes=(MXU_DTYPE,))
    dpre, (dlg, dlb, dws, dbs) = _sgu_bwd(uv_pre, dm, w["ln_g"], w["ln_b"], w["w_s"], w["b_st"])
    (dw_in,) = _mm("sg_dwin", h, dpre, ta=True, out_dtypes=(MXU_DTYPE,))
    (dh,) = _mm("sg_dh", dpre, w["w_in"], tb=True)
    return dh, dict(w_in=dw_in, w_out=dw_out, ln_g=dlg, ln_b=dlb, w_s=dws, b_s=dbs[:, :SG_GROUPS].T)


def _conv_fwd_kernel(ypad, w_dw, b_dw):
    s = ypad.shape[0] - CONV_PAD
    d = ypad.shape[1]
    tt = min(256, s)
    ext = tt + CONV_PAD

    def body(y_ref, w_ref, b_ref, o_ref):
        def chunk(ci, _):
            base = pl.multiple_of(ci * tt, tt)
            e = y_ref[pl.ds(base, ext), :]
            acc = jnp.zeros((tt, LANES), F32) + b_ref[...]
            for j in range(CONV_WIDTH):
                sh = pltpu.roll(e, ext - (CONV_PAD - CONV_WIDTH + 1 + j), 0)[:tt, :]
                acc = acc + w_ref[j:j + 1, :] * sh
            o_ref[pl.ds(base, tt), :] = acc
            return 0
        lax.fori_loop(0, s // tt, chunk, 0)

    return pl.pallas_call(
        body, name="conv_fwd", grid=(d // LANES,),
        in_specs=[pl.BlockSpec((s + CONV_PAD, LANES), lambda i: (0, i)),
                  pl.BlockSpec((CONV_PAD, LANES), lambda i: (0, i)), pl.BlockSpec((1, LANES), lambda i: (0, i))],
        out_specs=pl.BlockSpec((s, LANES), lambda i: (0, i)),
        out_shape=jax.ShapeDtypeStruct((s, d), F32),
        compiler_params=_cparams(("parallel",)),
    )(ypad, w_dw, b_dw)


def _conv_bwd_kernel(ypad, dpad, w_dw):
    s = ypad.shape[0] - CONV_PAD
    d = ypad.shape[1]
    tt = min(256, s)
    ext = tt + CONV_PAD

    def body(y_ref, d_ref, w_ref, o_ref, dw_ref):
        dw_ref[...] = jnp.zeros_like(dw_ref)

        def chunk(ci, _):
            base = pl.multiple_of(ci * tt, tt)
            ye = y_ref[pl.ds(base, ext), :]
            de = d_ref[pl.ds(base, ext), :]
            dcur = de[:tt, :]
            acc = jnp.zeros((tt, LANES), F32)
            for j in range(CONV_WIDTH):
                back = CONV_WIDTH - 1 - j
                dsh = dcur if back == 0 else pltpu.roll(de, ext - back, 0)[:tt, :]
                acc = acc + w_ref[j:j + 1, :] * dsh
                ysh = pltpu.roll(ye, ext - (CONV_PAD - CONV_WIDTH + 1 + j), 0)[:tt, :]
                dw_ref[j:j + 1, :] += _colsum(dcur * ysh)
            o_ref[pl.ds(base, tt), :] = acc
            return 0
        lax.fori_loop(0, s // tt, chunk, 0)

    return pl.pallas_call(
        body, name="conv_bwd", grid=(d // LANES,),
        in_specs=[pl.BlockSpec((s + CONV_PAD, LANES), lambda i: (0, i)),
                  pl.BlockSpec((s + CONV_PAD, LANES), lambda i: (0, i)),
                  pl.BlockSpec((CONV_PAD, LANES), lambda i: (0, i))],
        out_specs=[pl.BlockSpec((s, LANES), lambda i: (0, i)), pl.BlockSpec((CONV_PAD, LANES), lambda i: (0, i))],
        out_shape=[jax.ShapeDtypeStruct((s, d), F32), jax.ShapeDtypeStruct((CONV_PAD, d), F32)],
        compiler_params=_cparams(("parallel",)),
    )(ypad, dpad, w_dw)


def _cv_fwd(h, w):
    d = h.shape[1]
    (y1,) = _mm("cv_pw1", h, w["w_pw1"], tm=1024, tn=512)

    def glu(y1, b1):
        t = y1 + b1
        return (t[:, :d] * _sigmoid(t[:, d:]),), ()
    (y2,), _ = _rowwise("cv_glu", glu, [y1], [w["b_pw1"]], [(d, F32)], [], 256)
    y3 = _conv_fwd_kernel(jnp.pad(y2, ((CONV_PAD, 0), (0, 0))), w["w_dw"], w["b_dw"])

    def lnsilu(y3, g, b):
        vh, _ = _ln_stats(y3)
        y4 = vh * g + b
        return (y4 * _sigmoid(y4),), ()
    (y5,), _ = _rowwise("cv_lnsilu", lnsilu, [y3], [w["ln_g"], w["ln_b"]], [(d, MXU_DTYPE)], [], 256)
    (y,) = _mm("cv_pw2", y5, w["w_pw2"], epi=lambda acc, b: (acc + b,), vecs=(w["b_pw2"],))
    return y, (y1, y2, y3, y5)


def _cv_bwd(dy, h, w, saved):
    y1, y2, y3, y5 = saved
    d = h.shape[1]
    (dy5,) = _mm("cv_dy5", dy, w["w_pw2"], tb=True)
    (dw_pw2,) = _mm("cv_dwpw2", y5, dy, ta=True, out_dtypes=(MXU_DTYPE,))

    def ln_bwd(dy5, y3, dyb, g, b):
        vh, r = _ln_stats(y3)
        y4 = vh * g + b
        sg = _sigmoid(y4)
        dy4 = dy5 * (sg * (1.0 + y4 * (1.0 - sg)))
        dvh = dy4 * g
        dy3 = r * (dvh - jnp.mean(dvh, axis=-1, keepdims=True) - vh * jnp.mean(dvh * vh, axis=-1, keepdims=True))
        return (dy3,), (_colsum(dy4 * vh), _colsum(dy4), _colsum(dy3), _colsum(dyb.astype(F32)))
    (dy3,), (dlg, dlb, db_dw, db_pw2) = _rowwise("cv_ln_bwd", ln_bwd, [dy5, y3, dy], [w["ln_g"], w["ln_b"]],
                                                 [(d, F32)], [(1, d)] * 4, 256)
    dy2, dw_dw = _conv_bwd_kernel(jnp.pad(y2, ((CONV_PAD, 0), (0, 0))), jnp.pad(dy3, ((0, CONV_PAD), (0, 0))),
                                  w["w_dw"])

    def glu_bwd(y1, dy2, b1):
        t = y1 + b1
        a, sg = t[:, :d], _sigmoid(t[:, d:])
        dy1 = jnp.concatenate([dy2 * sg, dy2 * a * sg * (1.0 - sg)], axis=1)
        return (dy1,), (_colsum(dy1),)
    (dy1,), (db_pw1,) = _rowwise("cv_glu_bwd", glu_bwd, [y1, dy2], [w["b_pw1"]], [(2 * d, MXU_DTYPE)],
                                 [(1, 2 * d)], 256)
    (dw_pw1,) = _mm("cv_dwpw1", h, dy1, ta=True, out_dtypes=(MXU_DTYPE,))
    (dh,) = _mm("cv_dh", dy1, w["w_pw1"], tb=True)
    return dh, dict(w_pw1=dw_pw1, w_pw2=dw_pw2, b_pw1=db_pw1, b_pw2=db_pw2, w_dw=dw_dw[:CONV_WIDTH],
                    b_dw=db_dw, ln_g=dlg, ln_b=dlb)


def _ada_outer(c_t, dmod):
    def fn(c_t, dmod):
        acc = c_t[:, 0:1] * dmod[0:1, :]
        for b in range(1, N_DEV):
            acc = acc + c_t[:, b:b + 1] * dmod[b:b + 1, :]
        return (acc,), ()
    (g,), _ = _rowwise("ada_outer", fn, [c_t], [dmod], [(dmod.shape[1], F32)], [], 256)
    return g


def _adamw(name, parts, w, m, v, tr, layer=0, prev=None):
    npart, rows = parts.shape[0], parts.shape[1]
    cols = w.shape[1]

    def fn(parts, w, m, v):
        g = parts[0].astype(F32)
        for q in range(1, npart):
            g = g + parts[q].astype(F32)
        m_new = ADAM_B1 * m + (1.0 - ADAM_B1) * g
        v_new = ADAM_B2 * v + (1.0 - ADAM_B2) * (g * g)
        m_hat = m_new / (1.0 - ADAM_B1 ** ADAM_STEP)
        v_hat = v_new / (1.0 - ADAM_B2 ** ADAM_STEP)
        delta = -ADAM_LR * (m_hat / (jnp.sqrt(v_hat) + ADAM_EPS) + ADAM_WD * w)
        return (g, delta, m_new, v_new), ()
    tr = min(tr, rows)
    nblk = rows // tr
    n_prev = 0 if prev is None else 4

    def body(p_ref, w_ref, m_ref, v_ref, *rest):
        outs, _ = fn(p_ref[...], w_ref[...], m_ref[...], v_ref[...])
        for o_ref, o in zip(rest[n_prev:], outs):
            o_ref[...] = o

    spec = pl.BlockSpec((tr, cols), lambda i: (layer * nblk + i, 0))
    return pl.pallas_call(
        body, name=name, grid=(nblk,),
        in_specs=[pl.BlockSpec((npart, tr, cols), lambda i: (0, i, 0)), spec, spec, spec]
        + [pl.BlockSpec(memory_space=pl.ANY)] * n_prev,
        out_specs=[spec] * 4, out_shape=[jax.ShapeDtypeStruct(w.shape, F32)] * 4,
        input_output_aliases={4 + t: t for t in range(n_prev)},
        compiler_params=_cparams(("parallel",)),
    )(parts, w, m, v, *(prev or ()))


def _pack(arrays):
    flat = jnp.concatenate([a.reshape(-1).astype(F32) for a in arrays])
    n = flat.shape[0]
    rows = -(-n // (8 * LANES)) * 8
    return jnp.pad(flat, (0, rows * LANES - n)).reshape(rows, LANES)


def _unpack(buf, shapes, lead=()):
    flat = buf.reshape(lead + (-1,))
    out, off = [], 0
    for shp in shapes:
        n = math.prod(shp)
        out.append(flat[..., off:off + n].reshape(lead + tuple(shp)))
        off += n
    return out


ADAM_TILE_ELEMS = 1 << 17


def _row_tile(rows, cols):
    want = max(8, ADAM_TILE_ELEMS // max(cols, LANES))
    if rows <= want:
        return rows
    best = None
    for t in range(8, want + 1, 8):
        if rows % t == 0:
            best = t
    assert best is not None, (rows, cols)
    return best


def _local_step(xs, tgt, mods, norm_mix, norm_mlp, fetch, send):
    depth = len(mods)
    mixer_fwd = (_fox_fwd, _sg_fwd, _cv_fwd)
    mixer_bwd = (_fox_bwd, _sg_bwd, _cv_bwd)
    mw, w1, w2 = [None] * depth, [None] * depth, [None] * depth

    nsub = 2 * depth
    sub = []
    x_in = xs
    y_prev = gate_prev = None
    for k in range(nsub):
        i, is_mlp = k // 2, k % 2
        sh, sc = mods[i][3 * is_mlp], mods[i][3 * is_mlp + 1]
        g = (norm_mlp if is_mlp else norm_mix)[i:i + 1]
        wts, token = fetch(k, xs if k == 0 else y_prev)
        g = g + token
        if is_mlp:
            w1[i], w2[i] = wts
        else:
            mw[i] = wts
        if k == 0:
            h = _first_norm(x_in, g, sc, sh)
        else:
            x_in, h = _res_norm(x_in, y_prev, gate_prev, g, sc, sh)
        if is_mlp:
            y, saved = _mlp_fwd(h, w1[i], w2[i])
        else:
            y, saved = mixer_fwd[i % 3](h, mw[i])
        sub.append((x_in, h, y, saved))
        y_prev, gate_prev = y, mods[i][3 * is_mlp + 2]

    loss_part, dxo, dy, dgate = _final_loss(x_in, y_prev, gate_prev, tgt)

    dmods = [[None] * 6 for _ in range(depth)]
    g_norm = {'norm_mix': [None] * depth, 'norm_mlp': [None] * depth}
    g_mix = [None] * depth
    g_w1, g_w2 = [None] * depth, [None] * depth
    for k in reversed(range(nsub)):
        i, is_mlp = k // 2, k % 2
        x_k, h_k, _, saved = sub[k]
        dmods[i][3 * is_mlp + 2] = dgate
        if is_mlp:
            dh, g_w1[i], g_w2[i] = _mlp_bwd(dy, h_k, w1[i], w2[i], saved)
        else:
            dh, g_mix[i] = mixer_bwd[i % 3](dy, h_k, mw[i], saved)
        sc = mods[i][3 * is_mlp + 1]
        g = (norm_mlp if is_mlp else norm_mix)[i:i + 1]
        g = g + send(k, (g_w1[i], g_w2[i]) if is_mlp else g_mix[i])
        if k > 0:
            ip, mp = (k - 1) // 2, (k - 1) % 2
            dxo, dy, (dsh, dsc, dg, dgate) = _bwd_norm_gate(dxo, dh, x_k, sub[k - 1][2], g, sc, mods[ip][3 * mp + 2])
        else:
            dxo, (dsh, dsc, dg) = _bwd_norm_first(dxo, dh, x_k, g, sc)
        dmods[i][3 * is_mlp], dmods[i][3 * is_mlp + 1] = dsh, dsc
        g_norm['norm_mlp' if is_mlp else 'norm_mix'][i] = dg
    return loss_part, dxo, dmods, g_norm, g_mix, g_w1, g_w2


def kernel(x, c, norm_mix, norm_mlp, w_ada, b_ada, w_mlp_in, w_mlp_out, fox_w_in, fox_b_f, fox_q_norm, fox_k_norm, fox_w_out, sg_w_in, sg_ln_g, sg_ln_b, sg_w_s, sg_b_s, sg_w_out, cv_w_pw1, cv_b_pw1, cv_w_dw, cv_b_dw, cv_ln_g, cv_ln_b, cv_w_pw2, cv_b_pw2, loss_target, m_norm_mix, m_norm_mlp, m_w_ada, m_b_ada, m_w_mlp_in, m_w_mlp_out, m_fox_w_in, m_fox_b_f, m_fox_q_norm, m_fox_k_norm, m_fox_w_out, m_sg_w_in, m_sg_ln_g, m_sg_ln_b, m_sg_w_s, m_sg_b_s, m_sg_w_out, m_cv_w_pw1, m_cv_b_pw1, m_cv_w_dw, m_cv_b_dw, m_cv_ln_g, m_cv_ln_b, m_cv_w_pw2, m_cv_b_pw2, v_norm_mix, v_norm_mlp, v_w_ada, v_b_ada, v_w_mlp_in, v_w_mlp_out, v_fox_w_in, v_fox_b_f, v_fox_q_norm, v_fox_k_norm, v_fox_w_out, v_sg_w_in, v_sg_ln_g, v_sg_ln_b, v_sg_w_s, v_sg_b_s, v_sg_w_out, v_cv_w_pw1, v_cv_b_pw1, v_cv_w_dw, v_cv_b_dw, v_cv_ln_g, v_cv_ln_b, v_cv_w_pw2, v_cv_b_pw2):
    P = dict(zip(_ARGS, (x, c, norm_mix, norm_mlp, w_ada, b_ada, w_mlp_in, w_mlp_out, fox_w_in, fox_b_f, fox_q_norm, fox_k_norm, fox_w_out, sg_w_in, sg_ln_g, sg_ln_b, sg_w_s, sg_b_s, sg_w_out, cv_w_pw1, cv_b_pw1, cv_w_dw, cv_b_dw, cv_ln_g, cv_ln_b, cv_w_pw2, cv_b_pw2, loss_target, m_norm_mix, m_norm_mlp, m_w_ada, m_b_ada, m_w_mlp_in, m_w_mlp_out, m_fox_w_in, m_fox_b_f, m_fox_q_norm, m_fox_k_norm, m_fox_w_out, m_sg_w_in, m_sg_ln_g, m_sg_ln_b, m_sg_w_s, m_sg_b_s, m_sg_w_out, m_cv_w_pw1, m_cv_b_pw1, m_cv_w_dw, m_cv_b_dw, m_cv_ln_g, m_cv_ln_b, m_cv_w_pw2, m_cv_b_pw2, v_norm_mix, v_norm_mlp, v_w_ada, v_b_ada, v_w_mlp_in, v_w_mlp_out, v_fox_w_in, v_fox_b_f, v_fox_q_norm, v_fox_k_norm, v_fox_w_out, v_sg_w_in, v_sg_ln_g, v_sg_ln_b, v_sg_w_s, v_sg_b_s, v_sg_w_out, v_cv_w_pw1, v_cv_b_pw1, v_cv_w_dw, v_cv_b_dw, v_cv_ln_g, v_cv_ln_b, v_cv_w_pw2, v_cv_b_pw2)))
    me = 4 * lax.axis_index("x") + 2 * lax.axis_index("y") + lax.axis_index("c")
    xs = x[0]
    tgt = loss_target[0]
    s_len, d = xs.shape
    depth = norm_mix.shape[0]
    bf = lambda a: a.astype(MXU_DTYPE)

    cv_small = ['cv_b_pw1', 'cv_w_dw', 'cv_b_dw', 'cv_ln_g', 'cv_ln_b', 'cv_b_pw2']
    small_shapes = [c.shape] + [P[n].shape for n in cv_small]
    (small_all,) = _exchange("gather_small", [_pack([c] + [P[n] for n in cv_small])], scatter=False)
    sm = dict(zip(['c'] + cv_small, _unpack(small_all, small_shapes, lead=(N_DEV,))))
    c_all = sm['c'][:, 0, :]
    cat_last = lambda a: jnp.moveaxis(a, 0, -2).reshape(a.shape[1:-1] + (-1,))
    cvf = {n: cat_last(sm[n]) for n in cv_small}

    big = ['w_mlp_in', 'w_mlp_out', 'fox_w_in', 'fox_w_out', 'sg_w_in', 'sg_w_out', 'cv_w_pw1', 'cv_w_pw2']
    col_sharded = {'w_mlp_in', 'fox_w_in', 'sg_w_in', 'cv_w_pw1'}
    mixer_names = (('fox_w_in', 'fox_w_out'), ('sg_w_in', 'sg_w_out'), ('cv_w_pw1', 'cv_w_pw2'))
    nsub = 2 * depth
    groups = [[('w_mlp_in', k // 2), ('w_mlp_out', k // 2)] if k % 2 else
              [(nm, k // 6) for nm in mixer_names[(k // 2) % 3]] for k in range(nsub)]
    gather_handles = [None] * nsub
    scatter_handles = [None] * nsub

    c_act = c_all * _sigmoid(c_all)
    c_pad = bf(jnp.pad(c_act, ((0, 16 - N_DEV), (0, 0))))
    n_ada = w_ada.shape[2]
    (mod_part,) = _mm("ada_mod", c_pad, bf(jnp.transpose(w_ada, (1, 0, 2)).reshape(d, depth * n_ada)),
                      epi=lambda acc, b: (acc + b,),
                      vecs=(lax.dynamic_slice_in_dim(b_ada, me * n_ada, n_ada, axis=1).reshape(1, depth * n_ada),),
                      tn=n_ada)
    (mod_all,) = _exchange("gather_mod", [mod_part], scatter=False)
    mod_me = lax.dynamic_index_in_dim(mod_all, me, axis=1, keepdims=False)
    mod = jnp.transpose(mod_me.reshape(N_DEV, depth, n_ada), (1, 0, 2)).reshape(depth, 6 * d)
    mods = [[mod[i:i + 1, k * d:(k + 1) * d] for k in range(6)] for i in range(depth)]

    def mixer_weights(i, full_weight):
        kind, j = i % 3, i // 3
        if kind == 0:
            w_in = full_weight('fox_w_in')
            n_pad = -(-w_in.shape[1] // (5 * LANES)) * (5 * LANES)
            return dict(w_in=_pad_cols(w_in, n_pad), w_out=full_weight('fox_w_out'),
                        b_f=_pad_cols(fox_b_f[j:j + 1], LANES),
                        qg=jnp.tile(fox_q_norm[j:j + 1], (1, 2)), kg=jnp.tile(fox_k_norm[j:j + 1], (1, 2)))
        if kind == 1:
            return dict(w_in=full_weight('sg_w_in'), w_out=full_weight('sg_w_out'),
                        ln_g=sg_ln_g[j:j + 1], ln_b=sg_ln_b[j:j + 1],
                        w_s=sg_w_s[j].reshape(SG_GROUPS * SG_CHUNK, SG_CHUNK), b_st=_pad_cols(sg_b_s[j].T, LANES))
        return dict(w_pw1=full_weight('cv_w_pw1'), w_pw2=full_weight('cv_w_pw2'),
                    b_pw1=cvf['cv_b_pw1'][j:j + 1], b_pw2=cvf['cv_b_pw2'][j:j + 1],
                    w_dw=jnp.pad(cvf['cv_w_dw'][j], ((0, CONV_PAD - CONV_WIDTH), (0, 0))),
                    b_dw=cvf['cv_b_dw'][j:j + 1], ln_g=cvf['cv_ln_g'][j:j + 1], ln_b=cvf['cv_ln_b'][j:j + 1])

    dep = mod_all
    for k in range(nsub):
        gather_handles[k], dep = _exchange_start(f"gather_start_{k}", [bf(P[nm][j]) for nm, j in groups[k]],
                                                 False, dep)
    first_token = dep

    def fetch(k, dep):
        got = _exchange_wait(f"gather_wait_{k}", gather_handles[k], False, dep)
        by_name = {nm: g for (nm, _), g in zip(groups[k], got)}

        def full_weight(name):
            g = by_name[name]
            if name in col_sharded:
                return jnp.transpose(g, (1, 0, 2)).reshape(g.shape[1], -1)
            return g.reshape(-1, g.shape[2])
        wts = (full_weight('w_mlp_in'), full_weight('w_mlp_out')) if k % 2 else mixer_weights(k // 2, full_weight)
        return wts, (first_token if k == 0 else jnp.zeros((1, 1), F32))

    def to_slots(name, g2d):
        if name in col_sharded:
            r = g2d.shape[0]
            return jnp.transpose(g2d.reshape(r, N_DEV, -1), (1, 0, 2))
        return g2d.reshape(N_DEV, -1, g2d.shape[1])

    def send(k, grads, dep=None):
        key = {'fox_w_in': 'w_in', 'fox_w_out': 'w_out', 'sg_w_in': 'w_in', 'sg_w_out': 'w_out',
               'cv_w_pw1': 'w_pw1', 'cv_w_pw2': 'w_pw2'}
        slots = []
        for nm, _ in groups[k]:
            g2d = grads[0] if nm == 'w_mlp_in' else grads[1] if nm == 'w_mlp_out' else grads[key[nm]]
            slots.append(to_slots(nm, g2d[:, :P[nm].shape[-1] * N_DEV] if nm in col_sharded else g2d))
        scatter_handles[k], token = _exchange_start(f"scatter_start_{k}", slots, True, slots[0] if dep is None else dep)
        return token

    send_later = lambda k, grads: jnp.zeros((1, 1), F32) if k == 0 else send(k, grads)
    loss_part, dxo, dmods, g_norm, g_mix, _, _ = _local_step(xs, tgt, mods, norm_mix, norm_mlp, fetch, send_later)
    loss = lax.psum(loss_part, ("x", "y", "c"))
    grad_x = dxo[None]

    stack = lambda key, kind: jnp.stack([g_mix[i][key].reshape(P[name_of[(kind, key)]].shape[1:])
                                         for i in range(depth) if i % 3 == kind])
    name_of = {(0, 'b_f'): 'fox_b_f', (0, 'qg'): 'fox_q_norm', (0, 'kg'): 'fox_k_norm',
               (1, 'ln_g'): 'sg_ln_g', (1, 'ln_b'): 'sg_ln_b', (1, 'w_s'): 'sg_w_s', (1, 'b_s'): 'sg_b_s'}
    dmod_me = jnp.concatenate([jnp.concatenate(r, axis=1) for r in dmods], axis=0)
    small_g = {'dmod': dmod_me,
               'norm_mix': jnp.concatenate(g_norm['norm_mix'], axis=0),
               'norm_mlp': jnp.concatenate(g_norm['norm_mlp'], axis=0)}
    for (kind, key), nm in name_of.items():
        small_g[nm] = stack(key, kind)
    cv_keys = {'cv_b_pw1': 'b_pw1', 'cv_w_dw': 'w_dw', 'cv_b_dw': 'b_dw', 'cv_ln_g': 'ln_g', 'cv_ln_b': 'ln_b',
               'cv_b_pw2': 'b_pw2'}
    for nm, key in cv_keys.items():
        small_g[nm] = jnp.stack([g_mix[i][key].reshape(cvf[nm].shape[1:]) for i in range(depth) if i % 3 == 2])
    sg_names = list(small_g)
    sg_shapes = [small_g[n].shape for n in sg_names]
    (sg_all,) = _exchange("gather_small_grads", [_pack([small_g[n] for n in sg_names])], scatter=False)
    send(0, g_mix[0], sg_all)

    dmod_all = _unpack(sg_all, sg_shapes, lead=(N_DEV,))[0]
    out = {}

    def finish(name, parts, shard_of=None):
        w, m, v = P[name], P['m_' + name], P['v_' + name]
        cols = w.shape[-1]
        r2 = lambda a: a.reshape(-1, cols)
        rows = r2(w).shape[0]
        res = _adamw("adamw_" + name, parts.reshape(parts.shape[0], rows, cols), r2(w), r2(m), r2(v),
                     _row_tile(rows, cols))
        out[name] = tuple(r.reshape(w.shape) for r in res)

    c_t = c_act.T
    ada_g = []
    for i in range(depth):
        blk = lax.dynamic_slice_in_dim(dmod_all[:, i, :], me * n_ada, n_ada, axis=1)
        ada_g.append(_ada_outer(c_t, blk))
    finish('w_ada', jnp.stack(ada_g)[None])
    finish('b_ada', dmod_all)

    sm_names = [n for n in sg_names if n != 'dmod']
    sm_parts = jnp.stack([_pack([_unpack(sg_all[q], sg_shapes)[sg_names.index(n)] for n in sm_names])
                          for q in range(N_DEV)])

    def local_block(nm, a):
        if nm in cv_keys:
            n_loc = P[nm].shape[-1]
            return lax.dynamic_slice_in_dim(a, me * n_loc, n_loc, axis=a.ndim - 1)
        return a
    full_shapes = [small_g[n].shape for n in sm_names]

    def pack_full(prefix):
        arrs = []
        for nm in sm_names:
            a = P[prefix + nm]
            if nm in cv_keys:
                full = jnp.zeros(small_g[nm].shape, F32)
                a = lax.dynamic_update_slice_in_dim(full, a, me * a.shape[-1], axis=a.ndim - 1)
            arrs.append(a)
        return _pack(arrs)
    res = _adamw("adamw_small", sm_parts, pack_full(''), pack_full('m_'), pack_full('v_'),
                 _row_tile(sm_parts.shape[1], LANES))
    unp = [_unpack(r, full_shapes) for r in res]
    for idx, nm in enumerate(sm_names):
        out[nm] = tuple(local_block(nm, unp[t][idx]) for t in range(4))

    stacked = {}
    last = out['w_ada'][0]
    for k in reversed(range(nsub)):
        recv = _exchange_wait(f"scatter_wait_{k}", scatter_handles[k], True, last)
        for (nm, j), parts in zip(groups[k], recv):
            nl, r, cc = P[nm].shape
            flat = lambda a: a.reshape(nl * r, cc)
            stacked[nm] = _adamw(f"adamw_{nm}_{j}", parts, flat(P[nm]), flat(P['m_' + nm]), flat(P['v_' + nm]),
                                 _row_tile(r, cc), layer=j, prev=stacked.get(nm))
            last = stacked[nm][0]
    for nm in big:
        out[nm] = tuple(a.reshape(P[nm].shape) for a in stacked[nm])

    outs = [loss, grad_x]
    for t in range(4):
        outs += [out[n][t] for n in _WEIGHTS]
    return tuple(outs)
```

```python
import functools
import math

import jax
import jax.numpy as jnp
from jax import lax
from jax.experimental import pallas as pl
from jax.experimental.pallas import tpu as pltpu

F32 = jnp.float32
MXU_DTYPE = jnp.bfloat16
EPS = 1e-6
N_DEV = 8
HEAD_DIM = 64
LANES = 128
CONV_WIDTH = 31
CONV_PAD = 32
SG_CHUNK = 128
SG_BLOCK = 64
SG_GROUPS = 8
SCAN_BLOCK = 256
VMEM_LIMIT = 48 * 1024 * 1024
ATT_BWD_VMEM_LIMIT = 56 * 1024 * 1024
ADAM_LR, ADAM_B1, ADAM_B2, ADAM_EPS, ADAM_WD, ADAM_STEP = 0.001, 0.9, 0.999, 1e-08, 0.01, 10
NEG = -1e30

_WEIGHTS = ['norm_mix', 'norm_mlp', 'w_ada', 'b_ada', 'w_mlp_in', 'w_mlp_out', 'fox_w_in', 'fox_b_f',
            'fox_q_norm', 'fox_k_norm', 'fox_w_out', 'sg_w_in', 'sg_ln_g', 'sg_ln_b', 'sg_w_s', 'sg_b_s',
            'sg_w_out', 'cv_w_pw1', 'cv_b_pw1', 'cv_w_dw', 'cv_b_dw', 'cv_ln_g', 'cv_ln_b', 'cv_w_pw2',
            'cv_b_pw2']
_ARGS = ['x', 'c'] + _WEIGHTS + ['loss_target'] + ['m_' + n for n in _WEIGHTS] + ['v_' + n for n in _WEIGHTS]


def _cparams(sem=None, vmem=VMEM_LIMIT):
    return pltpu.CompilerParams(dimension_semantics=sem, vmem_limit_bytes=vmem)


def _colsum(v):
    return jnp.sum(v, axis=0, keepdims=True)


def _sigmoid(v):
    return 1.0 / (1.0 + jnp.exp(-v))


def _rowwise(name, fn, rows, consts, row_out, red_out, tr):
    n_rows = rows[0].shape[0]
    tr = min(tr, n_rows)
    assert n_rows % tr == 0
    nr, nc, no = len(rows), len(consts), len(row_out)

    def body(*refs):
        ins = [r[...] for r in refs[:nr + nc]]
        outs, reds = fn(*ins)
        out_refs = refs[nr + nc:nr + nc + no]
        red_refs = refs[nr + nc + no:]
        for o_ref, o in zip(out_refs, outs):
            o_ref[...] = o.astype(o_ref.dtype)
        if red_refs:
            @pl.when(pl.program_id(0) == 0)
            def _():
                for r_ref in red_refs:
                    r_ref[...] = jnp.zeros_like(r_ref)
            for r_ref, r in zip(red_refs, reds):
                r_ref[...] += r

    def rspec(a):
        return pl.BlockSpec((tr,) + a.shape[1:], lambda i: (i,) + (0,) * (a.ndim - 1))

    def cspec(shape):
        return pl.BlockSpec(shape, lambda i: (0,) * len(shape))

    out_shape = [jax.ShapeDtypeStruct((n_rows, w), dt) for w, dt in row_out]
    out_shape += [jax.ShapeDtypeStruct(s, F32) for s in red_out]
    out_specs = [pl.BlockSpec((tr, w), lambda i: (i, 0)) for w, _ in row_out] + [cspec(s) for s in red_out]
    res = pl.pallas_call(
        body, name=name, grid=(n_rows // tr,),
        in_specs=[rspec(a) for a in rows] + [cspec(a.shape) for a in consts],
        out_specs=out_specs, out_shape=out_shape,
        compiler_params=_cparams(("arbitrary",)),
    )(*rows, *consts)
    return res[:no], res[no:]


def _mm(name, a, b, *, ta=False, tb=False, out_dtypes=(F32,), epi=None, tiles=(), vecs=(), tm=512, tn=512,
        col_slots=0):
    m_dim, k_dim = (a.shape[1], a.shape[0]) if ta else a.shape
    n_dim = b.shape[0] if tb else b.shape[1]
    assert (b.shape[1] if tb else b.shape[0]) == k_dim
    tm, tn = min(tm, m_dim), min(tn, n_dim)
    assert m_dim % tm == 0 and n_dim % tn == 0, (name, m_dim, n_dim, tm, tn)
    dims = (((0 if ta else 1,), (1 if tb else 0,)), ((), ()))
    nx = len(tiles) + len(vecs)

    def body(a_ref, b_ref, *rest):
        acc = lax.dot_general(a_ref[...], b_ref[...], dims, preferred_element_type=F32)
        outs = epi(acc, *[r[...] for r in rest[:nx]]) if epi is not None else (acc,)
        for o_ref, o in zip(rest[nx:], outs):
            o_ref[...] = o.astype(o_ref.dtype)

    a_spec = pl.BlockSpec((k_dim, tm), lambda i, j: (0, i)) if ta else pl.BlockSpec((tm, k_dim), lambda i, j: (i, 0))
    b_spec = pl.BlockSpec((tn, k_dim), lambda i, j: (j, 0)) if tb else pl.BlockSpec((k_dim, tn), lambda i, j: (0, j))
    t_spec = pl.BlockSpec((tm, tn), lambda i, j: (i, j))
    v_spec = pl.BlockSpec((1, tn), lambda i, j: (0, j))
    if col_slots:
        assert n_dim == col_slots * tn
        o_spec = pl.BlockSpec((None, tm, tn), lambda i, j: (j, i, 0))
        o_shape = (col_slots, m_dim, tn)
    else:
        o_spec, o_shape = t_spec, (m_dim, n_dim)
    res = pl.pallas_call(
        body, name=name, grid=(m_dim // tm, n_dim // tn),
        in_specs=[a_spec, b_spec] + [t_spec] * len(tiles) + [v_spec] * len(vecs),
        out_specs=[o_spec] * len(out_dtypes),
        out_shape=[jax.ShapeDtypeStruct(o_shape, dt) for dt in out_dtypes],
        compiler_params=_cparams(("parallel", "parallel")),
    )(a, b, *tiles, *vecs)
    return res


def _exchange_copies(scatter, in_refs, land_refs, send_sems, recv_sems, local_sems):
    n = len(in_refs)
    x, y, c = lax.axis_index("x"), lax.axis_index("y"), lax.axis_index("c")
    me = 4 * x + 2 * y + c
    local = [pltpu.make_async_copy(in_refs[a].at[me] if scatter else in_refs[a], land_refs[a].at[me],
                                   local_sems.at[a]) for a in range(n)]
    send, arrive = [], []
    for k in range(1, N_DEV):
        px, py, pc = x ^ ((k >> 2) & 1), y ^ ((k >> 1) & 1), c ^ (k & 1)
        peer = 4 * px + 2 * py + pc
        for a in range(n):
            src = in_refs[a].at[peer] if scatter else in_refs[a]
            sems = dict(send_sem=send_sems.at[a * (N_DEV - 1) + k - 1], recv_sem=recv_sems.at[a * (N_DEV - 1) + k - 1],
                        device_id=(px, py, pc), device_id_type=pl.DeviceIdType.MESH)
            send.append(pltpu.make_async_remote_copy(src_ref=src, dst_ref=land_refs[a].at[me], **sems))
            arrive.append(pltpu.make_async_remote_copy(src_ref=src, dst_ref=land_refs[a].at[peer], **sems))
    return local, send, arrive


def _land_shape(a, scatter):
    return ((N_DEV,) + a.shape[1:]) if scatter else ((N_DEV,) + a.shape)


def _exchange(name, arrays, scatter):
    n = len(arrays)

    def body(*refs):
        local, send, arrive = _exchange_copies(scatter, refs[:n], refs[n:2 * n], *refs[2 * n:])
        for cp in local + send:
            cp.start()
        for cp, arr in zip(send, arrive):
            cp.wait_send()
            arr.wait_recv()
        for cp in local:
            cp.wait()

    any_spec = pl.BlockSpec(memory_space=pl.ANY)
    return pl.pallas_call(
        body, name=name,
        in_specs=[any_spec] * n, out_specs=[any_spec] * n,
        out_shape=[jax.ShapeDtypeStruct(_land_shape(a, scatter), a.dtype) for a in arrays],
        scratch_shapes=[pltpu.SemaphoreType.DMA((n * (N_DEV - 1),)),
                        pltpu.SemaphoreType.DMA((n * (N_DEV - 1),)),
                        pltpu.SemaphoreType.DMA((n,))],
        compiler_params=pltpu.CompilerParams(has_side_effects=True),
    )(*arrays)


_HBM = pl.BlockSpec(memory_space=pltpu.HBM)
_SEM = pl.BlockSpec(memory_space=pltpu.SEMAPHORE)
_EFFECT = pltpu.SideEffectType.DATAFLOW_SIDE_EFFECTING


def _exchange_start(name, arrays, scatter, dep):
    n = len(arrays)
    nsem = n * (N_DEV - 1)
    srcs = [pltpu.with_memory_space_constraint(a, pltpu.HBM) for a in arrays]
    lands = [pltpu.with_memory_space_constraint(lax.empty(_land_shape(a, scatter), a.dtype), pltpu.HBM) for a in arrays]

    def body(*refs):
        sems = refs[2 * n + 1:2 * n + 4]
        local, send, _ = _exchange_copies(scatter, refs[:n], refs[n:2 * n], *sems)
        for cp in local + send:
            cp.start()
        token = refs[-1]
        token[...] = jnp.zeros_like(token)

    res = pl.pallas_call(
        body, name=name,
        in_specs=[_HBM] * (2 * n) + [pl.BlockSpec(memory_space=pl.ANY)],
        out_specs=[_SEM] * 3 + [_HBM] * (2 * n) + [pl.BlockSpec(memory_space=pltpu.VMEM)],
        out_shape=[pltpu.SemaphoreType.DMA((nsem,)), pltpu.SemaphoreType.DMA((nsem,)), pltpu.SemaphoreType.DMA((n,))]
        + [pltpu.HBM(a.shape, a.dtype) for a in arrays]
        + [pltpu.HBM(_land_shape(a, scatter), a.dtype) for a in arrays]
        + [jax.ShapeDtypeStruct((8, LANES), F32)],
        input_output_aliases={i: 3 + i for i in range(2 * n)},
        compiler_params=pltpu.CompilerParams(has_side_effects=_EFFECT),
    )(*srcs, *lands, dep)
    return res[:-1], res[-1][0:1, 0:1]


def _exchange_wait(name, handles, scatter, after):
    n = (len(handles) - 3) // 2
    sems, thru = handles[:3], handles[3:]

    def body(*refs):
        local, send, arrive = _exchange_copies(scatter, refs[:n], refs[n:2 * n], *refs[2 * n:2 * n + 3])
        for cp, arr in zip(send, arrive):
            cp.wait_send()
            arr.wait_recv()
        for cp in local:
            cp.wait()

    res = pl.pallas_call(
        body, name=name,
        in_specs=[_HBM] * (2 * n) + [_SEM] * 3 + [pl.BlockSpec(memory_space=pl.ANY)],
        out_specs=[_HBM] * (2 * n),
        out_shape=[pltpu.HBM(t.shape, t.dtype) for t in thru],
        input_output_aliases={i: i for i in range(2 * n)},
        compiler_params=pltpu.CompilerParams(has_side_effects=_EFFECT),
    )(*thru, *sems, after)
    return res[n:]


def _norm_mod(x, g, sc, sh):
    r = lax.rsqrt(jnp.mean(x * x, axis=-1, keepdims=True) + EPS)
    return (x * r * g) * (1.0 + sc) + sh


def _first_norm(x, g, sc, sh):
    (h,), _ = _rowwise("first_norm", lambda x, g, sc, sh: ((_norm_mod(x, g, sc, sh),), ()),
                       [x], [g, sc, sh], [(x.shape[1], MXU_DTYPE)], [], 256)
    return h


def _res_norm(x, y, gate, g, sc, sh):
    def fn(x, y, gate, g, sc, sh):
        xn = x + gate * y
        return (xn, _norm_mod(xn, g, sc, sh)), ()
    (xn, h), _ = _rowwise("res_norm", fn, [x, y], [gate, g, sc, sh],
                          [(x.shape[1], F32), (x.shape[1], MXU_DTYPE)], [], 256)
    return xn, h


def _final_loss(x, y, gate, target):
    d = x.shape[1]

    def fn(x, y, target, gate):
        err = (x + gate * y) - target
        part = jnp.sum(jnp.sum(err * err, axis=-1, keepdims=True), axis=0, keepdims=True) * (0.5 / d)
        dx = err * (1.0 / d)
        return (dx, dx * gate), (jnp.broadcast_to(part, (1, LANES)), _colsum(dx * y))
    (dx, dy), (loss, dgate) = _rowwise("final_loss", fn, [x, y, target], [gate],
                                       [(d, F32), (d, MXU_DTYPE)], [(1, LANES), (1, d)], 256)
    return loss[0, 0], dx, dy, dgate


def _norm_bwd_core(dxo, dh, x, g, sc):
    r = lax.rsqrt(jnp.mean(x * x, axis=-1, keepdims=True) + EPS)
    xn = x * r
    dsh = _colsum(dh)
    dsc = _colsum(dh * (xn * g))
    dyy = dh * (1.0 + sc)
    dg = _colsum(dyy * xn)
    dxn = dyy * g
    dxi = dxo + r * (dxn - xn * jnp.mean(dxn * xn, axis=-1, keepdims=True))
    return dxi, dsh, dsc, dg


def _bwd_norm_gate(dxo, dh, x, y_prev, g, sc, gate_prev):
    d = x.shape[1]

    def fn(dxo, dh, x, y_prev, g, sc, gate_prev):
        dxi, dsh, dsc, dg = _norm_bwd_core(dxo, dh, x, g, sc)
        return (dxi, dxi * gate_prev), (dsh, dsc, dg, _colsum(dxi * y_prev))
    (dxi, dy), reds = _rowwise("bwd_norm_gate", fn, [dxo, dh, x, y_prev], [g, sc, gate_prev],
                               [(d, F32), (d, MXU_DTYPE)], [(1, d)] * 4, 256)
    return dxi, dy, reds


def _bwd_norm_first(dxo, dh, x, g, sc):
    d = x.shape[1]

    def fn(dxo, dh, x, g, sc):
        dxi, dsh, dsc, dg = _norm_bwd_core(dxo, dh, x, g, sc)
        return (dxi,), (dsh, dsc, dg)
    (dxi,), reds = _rowwise("bwd_norm_first", fn, [dxo, dh, x], [g, sc], [(d, F32)], [(1, d)] * 3, 256)
    return dxi, reds


def _mlp_fwd(h, w1, w2):
    def epi(acc):
        r = jnp.maximum(acc, 0.0)
        return acc, r * r
    a, z = _mm("mlp_in", h, w1, out_dtypes=(MXU_DTYPE, MXU_DTYPE), epi=epi, tm=1024, tn=512)
    (out,) = _mm("mlp_out", z, w2, tm=512, tn=512)
    return out, (a, z)


def _mlp_bwd(dy, h, w1, w2, saved):
    a, z = saved

    def epi(acc, a):
        return (acc * (2.0 * jnp.maximum(a.astype(F32), 0.0)),)
    (da,) = _mm("mlp_dz", dy, w2, tb=True, out_dtypes=(MXU_DTYPE,), epi=epi, tiles=(a,), tm=1024, tn=512)
    (dw2,) = _mm("mlp_dw2", z, dy, ta=True, out_dtypes=(MXU_DTYPE,))
    (dw1,) = _mm("mlp_dw1", h, da, ta=True, out_dtypes=(MXU_DTYPE,), tn=da.shape[1] // N_DEV, col_slots=N_DEV)
    (dh,) = _mm("mlp_dh", da, w1, tb=True)
    return dh, dw1, dw2


def _split3(v):
    hi = v.astype(jnp.bfloat16)
    r1 = v - hi.astype(F32)
    mid = r1.astype(jnp.bfloat16)
    lo = (r1 - mid.astype(F32)).astype(jnp.bfloat16)
    return hi, mid, lo


def _tri_matmul(tri, v):
    hi, mid, lo = _split3(v)
    dot = functools.partial(jnp.dot, preferred_element_type=F32)
    return dot(tri, hi) + dot(tri, mid) + dot(tri, lo)


def _log_sigmoid(v):
    return jnp.minimum(v, 0.0) - jnp.log(1.0 + jnp.exp(-jnp.abs(v)))


def _gate_fwd(proj, b_pad, col_block):
    s = proj.shape[0]
    tb = min(SCAN_BLOCK, s)
    nblk = s // tb

    def body(f_ref, b_ref, o_ref):
        row = lax.broadcasted_iota(jnp.int32, (tb, tb), 0)
        col = lax.broadcasted_iota(jnp.int32, (tb, tb), 1)
        tri = (col <= row).astype(jnp.bfloat16)

        def step(i, carry):
            rows = pl.ds(pl.multiple_of(i * tb, tb), tb)
            lf = _log_sigmoid(f_ref[rows, :] + b_ref[...])
            f = _tri_matmul(tri, lf) + carry
            o_ref[rows, :] = f
            return f[tb - 1:tb, :]
        lax.fori_loop(0, nblk, step, jnp.zeros((1, LANES), F32))

    return pl.pallas_call(
        body, name="gate_fwd", grid=(1,),
        in_specs=[pl.BlockSpec((s, LANES), lambda i: (0, col_block)), pl.BlockSpec((1, LANES), lambda i: (0, 0))],
        out_specs=pl.BlockSpec((s, LANES), lambda i: (0, 0)),
        out_shape=jax.ShapeDtypeStruct((s, LANES), F32),
        compiler_params=_cparams(("arbitrary",)),
    )(proj, b_pad)


def _gate_bwd(proj, b_pad, d_f, col_block):
    s = proj.shape[0]
    tb = min(SCAN_BLOCK, s)
    nblk = s // tb

    def body(f_ref, b_ref, d_ref, o_ref, db_ref):
        row = lax.broadcasted_iota(jnp.int32, (tb, tb), 0)
        col = lax.broadcasted_iota(jnp.int32, (tb, tb), 1)
        tri = (col >= row).astype(jnp.bfloat16)

        def step(j, carry):
            acc, db = carry
            i = nblk - 1 - j
            rows = pl.ds(pl.multiple_of(i * tb, tb), tb)
            dlf = _tri_matmul(tri, d_ref[rows, :]) + acc
            dpre = dlf * _sigmoid(-(f_ref[rows, :] + b_ref[...]))
            o_ref[rows, :] = dpre.astype(o_ref.dtype)
            return dlf[0:1, :], db + _colsum(dpre)
        _, db = lax.fori_loop(0, nblk, step, (jnp.zeros((1, LANES), F32), jnp.zeros((1, LANES), F32)))
        db_ref[...] = db

    return pl.pallas_call(
        body, name="gate_bwd", grid=(1,),
        in_specs=[pl.BlockSpec((s, LANES), lambda i: (0, col_block)), pl.BlockSpec((1, LANES), lambda i: (0, 0)),
                  pl.BlockSpec((s, LANES), lambda i: (0, 0))],
        out_specs=[pl.BlockSpec((s, LANES), lambda i: (0, 0)), pl.BlockSpec((1, LANES), lambda i: (0, 0))],
        out_shape=[jax.ShapeDtypeStruct((s, LANES), MXU_DTYPE), jax.ShapeDtypeStruct((1, LANES), F32)],
        compiler_params=_cparams(("arbitrary",)),
    )(proj, b_pad, d_f)


def _head_masks():
    lane = lax.broadcasted_iota(jnp.int32, (1, LANES), 1)
    return lane < HEAD_DIM


def _pair_norm(v, g, first):
    v2 = v * v
    ss0 = jnp.sum(jnp.where(first, v2, 0.0), axis=-1, keepdims=True)
    ss1 = jnp.sum(jnp.where(first, 0.0, v2), axis=-1, keepdims=True)
    r = jnp.where(first, lax.rsqrt(ss0 * (1.0 / HEAD_DIM) + EPS), lax.rsqrt(ss1 * (1.0 / HEAD_DIM) + EPS))
    vn = v * r
    return vn * g, vn, r


_NT = (((1,), (1,)), ((), ()))
_TN = (((0,), (0,)), ((), ()))

ATT_TQ = 512
ATT_FWD_TQ = 1024
ATT_TK = 256
AUG_F, AUG_ONE = 0, 3


def _own_lanes(hd):
    lane = lax.broadcasted_iota(jnp.int32, (1, LANES), 1)
    return (lane < HEAD_DIM) if hd == 0 else (lane >= HEAD_DIM)


def _aug_lanes(hd, f_other):
    lane = lax.broadcasted_iota(jnp.int32, (1, LANES), 1) - (HEAD_DIM if hd == 0 else 0)
    hi, mid, lo = [t.astype(F32) for t in _split3(f_other)]
    zero = jnp.zeros_like(f_other)
    f_terms = jnp.where(lane == 0, hi, jnp.where(lane == 1, mid, jnp.where(lane == 2, lo, zero)))
    f_shift = jnp.where(lane == 3, hi, jnp.where(lane == 4, mid, jnp.where(lane == 5, lo, zero)))
    ones_lo = jnp.where(lane < 3, 1.0, 0.0) * jnp.where(lane >= 0, 1.0, 0.0)
    ones_hi = jnp.where(lane < 6, 1.0, 0.0) * jnp.where(lane >= 3, 1.0, 0.0)
    return f_terms + ones_hi, ones_lo - f_shift


def _attn_operands(q_raw, k_raw, f_rep, qg, kg, scale):
    first = _head_masks()
    qn, _, _ = _pair_norm(q_raw, qg, first)
    kn, _, _ = _pair_norm(k_raw, kg, first)
    f_other = pltpu.roll(f_rep, HEAD_DIM, 1)
    out = []
    for hd in range(2):
        own = _own_lanes(hd)
        q_x, k_x = _aug_lanes(hd, f_other)
        out.append((jnp.where(own, qn * scale, q_x), jnp.where(own, kn, k_x)))
    return out


def _causal_t(tk, tq, off):
    r = lax.broadcasted_iota(jnp.int32, (tk, tq), 0)
    c = lax.broadcasted_iota(jnp.int32, (tk, tq), 1)
    return (r - c) <= off


def _big(shape, index_map):
    return pl.BlockSpec(shape, index_map, pipeline_mode=pl.Buffered(1))


def _attn_fwd_t(proj, f_rep, qg, kg, d_model):
    s = proj.shape[0]
    pairs = d_model // LANES
    tq, tk = min(ATT_FWD_TQ, s), min(ATT_TK, s)
    assert (tk % tq == 0 or tq % tk == 0) and s % tk == 0 and s % tq == 0
    nq = s // tq
    n_diag = max(1, tq // tk)
    scale = HEAD_DIM ** -0.5
    ch = tk

    def body(q_ref, k_ref, v_ref, frep_ref, qg_ref, kg_ref, o_ref, lse_ref, qt_s, k_s, vt_s):
        for ci in range(s // ch):
            rows = pl.ds(ci * ch, ch)
            ops = _attn_operands(q_ref[rows, :], k_ref[rows, :], frep_ref[rows, :], qg_ref[...], kg_ref[...], scale)
            vv = v_ref[rows, :]
            for hd in range(2):
                own = _own_lanes(hd)
                lane = lax.broadcasted_iota(jnp.int32, (1, LANES), 1)
                one_lane = lane == (HEAD_DIM if hd == 0 else 0)
                qt_s[hd, :, rows] = ops[hd][0].T.astype(qt_s.dtype)
                k_s[hd, rows, :] = ops[hd][1].astype(k_s.dtype)
                vt_s[hd, :, rows] = jnp.where(own, vv, jnp.where(one_lane, 1.0, 0.0)).T.astype(vt_s.dtype)

        def q_block(qi, _):
            q0 = pl.multiple_of(qi * tq, tq)
            qcols = pl.ds(q0, tq)
            nfull = q0 // tk
            qts = [qt_s[hd, :, qcols] for hd in range(2)]

            def krows(kj):
                return pl.ds(pl.multiple_of(kj * tk, tk), tk)

            def scores(hd, kj):
                return jnp.dot(k_s[hd, krows(kj), :], qts[hd], preferred_element_type=F32)

            def kv_step(kj, carry, masked, last=False):
                new = []
                for hd in range(2):
                    m, acc, p_prev = carry[hd]
                    st = scores(hd, kj)
                    pv = jnp.dot(vt_s[hd, :, krows(jnp.maximum(kj - 1, 0))], p_prev, preferred_element_type=F32)
                    if masked:
                        st = jnp.where(_causal_t(tk, tq, q0 - kj * tk), st, NEG)
                    m_new = jnp.maximum(m, jnp.max(st, axis=0, keepdims=True))
                    p = jnp.exp(st - m_new).astype(vt_s.dtype)
                    acc = jnp.exp(m - m_new) * (acc + pv)
                    if last:
                        acc = acc + jnp.dot(vt_s[hd, :, krows(kj)], p, preferred_element_type=F32)
                        new.append((m_new, acc))
                    else:
                        new.append((m_new, acc, p))
                return tuple(new)

            init = tuple((jnp.full((1, tq), NEG, F32), jnp.zeros((LANES, tq), F32),
                          jnp.zeros((tk, tq), vt_s.dtype)) for hd in range(2))
            carry = lax.fori_loop(0, nfull, lambda kj, cr: kv_step(kj, cr, False), init)
            for t in range(n_diag):
                carry = kv_step(nfull + t, carry, True, t == n_diag - 1)
            o_parts, lse_parts = [], []
            for hd, (m, acc) in enumerate(carry):
                e0 = HEAD_DIM if hd == 0 else 0
                l = acc[e0:e0 + 1, :]
                o_parts.append((acc / l).T)
                lse_parts.append(m + jnp.log(l))
            o_ref[pl.ds(q0, tq), :] = jnp.where(_head_masks(), o_parts[0], o_parts[1]).astype(o_ref.dtype)
            lse_ref[0, :, qcols] = jnp.concatenate(lse_parts, axis=0)
            return 0
        lax.fori_loop(0, nq, q_block, 0)

    blk = lambda off: _big((s, LANES), lambda h: (0, off + h))
    vec = pl.BlockSpec((1, LANES), lambda h: (0, 0))
    return pl.pallas_call(
        body, name="attn_fwd", grid=(pairs,),
        in_specs=[blk(0), blk(pairs), blk(2 * pairs), blk(0), vec, vec],
        out_specs=[pl.BlockSpec((s, LANES), lambda h: (0, h)), pl.BlockSpec((1, 2, s), lambda h: (h, 0, 0))],
        out_shape=[jax.ShapeDtypeStruct((s, d_model), MXU_DTYPE), jax.ShapeDtypeStruct((pairs, 2, s), F32)],
        scratch_shapes=[pltpu.VMEM((2, LANES, s), MXU_DTYPE), pltpu.VMEM((2, s, LANES), MXU_DTYPE),
                        pltpu.VMEM((2, LANES, s), MXU_DTYPE)],
        compiler_params=_cparams(("arbitrary",)),
    )(proj, proj, proj, f_rep, qg, kg)


def _attn_bwd_t(proj, do, o, lse, f_rep, qg, kg, d_model):
    s = proj.shape[0]
    pairs = d_model // LANES
    tq, tk = min(ATT_TQ, s), min(ATT_TK, s)
    assert (tk % tq == 0 or tq % tk == 0) and s % tk == 0 and s % tq == 0
    nq = s // tq
    n_diag = max(1, tq // tk)
    scale = HEAD_DIM ** -0.5
    ch = tk

    def norm_bwd(raw, g, dn, first):
        _, xn, r = _pair_norm(raw, g, first)
        dxn = dn * g
        t = dxn * xn
        mu0 = jnp.sum(jnp.where(first, t, 0.0), axis=-1, keepdims=True)
        mu1 = jnp.sum(jnp.where(first, 0.0, t), axis=-1, keepdims=True)
        mu = jnp.where(first, mu0, mu1) * (1.0 / HEAD_DIM)
        return r * (dxn - xn * mu), _colsum(dn * xn)

    def body(q_ref, k_ref, v_ref, do_ref, o_ref, lse_ref, frep_ref, qg_ref, kg_ref,
             dq_ref, dk_ref, dv_ref, df_ref, dqg_ref, dkg_ref,
             q_s, qt_s, k_s, kt_s, v_s, do_s, dot_s, dl_s, dk_s, dv_s):
        first = _head_masks()
        hp = pl.program_id(0)
        lane = lax.broadcasted_iota(jnp.int32, (1, LANES), 1)
        for ci in range(s // ch):
            rows = pl.ds(ci * ch, ch)
            ops = _attn_operands(q_ref[rows, :], k_ref[rows, :], frep_ref[rows, :], qg_ref[...], kg_ref[...], scale)
            v_s[rows, :] = v_ref[rows, :].astype(v_s.dtype)
            dov = do_ref[rows, :].astype(F32)
            ot = o_ref[rows, :].astype(F32).T
            for hd in range(2):
                own = _own_lanes(hd)
                q_s[hd, rows, :] = ops[hd][0].astype(q_s.dtype)
                qt_s[hd, :, rows] = ops[hd][0].T.astype(qt_s.dtype)
                k_s[hd, rows, :] = ops[hd][1].astype(k_s.dtype)
                kt_s[hd, :, rows] = ops[hd][1].T.astype(kt_s.dtype)
                doh = jnp.where(own, dov, 0.0)
                do_s[hd, rows, :] = doh.astype(do_s.dtype)
                doht = doh.T
                dot_s[hd, :, rows] = doht.astype(dot_s.dtype)
                dl_s[hd:hd + 1, rows] = jnp.sum(doht * ot, axis=0, keepdims=True)
        dk_s[...] = jnp.zeros_like(dk_s)
        dv_s[...] = jnp.zeros_like(dv_s)

        @pl.when(hp == 0)
        def _():
            df_ref[...] = jnp.zeros_like(df_ref)

        def q_block(qi, dqg):
            q0 = pl.multiple_of(qi * tq, tq)
            qcols = pl.ds(q0, tq)
            qrows = pl.ds(q0, tq)
            nfull = q0 // tk
            qts = [qt_s[hd, :, qcols] for hd in range(2)]
            dots = [dot_s[hd, :, qcols] for hd in range(2)]
            qns = [q_s[hd, qrows, :] for hd in range(2)]
            dons = [do_s[hd, qrows, :] for hd in range(2)]
            lse_r = [lse_ref[0, hd:hd + 1, qcols] for hd in range(2)]
            dl_r = [dl_s[hd:hd + 1, qcols] for hd in range(2)]
            bdt = qt_s.dtype

            def krows(kj):
                return pl.ds(pl.multiple_of(kj * tk, tk), tk)

            def scores(hd, kj):
                return (jnp.dot(k_s[hd, krows(kj), :], qts[hd], preferred_element_type=F32),
                        jnp.dot(v_s[krows(kj), :], dots[hd], preferred_element_type=F32))

            def products(hd, rows, ds, p, dqt):
                dk_s[hd, rows, :] += jnp.dot(ds, qns[hd], preferred_element_type=F32)
                dv_s[rows, :] += jnp.dot(p, dons[hd], preferred_element_type=F32)
                return dqt + jnp.dot(kt_s[hd, :, rows], ds, preferred_element_type=F32)

            def kv_step(kj, carry, masked, last=False):
                new = []
                for hd in range(2):
                    dqt, ds_prev, p_prev = carry[hd]
                    st, dp = scores(hd, kj)
                    dqt = products(hd, krows(jnp.maximum(kj - 1, 0)), ds_prev, p_prev, dqt)
                    if masked:
                        st = jnp.where(_causal_t(tk, tq, q0 - kj * tk), st, NEG)
                    p = jnp.exp(st - lse_r[hd])
                    ds = (p * (dp - dl_r[hd])).astype(bdt)
                    if last:
                        new.append(products(hd, krows(kj), ds, p.astype(bdt), dqt))
                    else:
                        new.append((dqt, ds, p.astype(bdt)))
                return tuple(new)

            init = tuple((jnp.zeros((LANES, tq), F32), jnp.zeros((tk, tq), bdt), jnp.zeros((tk, tq), bdt))
                         for hd in range(2))
            carry = lax.fori_loop(0, nfull, lambda kj, cr: kv_step(kj, cr, False), init)
            for t in range(n_diag):
                carry = kv_step(nfull + t, carry, True, t == n_diag - 1)
            dq_parts = [dqt.T for dqt in carry]
            rs0 = dq_parts[0][:, HEAD_DIM + AUG_F:HEAD_DIM + AUG_F + 1]
            rs1 = dq_parts[1][:, AUG_F:AUG_F + 1]
            df_ref[qrows, :] += jnp.where(lane == 2 * hp, rs0, 0.0) + jnp.where(lane == 2 * hp + 1, rs1, 0.0)
            dqn = jnp.where(first, dq_parts[0], dq_parts[1]) * scale
            dq_raw, dg = norm_bwd(q_ref[qrows, :], qg_ref[...], dqn, first)
            dq_ref[qrows, :] = dq_raw.astype(dq_ref.dtype)
            return dqg + dg
        dqg_ref[0] = lax.fori_loop(0, nq, q_block, jnp.zeros((1, LANES), F32))

        dkg = jnp.zeros((1, LANES), F32)
        for ci in range(s // ch):
            rows = pl.ds(ci * ch, ch)
            dk0, dk1 = dk_s[0, rows, :], dk_s[1, rows, :]
            cs0 = dk0[:, HEAD_DIM + AUG_ONE:HEAD_DIM + AUG_ONE + 1]
            cs1 = dk1[:, AUG_ONE:AUG_ONE + 1]
            df_ref[rows, :] -= jnp.where(lane == 2 * hp, cs0, 0.0) + jnp.where(lane == 2 * hp + 1, cs1, 0.0)
            dk_raw, dg = norm_bwd(k_ref[rows, :], kg_ref[...], jnp.where(first, dk0, dk1), first)
            dk_ref[rows, :] = dk_raw.astype(dk_ref.dtype)
            dkg = dkg + dg
            dv_ref[rows, :] = dv_s[rows, :].astype(dv_ref.dtype)
        dkg_ref[0] = dkg

    blk = lambda off: _big((s, LANES), lambda h: (0, off + h))
    outb = pl.BlockSpec((s, LANES), lambda h: (0, h))
    vec = pl.BlockSpec((1, LANES), lambda h: (0, 0))
    gout = pl.BlockSpec((1, 1, LANES), lambda h: (h, 0, 0))
    act = jax.ShapeDtypeStruct((s, d_model), MXU_DTYPE)
    gsh = jax.ShapeDtypeStruct((pairs, 1, LANES), F32)
    pair_rows = pltpu.VMEM((2, s, LANES), MXU_DTYPE)
    pair_cols = pltpu.VMEM((2, LANES, s), MXU_DTYPE)
    return pl.pallas_call(
        body, name="attn_bwd", grid=(pairs,),
        in_specs=[blk(0), blk(pairs), blk(2 * pairs), blk(0), blk(0),
                  pl.BlockSpec((1, 2, s), lambda h: (h, 0, 0)), blk(0), vec, vec],
        out_specs=[outb, outb, outb, pl.BlockSpec((s, LANES), lambda h: (0, 0)), gout, gout],
        out_shape=[act, act, act, jax.ShapeDtypeStruct((s, LANES), F32), gsh, gsh],
        scratch_shapes=[pair_rows, pair_cols, pair_rows, pair_cols, pltpu.VMEM((s, LANES), MXU_DTYPE),
                        pair_rows, pair_cols, pltpu.VMEM((8, s), F32),
                        pltpu.VMEM((2, s, LANES), F32), pltpu.VMEM((s, LANES), F32)],
        compiler_params=_cparams(("arbitrary",), ATT_BWD_VMEM_LIMIT),
    )(proj, proj, proj, do, o, lse, f_rep, qg, kg)


def _pad_cols(a, n):
    return jnp.pad(a, ((0, 0), (0, n - a.shape[1])))


def _fox_fwd(h, w):
    d = h.shape[1]
    pairs = d // LANES
    (proj,) = _mm("fox_in", h, w["w_in"], tm=1024, tn=640)
    f_cum = _gate_fwd(proj, w["b_f"], 3 * pairs)
    f16 = f_cum[:, :d // HEAD_DIM]
    f_rep = jnp.repeat(f16, HEAD_DIM, axis=1)
    o, lse = _attn_fwd_t(proj, f_rep, w["qg"], w["kg"], d)
    (y,) = _mm("fox_out", o, w["w_out"])
    return y, (proj, f_rep, o, lse)


def _fox_bwd(dy, h, w, saved):
    proj, f_rep, o, lse = saved
    s, d = h.shape
    pairs = d // LANES
    (do,) = _mm("fox_do", dy, w["w_out"], tb=True, out_dtypes=(MXU_DTYPE,))
    (dw_out,) = _mm("fox_dwout", o, dy, ta=True, out_dtypes=(MXU_DTYPE,))
    dq, dk, dv, d_f, dqg, dkg = _attn_bwd_t(proj, do, o, lse, f_rep, w["qg"], w["kg"], d)
    dfpre, db_f = _gate_bwd(proj, w["b_f"], d_f, 3 * pairs)
    dproj = jnp.concatenate([dq, dk, dv, dfpre], axis=1)
    (dw_in,) = _mm("fox_dwin", h, dproj, ta=True, out_dtypes=(MXU_DTYPE,), tn=640)
    (dh,) = _mm("fox_dh", dproj, w["w_in"], tb=True)
    fold = lambda g: jnp.sum(g, axis=(0, 1)).reshape(2, HEAD_DIM).sum(axis=0)
    return dh, dict(w_in=dw_in, w_out=dw_out, b_f=db_f[0, :d // HEAD_DIM], qg=fold(dqg), kg=fold(dkg))


_GELU_C = math.sqrt(2.0 / math.pi)


def _gelu(v):
    return 0.5 * v * (1.0 + jnp.tanh(_GELU_C * (v + 0.044715 * v * v * v)))


def _gelu_grad(v):
    t = jnp.tanh(_GELU_C * (v + 0.044715 * v * v * v))
    return 0.5 * (1.0 + t) + 0.5 * v * (1.0 - t * t) * (_GELU_C * (1.0 + 3.0 * 0.044715 * v * v))


def _ln_stats(v):
    mu = jnp.mean(v, axis=-1, keepdims=True)
    vc = v - mu
    r = lax.rsqrt(jnp.mean(vc * vc, axis=-1, keepdims=True) + EPS)
    return vc * r, r


def _sg_mask():
    t = lax.broadcasted_iota(jnp.int32, (SG_CHUNK, SG_CHUNK), 0) // SG_BLOCK
    sidx = lax.broadcasted_iota(jnp.int32, (SG_CHUNK, SG_CHUNK), 1) // SG_BLOCK
    return sidx <= t


def _sgu_fwd(uv_pre, ln_g, ln_b, w_s, b_st):
    s, w2 = uv_pre.shape
    wd = w2 // 2
    tr = min(256, s)

    def fn(uv_pre, ln_g, ln_b, w_s, b_st):
        uv = _gelu(uv_pre)
        u = uv[:, :wd]
        vh, _ = _ln_stats(uv[:, wd:])
        vl = (vh * ln_g + ln_b).astype(MXU_DTYPE)
        mask = _sg_mask()
        cols = []
        for g in range(SG_GROUPS):
            wg = jnp.where(mask, w_s[g * SG_CHUNK:(g + 1) * SG_CHUNK, :], 0.0).astype(MXU_DTYPE)
            parts = []
            for ci in range(tr // SG_CHUNK):
                vt = vl[ci * SG_CHUNK:(ci + 1) * SG_CHUNK, g * SG_CHUNK:(g + 1) * SG_CHUNK]
                parts.append(jnp.dot(wg, vt, preferred_element_type=F32) + b_st[:, g:g + 1])
            cols.append(jnp.concatenate(parts, axis=0) if len(parts) > 1 else parts[0])
        vout = jnp.concatenate(cols, axis=1)
        return (u * vout,), ()
    (m,), _ = _rowwise("sgu_fwd", fn, [uv_pre], [ln_g, ln_b, w_s, b_st], [(wd, MXU_DTYPE)], [], tr)
    return m


def _sgu_bwd(uv_pre, dm, ln_g, ln_b, w_s, b_st):
    s, w2 = uv_pre.shape
    wd = w2 // 2
    tr = min(256, s)

    def fn(uv_pre, dm, ln_g, ln_b, w_s, b_st):
        uv = _gelu(uv_pre)
        u = uv[:, :wd]
        vh, r = _ln_stats(uv[:, wd:])
        vl = (vh * ln_g + ln_b).astype(MXU_DTYPE)
        mask = _sg_mask()
        lane = lax.broadcasted_iota(jnp.int32, (1, LANES), 1)
        cols, dcols, dws, dbs = [], [], [], jnp.zeros((SG_CHUNK, LANES), F32)
        for g in range(SG_GROUPS):
            wg = jnp.where(mask, w_s[g * SG_CHUNK:(g + 1) * SG_CHUNK, :], 0.0).astype(MXU_DTYPE)
            parts, dparts = [], []
            dwg = jnp.zeros((SG_CHUNK, SG_CHUNK), F32)
            dbg = jnp.zeros((SG_CHUNK, 1), F32)
            for ci in range(tr // SG_CHUNK):
                rs = slice(ci * SG_CHUNK, (ci + 1) * SG_CHUNK)
                cs = slice(g * SG_CHUNK, (g + 1) * SG_CHUNK)
                vt = vl[rs, cs]
                parts.append(jnp.dot(wg, vt, preferred_element_type=F32) + b_st[:, g:g + 1])
                dvo = dm[rs, cs] * u[rs, cs]
                dvob = dvo.astype(MXU_DTYPE)
                dparts.append(lax.dot_general(wg, dvob, _TN, preferred_element_type=F32))
                dwg = dwg + lax.dot_general(dvob, vt, _NT, preferred_element_type=F32)
                dbg = dbg + jnp.sum(dvo, axis=-1, keepdims=True)
            cols.append(jnp.concatenate(parts, axis=0) if len(parts) > 1 else parts[0])
            dcols.append(jnp.concatenate(dparts, axis=0) if len(dparts) > 1 else dparts[0])
            dws.append(jnp.where(mask, dwg, 0.0))
            dbs = dbs + jnp.where(lane == g, dbg, 0.0)
        vout = jnp.concatenate(cols, axis=1)
        dvl = jnp.concatenate(dcols, axis=1)
        du = dm * vout
        dlg = _colsum(dvl * vh)
        dlb = _colsum(dvl)
        dvh = dvl * ln_g
        dv = r * (dvh - jnp.mean(dvh, axis=-1, keepdims=True) - vh * jnp.mean(dvh * vh, axis=-1, keepdims=True))
        dpre = jnp.concatenate([du, dv], axis=1) * _gelu_grad(uv_pre)
        return (dpre,), (dlg, dlb, jnp.concatenate(dws, axis=0), dbs)
    (dpre,), reds = _rowwise("sgu_bwd", fn, [uv_pre, dm], [ln_g, ln_b, w_s, b_st], [(w2, MXU_DTYPE)],
                             [(1, wd), (1, wd), (SG_GROUPS * SG_CHUNK, SG_CHUNK), (SG_CHUNK, LANES)], tr)
    return dpre, reds


def _sg_fwd(h, w):
    (uv_pre,) = _mm("sg_in", h, w["w_in"], tm=1024, tn=512)
    m = _sgu_fwd(uv_pre, w["ln_g"], w["ln_b"], w["w_s"], w["b_st"])
    (y,) = _mm("sg_out", m, w["w_out"])
    return y, (uv_pre, m)


def _sg_bwd(dy, h, w, saved):
    uv_pre, m = saved
    (dm,) = _mm("sg_dm", dy, w["w_out"], tb=True)
    (dw_out,) = _mm("sg_dwout", m, dy, ta=True, out_dtypes=(MXU_DTYPE,))
    dpre, (dlg, dlb, dws, dbs) = _sgu_bwd(uv_pre, dm, w["ln_g"], w["ln_b"], w["w_s"], w["b_st"])
    (dw_in,) = _mm("sg_dwin", h, dpre, ta=True, out_dtypes=(MXU_DTYPE,), tn=dpre.shape[1] // N_DEV, col_slots=N_DEV)
    (dh,) = _mm("sg_dh", dpre, w["w_in"], tb=True)
    return dh, dict(w_in=dw_in, w_out=dw_out, ln_g=dlg, ln_b=dlb, w_s=dws, b_s=dbs[:, :SG_GROUPS].T)


def _conv_fwd_kernel(ypad, w_dw, b_dw):
    s = ypad.shape[0] - CONV_PAD
    d = ypad.shape[1]
    tt = min(256, s)
    ext = tt + CONV_PAD

    def body(y_ref, w_ref, b_ref, o_ref):
        def chunk(ci, _):
            base = pl.multiple_of(ci * tt, tt)
            e = y_ref[pl.ds(base, ext), :]
            acc = jnp.zeros((tt, LANES), F32) + b_ref[...]
            for j in range(CONV_WIDTH):
                sh = pltpu.roll(e, ext - (CONV_PAD - CONV_WIDTH + 1 + j), 0)[:tt, :]
                acc = acc + w_ref[j:j + 1, :] * sh
            o_ref[pl.ds(base, tt), :] = acc
            return 0
        lax.fori_loop(0, s // tt, chunk, 0)

    return pl.pallas_call(
        body, name="conv_fwd", grid=(d // LANES,),
        in_specs=[pl.BlockSpec((s + CONV_PAD, LANES), lambda i: (0, i)),
                  pl.BlockSpec((CONV_PAD, LANES), lambda i: (0, i)), pl.BlockSpec((1, LANES), lambda i: (0, i))],
        out_specs=pl.BlockSpec((s, LANES), lambda i: (0, i)),
        out_shape=jax.ShapeDtypeStruct((s, d), F32),
        compiler_params=_cparams(("parallel",)),
    )(ypad, w_dw, b_dw)


def _conv_bwd_kernel(ypad, dpad, w_dw):
    s = ypad.shape[0] - CONV_PAD
    d = ypad.shape[1]
    tt = min(256, s)
    ext = tt + CONV_PAD

    def body(y_ref, d_ref, w_ref, o_ref, dw_ref):
        dw_ref[...] = jnp.zeros_like(dw_ref)

        def chunk(ci, _):
            base = pl.multiple_of(ci * tt, tt)
            ye = y_ref[pl.ds(base, ext), :]
            de = d_ref[pl.ds(base, ext), :]
            dcur = de[:tt, :]
            acc = jnp.zeros((tt, LANES), F32)
            for j in range(CONV_WIDTH):
                back = CONV_WIDTH - 1 - j
                dsh = dcur if back == 0 else pltpu.roll(de, ext - back, 0)[:tt, :]
                acc = acc + w_ref[j:j + 1, :] * dsh
                ysh = pltpu.roll(ye, ext - (CONV_PAD - CONV_WIDTH + 1 + j), 0)[:tt, :]
                dw_ref[j:j + 1, :] += _colsum(dcur * ysh)
            o_ref[pl.ds(base, tt), :] = acc
            return 0
        lax.fori_loop(0, s // tt, chunk, 0)

    return pl.pallas_call(
        body, name="conv_bwd", grid=(d // LANES,),
        in_specs=[pl.BlockSpec((s + CONV_PAD, LANES), lambda i: (0, i)),
                  pl.BlockSpec((s + CONV_PAD, LANES), lambda i: (0, i)),
                  pl.BlockSpec((CONV_PAD, LANES), lambda i: (0, i))],
        out_specs=[pl.BlockSpec((s, LANES), lambda i: (0, i)), pl.BlockSpec((CONV_PAD, LANES), lambda i: (0, i))],
        out_shape=[jax.ShapeDtypeStruct((s, d), F32), jax.ShapeDtypeStruct((CONV_PAD, d), F32)],
        compiler_params=_cparams(("parallel",)),
    )(ypad, dpad, w_dw)


def _cv_fwd(h, w):
    d = h.shape[1]
    (y1,) = _mm("cv_pw1", h, w["w_pw1"], tm=1024, tn=512)

    def glu(y1, b1):
        t = y1 + b1
        return (t[:, :d] * _sigmoid(t[:, d:]),), ()
    (y2,), _ = _rowwise("cv_glu", glu, [y1], [w["b_pw1"]], [(d, F32)], [], 256)
    y3 = _conv_fwd_kernel(jnp.pad(y2, ((CONV_PAD, 0), (0, 0))), w["w_dw"], w["b_dw"])

    def lnsilu(y3, g, b):
        vh, _ = _ln_stats(y3)
        y4 = vh * g + b
        return (y4 * _sigmoid(y4),), ()
    (y5,), _ = _rowwise("cv_lnsilu", lnsilu, [y3], [w["ln_g"], w["ln_b"]], [(d, MXU_DTYPE)], [], 256)
    (y,) = _mm("cv_pw2", y5, w["w_pw2"], epi=lambda acc, b: (acc + b,), vecs=(w["b_pw2"],))
    return y, (y1, y2, y3, y5)


def _cv_bwd(dy, h, w, saved):
    y1, y2, y3, y5 = saved
    d = h.shape[1]
    (dy5,) = _mm("cv_dy5", dy, w["w_pw2"], tb=True)
    (dw_pw2,) = _mm("cv_dwpw2", y5, dy, ta=True, out_dtypes=(MXU_DTYPE,))

    def ln_bwd(dy5, y3, dyb, g, b):
        vh, r = _ln_stats(y3)
        y4 = vh * g + b
        sg = _sigmoid(y4)
        dy4 = dy5 * (sg * (1.0 + y4 * (1.0 - sg)))
        dvh = dy4 * g
        dy3 = r * (dvh - jnp.mean(dvh, axis=-1, keepdims=True) - vh * jnp.mean(dvh * vh, axis=-1, keepdims=True))
        return (dy3,), (_colsum(dy4 * vh), _colsum(dy4), _colsum(dy3), _colsum(dyb.astype(F32)))
    (dy3,), (dlg, dlb, db_dw, db_pw2) = _rowwise("cv_ln_bwd", ln_bwd, [dy5, y3, dy], [w["ln_g"], w["ln_b"]],
                                                 [(d, F32)], [(1, d)] * 4, 256)
    dy2, dw_dw = _conv_bwd_kernel(jnp.pad(y2, ((CONV_PAD, 0), (0, 0))), jnp.pad(dy3, ((0, CONV_PAD), (0, 0))),
                                  w["w_dw"])

    def glu_bwd(y1, dy2, b1):
        t = y1 + b1
        a, sg = t[:, :d], _sigmoid(t[:, d:])
        dy1 = jnp.concatenate([dy2 * sg, dy2 * a * sg * (1.0 - sg)], axis=1)
        return (dy1,), (_colsum(dy1),)
    (dy1,), (db_pw1,) = _rowwise("cv_glu_bwd", glu_bwd, [y1, dy2], [w["b_pw1"]], [(2 * d, MXU_DTYPE)],
                                 [(1, 2 * d)], 256)
    (dw_pw1,) = _mm("cv_dwpw1", h, dy1, ta=True, out_dtypes=(MXU_DTYPE,), tn=dy1.shape[1] // N_DEV, col_slots=N_DEV)
    (dh,) = _mm("cv_dh", dy1, w["w_pw1"], tb=True)
    return dh, dict(w_pw1=dw_pw1, w_pw2=dw_pw2, b_pw1=db_pw1, b_pw2=db_pw2, w_dw=dw_dw[:CONV_WIDTH],
                    b_dw=db_dw, ln_g=dlg, ln_b=dlb)


def _ada_outer(c_t, dmod):
    def fn(c_t, dmod):
        acc = c_t[:, 0:1] * dmod[0:1, :]
        for b in range(1, N_DEV):
            acc = acc + c_t[:, b:b + 1] * dmod[b:b + 1, :]
        return (acc,), ()
    (g,), _ = _rowwise("ada_outer", fn, [c_t], [dmod], [(dmod.shape[1], F32)], [], 256)
    return g


def _adamw(name, parts, w, m, v, tr, layer=0, prev=None):
    npart, rows = parts.shape[0], parts.shape[1]
    cols = w.shape[1]

    def fn(parts, w, m, v):
        g = parts[0].astype(F32)
        for q in range(1, npart):
            g = g + parts[q].astype(F32)
        m_new = ADAM_B1 * m + (1.0 - ADAM_B1) * g
        v_new = ADAM_B2 * v + (1.0 - ADAM_B2) * (g * g)
        m_hat = m_new / (1.0 - ADAM_B1 ** ADAM_STEP)
        v_hat = v_new / (1.0 - ADAM_B2 ** ADAM_STEP)
        delta = -ADAM_LR * (m_hat / (jnp.sqrt(v_hat) + ADAM_EPS) + ADAM_WD * w)
        return (g, delta, m_new, v_new), ()
    tr = min(tr, rows)
    nblk = rows // tr
    n_prev = 0 if prev is None else 4

    def body(p_ref, w_ref, m_ref, v_ref, *rest):
        outs, _ = fn(p_ref[...], w_ref[...], m_ref[...], v_ref[...])
        for o_ref, o in zip(rest[n_prev:], outs):
            o_ref[...] = o

    spec = pl.BlockSpec((tr, cols), lambda i: (layer * nblk + i, 0))
    return pl.pallas_call(
        body, name=name, grid=(nblk,),
        in_specs=[pl.BlockSpec((npart, tr, cols), lambda i: (0, i, 0)), spec, spec, spec]
        + [pl.BlockSpec(memory_space=pl.ANY)] * n_prev,
        out_specs=[spec] * 4, out_shape=[jax.ShapeDtypeStruct(w.shape, F32)] * 4,
        input_output_aliases={4 + t: t for t in range(n_prev)},
        compiler_params=_cparams(("parallel",)),
    )(parts, w, m, v, *(prev or ()))


def _pack(arrays):
    flat = jnp.concatenate([a.reshape(-1).astype(F32) for a in arrays])
    n = flat.shape[0]
    rows = -(-n // (8 * LANES)) * 8
    return jnp.pad(flat, (0, rows * LANES - n)).reshape(rows, LANES)


def _unpack(buf, shapes, lead=()):
    flat = buf.reshape(lead + (-1,))
    out, off = [], 0
    for shp in shapes:
        n = math.prod(shp)
        out.append(flat[..., off:off + n].reshape(lead + tuple(shp)))
        off += n
    return out


ADAM_TILE_ELEMS = 1 << 17


def _row_tile(rows, cols):
    want = max(8, ADAM_TILE_ELEMS // max(cols, LANES))
    if rows <= want:
        return rows
    best = None
    for t in range(8, want + 1, 8):
        if rows % t == 0:
            best = t
    assert best is not None, (rows, cols)
    return best


def _local_step(xs, tgt, mods, norm_mix, norm_mlp, fetch, send):
    depth = len(mods)
    mixer_fwd = (_fox_fwd, _sg_fwd, _cv_fwd)
    mixer_bwd = (_fox_bwd, _sg_bwd, _cv_bwd)
    mw, w1, w2 = [None] * depth, [None] * depth, [None] * depth

    nsub = 2 * depth
    sub = []
    x_in = xs
    y_prev = gate_prev = None
    for k in range(nsub):
        i, is_mlp = k // 2, k % 2
        sh, sc = mods[i][3 * is_mlp], mods[i][3 * is_mlp + 1]
        g = (norm_mlp if is_mlp else norm_mix)[i:i + 1]
        wts, token = fetch(k, xs if k == 0 else y_prev)
        g = g + token
        if is_mlp:
            w1[i], w2[i] = wts
        else:
            mw[i] = wts
        if k == 0:
            h = _first_norm(x_in, g, sc, sh)
        else:
            x_in, h = _res_norm(x_in, y_prev, gate_prev, g, sc, sh)
        if is_mlp:
            y, saved = _mlp_fwd(h, w1[i], w2[i])
        else:
            y, saved = mixer_fwd[i % 3](h, mw[i])
        sub.append((x_in, h, y, saved))
        y_prev, gate_prev = y, mods[i][3 * is_mlp + 2]

    loss_part, dxo, dy, dgate = _final_loss(x_in, y_prev, gate_prev, tgt)

    dmods = [[None] * 6 for _ in range(depth)]
    g_norm = {'norm_mix': [None] * depth, 'norm_mlp': [None] * depth}
    g_mix = [None] * depth
    g_w1, g_w2 = [None] * depth, [None] * depth
    for k in reversed(range(nsub)):
        i, is_mlp = k // 2, k % 2
        x_k, h_k, _, saved = sub[k]
        dmods[i][3 * is_mlp + 2] = dgate
        if is_mlp:
            dh, g_w1[i], g_w2[i] = _mlp_bwd(dy, h_k, w1[i], w2[i], saved)
        else:
            dh, g_mix[i] = mixer_bwd[i % 3](dy, h_k, mw[i], saved)
        sc = mods[i][3 * is_mlp + 1]
        g = (norm_mlp if is_mlp else norm_mix)[i:i + 1]
        g = g + send(k, (g_w1[i], g_w2[i]) if is_mlp else g_mix[i])
        if k > 0:
            ip, mp = (k - 1) // 2, (k - 1) % 2
            dxo, dy, (dsh, dsc, dg, dgate) = _bwd_norm_gate(dxo, dh, x_k, sub[k - 1][2], g, sc, mods[ip][3 * mp + 2])
        else:
            dxo, (dsh, dsc, dg) = _bwd_norm_first(dxo, dh, x_k, g, sc)
        dmods[i][3 * is_mlp], dmods[i][3 * is_mlp + 1] = dsh, dsc
        g_norm['norm_mlp' if is_mlp else 'norm_mix'][i] = dg
    return loss_part, dxo, dmods, g_norm, g_mix, g_w1, g_w2


def kernel(x, c, norm_mix, norm_mlp, w_ada, b_ada, w_mlp_in, w_mlp_out, fox_w_in, fox_b_f, fox_q_norm, fox_k_norm, fox_w_out, sg_w_in, sg_ln_g, sg_ln_b, sg_w_s, sg_b_s, sg_w_out, cv_w_pw1, cv_b_pw1, cv_w_dw, cv_b_dw, cv_ln_g, cv_ln_b, cv_w_pw2, cv_b_pw2, loss_target, m_norm_mix, m_norm_mlp, m_w_ada, m_b_ada, m_w_mlp_in, m_w_mlp_out, m_fox_w_in, m_fox_b_f, m_fox_q_norm, m_fox_k_norm, m_fox_w_out, m_sg_w_in, m_sg_ln_g, m_sg_ln_b, m_sg_w_s, m_sg_b_s, m_sg_w_out, m_cv_w_pw1, m_cv_b_pw1, m_cv_w_dw, m_cv_b_dw, m_cv_ln_g, m_cv_ln_b, m_cv_w_pw2, m_cv_b_pw2, v_norm_mix, v_norm_mlp, v_w_ada, v_b_ada, v_w_mlp_in, v_w_mlp_out, v_fox_w_in, v_fox_b_f, v_fox_q_norm, v_fox_k_norm, v_fox_w_out, v_sg_w_in, v_sg_ln_g, v_sg_ln_b, v_sg_w_s, v_sg_b_s, v_sg_w_out, v_cv_w_pw1, v_cv_b_pw1, v_cv_w_dw, v_cv_b_dw, v_cv_ln_g, v_cv_ln_b, v_cv_w_pw2, v_cv_b_pw2):
    P = dict(zip(_ARGS, (x, c, norm_mix, norm_mlp, w_ada, b_ada, w_mlp_in, w_mlp_out, fox_w_in, fox_b_f, fox_q_norm, fox_k_norm, fox_w_out, sg_w_in, sg_ln_g, sg_ln_b, sg_w_s, sg_b_s, sg_w_out, cv_w_pw1, cv_b_pw1, cv_w_dw, cv_b_dw, cv_ln_g, cv_ln_b, cv_w_pw2, cv_b_pw2, loss_target, m_norm_mix, m_norm_mlp, m_w_ada, m_b_ada, m_w_mlp_in, m_w_mlp_out, m_fox_w_in, m_fox_b_f, m_fox_q_norm, m_fox_k_norm, m_fox_w_out, m_sg_w_in, m_sg_ln_g, m_sg_ln_b, m_sg_w_s, m_sg_b_s, m_sg_w_out, m_cv_w_pw1, m_cv_b_pw1, m_cv_w_dw, m_cv_b_dw, m_cv_ln_g, m_cv_ln_b, m_cv_w_pw2, m_cv_b_pw2, v_norm_mix, v_norm_mlp, v_w_ada, v_b_ada, v_w_mlp_in, v_w_mlp_out, v_fox_w_in, v_fox_b_f, v_fox_q_norm, v_fox_k_norm, v_fox_w_out, v_sg_w_in, v_sg_ln_g, v_sg_ln_b, v_sg_w_s, v_sg_b_s, v_sg_w_out, v_cv_w_pw1, v_cv_b_pw1, v_cv_w_dw, v_cv_b_dw, v_cv_ln_g, v_cv_ln_b, v_cv_w_pw2, v_cv_b_pw2)))
    me = 4 * lax.axis_index("x") + 2 * lax.axis_index("y") + lax.axis_index("c")
    xs = x[0]
    tgt = loss_target[0]
    s_len, d = xs.shape
    depth = norm_mix.shape[0]
    bf = lambda a: a.astype(MXU_DTYPE)

    cv_small = ['cv_b_pw1', 'cv_w_dw', 'cv_b_dw', 'cv_ln_g', 'cv_ln_b', 'cv_b_pw2']
    small_shapes = [c.shape] + [P[n].shape for n in cv_small]
    (small_all,) = _exchange("gather_small", [_pack([c] + [P[n] for n in cv_small])], scatter=False)
    sm = dict(zip(['c'] + cv_small, _unpack(small_all, small_shapes, lead=(N_DEV,))))
    c_all = sm['c'][:, 0, :]
    cat_last = lambda a: jnp.moveaxis(a, 0, -2).reshape(a.shape[1:-1] + (-1,))
    cvf = {n: cat_last(sm[n]) for n in cv_small}

    big = ['w_mlp_in', 'w_mlp_out', 'fox_w_in', 'fox_w_out', 'sg_w_in', 'sg_w_out', 'cv_w_pw1', 'cv_w_pw2']
    col_sharded = {'w_mlp_in', 'fox_w_in', 'sg_w_in', 'cv_w_pw1'}
    mixer_names = (('fox_w_in', 'fox_w_out'), ('sg_w_in', 'sg_w_out'), ('cv_w_pw1', 'cv_w_pw2'))
    nsub = 2 * depth
    groups = [[('w_mlp_in', k // 2), ('w_mlp_out', k // 2)] if k % 2 else
              [(nm, k // 6) for nm in mixer_names[(k // 2) % 3]] for k in range(nsub)]
    gather_handles = [None] * nsub
    scatter_handles = [None] * nsub

    c_act = c_all * _sigmoid(c_all)
    c_pad = bf(jnp.pad(c_act, ((0, 16 - N_DEV), (0, 0))))
    n_ada = w_ada.shape[2]
    (mod_part,) = _mm("ada_mod", c_pad, bf(jnp.transpose(w_ada, (1, 0, 2)).reshape(d, depth * n_ada)),
                      epi=lambda acc, b: (acc + b,),
                      vecs=(lax.dynamic_slice_in_dim(b_ada, me * n_ada, n_ada, axis=1).reshape(1, depth * n_ada),),
                      tn=n_ada)
    (mod_all,) = _exchange("gather_mod", [mod_part], scatter=False)
    mod_me = lax.dynamic_index_in_dim(mod_all, me, axis=1, keepdims=False)
    mod = jnp.transpose(mod_me.reshape(N_DEV, depth, n_ada), (1, 0, 2)).reshape(depth, 6 * d)
    mods = [[mod[i:i + 1, k * d:(k + 1) * d] for k in range(6)] for i in range(depth)]

    def mixer_weights(i, full_weight):
        kind, j = i % 3, i // 3
        if kind == 0:
            w_in = full_weight('fox_w_in')
            n_pad = -(-w_in.shape[1] // (5 * LANES)) * (5 * LANES)
            return dict(w_in=_pad_cols(w_in, n_pad), w_out=full_weight('fox_w_out'),
                        b_f=_pad_cols(fox_b_f[j:j + 1], LANES),
                        qg=jnp.tile(fox_q_norm[j:j + 1], (1, 2)), kg=jnp.tile(fox_k_norm[j:j + 1], (1, 2)))
        if kind == 1:
            return dict(w_in=full_weight('sg_w_in'), w_out=full_weight('sg_w_out'),
                        ln_g=sg_ln_g[j:j + 1], ln_b=sg_ln_b[j:j + 1],
                        w_s=sg_w_s[j].reshape(SG_GROUPS * SG_CHUNK, SG_CHUNK), b_st=_pad_cols(sg_b_s[j].T, LANES))
        return dict(w_pw1=full_weight('cv_w_pw1'), w_pw2=full_weight('cv_w_pw2'),
                    b_pw1=cvf['cv_b_pw1'][j:j + 1], b_pw2=cvf['cv_b_pw2'][j:j + 1],
                    w_dw=jnp.pad(cvf['cv_w_dw'][j], ((0, CONV_PAD - CONV_WIDTH), (0, 0))),
                    b_dw=cvf['cv_b_dw'][j:j + 1], ln_g=cvf['cv_ln_g'][j:j + 1], ln_b=cvf['cv_ln_b'][j:j + 1])

    dep = mod_all
    for k in range(nsub):
        gather_handles[k], dep = _exchange_start(f"gather_start_{k}", [bf(P[nm][j]) for nm, j in groups[k]],
                                                 False, dep)
    first_token = dep

    def fetch(k, dep):
        got = _exchange_wait(f"gather_wait_{k}", gather_handles[k], False, dep)
        by_name = {nm: g for (nm, _), g in zip(groups[k], got)}

        def full_weight(name):
            g = by_name[name]
            if name in col_sharded:
                return jnp.transpose(g, (1, 0, 2)).reshape(g.shape[1], -1)
            return g.reshape(-1, g.shape[2])
        wts = (full_weight('w_mlp_in'), full_weight('w_mlp_out')) if k % 2 else mixer_weights(k // 2, full_weight)
        return wts, (first_token if k == 0 else jnp.zeros((1, 1), F32))

    def to_slots(name, g2d):
        if name in col_sharded:
            r = g2d.shape[0]
            return jnp.transpose(g2d.reshape(r, N_DEV, -1), (1, 0, 2))
        return g2d.reshape(N_DEV, -1, g2d.shape[1])

    def send(k, grads, dep=None):
        key = {'fox_w_in': 'w_in', 'fox_w_out': 'w_out', 'sg_w_in': 'w_in', 'sg_w_out': 'w_out',
               'cv_w_pw1': 'w_pw1', 'cv_w_pw2': 'w_pw2'}
        slots = []
        for nm, _ in groups[k]:
            g = grads[0] if nm == 'w_mlp_in' else grads[1] if nm == 'w_mlp_out' else grads[key[nm]]
            if g.ndim == 2:
                g = to_slots(nm, g[:, :P[nm].shape[-1] * N_DEV] if nm in col_sharded else g)
            slots.append(g)
        scatter_handles[k], token = _exchange_start(f"scatter_start_{k}", slots, True, slots[0] if dep is None else dep)
        return token

    send_later = lambda k, grads: jnp.zeros((1, 1), F32) if k == 0 else send(k, grads)
    loss_part, dxo, dmods, g_norm, g_mix, _, _ = _local_step(xs, tgt, mods, norm_mix, norm_mlp, fetch, send_later)
    loss = lax.psum(loss_part, ("x", "y", "c"))
    grad_x = dxo[None]

    stack = lambda key, kind: jnp.stack([g_mix[i][key].reshape(P[name_of[(kind, key)]].shape[1:])
                                         for i in range(depth) if i % 3 == kind])
    name_of = {(0, 'b_f'): 'fox_b_f', (0, 'qg'): 'fox_q_norm', (0, 'kg'): 'fox_k_norm',
               (1, 'ln_g'): 'sg_ln_g', (1, 'ln_b'): 'sg_ln_b', (1, 'w_s'): 'sg_w_s', (1, 'b_s'): 'sg_b_s'}
    dmod_me = jnp.concatenate([jnp.concatenate(r, axis=1) for r in dmods], axis=0)
    small_g = {'dmod': dmod_me,
               'norm_mix': jnp.concatenate(g_norm['norm_mix'], axis=0),
               'norm_mlp': jnp.concatenate(g_norm['norm_mlp'], axis=0)}
    for (kind, key), nm in name_of.items():
        small_g[nm] = stack(key, kind)
    cv_keys = {'cv_b_pw1': 'b_pw1', 'cv_w_dw': 'w_dw', 'cv_b_dw': 'b_dw', 'cv_ln_g': 'ln_g', 'cv_ln_b': 'ln_b',
               'cv_b_pw2': 'b_pw2'}
    for nm, key in cv_keys.items():
        small_g[nm] = jnp.stack([g_mix[i][key].reshape(cvf[nm].shape[1:]) for i in range(depth) if i % 3 == 2])
    sg_names = list(small_g)
    sg_shapes = [small_g[n].shape for n in sg_names]
    (sg_all,) = _exchange("gather_small_grads", [_pack([small_g[n] for n in sg_names])], scatter=False)
    send(0, g_mix[0], sg_all)

    dmod_all = _unpack(sg_all, sg_shapes, lead=(N_DEV,))[0]
    out = {}

    def finish(name, parts, shard_of=None):
        w, m, v = P[name], P['m_' + name], P['v_' + name]
        cols = w.shape[-1]
        r2 = lambda a: a.reshape(-1, cols)
        rows = r2(w).shape[0]
        res = _adamw("adamw_" + name, parts.reshape(parts.shape[0], rows, cols), r2(w), r2(m), r2(v),
                     _row_tile(rows, cols))
        out[name] = tuple(r.reshape(w.shape) for r in res)

    c_t = c_act.T
    ada_g = []
    for i in range(depth):
        blk = lax.dynamic_slice_in_dim(dmod_all[:, i, :], me * n_ada, n_ada, axis=1)
        ada_g.append(_ada_outer(c_t, blk))
    finish('w_ada', jnp.stack(ada_g)[None])
    finish('b_ada', dmod_all)

    sm_names = [n for n in sg_names if n != 'dmod']
    sm_parts = jnp.stack([_pack([_unpack(sg_all[q], sg_shapes)[sg_names.index(n)] for n in sm_names])
                          for q in range(N_DEV)])

    def local_block(nm, a):
        if nm in cv_keys:
            n_loc = P[nm].shape[-1]
            return lax.dynamic_slice_in_dim(a, me * n_loc, n_loc, axis=a.ndim - 1)
        return a
    full_shapes = [small_g[n].shape for n in sm_names]

    def pack_full(prefix):
        arrs = []
        for nm in sm_names:
            a = P[prefix + nm]
            if nm in cv_keys:
                full = jnp.zeros(small_g[nm].shape, F32)
                a = lax.dynamic_update_slice_in_dim(full, a, me * a.shape[-1], axis=a.ndim - 1)
            arrs.append(a)
        return _pack(arrs)
    res = _adamw("adamw_small", sm_parts, pack_full(''), pack_full('m_'), pack_full('v_'),
                 _row_tile(sm_parts.shape[1], LANES))
    unp = [_unpack(r, full_shapes) for r in res]
    for idx, nm in enumerate(sm_names):
        out[nm] = tuple(local_block(nm, unp[t][idx]) for t in range(4))

    stacked = {}
    last = out['w_ada'][0]
    for k in reversed(range(nsub)):
        recv = _exchange_wait(f"scatter_wait_{k}", scatter_handles[k], True, last)
        for (nm, j), parts in zip(groups[k], recv):
            nl, r, cc = P[nm].shape
            flat = lambda a: a.reshape(nl * r, cc)
            stacked[nm] = _adamw(f"adamw_{nm}_{j}", parts, flat(P[nm]), flat(P['m_' + nm]), flat(P['v_' + nm]),
                                 _row_tile(r, cc), layer=j, prev=stacked.get(nm))
            last = stacked[nm][0]
    for nm in big:
        out[nm] = tuple(a.reshape(P[nm].shape) for a in stacked[nm])

    outs = [loss, grad_x]
    for t in range(4):
        outs += [out[n][t] for n in _WEIGHTS]
    return tuple(outs)
```

```python
import functools
import math

import jax
import jax.numpy as jnp
from jax import lax
from jax.experimental import pallas as pl
from jax.experimental.pallas import tpu as pltpu

F32 = jnp.float32
MXU_DTYPE = jnp.bfloat16
EPS = 1e-6
N_DEV = 8
HEAD_DIM = 64
LANES = 128
CONV_WIDTH = 31
CONV_PAD = 32
SG_CHUNK = 128
SG_BLOCK = 64
SG_GROUPS = 8
SCAN_BLOCK = 256
VMEM_LIMIT = 48 * 1024 * 1024
ATT_BWD_VMEM_LIMIT = 56 * 1024 * 1024
ADAM_LR, ADAM_B1, ADAM_B2, ADAM_EPS, ADAM_WD, ADAM_STEP = 0.001, 0.9, 0.999, 1e-08, 0.01, 10
NEG = -1e30

_WEIGHTS = ['norm_mix', 'norm_mlp', 'w_ada', 'b_ada', 'w_mlp_in', 'w_mlp_out', 'fox_w_in', 'fox_b_f',
            'fox_q_norm', 'fox_k_norm', 'fox_w_out', 'sg_w_in', 'sg_ln_g', 'sg_ln_b', 'sg_w_s', 'sg_b_s',
            'sg_w_out', 'cv_w_pw1', 'cv_b_pw1', 'cv_w_dw', 'cv_b_dw', 'cv_ln_g', 'cv_ln_b', 'cv_w_pw2',
            'cv_b_pw2']
_ARGS = ['x', 'c'] + _WEIGHTS + ['loss_target'] + ['m_' + n for n in _WEIGHTS] + ['v_' + n for n in _WEIGHTS]


def _cparams(sem=None, vmem=VMEM_LIMIT):
    return pltpu.CompilerParams(dimension_semantics=sem, vmem_limit_bytes=vmem)


def _colsum(v):
    return jnp.sum(v, axis=0, keepdims=True)


def _sigmoid(v):
    return 1.0 / (1.0 + jnp.exp(-v))


def _rowwise(name, fn, rows, consts, row_out, red_out, tr):
    n_rows = rows[0].shape[0]
    tr = min(tr, n_rows)
    assert n_rows % tr == 0
    nr, nc, no = len(rows), len(consts), len(row_out)

    def body(*refs):
        ins = [r[...] for r in refs[:nr + nc]]
        outs, reds = fn(*ins)
        out_refs = refs[nr + nc:nr + nc + no]
        red_refs = refs[nr + nc + no:]
        for o_ref, o in zip(out_refs, outs):
            o_ref[...] = o.astype(o_ref.dtype)
        if red_refs:
            @pl.when(pl.program_id(0) == 0)
            def _():
                for r_ref in red_refs:
                    r_ref[...] = jnp.zeros_like(r_ref)
            for r_ref, r in zip(red_refs, reds):
                r_ref[...] += r

    def rspec(a):
        return pl.BlockSpec((tr,) + a.shape[1:], lambda i: (i,) + (0,) * (a.ndim - 1))

    def cspec(shape):
        return pl.BlockSpec(shape, lambda i: (0,) * len(shape))

    out_shape = [jax.ShapeDtypeStruct((n_rows, w), dt) for w, dt in row_out]
    out_shape += [jax.ShapeDtypeStruct(s, F32) for s in red_out]
    out_specs = [pl.BlockSpec((tr, w), lambda i: (i, 0)) for w, _ in row_out] + [cspec(s) for s in red_out]
    res = pl.pallas_call(
        body, name=name, grid=(n_rows // tr,),
        in_specs=[rspec(a) for a in rows] + [cspec(a.shape) for a in consts],
        out_specs=out_specs, out_shape=out_shape,
        compiler_params=_cparams(("arbitrary",)),
    )(*rows, *consts)
    return res[:no], res[no:]


def _mm(name, a, b, *, ta=False, tb=False, out_dtypes=(F32,), epi=None, tiles=(), vecs=(), tm=512, tn=512,
        col_slots=0):
    m_dim, k_dim = (a.shape[1], a.shape[0]) if ta else a.shape
    n_dim = b.shape[0] if tb else b.shape[1]
    assert (b.shape[1] if tb else b.shape[0]) == k_dim
    tm, tn = min(tm, m_dim), min(tn, n_dim)
    assert m_dim % tm == 0 and n_dim % tn == 0, (name, m_dim, n_dim, tm, tn)
    dims = (((0 if ta else 1,), (1 if tb else 0,)), ((), ()))
    nx = len(tiles) + len(vecs)

    def body(a_ref, b_ref, *rest):
        acc = lax.dot_general(a_ref[...], b_ref[...], dims, preferred_element_type=F32)
        outs = epi(acc, *[r[...] for r in rest[:nx]]) if epi is not None else (acc,)
        for o_ref, o in zip(rest[nx:], outs):
            o_ref[...] = o.astype(o_ref.dtype)

    a_spec = pl.BlockSpec((k_dim, tm), lambda i, j: (0, i)) if ta else pl.BlockSpec((tm, k_dim), lambda i, j: (i, 0))
    b_spec = pl.BlockSpec((tn, k_dim), lambda i, j: (j, 0)) if tb else pl.BlockSpec((k_dim, tn), lambda i, j: (0, j))
    t_spec = pl.BlockSpec((tm, tn), lambda i, j: (i, j))
    v_spec = pl.BlockSpec((1, tn), lambda i, j: (0, j))
    if col_slots:
        assert n_dim == col_slots * tn
        o_spec = pl.BlockSpec((None, tm, tn), lambda i, j: (j, i, 0))
        o_shape = (col_slots, m_dim, tn)
    else:
        o_spec, o_shape = t_spec, (m_dim, n_dim)
    res = pl.pallas_call(
        body, name=name, grid=(m_dim // tm, n_dim // tn),
        in_specs=[a_spec, b_spec] + [t_spec] * len(tiles) + [v_spec] * len(vecs),
        out_specs=[o_spec] * len(out_dtypes),
        out_shape=[jax.ShapeDtypeStruct(o_shape, dt) for dt in out_dtypes],
        compiler_params=_cparams(("parallel", "parallel")),
    )(a, b, *tiles, *vecs)
    return res


def _exchange_copies(scatter, in_refs, land_refs, send_sems, recv_sems, local_sems):
    n = len(in_refs)
    x, y, c = lax.axis_index("x"), lax.axis_index("y"), lax.axis_index("c")
    me = 4 * x + 2 * y + c
    local = [pltpu.make_async_copy(in_refs[a].at[me] if scatter else in_refs[a], land_refs[a].at[me],
                                   local_sems.at[a]) for a in range(n)]
    send, arrive = [], []
    for k in range(1, N_DEV):
        px, py, pc = x ^ ((k >> 2) & 1), y ^ ((k >> 1) & 1), c ^ (k & 1)
        peer = 4 * px + 2 * py + pc
        for a in range(n):
            src = in_refs[a].at[peer] if scatter else in_refs[a]
            sems = dict(send_sem=send_sems.at[a * (N_DEV - 1) + k - 1], recv_sem=recv_sems.at[a * (N_DEV - 1) + k - 1],
                        device_id=(px, py, pc), device_id_type=pl.DeviceIdType.MESH)
            send.append(pltpu.make_async_remote_copy(src_ref=src, dst_ref=land_refs[a].at[me], **sems))
            arrive.append(pltpu.make_async_remote_copy(src_ref=src, dst_ref=land_refs[a].at[peer], **sems))
    return local, send, arrive


def _land_shape(a, scatter):
    return ((N_DEV,) + a.shape[1:]) if scatter else ((N_DEV,) + a.shape)


def _exchange(name, arrays, scatter):
    n = len(arrays)

    def body(*refs):
        local, send, arrive = _exchange_copies(scatter, refs[:n], refs[n:2 * n], *refs[2 * n:])
        for cp in local + send:
            cp.start()
        for cp, arr in zip(send, arrive):
            cp.wait_send()
            arr.wait_recv()
        for cp in local:
            cp.wait()

    any_spec = pl.BlockSpec(memory_space=pl.ANY)
    return pl.pallas_call(
        body, name=name,
        in_specs=[any_spec] * n, out_specs=[any_spec] * n,
        out_shape=[jax.ShapeDtypeStruct(_land_shape(a, scatter), a.dtype) for a in arrays],
        scratch_shapes=[pltpu.SemaphoreType.DMA((n * (N_DEV - 1),)),
                        pltpu.SemaphoreType.DMA((n * (N_DEV - 1),)),
                        pltpu.SemaphoreType.DMA((n,))],
        compiler_params=pltpu.CompilerParams(has_side_effects=True),
    )(*arrays)


_HBM = pl.BlockSpec(memory_space=pltpu.HBM)
_SEM = pl.BlockSpec(memory_space=pltpu.SEMAPHORE)
_EFFECT = pltpu.SideEffectType.DATAFLOW_SIDE_EFFECTING


def _exchange_start(name, arrays, scatter, dep):
    n = len(arrays)
    nsem = n * (N_DEV - 1)
    srcs = [pltpu.with_memory_space_constraint(a, pltpu.HBM) for a in arrays]
    lands = [pltpu.with_memory_space_constraint(lax.empty(_land_shape(a, scatter), a.dtype), pltpu.HBM) for a in arrays]

    def body(*refs):
        sems = refs[2 * n + 1:2 * n + 4]
        local, send, _ = _exchange_copies(scatter, refs[:n], refs[n:2 * n], *sems)
        for cp in local + send:
            cp.start()
        token = refs[-1]
        token[...] = jnp.zeros_like(token)

    res = pl.pallas_call(
        body, name=name,
        in_specs=[_HBM] * (2 * n) + [pl.BlockSpec(memory_space=pl.ANY)],
        out_specs=[_SEM] * 3 + [_HBM] * (2 * n) + [pl.BlockSpec(memory_space=pltpu.VMEM)],
        out_shape=[pltpu.SemaphoreType.DMA((nsem,)), pltpu.SemaphoreType.DMA((nsem,)), pltpu.SemaphoreType.DMA((n,))]
        + [pltpu.HBM(a.shape, a.dtype) for a in arrays]
        + [pltpu.HBM(_land_shape(a, scatter), a.dtype) for a in arrays]
        + [jax.ShapeDtypeStruct((8, LANES), F32)],
        input_output_aliases={i: 3 + i for i in range(2 * n)},
        compiler_params=pltpu.CompilerParams(has_side_effects=_EFFECT),
    )(*srcs, *lands, dep)
    return res[:-1], res[-1][0:1, 0:1]


def _exchange_wait(name, handles, scatter, after):
    n = (len(handles) - 3) // 2
    sems, thru = handles[:3], handles[3:]

    def body(*refs):
        local, send, arrive = _exchange_copies(scatter, refs[:n], refs[n:2 * n], *refs[2 * n:2 * n + 3])
        for cp, arr in zip(send, arrive):
            cp.wait_send()
            arr.wait_recv()
        for cp in local:
            cp.wait()

    res = pl.pallas_call(
        body, name=name,
        in_specs=[_HBM] * (2 * n) + [_SEM] * 3 + [pl.BlockSpec(memory_space=pl.ANY)],
        out_specs=[_HBM] * (2 * n),
        out_shape=[pltpu.HBM(t.shape, t.dtype) for t in thru],
        input_output_aliases={i: i for i in range(2 * n)},
        compiler_params=pltpu.CompilerParams(has_side_effects=_EFFECT),
    )(*thru, *sems, after)
    return res[n:]


def _norm_mod(x, g, sc, sh):
    r = lax.rsqrt(jnp.mean(x * x, axis=-1, keepdims=True) + EPS)
    return (x * r * g) * (1.0 + sc) + sh


def _first_norm(x, g, sc, sh):
    (h,), _ = _rowwise("first_norm", lambda x, g, sc, sh: ((_norm_mod(x, g, sc, sh),), ()),
                       [x], [g, sc, sh], [(x.shape[1], MXU_DTYPE)], [], 256)
    return h


def _res_norm(x, y, gate, g, sc, sh):
    def fn(x, y, gate, g, sc, sh):
        xn = x + gate * y
        return (xn, _norm_mod(xn, g, sc, sh)), ()
    (xn, h), _ = _rowwise("res_norm", fn, [x, y], [gate, g, sc, sh],
                          [(x.shape[1], F32), (x.shape[1], MXU_DTYPE)], [], 256)
    return xn, h


def _final_loss(x, y, gate, target):
    d = x.shape[1]

    def fn(x, y, target, gate):
        err = (x + gate * y) - target
        part = jnp.sum(jnp.sum(err * err, axis=-1, keepdims=True), axis=0, keepdims=True) * (0.5 / d)
        dx = err * (1.0 / d)
        return (dx, dx * gate), (jnp.broadcast_to(part, (1, LANES)), _colsum(dx * y))
    (dx, dy), (loss, dgate) = _rowwise("final_loss", fn, [x, y, target], [gate],
                                       [(d, F32), (d, MXU_DTYPE)], [(1, LANES), (1, d)], 256)
    return loss[0, 0], dx, dy, dgate


def _norm_bwd_core(dxo, dh, x, g, sc):
    r = lax.rsqrt(jnp.mean(x * x, axis=-1, keepdims=True) + EPS)
    xn = x * r
    dsh = _colsum(dh)
    dsc = _colsum(dh * (xn * g))
    dyy = dh * (1.0 + sc)
    dg = _colsum(dyy * xn)
    dxn = dyy * g
    dxi = dxo + r * (dxn - xn * jnp.mean(dxn * xn, axis=-1, keepdims=True))
    return dxi, dsh, dsc, dg


def _bwd_norm_gate(dxo, dh, x, y_prev, g, sc, gate_prev):
    d = x.shape[1]

    def fn(dxo, dh, x, y_prev, g, sc, gate_prev):
        dxi, dsh, dsc, dg = _norm_bwd_core(dxo, dh, x, g, sc)
        return (dxi, dxi * gate_prev), (dsh, dsc, dg, _colsum(dxi * y_prev))
    (dxi, dy), reds = _rowwise("bwd_norm_gate", fn, [dxo, dh, x, y_prev], [g, sc, gate_prev],
                               [(d, F32), (d, MXU_DTYPE)], [(1, d)] * 4, 256)
    return dxi, dy, reds


def _bwd_norm_first(dxo, dh, x, g, sc):
    d = x.shape[1]

    def fn(dxo, dh, x, g, sc):
        dxi, dsh, dsc, dg = _norm_bwd_core(dxo, dh, x, g, sc)
        return (dxi,), (dsh, dsc, dg)
    (dxi,), reds = _rowwise("bwd_norm_first", fn, [dxo, dh, x], [g, sc], [(d, F32)], [(1, d)] * 3, 256)
    return dxi, reds


def _mlp_fwd(h, w1, w2):
    def epi(acc):
        r = jnp.maximum(acc, 0.0)
        return acc, r * r
    a, z = _mm("mlp_in", h, w1, out_dtypes=(MXU_DTYPE, MXU_DTYPE), epi=epi, tm=1024, tn=512)
    (out,) = _mm("mlp_out", z, w2, tm=512, tn=512)
    return out, (a, z)


def _mlp_bwd(dy, h, w1, w2, saved):
    a, z = saved

    def epi(acc, a):
        return (acc * (2.0 * jnp.maximum(a.astype(F32), 0.0)),)
    (da,) = _mm("mlp_dz", dy, w2, tb=True, out_dtypes=(MXU_DTYPE,), epi=epi, tiles=(a,), tm=1024, tn=512)
    (dw2,) = _mm("mlp_dw2", z, dy, ta=True, out_dtypes=(MXU_DTYPE,))
    (dw1,) = _mm("mlp_dw1", h, da, ta=True, out_dtypes=(MXU_DTYPE,), tn=da.shape[1] // N_DEV, col_slots=N_DEV)
    (dh,) = _mm("mlp_dh", da, w1, tb=True)
    return dh, dw1, dw2


def _split3(v):
    hi = v.astype(jnp.bfloat16)
    r1 = v - hi.astype(F32)
    mid = r1.astype(jnp.bfloat16)
    lo = (r1 - mid.astype(F32)).astype(jnp.bfloat16)
    return hi, mid, lo


def _tri_matmul(tri, v):
    hi, mid, lo = _split3(v)
    dot = functools.partial(jnp.dot, preferred_element_type=F32)
    return dot(tri, hi) + dot(tri, mid) + dot(tri, lo)


def _log_sigmoid(v):
    return jnp.minimum(v, 0.0) - jnp.log(1.0 + jnp.exp(-jnp.abs(v)))


def _gate_fwd(proj, b_pad, col_block):
    s = proj.shape[0]
    tb = min(SCAN_BLOCK, s)
    nblk = s // tb

    def body(f_ref, b_ref, o_ref):
        row = lax.broadcasted_iota(jnp.int32, (tb, tb), 0)
        col = lax.broadcasted_iota(jnp.int32, (tb, tb), 1)
        tri = (col <= row).astype(jnp.bfloat16)

        def step(i, carry):
            rows = pl.ds(pl.multiple_of(i * tb, tb), tb)
            lf = _log_sigmoid(f_ref[rows, :] + b_ref[...])
            f = _tri_matmul(tri, lf) + carry
            o_ref[rows, :] = f
            return f[tb - 1:tb, :]
        lax.fori_loop(0, nblk, step, jnp.zeros((1, LANES), F32))

    return pl.pallas_call(
        body, name="gate_fwd", grid=(1,),
        in_specs=[pl.BlockSpec((s, LANES), lambda i: (0, col_block)), pl.BlockSpec((1, LANES), lambda i: (0, 0))],
        out_specs=pl.BlockSpec((s, LANES), lambda i: (0, 0)),
        out_shape=jax.ShapeDtypeStruct((s, LANES), F32),
        compiler_params=_cparams(("arbitrary",)),
    )(proj, b_pad)


def _gate_bwd(proj, b_pad, d_f, col_block):
    s = proj.shape[0]
    tb = min(SCAN_BLOCK, s)
    nblk = s // tb

    def body(f_ref, b_ref, d_ref, o_ref, db_ref):
        row = lax.broadcasted_iota(jnp.int32, (tb, tb), 0)
        col = lax.broadcasted_iota(jnp.int32, (tb, tb), 1)
        tri = (col >= row).astype(jnp.bfloat16)

        def step(j, carry):
            acc, db = carry
            i = nblk - 1 - j
            rows = pl.ds(pl.multiple_of(i * tb, tb), tb)
            dlf = _tri_matmul(tri, d_ref[rows, :]) + acc
            dpre = dlf * _sigmoid(-(f_ref[rows, :] + b_ref[...]))
            o_ref[rows, :] = dpre.astype(o_ref.dtype)
            return dlf[0:1, :], db + _colsum(dpre)
        _, db = lax.fori_loop(0, nblk, step, (jnp.zeros((1, LANES), F32), jnp.zeros((1, LANES), F32)))
        db_ref[...] = db

    return pl.pallas_call(
        body, name="gate_bwd", grid=(1,),
        in_specs=[pl.BlockSpec((s, LANES), lambda i: (0, col_block)), pl.BlockSpec((1, LANES), lambda i: (0, 0)),
                  pl.BlockSpec((s, LANES), lambda i: (0, 0))],
        out_specs=[pl.BlockSpec((s, LANES), lambda i: (0, 0)), pl.BlockSpec((1, LANES), lambda i: (0, 0))],
        out_shape=[jax.ShapeDtypeStruct((s, LANES), MXU_DTYPE), jax.ShapeDtypeStruct((1, LANES), F32)],
        compiler_params=_cparams(("arbitrary",)),
    )(proj, b_pad, d_f)


def _head_masks():
    lane = lax.broadcasted_iota(jnp.int32, (1, LANES), 1)
    return lane < HEAD_DIM


def _pair_norm(v, g, first):
    v2 = v * v
    ss0 = jnp.sum(jnp.where(first, v2, 0.0), axis=-1, keepdims=True)
    ss1 = jnp.sum(jnp.where(first, 0.0, v2), axis=-1, keepdims=True)
    r = jnp.where(first, lax.rsqrt(ss0 * (1.0 / HEAD_DIM) + EPS), lax.rsqrt(ss1 * (1.0 / HEAD_DIM) + EPS))
    vn = v * r
    return vn * g, vn, r


_NT = (((1,), (1,)), ((), ()))
_TN = (((0,), (0,)), ((), ()))

ATT_TQ = 512
ATT_FWD_TQ = 1024
ATT_TK = 256
AUG_F, AUG_ONE = 0, 3


def _own_lanes(hd):
    lane = lax.broadcasted_iota(jnp.int32, (1, LANES), 1)
    return (lane < HEAD_DIM) if hd == 0 else (lane >= HEAD_DIM)


def _aug_lanes(hd, f_other):
    lane = lax.broadcasted_iota(jnp.int32, (1, LANES), 1) - (HEAD_DIM if hd == 0 else 0)
    hi, mid, lo = [t.astype(F32) for t in _split3(f_other)]
    zero = jnp.zeros_like(f_other)
    f_terms = jnp.where(lane == 0, hi, jnp.where(lane == 1, mid, jnp.where(lane == 2, lo, zero)))
    f_shift = jnp.where(lane == 3, hi, jnp.where(lane == 4, mid, jnp.where(lane == 5, lo, zero)))
    ones_lo = jnp.where(lane < 3, 1.0, 0.0) * jnp.where(lane >= 0, 1.0, 0.0)
    ones_hi = jnp.where(lane < 6, 1.0, 0.0) * jnp.where(lane >= 3, 1.0, 0.0)
    return f_terms + ones_hi, ones_lo - f_shift


def _attn_operands(q_raw, k_raw, f_rep, qg, kg, scale):
    first = _head_masks()
    qn, _, _ = _pair_norm(q_raw, qg, first)
    kn, _, _ = _pair_norm(k_raw, kg, first)
    f_other = pltpu.roll(f_rep, HEAD_DIM, 1)
    out = []
    for hd in range(2):
        own = _own_lanes(hd)
        q_x, k_x = _aug_lanes(hd, f_other)
        out.append((jnp.where(own, qn * scale, q_x), jnp.where(own, kn, k_x)))
    return out


def _set_cols(a, b, c0):
    return b if c0 == 0 else jnp.concatenate([a[:, :c0], b], axis=1)


def _add_cols(a, b, c0):
    return _set_cols(a, a[:, c0:] + b, c0)


def _causal_t(tk, tq, off):
    r = lax.broadcasted_iota(jnp.int32, (tk, tq), 0)
    c = lax.broadcasted_iota(jnp.int32, (tk, tq), 1)
    return (r - c) <= off


def _big(shape, index_map):
    return pl.BlockSpec(shape, index_map, pipeline_mode=pl.Buffered(1))


def _attn_fwd_t(proj, f_rep, qg, kg, d_model):
    s = proj.shape[0]
    pairs = d_model // LANES
    tq, tk = min(ATT_FWD_TQ, s), min(ATT_TK, s)
    assert tq % tk == 0 and s % tq == 0
    nq = s // tq
    n_diag = tq // tk
    scale = HEAD_DIM ** -0.5
    ch = tk

    def body(q_ref, k_ref, v_ref, frep_ref, qg_ref, kg_ref, o_ref, lse_ref, qt_s, k_s, vt_s):
        for ci in range(s // ch):
            rows = pl.ds(ci * ch, ch)
            ops = _attn_operands(q_ref[rows, :], k_ref[rows, :], frep_ref[rows, :], qg_ref[...], kg_ref[...], scale)
            vv = v_ref[rows, :]
            for hd in range(2):
                own = _own_lanes(hd)
                lane = lax.broadcasted_iota(jnp.int32, (1, LANES), 1)
                one_lane = lane == (HEAD_DIM if hd == 0 else 0)
                qt_s[hd, :, rows] = ops[hd][0].T.astype(qt_s.dtype)
                k_s[hd, rows, :] = ops[hd][1].astype(k_s.dtype)
                vt_s[hd, :, rows] = jnp.where(own, vv, jnp.where(one_lane, 1.0, 0.0)).T.astype(vt_s.dtype)

        def q_block(qi, _):
            q0 = pl.multiple_of(qi * tq, tq)
            qcols = pl.ds(q0, tq)
            nfull = q0 // tk
            qts = [qt_s[hd, :, qcols] for hd in range(2)]

            def krows(kj):
                return pl.ds(pl.multiple_of(kj * tk, tk), tk)

            def scores(hd, kj):
                return jnp.dot(k_s[hd, krows(kj), :], qts[hd], preferred_element_type=F32)

            def kv_step(kj, carry):
                new = []
                for hd in range(2):
                    m, acc, p_prev = carry[hd]
                    st = scores(hd, kj)
                    pv = jnp.dot(vt_s[hd, :, krows(jnp.maximum(kj - 1, 0))], p_prev, preferred_element_type=F32)
                    m_new = jnp.maximum(m, jnp.max(st, axis=0, keepdims=True))
                    p = jnp.exp(st - m_new).astype(vt_s.dtype)
                    new.append((m_new, jnp.exp(m - m_new) * (acc + pv), p))
                return tuple(new)

            def diag_step(t, carry, last):
                c0, kj = t * tk, nfull + t
                new = []
                for hd in range(2):
                    m, acc, p_prev, c_prev = carry[hd]
                    pv = jnp.dot(vt_s[hd, :, krows(jnp.maximum(kj - 1, 0))], p_prev, preferred_element_type=F32)
                    acc = _add_cols(acc, pv, c_prev)
                    st = jnp.dot(k_s[hd, krows(kj), :], qts[hd][:, c0:], preferred_element_type=F32)
                    st = jnp.where(_causal_t(tk, tq - c0, 0), st, NEG)
                    m_new = jnp.maximum(m[:, c0:], jnp.max(st, axis=0, keepdims=True))
                    p = jnp.exp(st - m_new).astype(vt_s.dtype)
                    acc = _set_cols(acc, jnp.exp(m[:, c0:] - m_new) * acc[:, c0:], c0)
                    m = _set_cols(m, m_new, c0)
                    if last:
                        pv = jnp.dot(vt_s[hd, :, krows(kj)], p, preferred_element_type=F32)
                        new.append((m, _add_cols(acc, pv, c0)))
                    else:
                        new.append((m, acc, p, c0))
                return tuple(new)

            init = tuple((jnp.full((1, tq), NEG, F32), jnp.zeros((LANES, tq), F32),
                          jnp.zeros((tk, tq), vt_s.dtype)) for hd in range(2))
            carry = lax.fori_loop(0, nfull, kv_step, init)
            carry = tuple(cr + (0,) for cr in carry)
            for t in range(n_diag):
                carry = diag_step(t, carry, t == n_diag - 1)
            o_parts, lse_parts = [], []
            for hd, (m, acc) in enumerate(carry):
                e0 = HEAD_DIM if hd == 0 else 0
                l = acc[e0:e0 + 1, :]
                o_parts.append((acc / l).T)
                lse_parts.append(m + jnp.log(l))
            o_ref[pl.ds(q0, tq), :] = jnp.where(_head_masks(), o_parts[0], o_parts[1]).astype(o_ref.dtype)
            lse_ref[0, :, qcols] = jnp.concatenate(lse_parts, axis=0)
            return 0
        lax.fori_loop(0, nq, q_block, 0)

    blk = lambda off: _big((s, LANES), lambda h: (0, off + h))
    vec = pl.BlockSpec((1, LANES), lambda h: (0, 0))
    return pl.pallas_call(
        body, name="attn_fwd", grid=(pairs,),
        in_specs=[blk(0), blk(pairs), blk(2 * pairs), blk(0), vec, vec],
        out_specs=[pl.BlockSpec((s, LANES), lambda h: (0, h)), pl.BlockSpec((1, 2, s), lambda h: (h, 0, 0))],
        out_shape=[jax.ShapeDtypeStruct((s, d_model), MXU_DTYPE), jax.ShapeDtypeStruct((pairs, 2, s), F32)],
        scratch_shapes=[pltpu.VMEM((2, LANES, s), MXU_DTYPE), pltpu.VMEM((2, s, LANES), MXU_DTYPE),
                        pltpu.VMEM((2, LANES, s), MXU_DTYPE)],
        compiler_params=_cparams(("arbitrary",)),
    )(proj, proj, proj, f_rep, qg, kg)


def _attn_bwd_t(proj, do, o, lse, f_rep, qg, kg, d_model):
    s = proj.shape[0]
    pairs = d_model // LANES
    tq, tk = min(ATT_TQ, s), min(ATT_TK, s)
    assert tq % tk == 0 and s % tq == 0
    nq = s // tq
    n_diag = tq // tk
    scale = HEAD_DIM ** -0.5
    ch = tk

    def norm_bwd(raw, g, dn, first):
        _, xn, r = _pair_norm(raw, g, first)
        dxn = dn * g
        t = dxn * xn
        mu0 = jnp.sum(jnp.where(first, t, 0.0), axis=-1, keepdims=True)
        mu1 = jnp.sum(jnp.where(first, 0.0, t), axis=-1, keepdims=True)
        mu = jnp.where(first, mu0, mu1) * (1.0 / HEAD_DIM)
        return r * (dxn - xn * mu), _colsum(dn * xn)

    def body(q_ref, k_ref, v_ref, do_ref, o_ref, lse_ref, frep_ref, qg_ref, kg_ref,
             dq_ref, dk_ref, dv_ref, df_ref, dqg_ref, dkg_ref,
             q_s, qt_s, k_s, kt_s, v_s, do_s, dot_s, dl_s, dk_s, dv_s):
        first = _head_masks()
        hp = pl.program_id(0)
        lane = lax.broadcasted_iota(jnp.int32, (1, LANES), 1)
        for ci in range(s // ch):
            rows = pl.ds(ci * ch, ch)
            ops = _attn_operands(q_ref[rows, :], k_ref[rows, :], frep_ref[rows, :], qg_ref[...], kg_ref[...], scale)
            v_s[rows, :] = v_ref[rows, :].astype(v_s.dtype)
            dov = do_ref[rows, :].astype(F32)
            ot = o_ref[rows, :].astype(F32).T
            for hd in range(2):
                own = _own_lanes(hd)
                q_s[hd, rows, :] = ops[hd][0].astype(q_s.dtype)
                qt_s[hd, :, rows] = ops[hd][0].T.astype(qt_s.dtype)
                k_s[hd, rows, :] = ops[hd][1].astype(k_s.dtype)
                kt_s[hd, :, rows] = ops[hd][1].T.astype(kt_s.dtype)
                doh = jnp.where(own, dov, 0.0)
                do_s[hd, rows, :] = doh.astype(do_s.dtype)
                doht = doh.T
                dot_s[hd, :, rows] = doht.astype(dot_s.dtype)
                dl_s[hd:hd + 1, rows] = jnp.sum(doht * ot, axis=0, keepdims=True)
        dk_s[...] = jnp.zeros_like(dk_s)
        dv_s[...] = jnp.zeros_like(dv_s)

        @pl.when(hp == 0)
        def _():
            df_ref[...] = jnp.zeros_like(df_ref)

        def q_block(qi, dqg):
            q0 = pl.multiple_of(qi * tq, tq)
            qcols = pl.ds(q0, tq)
            qrows = pl.ds(q0, tq)
            nfull = q0 // tk
            qts = [qt_s[hd, :, qcols] for hd in range(2)]
            dots = [dot_s[hd, :, qcols] for hd in range(2)]
            qns = [q_s[hd, qrows, :] for hd in range(2)]
            dons = [do_s[hd, qrows, :] for hd in range(2)]
            lse_r = [lse_ref[0, hd:hd + 1, qcols] for hd in range(2)]
            dl_r = [dl_s[hd:hd + 1, qcols] for hd in range(2)]
            bdt = qt_s.dtype

            def krows(kj):
                return pl.ds(pl.multiple_of(kj * tk, tk), tk)

            def scores(hd, kj):
                return (jnp.dot(k_s[hd, krows(kj), :], qts[hd], preferred_element_type=F32),
                        jnp.dot(v_s[krows(kj), :], dots[hd], preferred_element_type=F32))

            def products(hd, rows, ds, p, dqt, c0=0):
                dk_s[hd, rows, :] += jnp.dot(ds, qns[hd][c0:, :], preferred_element_type=F32)
                dv_s[rows, :] += jnp.dot(p, dons[hd][c0:, :], preferred_element_type=F32)
                return _add_cols(dqt, jnp.dot(kt_s[hd, :, rows], ds, preferred_element_type=F32), c0)

            def kv_step(kj, carry):
                new = []
                for hd in range(2):
                    dqt, ds_prev, p_prev = carry[hd]
                    st, dp = scores(hd, kj)
                    dqt = products(hd, krows(jnp.maximum(kj - 1, 0)), ds_prev, p_prev, dqt)
                    p = jnp.exp(st - lse_r[hd])
                    new.append((dqt, (p * (dp - dl_r[hd])).astype(bdt), p.astype(bdt)))
                return tuple(new)

            def diag_step(t, carry, last):
                c0, kj = t * tk, nfull + t
                new = []
                for hd in range(2):
                    dqt, ds_prev, p_prev, c_prev = carry[hd]
                    st = jnp.dot(k_s[hd, krows(kj), :], qts[hd][:, c0:], preferred_element_type=F32)
                    dp = jnp.dot(v_s[krows(kj), :], dots[hd][:, c0:], preferred_element_type=F32)
                    dqt = products(hd, krows(jnp.maximum(kj - 1, 0)), ds_prev, p_prev, dqt, c_prev)
                    st = jnp.where(_causal_t(tk, tq - c0, 0), st, NEG)
                    p = jnp.exp(st - lse_r[hd][:, c0:])
                    ds = (p * (dp - dl_r[hd][:, c0:])).astype(bdt)
                    if last:
                        new.append(products(hd, krows(kj), ds, p.astype(bdt), dqt, c0))
                    else:
                        new.append((dqt, ds, p.astype(bdt), c0))
                return tuple(new)

            init = tuple((jnp.zeros((LANES, tq), F32), jnp.zeros((tk, tq), bdt), jnp.zeros((tk, tq), bdt))
                         for hd in range(2))
            carry = lax.fori_loop(0, nfull, kv_step, init)
            carry = tuple(cr + (0,) for cr in carry)
            for t in range(n_diag):
                carry = diag_step(t, carry, t == n_diag - 1)
            dq_parts = [dqt.T for dqt in carry]
            rs0 = dq_parts[0][:, HEAD_DIM + AUG_F:HEAD_DIM + AUG_F + 1]
            rs1 = dq_parts[1][:, AUG_F:AUG_F + 1]
            df_ref[qrows, :] += jnp.where(lane == 2 * hp, rs0, 0.0) + jnp.where(lane == 2 * hp + 1, rs1, 0.0)
            dqn = jnp.where(first, dq_parts[0], dq_parts[1]) * scale
            dq_raw, dg = norm_bwd(q_ref[qrows, :], qg_ref[...], dqn, first)
            dq_ref[qrows, :] = dq_raw.astype(dq_ref.dtype)
            return dqg + dg
        dqg_ref[0] = lax.fori_loop(0, nq, q_block, jnp.zeros((1, LANES), F32))

        dkg = jnp.zeros((1, LANES), F32)
        for ci in range(s // ch):
            rows = pl.ds(ci * ch, ch)
            dk0, dk1 = dk_s[0, rows, :], dk_s[1, rows, :]
            cs0 = dk0[:, HEAD_DIM + AUG_ONE:HEAD_DIM + AUG_ONE + 1]
            cs1 = dk1[:, AUG_ONE:AUG_ONE + 1]
            df_ref[rows, :] -= jnp.where(lane == 2 * hp, cs0, 0.0) + jnp.where(lane == 2 * hp + 1, cs1, 0.0)
            dk_raw, dg = norm_bwd(k_ref[rows, :], kg_ref[...], jnp.where(first, dk0, dk1), first)
            dk_ref[rows, :] = dk_raw.astype(dk_ref.dtype)
            dkg = dkg + dg
            dv_ref[rows, :] = dv_s[rows, :].astype(dv_ref.dtype)
        dkg_ref[0] = dkg

    blk = lambda off: _big((s, LANES), lambda h: (0, off + h))
    outb = pl.BlockSpec((s, LANES), lambda h: (0, h))
    vec = pl.BlockSpec((1, LANES), lambda h: (0, 0))
    gout = pl.BlockSpec((1, 1, LANES), lambda h: (h, 0, 0))
    act = jax.ShapeDtypeStruct((s, d_model), MXU_DTYPE)
    gsh = jax.ShapeDtypeStruct((pairs, 1, LANES), F32)
    pair_rows = pltpu.VMEM((2, s, LANES), MXU_DTYPE)
    pair_cols = pltpu.VMEM((2, LANES, s), MXU_DTYPE)
    return pl.pallas_call(
        body, name="attn_bwd", grid=(pairs,),
        in_specs=[blk(0), blk(pairs), blk(2 * pairs), blk(0), blk(0),
                  pl.BlockSpec((1, 2, s), lambda h: (h, 0, 0)), blk(0), vec, vec],
        out_specs=[outb, outb, outb, pl.BlockSpec((s, LANES), lambda h: (0, 0)), gout, gout],
        out_shape=[act, act, act, jax.ShapeDtypeStruct((s, LANES), F32), gsh, gsh],
        scratch_shapes=[pair_rows, pair_cols, pair_rows, pair_cols, pltpu.VMEM((s, LANES), MXU_DTYPE),
                        pair_rows, pair_cols, pltpu.VMEM((8, s), F32),
                        pltpu.VMEM((2, s, LANES), F32), pltpu.VMEM((s, LANES), F32)],
        compiler_params=_cparams(("arbitrary",), ATT_BWD_VMEM_LIMIT),
    )(proj, proj, proj, do, o, lse, f_rep, qg, kg)


def _pad_cols(a, n):
    return jnp.pad(a, ((0, 0), (0, n - a.shape[1])))


def _fox_fwd(h, w):
    d = h.shape[1]
    pairs = d // LANES
    (proj,) = _mm("fox_in", h, w["w_in"], tm=1024, tn=640)
    f_cum = _gate_fwd(proj, w["b_f"], 3 * pairs)
    f16 = f_cum[:, :d // HEAD_DIM]
    f_rep = jnp.repeat(f16, HEAD_DIM, axis=1)
    o, lse = _attn_fwd_t(proj, f_rep, w["qg"], w["kg"], d)
    (y,) = _mm("fox_out", o, w["w_out"])
    return y, (proj, f_rep, o, lse)


def _fox_bwd(dy, h, w, saved):
    proj, f_rep, o, lse = saved
    s, d = h.shape
    pairs = d // LANES
    (do,) = _mm("fox_do", dy, w["w_out"], tb=True, out_dtypes=(MXU_DTYPE,))
    (dw_out,) = _mm("fox_dwout", o, dy, ta=True, out_dtypes=(MXU_DTYPE,))
    dq, dk, dv, d_f, dqg, dkg = _attn_bwd_t(proj, do, o, lse, f_rep, w["qg"], w["kg"], d)
    dfpre, db_f = _gate_bwd(proj, w["b_f"], d_f, 3 * pairs)
    dproj = jnp.concatenate([dq, dk, dv, dfpre], axis=1)
    (dw_in,) = _mm("fox_dwin", h, dproj, ta=True, out_dtypes=(MXU_DTYPE,), tn=640)
    (dh,) = _mm("fox_dh", dproj, w["w_in"], tb=True)
    fold = lambda g: jnp.sum(g, axis=(0, 1)).reshape(2, HEAD_DIM).sum(axis=0)
    return dh, dict(w_in=dw_in, w_out=dw_out, b_f=db_f[0, :d // HEAD_DIM], qg=fold(dqg), kg=fold(dkg))


_GELU_C = math.sqrt(2.0 / math.pi)


def _gelu(v):
    return 0.5 * v * (1.0 + jnp.tanh(_GELU_C * (v + 0.044715 * v * v * v)))


def _gelu_grad(v):
    t = jnp.tanh(_GELU_C * (v + 0.044715 * v * v * v))
    return 0.5 * (1.0 + t) + 0.5 * v * (1.0 - t * t) * (_GELU_C * (1.0 + 3.0 * 0.044715 * v * v))


def _ln_stats(v):
    mu = jnp.mean(v, axis=-1, keepdims=True)
    vc = v - mu
    r = lax.rsqrt(jnp.mean(vc * vc, axis=-1, keepdims=True) + EPS)
    return vc * r, r


def _sg_mask():
    t = lax.broadcasted_iota(jnp.int32, (SG_CHUNK, SG_CHUNK), 0) // SG_BLOCK
    sidx = lax.broadcasted_iota(jnp.int32, (SG_CHUNK, SG_CHUNK), 1) // SG_BLOCK
    return sidx <= t


def _sgu_fwd(uv_pre, ln_g, ln_b, w_s, b_st):
    s, w2 = uv_pre.shape
    wd = w2 // 2
    tr = min(256, s)

    def fn(uv_pre, ln_g, ln_b, w_s, b_st):
        uv = _gelu(uv_pre)
        u = uv[:, :wd]
        vh, _ = _ln_stats(uv[:, wd:])
        vl = (vh * ln_g + ln_b).astype(MXU_DTYPE)
        mask = _sg_mask()
        cols = []
        for g in range(SG_GROUPS):
            wg = jnp.where(mask, w_s[g * SG_CHUNK:(g + 1) * SG_CHUNK, :], 0.0).astype(MXU_DTYPE)
            parts = []
            for ci in range(tr // SG_CHUNK):
                vt = vl[ci * SG_CHUNK:(ci + 1) * SG_CHUNK, g * SG_CHUNK:(g + 1) * SG_CHUNK]
                parts.append(jnp.dot(wg, vt, preferred_element_type=F32) + b_st[:, g:g + 1])
            cols.append(jnp.concatenate(parts, axis=0) if len(parts) > 1 else parts[0])
        vout = jnp.concatenate(cols, axis=1)
        return (u * vout,), ()
    (m,), _ = _rowwise("sgu_fwd", fn, [uv_pre], [ln_g, ln_b, w_s, b_st], [(wd, MXU_DTYPE)], [], tr)
    return m


def _sgu_bwd(uv_pre, dm, ln_g, ln_b, w_s, b_st):
    s, w2 = uv_pre.shape
    wd = w2 // 2
    tr = min(256, s)

    def fn(uv_pre, dm, ln_g, ln_b, w_s, b_st):
        uv = _gelu(uv_pre)
        u = uv[:, :wd]
        vh, r = _ln_stats(uv[:, wd:])
        vl = (vh * ln_g + ln_b).astype(MXU_DTYPE)
        mask = _sg_mask()
        lane = lax.broadcasted_iota(jnp.int32, (1, LANES), 1)
        cols, dcols, dws, dbs = [], [], [], jnp.zeros((SG_CHUNK, LANES), F32)
        for g in range(SG_GROUPS):
            wg = jnp.where(mask, w_s[g * SG_CHUNK:(g + 1) * SG_CHUNK, :], 0.0).astype(MXU_DTYPE)
            parts, dparts = [], []
            dwg = jnp.zeros((SG_CHUNK, SG_CHUNK), F32)
            dbg = jnp.zeros((SG_CHUNK, 1), F32)
            for ci in range(tr // SG_CHUNK):
                rs = slice(ci * SG_CHUNK, (ci + 1) * SG_CHUNK)
                cs = slice(g * SG_CHUNK, (g + 1) * SG_CHUNK)
                vt = vl[rs, cs]
                parts.append(jnp.dot(wg, vt, preferred_element_type=F32) + b_st[:, g:g + 1])
                dvo = dm[rs, cs] * u[rs, cs]
                dvob = dvo.astype(MXU_DTYPE)
                dparts.append(lax.dot_general(wg, dvob, _TN, preferred_element_type=F32))
                dwg = dwg + lax.dot_general(dvob, vt, _NT, preferred_element_type=F32)
                dbg = dbg + jnp.sum(dvo, axis=-1, keepdims=True)
            cols.append(jnp.concatenate(parts, axis=0) if len(parts) > 1 else parts[0])
            dcols.append(jnp.concatenate(dparts, axis=0) if len(dparts) > 1 else dparts[0])
            dws.append(jnp.where(mask, dwg, 0.0))
            dbs = dbs + jnp.where(lane == g, dbg, 0.0)
        vout = jnp.concatenate(cols, axis=1)
        dvl = jnp.concatenate(dcols, axis=1)
        du = dm * vout
        dlg = _colsum(dvl * vh)
        dlb = _colsum(dvl)
        dvh = dvl * ln_g
        dv = r * (dvh - jnp.mean(dvh, axis=-1, keepdims=True) - vh * jnp.mean(dvh * vh, axis=-1, keepdims=True))
        dpre = jnp.concatenate([du, dv], axis=1) * _gelu_grad(uv_pre)
        return (dpre,), (dlg, dlb, jnp.concatenate(dws, axis=0), dbs)
    (dpre,), reds = _rowwise("sgu_bwd", fn, [uv_pre, dm], [ln_g, ln_b, w_s, b_st], [(w2, MXU_DTYPE)],
                             [(1, wd), (1, wd), (SG_GROUPS * SG_CHUNK, SG_CHUNK), (SG_CHUNK, LANES)], tr)
    return dpre, reds


def _sg_fwd(h, w):
    (uv_pre,) = _mm("sg_in", h, w["w_in"], tm=1024, tn=512)
    m = _sgu_fwd(uv_pre, w["ln_g"], w["ln_b"], w["w_s"], w["b_st"])
    (y,) = _mm("sg_out", m, w["w_out"])
    return y, (uv_pre, m)


def _sg_bwd(dy, h, w, saved):
    uv_pre, m = saved
    (dm,) = _mm("sg_dm", dy, w["w_out"], tb=True)
    (dw_out,) = _mm("sg_dwout", m, dy, ta=True, out_dtypes=(MXU_DTYPE,))
    dpre, (dlg, dlb, dws, dbs) = _sgu_bwd(uv_pre, dm, w["ln_g"], w["ln_b"], w["w_s"], w["b_st"])
    (dw_in,) = _mm("sg_dwin", h, dpre, ta=True, out_dtypes=(MXU_DTYPE,), tn=dpre.shape[1] // N_DEV, col_slots=N_DEV)
    (dh,) = _mm("sg_dh", dpre, w["w_in"], tb=True)
    return dh, dict(w_in=dw_in, w_out=dw_out, ln_g=dlg, ln_b=dlb, w_s=dws, b_s=dbs[:, :SG_GROUPS].T)


def _conv_fwd_kernel(ypad, w_dw, b_dw):
    s = ypad.shape[0] - CONV_PAD
    d = ypad.shape[1]
    tt = min(256, s)
    ext = tt + CONV_PAD

    def body(y_ref, w_ref, b_ref, o_ref):
        def chunk(ci, _):
            base = pl.multiple_of(ci * tt, tt)
            e = y_ref[pl.ds(base, ext), :]
            acc = jnp.zeros((tt, LANES), F32) + b_ref[...]
            for j in range(CONV_WIDTH):
                sh = pltpu.roll(e, ext - (CONV_PAD - CONV_WIDTH + 1 + j), 0)[:tt, :]
                acc = acc + w_ref[j:j + 1, :] * sh
            o_ref[pl.ds(base, tt), :] = acc
            return 0
        lax.fori_loop(0, s // tt, chunk, 0)

    return pl.pallas_call(
        body, name="conv_fwd", grid=(d // LANES,),
        in_specs=[pl.BlockSpec((s + CONV_PAD, LANES), lambda i: (0, i)),
                  pl.BlockSpec((CONV_PAD, LANES), lambda i: (0, i)), pl.BlockSpec((1, LANES), lambda i: (0, i))],
        out_specs=pl.BlockSpec((s, LANES), lambda i: (0, i)),
        out_shape=jax.ShapeDtypeStruct((s, d), F32),
        compiler_params=_cparams(("parallel",)),
    )(ypad, w_dw, b_dw)


def _conv_bwd_kernel(ypad, dpad, w_dw):
    s = ypad.shape[0] - CONV_PAD
    d = ypad.shape[1]
    tt = min(256, s)
    ext = tt + CONV_PAD

    def body(y_ref, d_ref, w_ref, o_ref, dw_ref):
        dw_ref[...] = jnp.zeros_like(dw_ref)

        def chunk(ci, _):
            base = pl.multiple_of(ci * tt, tt)
            ye = y_ref[pl.ds(base, ext), :]
            de = d_ref[pl.ds(base, ext), :]
            dcur = de[:tt, :]
            acc = jnp.zeros((tt, LANES), F32)
            for j in range(CONV_WIDTH):
                back = CONV_WIDTH - 1 - j
                dsh = dcur if back == 0 else pltpu.roll(de, ext - back, 0)[:tt, :]
                acc = acc + w_ref[j:j + 1, :] * dsh
                ysh = pltpu.roll(ye, ext - (CONV_PAD - CONV_WIDTH + 1 + j), 0)[:tt, :]
                dw_ref[j:j + 1, :] += _colsum(dcur * ysh)
            o_ref[pl.ds(base, tt), :] = acc
            return 0
        lax.fori_loop(0, s // tt, chunk, 0)

    return pl.pallas_call(
        body, name="conv_bwd", grid=(d // LANES,),
        in_specs=[pl.BlockSpec((s + CONV_PAD, LANES), lambda i: (0, i)),
                  pl.BlockSpec((s + CONV_PAD, LANES), lambda i: (0, i)),
                  pl.BlockSpec((CONV_PAD, LANES), lambda i: (0, i))],
        out_specs=[pl.BlockSpec((s, LANES), lambda i: (0, i)), pl.BlockSpec((CONV_PAD, LANES), lambda i: (0, i))],
        out_shape=[jax.ShapeDtypeStruct((s, d), F32), jax.ShapeDtypeStruct((CONV_PAD, d), F32)],
        compiler_params=_cparams(("parallel",)),
    )(ypad, dpad, w_dw)


def _cv_fwd(h, w):
    d = h.shape[1]
    (y1,) = _mm("cv_pw1", h, w["w_pw1"], tm=1024, tn=512)

    def glu(y1, b1):
        t = y1 + b1
        return (t[:, :d] * _sigmoid(t[:, d:]),), ()
    (y2,), _ = _rowwise("cv_glu", glu, [y1], [w["b_pw1"]], [(d, F32)], [], 256)
    y3 = _conv_fwd_kernel(jnp.pad(y2, ((CONV_PAD, 0), (0, 0))), w["w_dw"], w["b_dw"])

    def lnsilu(y3, g, b):
        vh, _ = _ln_stats(y3)
        y4 = vh * g + b
        return (y4 * _sigmoid(y4),), ()
    (y5,), _ = _rowwise("cv_lnsilu", lnsilu, [y3], [w["ln_g"], w["ln_b"]], [(d, MXU_DTYPE)], [], 256)
    (y,) = _mm("cv_pw2", y5, w["w_pw2"], epi=lambda acc, b: (acc + b,), vecs=(w["b_pw2"],))
    return y, (y1, y2, y3, y5)


def _cv_bwd(dy, h, w, saved):
    y1, y2, y3, y5 = saved
    d = h.shape[1]
    (dy5,) = _mm("cv_dy5", dy, w["w_pw2"], tb=True)
    (dw_pw2,) = _mm("cv_dwpw2", y5, dy, ta=True, out_dtypes=(MXU_DTYPE,))

    def ln_bwd(dy5, y3, dyb, g, b):
        vh, r = _ln_stats(y3)
        y4 = vh * g + b
        sg = _sigmoid(y4)
        dy4 = dy5 * (sg * (1.0 + y4 * (1.0 - sg)))
        dvh = dy4 * g
        dy3 = r * (dvh - jnp.mean(dvh, axis=-1, keepdims=True) - vh * jnp.mean(dvh * vh, axis=-1, keepdims=True))
        return (dy3,), (_colsum(dy4 * vh), _colsum(dy4), _colsum(dy3), _colsum(dyb.astype(F32)))
    (dy3,), (dlg, dlb, db_dw, db_pw2) = _rowwise("cv_ln_bwd", ln_bwd, [dy5, y3, dy], [w["ln_g"], w["ln_b"]],
                                                 [(d, F32)], [(1, d)] * 4, 256)
    dy2, dw_dw = _conv_bwd_kernel(jnp.pad(y2, ((CONV_PAD, 0), (0, 0))), jnp.pad(dy3, ((0, CONV_PAD), (0, 0))),
                                  w["w_dw"])

    def glu_bwd(y1, dy2, b1):
        t = y1 + b1
        a, sg = t[:, :d], _sigmoid(t[:, d:])
        dy1 = jnp.concatenate([dy2 * sg, dy2 * a * sg * (1.0 - sg)], axis=1)
        return (dy1,), (_colsum(dy1),)
    (dy1,), (db_pw1,) = _rowwise("cv_glu_bwd", glu_bwd, [y1, dy2], [w["b_pw1"]], [(2 * d, MXU_DTYPE)],
                                 [(1, 2 * d)], 256)
    (dw_pw1,) = _mm("cv_dwpw1", h, dy1, ta=True, out_dtypes=(MXU_DTYPE,), tn=dy1.shape[1] // N_DEV, col_slots=N_DEV)
    (dh,) = _mm("cv_dh", dy1, w["w_pw1"], tb=True)
    return dh, dict(w_pw1=dw_pw1, w_pw2=dw_pw2, b_pw1=db_pw1, b_pw2=db_pw2, w_dw=dw_dw[:CONV_WIDTH],
                    b_dw=db_dw, ln_g=dlg, ln_b=dlb)


def _ada_outer(c_t, dmod):
    def fn(c_t, dmod):
        acc = c_t[:, 0:1] * dmod[0:1, :]
        for b in range(1, N_DEV):
            acc = acc + c_t[:, b:b + 1] * dmod[b:b + 1, :]
        return (acc,), ()
    (g,), _ = _rowwise("ada_outer", fn, [c_t], [dmod], [(dmod.shape[1], F32)], [], 256)
    return g


def _adamw(name, parts, w, m, v, tr, layer=0, prev=None):
    npart, rows = parts.shape[0], parts.shape[1]
    cols = w.shape[1]

    def fn(parts, w, m, v):
        g = parts[0].astype(F32)
        for q in range(1, npart):
            g = g + parts[q].astype(F32)
        m_new = ADAM_B1 * m + (1.0 - ADAM_B1) * g
        v_new = ADAM_B2 * v + (1.0 - ADAM_B2) * (g * g)
        m_hat = m_new / (1.0 - ADAM_B1 ** ADAM_STEP)
        v_hat = v_new / (1.0 - ADAM_B2 ** ADAM_STEP)
        delta = -ADAM_LR * (m_hat / (jnp.sqrt(v_hat) + ADAM_EPS) + ADAM_WD * w)
        return (g, delta, m_new, v_new), ()
    tr = min(tr, rows)
    nblk = rows // tr
    n_prev = 0 if prev is None else 4

    def body(p_ref, w_ref, m_ref, v_ref, *rest):
        outs, _ = fn(p_ref[...], w_ref[...], m_ref[...], v_ref[...])
        for o_ref, o in zip(rest[n_prev:], outs):
            o_ref[...] = o

    spec = pl.BlockSpec((tr, cols), lambda i: (layer * nblk + i, 0))
    return pl.pallas_call(
        body, name=name, grid=(nblk,),
        in_specs=[pl.BlockSpec((npart, tr, cols), lambda i: (0, i, 0)), spec, spec, spec]
        + [pl.BlockSpec(memory_space=pl.ANY)] * n_prev,
        out_specs=[spec] * 4, out_shape=[jax.ShapeDtypeStruct(w.shape, F32)] * 4,
        input_output_aliases={4 + t: t for t in range(n_prev)},
        compiler_params=_cparams(("parallel",)),
    )(parts, w, m, v, *(prev or ()))


def _pack(arrays):
    flat = jnp.concatenate([a.reshape(-1).astype(F32) for a in arrays])
    n = flat.shape[0]
    rows = -(-n // (8 * LANES)) * 8
    return jnp.pad(flat, (0, rows * LANES - n)).reshape(rows, LANES)


def _unpack(buf, shapes, lead=()):
    flat = buf.reshape(lead + (-1,))
    out, off = [], 0
    for shp in shapes:
        n = math.prod(shp)
        out.append(flat[..., off:off + n].reshape(lead + tuple(shp)))
        off += n
    return out


ADAM_TILE_ELEMS = 1 << 17


def _row_tile(rows, cols):
    want = max(8, ADAM_TILE_ELEMS // max(cols, LANES))
    if rows <= want:
        return rows
    best = None
    for t in range(8, want + 1, 8):
        if rows % t == 0:
            best = t
    assert best is not None, (rows, cols)
    return best


def _local_step(xs, tgt, mods, norm_mix, norm_mlp, fetch, send):
    depth = len(mods)
    mixer_fwd = (_fox_fwd, _sg_fwd, _cv_fwd)
    mixer_bwd = (_fox_bwd, _sg_bwd, _cv_bwd)
    mw, w1, w2 = [None] * depth, [None] * depth, [None] * depth

    nsub = 2 * depth
    sub = []
    x_in = xs
    y_prev = gate_prev = None
    for k in range(nsub):
        i, is_mlp = k // 2, k % 2
        sh, sc = mods[i][3 * is_mlp], mods[i][3 * is_mlp + 1]
        g = (norm_mlp if is_mlp else norm_mix)[i:i + 1]
        wts, token = fetch(k, xs if k == 0 else y_prev)
        g = g + token
        if is_mlp:
            w1[i], w2[i] = wts
        else:
            mw[i] = wts
        if k == 0:
            h = _first_norm(x_in, g, sc, sh)
        else:
            x_in, h = _res_norm(x_in, y_prev, gate_prev, g, sc, sh)
        if is_mlp:
            y, saved = _mlp_fwd(h, w1[i], w2[i])
        else:
            y, saved = mixer_fwd[i % 3](h, mw[i])
        sub.append((x_in, h, y, saved))
        y_prev, gate_prev = y, mods[i][3 * is_mlp + 2]

    loss_part, dxo, dy, dgate = _final_loss(x_in, y_prev, gate_prev, tgt)

    dmods = [[None] * 6 for _ in range(depth)]
    g_norm = {'norm_mix': [None] * depth, 'norm_mlp': [None] * depth}
    g_mix = [None] * depth
    g_w1, g_w2 = [None] * depth, [None] * depth
    for k in reversed(range(nsub)):
        i, is_mlp = k // 2, k % 2
        x_k, h_k, _, saved = sub[k]
        dmods[i][3 * is_mlp + 2] = dgate
        if is_mlp:
            dh, g_w1[i], g_w2[i] = _mlp_bwd(dy, h_k, w1[i], w2[i], saved)
        else:
            dh, g_mix[i] = mixer_bwd[i % 3](dy, h_k, mw[i], saved)
        sc = mods[i][3 * is_mlp + 1]
        g = (norm_mlp if is_mlp else norm_mix)[i:i + 1]
        g = g + send(k, (g_w1[i], g_w2[i]) if is_mlp else g_mix[i])
        if k > 0:
            ip, mp = (k - 1) // 2, (k - 1) % 2
            dxo, dy, (dsh, dsc, dg, dgate) = _bwd_norm_gate(dxo, dh, x_k, sub[k - 1][2], g, sc, mods[ip][3 * mp + 2])
        else:
            dxo, (dsh, dsc, dg) = _bwd_norm_first(dxo, dh, x_k, g, sc)
        dmods[i][3 * is_mlp], dmods[i][3 * is_mlp + 1] = dsh, dsc
        g_norm['norm_mlp' if is_mlp else 'norm_mix'][i] = dg
    return loss_part, dxo, dmods, g_norm, g_mix, g_w1, g_w2


def kernel(x, c, norm_mix, norm_mlp, w_ada, b_ada, w_mlp_in, w_mlp_out, fox_w_in, fox_b_f, fox_q_norm, fox_k_norm, fox_w_out, sg_w_in, sg_ln_g, sg_ln_b, sg_w_s, sg_b_s, sg_w_out, cv_w_pw1, cv_b_pw1, cv_w_dw, cv_b_dw, cv_ln_g, cv_ln_b, cv_w_pw2, cv_b_pw2, loss_target, m_norm_mix, m_norm_mlp, m_w_ada, m_b_ada, m_w_mlp_in, m_w_mlp_out, m_fox_w_in, m_fox_b_f, m_fox_q_norm, m_fox_k_norm, m_fox_w_out, m_sg_w_in, m_sg_ln_g, m_sg_ln_b, m_sg_w_s, m_sg_b_s, m_sg_w_out, m_cv_w_pw1, m_cv_b_pw1, m_cv_w_dw, m_cv_b_dw, m_cv_ln_g, m_cv_ln_b, m_cv_w_pw2, m_cv_b_pw2, v_norm_mix, v_norm_mlp, v_w_ada, v_b_ada, v_w_mlp_in, v_w_mlp_out, v_fox_w_in, v_fox_b_f, v_fox_q_norm, v_fox_k_norm, v_fox_w_out, v_sg_w_in, v_sg_ln_g, v_sg_ln_b, v_sg_w_s, v_sg_b_s, v_sg_w_out, v_cv_w_pw1, v_cv_b_pw1, v_cv_w_dw, v_cv_b_dw, v_cv_ln_g, v_cv_ln_b, v_cv_w_pw2, v_cv_b_pw2):
    P = dict(zip(_ARGS, (x, c, norm_mix, norm_mlp, w_ada, b_ada, w_mlp_in, w_mlp_out, fox_w_in, fox_b_f, fox_q_norm, fox_k_norm, fox_w_out, sg_w_in, sg_ln_g, sg_ln_b, sg_w_s, sg_b_s, sg_w_out, cv_w_pw1, cv_b_pw1, cv_w_dw, cv_b_dw, cv_ln_g, cv_ln_b, cv_w_pw2, cv_b_pw2, loss_target, m_norm_mix, m_norm_mlp, m_w_ada, m_b_ada, m_w_mlp_in, m_w_mlp_out, m_fox_w_in, m_fox_b_f, m_fox_q_norm, m_fox_k_norm, m_fox_w_out, m_sg_w_in, m_sg_ln_g, m_sg_ln_b, m_sg_w_s, m_sg_b_s, m_sg_w_out, m_cv_w_pw1, m_cv_b_pw1, m_cv_w_dw, m_cv_b_dw, m_cv_ln_g, m_cv_ln_b, m_cv_w_pw2, m_cv_b_pw2, v_norm_mix, v_norm_mlp, v_w_ada, v_b_ada, v_w_mlp_in, v_w_mlp_out, v_fox_w_in, v_fox_b_f, v_fox_q_norm, v_fox_k_norm, v_fox_w_out, v_sg_w_in, v_sg_ln_g, v_sg_ln_b, v_sg_w_s, v_sg_b_s, v_sg_w_out, v_cv_w_pw1, v_cv_b_pw1, v_cv_w_dw, v_cv_b_dw, v_cv_ln_g, v_cv_ln_b, v_cv_w_pw2, v_cv_b_pw2)))
    me = 4 * lax.axis_index("x") + 2 * lax.axis_index("y") + lax.axis_index("c")
    xs = x[0]
    tgt = loss_target[0]
    s_len, d = xs.shape
    depth = norm_mix.shape[0]
    bf = lambda a: a.astype(MXU_DTYPE)

    cv_small = ['cv_b_pw1', 'cv_w_dw', 'cv_b_dw', 'cv_ln_g', 'cv_ln_b', 'cv_b_pw2']
    small_shapes = [c.shape] + [P[n].shape for n in cv_small]
    (small_all,) = _exchange("gather_small", [_pack([c] + [P[n] for n in cv_small])], scatter=False)
    sm = dict(zip(['c'] + cv_small, _unpack(small_all, small_shapes, lead=(N_DEV,))))
    c_all = sm['c'][:, 0, :]
    cat_last = lambda a: jnp.moveaxis(a, 0, -2).reshape(a.shape[1:-1] + (-1,))
    cvf = {n: cat_last(sm[n]) for n in cv_small}

    big = ['w_mlp_in', 'w_mlp_out', 'fox_w_in', 'fox_w_out', 'sg_w_in', 'sg_w_out', 'cv_w_pw1', 'cv_w_pw2']
    col_sharded = {'w_mlp_in', 'fox_w_in', 'sg_w_in', 'cv_w_pw1'}
    mixer_names = (('fox_w_in', 'fox_w_out'), ('sg_w_in', 'sg_w_out'), ('cv_w_pw1', 'cv_w_pw2'))
    nsub = 2 * depth
    groups = [[('w_mlp_in', k // 2), ('w_mlp_out', k // 2)] if k % 2 else
              [(nm, k // 6) for nm in mixer_names[(k // 2) % 3]] for k in range(nsub)]
    gather_handles = [None] * nsub
    scatter_handles = [None] * nsub

    c_act = c_all * _sigmoid(c_all)
    c_pad = bf(jnp.pad(c_act, ((0, 16 - N_DEV), (0, 0))))
    n_ada = w_ada.shape[2]
    (mod_part,) = _mm("ada_mod", c_pad, bf(jnp.transpose(w_ada, (1, 0, 2)).reshape(d, depth * n_ada)),
                      epi=lambda acc, b: (acc + b,),
                      vecs=(lax.dynamic_slice_in_dim(b_ada, me * n_ada, n_ada, axis=1).reshape(1, depth * n_ada),),
                      tn=n_ada)
    (mod_all,) = _exchange("gather_mod", [mod_part], scatter=False)
    mod_me = lax.dynamic_index_in_dim(mod_all, me, axis=1, keepdims=False)
    mod = jnp.transpose(mod_me.reshape(N_DEV, depth, n_ada), (1, 0, 2)).reshape(depth, 6 * d)
    mods = [[mod[i:i + 1, k * d:(k + 1) * d] for k in range(6)] for i in range(depth)]

    def mixer_weights(i, full_weight):
        kind, j = i % 3, i // 3
        if kind == 0:
            w_in = full_weight('fox_w_in')
            n_pad = -(-w_in.shape[1] // (5 * LANES)) * (5 * LANES)
            return dict(w_in=_pad_cols(w_in, n_pad), w_out=full_weight('fox_w_out'),
                        b_f=_pad_cols(fox_b_f[j:j + 1], LANES),
                        qg=jnp.tile(fox_q_norm[j:j + 1], (1, 2)), kg=jnp.tile(fox_k_norm[j:j + 1], (1, 2)))
        if kind == 1:
            return dict(w_in=full_weight('sg_w_in'), w_out=full_weight('sg_w_out'),
                        ln_g=sg_ln_g[j:j + 1], ln_b=sg_ln_b[j:j + 1],
                        w_s=sg_w_s[j].reshape(SG_GROUPS * SG_CHUNK, SG_CHUNK), b_st=_pad_cols(sg_b_s[j].T, LANES))
        return dict(w_pw1=full_weight('cv_w_pw1'), w_pw2=full_weight('cv_w_pw2'),
                    b_pw1=cvf['cv_b_pw1'][j:j + 1], b_pw2=cvf['cv_b_pw2'][j:j + 1],
                    w_dw=jnp.pad(cvf['cv_w_dw'][j], ((0, CONV_PAD - CONV_WIDTH), (0, 0))),
                    b_dw=cvf['cv_b_dw'][j:j + 1], ln_g=cvf['cv_ln_g'][j:j + 1], ln_b=cvf['cv_ln_b'][j:j + 1])

    dep = mod_all
    for k in range(nsub):
        gather_handles[k], dep = _exchange_start(f"gather_start_{k}", [bf(P[nm][j]) for nm, j in groups[k]],
                                                 False, dep)
    first_token = dep

    def fetch(k, dep):
        got = _exchange_wait(f"gather_wait_{k}", gather_handles[k], False, dep)
        by_name = {nm: g for (nm, _), g in zip(groups[k], got)}

        def full_weight(name):
            g = by_name[name]
            if name in col_sharded:
                return jnp.transpose(g, (1, 0, 2)).reshape(g.shape[1], -1)
            return g.reshape(-1, g.shape[2])
        wts = (full_weight('w_mlp_in'), full_weight('w_mlp_out')) if k % 2 else mixer_weights(k // 2, full_weight)
        return wts, (first_token if k == 0 else jnp.zeros((1, 1), F32))

    def to_slots(name, g2d):
        if name in col_sharded:
            r = g2d.shape[0]
            return jnp.transpose(g2d.reshape(r, N_DEV, -1), (1, 0, 2))
        return g2d.reshape(N_DEV, -1, g2d.shape[1])

    def send(k, grads, dep=None):
        key = {'fox_w_in': 'w_in', 'fox_w_out': 'w_out', 'sg_w_in': 'w_in', 'sg_w_out': 'w_out',
               'cv_w_pw1': 'w_pw1', 'cv_w_pw2': 'w_pw2'}
        slots = []
        for nm, _ in groups[k]:
            g = grads[0] if nm == 'w_mlp_in' else grads[1] if nm == 'w_mlp_out' else grads[key[nm]]
            if g.ndim == 2:
                g = to_slots(nm, g[:, :P[nm].shape[-1] * N_DEV] if nm in col_sharded else g)
            slots.append(g)
        scatter_handles[k], token = _exchange_start(f"scatter_start_{k}", slots, True, slots[0] if dep is None else dep)
        return token

    send_later = lambda k, grads: jnp.zeros((1, 1), F32) if k == 0 else send(k, grads)
    loss_part, dxo, dmods, g_norm, g_mix, _, _ = _local_step(xs, tgt, mods, norm_mix, norm_mlp, fetch, send_later)
    loss = lax.psum(loss_part, ("x", "y", "c"))
    grad_x = dxo[None]

    stack = lambda key, kind: jnp.stack([g_mix[i][key].reshape(P[name_of[(kind, key)]].shape[1:])
                                         for i in range(depth) if i % 3 == kind])
    name_of = {(0, 'b_f'): 'fox_b_f', (0, 'qg'): 'fox_q_norm', (0, 'kg'): 'fox_k_norm',
               (1, 'ln_g'): 'sg_ln_g', (1, 'ln_b'): 'sg_ln_b', (1, 'w_s'): 'sg_w_s', (1, 'b_s'): 'sg_b_s'}
    dmod_me = jnp.concatenate([jnp.concatenate(r, axis=1) for r in dmods], axis=0)
    small_g = {'dmod': dmod_me,
               'norm_mix': jnp.concatenate(g_norm['norm_mix'], axis=0),
               'norm_mlp': jnp.concatenate(g_norm['norm_mlp'], axis=0)}
    for (kind, key), nm in name_of.items():
        small_g[nm] = stack(key, kind)
    cv_keys = {'cv_b_pw1': 'b_pw1', 'cv_w_dw': 'w_dw', 'cv_b_dw': 'b_dw', 'cv_ln_g': 'ln_g', 'cv_ln_b': 'ln_b',
               'cv_b_pw2': 'b_pw2'}
    for nm, key in cv_keys.items():
        small_g[nm] = jnp.stack([g_mix[i][key].reshape(cvf[nm].shape[1:]) for i in range(depth) if i % 3 == 2])
    sg_names = list(small_g)
    sg_shapes = [small_g[n].shape for n in sg_names]
    (sg_all,) = _exchange("gather_small_grads", [_pack([small_g[n] for n in sg_names])], scatter=False)
    send(0, g_mix[0], sg_all)

    dmod_all = _unpack(sg_all, sg_shapes, lead=(N_DEV,))[0]
    out = {}

    def finish(name, parts, shard_of=None):
        w, m, v = P[name], P['m_' + name], P['v_' + name]
        cols = w.shape[-1]
        r2 = lambda a: a.reshape(-1, cols)
        rows = r2(w).shape[0]
        res = _adamw("adamw_" + name, parts.reshape(parts.shape[0], rows, cols), r2(w), r2(m), r2(v),
                     _row_tile(rows, cols))
        out[name] = tuple(r.reshape(w.shape) for r in res)

    c_t = c_act.T
    ada_g = []
    for i in range(depth):
        blk = lax.dynamic_slice_in_dim(dmod_all[:, i, :], me * n_ada, n_ada, axis=1)
        ada_g.append(_ada_outer(c_t, blk))
    finish('w_ada', jnp.stack(ada_g)[None])
    finish('b_ada', dmod_all)

    sm_names = [n for n in sg_names if n != 'dmod']
    sm_parts = jnp.stack([_pack([_unpack(sg_all[q], sg_shapes)[sg_names.index(n)] for n in sm_names])
                          for q in range(N_DEV)])

    def local_block(nm, a):
        if nm in cv_keys:
            n_loc = P[nm].shape[-1]
            return lax.dynamic_slice_in_dim(a, me * n_loc, n_loc, axis=a.ndim - 1)
        return a
    full_shapes = [small_g[n].shape for n in sm_names]

    def pack_full(prefix):
        arrs = []
        for nm in sm_names:
            a = P[prefix + nm]
            if nm in cv_keys:
                full = jnp.zeros(small_g[nm].shape, F32)
                a = lax.dynamic_update_slice_in_dim(full, a, me * a.shape[-1], axis=a.ndim - 1)
            arrs.append(a)
        return _pack(arrs)
    res = _adamw("adamw_small", sm_parts, pack_full(''), pack_full('m_'), pack_full('v_'),
                 _row_tile(sm_parts.shape[1], LANES))
    unp = [_unpack(r, full_shapes) for r in res]
    for idx, nm in enumerate(sm_names):
        out[nm] = tuple(local_block(nm, unp[t][idx]) for t in range(4))

    stacked = {}
    last = out['w_ada'][0]
    for k in reversed(range(nsub)):
        recv = _exchange_wait(f"scatter_wait_{k}", scatter_handles[k], True, last)
        for (nm, j), parts in zip(groups[k], recv):
            nl, r, cc = P[nm].shape
            flat = lambda a: a.reshape(nl * r, cc)
            stacked[nm] = _adamw(f"adamw_{nm}_{j}", parts, flat(P[nm]), flat(P['m_' + nm]), flat(P['v_' + nm]),
                                 _row_tile(r, cc), layer=j, prev=stacked.get(nm))
            last = stacked[nm][0]
    for nm in big:
        out[nm] = tuple(a.reshape(P[nm].shape) for a in stacked[nm])

    outs = [loss, grad_x]
    for t in range(4):
        outs += [out[n][t] for n in _WEIGHTS]
    return tuple(outs)
```

```python
import functools
import math

import jax
import jax.numpy as jnp
from jax import lax
from jax.experimental import pallas as pl
from jax.experimental.pallas import tpu as pltpu

F32 = jnp.float32
MXU_DTYPE = jnp.bfloat16
EPS = 1e-6
N_DEV = 8
HEAD_DIM = 64
LANES = 128
CONV_WIDTH = 31
CONV_PAD = 32
SG_CHUNK = 128
SG_BLOCK = 64
SG_GROUPS = 8
SCAN_BLOCK = 256
VMEM_LIMIT = 48 * 1024 * 1024
ATT_BWD_VMEM_LIMIT = 56 * 1024 * 1024
ADAM_LR, ADAM_B1, ADAM_B2, ADAM_EPS, ADAM_WD, ADAM_STEP = 0.001, 0.9, 0.999, 1e-08, 0.01, 10
NEG = -1e30

_WEIGHTS = ['norm_mix', 'norm_mlp', 'w_ada', 'b_ada', 'w_mlp_in', 'w_mlp_out', 'fox_w_in', 'fox_b_f',
            'fox_q_norm', 'fox_k_norm', 'fox_w_out', 'sg_w_in', 'sg_ln_g', 'sg_ln_b', 'sg_w_s', 'sg_b_s',
            'sg_w_out', 'cv_w_pw1', 'cv_b_pw1', 'cv_w_dw', 'cv_b_dw', 'cv_ln_g', 'cv_ln_b', 'cv_w_pw2',
            'cv_b_pw2']
_ARGS = ['x', 'c'] + _WEIGHTS + ['loss_target'] + ['m_' + n for n in _WEIGHTS] + ['v_' + n for n in _WEIGHTS]


def _cparams(sem=None, vmem=VMEM_LIMIT):
    return pltpu.CompilerParams(dimension_semantics=sem, vmem_limit_bytes=vmem)


def _colsum(v):
    return jnp.sum(v, axis=0, keepdims=True)


def _sigmoid(v):
    return 1.0 / (1.0 + jnp.exp(-v))


def _rowwise(name, fn, rows, consts, row_out, red_out, tr):
    n_rows = rows[0].shape[0]
    tr = min(tr, n_rows)
    assert n_rows % tr == 0
    nr, nc, no = len(rows), len(consts), len(row_out)

    def body(*refs):
        ins = [r[...] for r in refs[:nr + nc]]
        outs, reds = fn(*ins)
        out_refs = refs[nr + nc:nr + nc + no]
        red_refs = refs[nr + nc + no:]
        for o_ref, o in zip(out_refs, outs):
            o_ref[...] = o.astype(o_ref.dtype)
        if red_refs:
            @pl.when(pl.program_id(0) == 0)
            def _():
                for r_ref in red_refs:
                    r_ref[...] = jnp.zeros_like(r_ref)
            for r_ref, r in zip(red_refs, reds):
                r_ref[...] += r

    def rspec(a):
        return pl.BlockSpec((tr,) + a.shape[1:], lambda i: (i,) + (0,) * (a.ndim - 1))

    def cspec(shape):
        return pl.BlockSpec(shape, lambda i: (0,) * len(shape))

    out_shape = [jax.ShapeDtypeStruct((n_rows, w), dt) for w, dt in row_out]
    out_shape += [jax.ShapeDtypeStruct(s, F32) for s in red_out]
    out_specs = [pl.BlockSpec((tr, w), lambda i: (i, 0)) for w, _ in row_out] + [cspec(s) for s in red_out]
    res = pl.pallas_call(
        body, name=name, grid=(n_rows // tr,),
        in_specs=[rspec(a) for a in rows] + [cspec(a.shape) for a in consts],
        out_specs=out_specs, out_shape=out_shape,
        compiler_params=_cparams(("arbitrary",)),
    )(*rows, *consts)
    return res[:no], res[no:]


def _mm(name, a, b, *, ta=False, tb=False, out_dtypes=(F32,), epi=None, tiles=(), vecs=(), tm=512, tn=512,
        col_slots=0):
    m_dim, k_dim = (a.shape[1], a.shape[0]) if ta else a.shape
    n_dim = b.shape[0] if tb else b.shape[1]
    assert (b.shape[1] if tb else b.shape[0]) == k_dim
    tm, tn = min(tm, m_dim), min(tn, n_dim)
    assert m_dim % tm == 0 and n_dim % tn == 0, (name, m_dim, n_dim, tm, tn)
    dims = (((0 if ta else 1,), (1 if tb else 0,)), ((), ()))
    nx = len(tiles) + len(vecs)

    def body(a_ref, b_ref, *rest):
        acc = lax.dot_general(a_ref[...], b_ref[...], dims, preferred_element_type=F32)
        outs = epi(acc, *[r[...] for r in rest[:nx]]) if epi is not None else (acc,)
        for o_ref, o in zip(rest[nx:], outs):
            o_ref[...] = o.astype(o_ref.dtype)

    a_spec = pl.BlockSpec((k_dim, tm), lambda i, j: (0, i)) if ta else pl.BlockSpec((tm, k_dim), lambda i, j: (i, 0))
    b_spec = pl.BlockSpec((tn, k_dim), lambda i, j: (j, 0)) if tb else pl.BlockSpec((k_dim, tn), lambda i, j: (0, j))
    t_spec = pl.BlockSpec((tm, tn), lambda i, j: (i, j))
    v_spec = pl.BlockSpec((1, tn), lambda i, j: (0, j))
    if col_slots:
        assert n_dim == col_slots * tn
        o_spec = pl.BlockSpec((None, tm, tn), lambda i, j: (j, i, 0))
        o_shape = (col_slots, m_dim, tn)
    else:
        o_spec, o_shape = t_spec, (m_dim, n_dim)
    res = pl.pallas_call(
        body, name=name, grid=(m_dim // tm, n_dim // tn),
        in_specs=[a_spec, b_spec] + [t_spec] * len(tiles) + [v_spec] * len(vecs),
        out_specs=[o_spec] * len(out_dtypes),
        out_shape=[jax.ShapeDtypeStruct(o_shape, dt) for dt in out_dtypes],
        compiler_params=_cparams(("parallel", "parallel")),
    )(a, b, *tiles, *vecs)
    return res


def _exchange_copies(scatter, in_refs, land_refs, send_sems, recv_sems, local_sems):
    n = len(in_refs)
    x, y, c = lax.axis_index("x"), lax.axis_index("y"), lax.axis_index("c")
    me = 4 * x + 2 * y + c
    local = [pltpu.make_async_copy(in_refs[a].at[me] if scatter else in_refs[a], land_refs[a].at[me],
                                   local_sems.at[a]) for a in range(n)]
    send, arrive = [], []
    for k in range(1, N_DEV):
        px, py, pc = x ^ ((k >> 2) & 1), y ^ ((k >> 1) & 1), c ^ (k & 1)
        peer = 4 * px + 2 * py + pc
        for a in range(n):
            src = in_refs[a].at[peer] if scatter else in_refs[a]
            sems = dict(send_sem=send_sems.at[a * (N_DEV - 1) + k - 1], recv_sem=recv_sems.at[a * (N_DEV - 1) + k - 1],
                        device_id=(px, py, pc), device_id_type=pl.DeviceIdType.MESH)
            send.append(pltpu.make_async_remote_copy(src_ref=src, dst_ref=land_refs[a].at[me], **sems))
            arrive.append(pltpu.make_async_remote_copy(src_ref=src, dst_ref=land_refs[a].at[peer], **sems))
    return local, send, arrive


def _land_shape(a, scatter):
    return ((N_DEV,) + a.shape[1:]) if scatter else ((N_DEV,) + a.shape)


def _exchange(name, arrays, scatter):
    n = len(arrays)

    def body(*refs):
        local, send, arrive = _exchange_copies(scatter, refs[:n], refs[n:2 * n], *refs[2 * n:])
        for cp in local + send:
            cp.start()
        for cp, arr in zip(send, arrive):
            cp.wait_send()
            arr.wait_recv()
        for cp in local:
            cp.wait()

    any_spec = pl.BlockSpec(memory_space=pl.ANY)
    return pl.pallas_call(
        body, name=name,
        in_specs=[any_spec] * n, out_specs=[any_spec] * n,
        out_shape=[jax.ShapeDtypeStruct(_land_shape(a, scatter), a.dtype) for a in arrays],
        scratch_shapes=[pltpu.SemaphoreType.DMA((n * (N_DEV - 1),)),
                        pltpu.SemaphoreType.DMA((n * (N_DEV - 1),)),
                        pltpu.SemaphoreType.DMA((n,))],
        compiler_params=pltpu.CompilerParams(has_side_effects=True),
    )(*arrays)


_HBM = pl.BlockSpec(memory_space=pltpu.HBM)
_SEM = pl.BlockSpec(memory_space=pltpu.SEMAPHORE)
_EFFECT = pltpu.SideEffectType.DATAFLOW_SIDE_EFFECTING


def _exchange_start(name, arrays, scatter, dep):
    n = len(arrays)
    nsem = n * (N_DEV - 1)
    srcs = [pltpu.with_memory_space_constraint(a, pltpu.HBM) for a in arrays]
    lands = [pltpu.with_memory_space_constraint(lax.empty(_land_shape(a, scatter), a.dtype), pltpu.HBM) for a in arrays]

    def body(*refs):
        sems = refs[2 * n + 1:2 * n + 4]
        local, send, _ = _exchange_copies(scatter, refs[:n], refs[n:2 * n], *sems)
        for cp in local + send:
            cp.start()
        token = refs[-1]
        token[...] = jnp.zeros_like(token)

    res = pl.pallas_call(
        body, name=name,
        in_specs=[_HBM] * (2 * n) + [pl.BlockSpec(memory_space=pl.ANY)],
        out_specs=[_SEM] * 3 + [_HBM] * (2 * n) + [pl.BlockSpec(memory_space=pltpu.VMEM)],
        out_shape=[pltpu.SemaphoreType.DMA((nsem,)), pltpu.SemaphoreType.DMA((nsem,)), pltpu.SemaphoreType.DMA((n,))]
        + [pltpu.HBM(a.shape, a.dtype) for a in arrays]
        + [pltpu.HBM(_land_shape(a, scatter), a.dtype) for a in arrays]
        + [jax.ShapeDtypeStruct((8, LANES), F32)],
        input_output_aliases={i: 3 + i for i in range(2 * n)},
        compiler_params=pltpu.CompilerParams(has_side_effects=_EFFECT),
    )(*srcs, *lands, dep)
    return res[:-1], res[-1][0:1, 0:1]


def _exchange_wait(name, handles, scatter, after):
    n = (len(handles) - 3) // 2
    sems, thru = handles[:3], handles[3:]

    def body(*refs):
        local, send, arrive = _exchange_copies(scatter, refs[:n], refs[n:2 * n], *refs[2 * n:2 * n + 3])
        for cp, arr in zip(send, arrive):
            cp.wait_send()
            arr.wait_recv()
        for cp in local:
            cp.wait()

    res = pl.pallas_call(
        body, name=name,
        in_specs=[_HBM] * (2 * n) + [_SEM] * 3 + [pl.BlockSpec(memory_space=pl.ANY)],
        out_specs=[_HBM] * (2 * n),
        out_shape=[pltpu.HBM(t.shape, t.dtype) for t in thru],
        input_output_aliases={i: i for i in range(2 * n)},
        compiler_params=pltpu.CompilerParams(has_side_effects=_EFFECT),
    )(*thru, *sems, after)
    return res[n:]


def _norm_mod(x, g, sc, sh):
    r = lax.rsqrt(jnp.mean(x * x, axis=-1, keepdims=True) + EPS)
    return (x * r * g) * (1.0 + sc) + sh


def _first_norm(x, g, sc, sh):
    (h,), _ = _rowwise("first_norm", lambda x, g, sc, sh: ((_norm_mod(x, g, sc, sh),), ()),
                       [x], [g, sc, sh], [(x.shape[1], MXU_DTYPE)], [], 256)
    return h


def _res_norm(x, y, gate, g, sc, sh):
    def fn(x, y, gate, g, sc, sh):
        xn = x + gate * y
        return (xn, _norm_mod(xn, g, sc, sh)), ()
    (xn, h), _ = _rowwise("res_norm", fn, [x, y], [gate, g, sc, sh],
                          [(x.shape[1], F32), (x.shape[1], MXU_DTYPE)], [], 256)
    return xn, h


def _final_loss(x, y, gate, target):
    d = x.shape[1]

    def fn(x, y, target, gate):
        err = (x + gate * y) - target
        part = jnp.sum(jnp.sum(err * err, axis=-1, keepdims=True), axis=0, keepdims=True) * (0.5 / d)
        dx = err * (1.0 / d)
        return (dx, dx * gate), (jnp.broadcast_to(part, (1, LANES)), _colsum(dx * y))
    (dx, dy), (loss, dgate) = _rowwise("final_loss", fn, [x, y, target], [gate],
                                       [(d, F32), (d, MXU_DTYPE)], [(1, LANES), (1, d)], 256)
    return loss[0, 0], dx, dy, dgate


def _norm_bwd_core(dxo, dh, x, g, sc):
    r = lax.rsqrt(jnp.mean(x * x, axis=-1, keepdims=True) + EPS)
    xn = x * r
    dsh = _colsum(dh)
    dsc = _colsum(dh * (xn * g))
    dyy = dh * (1.0 + sc)
    dg = _colsum(dyy * xn)
    dxn = dyy * g
    dxi = dxo + r * (dxn - xn * jnp.mean(dxn * xn, axis=-1, keepdims=True))
    return dxi, dsh, dsc, dg


def _bwd_norm_gate(dxo, dh, x, y_prev, g, sc, gate_prev):
    d = x.shape[1]

    def fn(dxo, dh, x, y_prev, g, sc, gate_prev):
        dxi, dsh, dsc, dg = _norm_bwd_core(dxo, dh, x, g, sc)
        return (dxi, dxi * gate_prev), (dsh, dsc, dg, _colsum(dxi * y_prev))
    (dxi, dy), reds = _rowwise("bwd_norm_gate", fn, [dxo, dh, x, y_prev], [g, sc, gate_prev],
                               [(d, F32), (d, MXU_DTYPE)], [(1, d)] * 4, 256)
    return dxi, dy, reds


def _bwd_norm_first(dxo, dh, x, g, sc):
    d = x.shape[1]

    def fn(dxo, dh, x, g, sc):
        dxi, dsh, dsc, dg = _norm_bwd_core(dxo, dh, x, g, sc)
        return (dxi,), (dsh, dsc, dg)
    (dxi,), reds = _rowwise("bwd_norm_first", fn, [dxo, dh, x], [g, sc], [(d, F32)], [(1, d)] * 3, 256)
    return dxi, reds


def _mlp_fwd(h, w1, w2):
    def epi(acc):
        r = jnp.maximum(acc, 0.0)
        return acc, r * r
    a, z = _mm("mlp_in", h, w1, out_dtypes=(MXU_DTYPE, MXU_DTYPE), epi=epi, tm=1024, tn=512)
    (out,) = _mm("mlp_out", z, w2, tm=512, tn=512)
    return out, (a, z)


def _mlp_bwd(dy, h, w1, w2, saved):
    a, z = saved

    def epi(acc, a):
        return (acc * (2.0 * jnp.maximum(a.astype(F32), 0.0)),)
    (da,) = _mm("mlp_dz", dy, w2, tb=True, out_dtypes=(MXU_DTYPE,), epi=epi, tiles=(a,), tm=1024, tn=512)
    (dw2,) = _mm("mlp_dw2", z, dy, ta=True, out_dtypes=(MXU_DTYPE,))
    (dw1,) = _mm("mlp_dw1", h, da, ta=True, out_dtypes=(MXU_DTYPE,), tn=da.shape[1] // N_DEV, col_slots=N_DEV)
    (dh,) = _mm("mlp_dh", da, w1, tb=True)
    return dh, dw1, dw2


def _split3(v):
    hi = v.astype(jnp.bfloat16)
    r1 = v - hi.astype(F32)
    mid = r1.astype(jnp.bfloat16)
    lo = (r1 - mid.astype(F32)).astype(jnp.bfloat16)
    return hi, mid, lo


def _tri_matmul(tri, v):
    hi, mid, lo = _split3(v)
    dot = functools.partial(jnp.dot, preferred_element_type=F32)
    return dot(tri, hi) + dot(tri, mid) + dot(tri, lo)


def _log_sigmoid(v):
    return jnp.minimum(v, 0.0) - jnp.log(1.0 + jnp.exp(-jnp.abs(v)))


def _gate_fwd(proj, b_pad, col_block):
    s = proj.shape[0]
    tb = min(SCAN_BLOCK, s)
    nblk = s // tb

    def body(f_ref, b_ref, o_ref):
        row = lax.broadcasted_iota(jnp.int32, (tb, tb), 0)
        col = lax.broadcasted_iota(jnp.int32, (tb, tb), 1)
        tri = (col <= row).astype(jnp.bfloat16)

        def step(i, carry):
            rows = pl.ds(pl.multiple_of(i * tb, tb), tb)
            lf = _log_sigmoid(f_ref[rows, :] + b_ref[...])
            f = _tri_matmul(tri, lf) + carry
            o_ref[rows, :] = f
            return f[tb - 1:tb, :]
        lax.fori_loop(0, nblk, step, jnp.zeros((1, LANES), F32))

    return pl.pallas_call(
        body, name="gate_fwd", grid=(1,),
        in_specs=[pl.BlockSpec((s, LANES), lambda i: (0, col_block)), pl.BlockSpec((1, LANES), lambda i: (0, 0))],
        out_specs=pl.BlockSpec((s, LANES), lambda i: (0, 0)),
        out_shape=jax.ShapeDtypeStruct((s, LANES), F32),
        compiler_params=_cparams(("arbitrary",)),
    )(proj, b_pad)


def _gate_bwd(proj, b_pad, d_f, col_block):
    s = proj.shape[0]
    tb = min(SCAN_BLOCK, s)
    nblk = s // tb

    def body(f_ref, b_ref, d_ref, o_ref, db_ref):
        row = lax.broadcasted_iota(jnp.int32, (tb, tb), 0)
        col = lax.broadcasted_iota(jnp.int32, (tb, tb), 1)
        tri = (col >= row).astype(jnp.bfloat16)

        def step(j, carry):
            acc, db = carry
            i = nblk - 1 - j
            rows = pl.ds(pl.multiple_of(i * tb, tb), tb)
            dlf = _tri_matmul(tri, d_ref[rows, :]) + acc
            dpre = dlf * _sigmoid(-(f_ref[rows, :] + b_ref[...]))
            o_ref[rows, :] = dpre.astype(o_ref.dtype)
            return dlf[0:1, :], db + _colsum(dpre)
        _, db = lax.fori_loop(0, nblk, step, (jnp.zeros((1, LANES), F32), jnp.zeros((1, LANES), F32)))
        db_ref[...] = db

    return pl.pallas_call(
        body, name="gate_bwd", grid=(1,),
        in_specs=[pl.BlockSpec((s, LANES), lambda i: (0, col_block)), pl.BlockSpec((1, LANES), lambda i: (0, 0)),
                  pl.BlockSpec((s, LANES), lambda i: (0, 0))],
        out_specs=[pl.BlockSpec((s, LANES), lambda i: (0, 0)), pl.BlockSpec((1, LANES), lambda i: (0, 0))],
        out_shape=[jax.ShapeDtypeStruct((s, LANES), MXU_DTYPE), jax.ShapeDtypeStruct((1, LANES), F32)],
        compiler_params=_cparams(("arbitrary",)),
    )(proj, b_pad, d_f)


def _head_masks():
    lane = lax.broadcasted_iota(jnp.int32, (1, LANES), 1)
    return lane < HEAD_DIM


def _pair_norm(v, g, first):
    v2 = v * v
    ss0 = jnp.sum(jnp.where(first, v2, 0.0), axis=-1, keepdims=True)
    ss1 = jnp.sum(jnp.where(first, 0.0, v2), axis=-1, keepdims=True)
    r = jnp.where(first, lax.rsqrt(ss0 * (1.0 / HEAD_DIM) + EPS), lax.rsqrt(ss1 * (1.0 / HEAD_DIM) + EPS))
    vn = v * r
    return vn * g, vn, r


_NT = (((1,), (1,)), ((), ()))
_TN = (((0,), (0,)), ((), ()))

ATT_TQ = 512
ATT_FWD_TQ = 1024
ATT_TK = 256
AUG_F, AUG_ONE = 0, 3


def _own_lanes(hd):
    lane = lax.broadcasted_iota(jnp.int32, (1, LANES), 1)
    return (lane < HEAD_DIM) if hd == 0 else (lane >= HEAD_DIM)


def _aug_lanes(hd, f_other):
    lane = lax.broadcasted_iota(jnp.int32, (1, LANES), 1) - (HEAD_DIM if hd == 0 else 0)
    hi, mid, lo = [t.astype(F32) for t in _split3(f_other)]
    zero = jnp.zeros_like(f_other)
    f_terms = jnp.where(lane == 0, hi, jnp.where(lane == 1, mid, jnp.where(lane == 2, lo, zero)))
    f_shift = jnp.where(lane == 3, hi, jnp.where(lane == 4, mid, jnp.where(lane == 5, lo, zero)))
    ones_lo = jnp.where(lane < 3, 1.0, 0.0) * jnp.where(lane >= 0, 1.0, 0.0)
    ones_hi = jnp.where(lane < 6, 1.0, 0.0) * jnp.where(lane >= 3, 1.0, 0.0)
    return f_terms + ones_hi, ones_lo - f_shift


def _attn_operands(q_raw, k_raw, f_rep, qg, kg, scale):
    first = _head_masks()
    qn, _, _ = _pair_norm(q_raw, qg, first)
    kn, _, _ = _pair_norm(k_raw, kg, first)
    f_other = pltpu.roll(f_rep, HEAD_DIM, 1)
    out = []
    for hd in range(2):
        own = _own_lanes(hd)
        q_x, k_x = _aug_lanes(hd, f_other)
        out.append((jnp.where(own, qn * scale, q_x), jnp.where(own, kn, k_x)))
    return out


def _set_cols(a, b, c0):
    return b if c0 == 0 else jnp.concatenate([a[:, :c0], b], axis=1)


def _add_cols(a, b, c0):
    return _set_cols(a, a[:, c0:] + b, c0)


def _causal_t(tk, tq, off):
    r = lax.broadcasted_iota(jnp.int32, (tk, tq), 0)
    c = lax.broadcasted_iota(jnp.int32, (tk, tq), 1)
    return (r - c) <= off


def _big(shape, index_map):
    return pl.BlockSpec(shape, index_map, pipeline_mode=pl.Buffered(1))


def _attn_fwd_t(proj, f_rep, qg, kg, d_model):
    s = proj.shape[0]
    pairs = d_model // LANES
    tq, tk = min(ATT_FWD_TQ, s), min(ATT_TK, s)
    assert tq % tk == 0 and s % tq == 0
    nq = s // tq
    n_diag = tq // tk
    scale = HEAD_DIM ** -0.5
    ch = tk

    def body(q_ref, k_ref, v_ref, frep_ref, qg_ref, kg_ref, o_ref, lse_ref, qt_s, k_s, vt_s):
        for ci in range(s // ch):
            rows = pl.ds(ci * ch, ch)
            ops = _attn_operands(q_ref[rows, :], k_ref[rows, :], frep_ref[rows, :], qg_ref[...], kg_ref[...], scale)
            vv = v_ref[rows, :]
            for hd in range(2):
                own = _own_lanes(hd)
                lane = lax.broadcasted_iota(jnp.int32, (1, LANES), 1)
                one_lane = lane == (HEAD_DIM if hd == 0 else 0)
                qt_s[hd, :, rows] = ops[hd][0].T.astype(qt_s.dtype)
                k_s[hd, rows, :] = ops[hd][1].astype(k_s.dtype)
                vt_s[hd, :, rows] = jnp.where(own, vv, jnp.where(one_lane, 1.0, 0.0)).T.astype(vt_s.dtype)

        def q_block(qi, _):
            q0 = pl.multiple_of(qi * tq, tq)
            qcols = pl.ds(q0, tq)
            nfull = q0 // tk
            qts = [qt_s[hd, :, qcols] for hd in range(2)]

            def krows(kj):
                return pl.ds(pl.multiple_of(kj * tk, tk), tk)

            def scores(hd, kj):
                return jnp.dot(k_s[hd, krows(kj), :], qts[hd], preferred_element_type=F32)

            def kv_step(kj, carry):
                new = []
                for hd in range(2):
                    m, acc, p_prev = carry[hd]
                    st = scores(hd, kj)
                    pv = jnp.dot(vt_s[hd, :, krows(jnp.maximum(kj - 1, 0))], p_prev, preferred_element_type=F32)
                    m_new = jnp.maximum(m, jnp.max(st, axis=0, keepdims=True))
                    p = jnp.exp(st - m_new).astype(vt_s.dtype)
                    new.append((m_new, jnp.exp(m - m_new) * (acc + pv), p))
                return tuple(new)

            def diag_step(t, carry, last):
                c0, kj = t * tk, nfull + t
                new = []
                for hd in range(2):
                    m, acc, p_prev, c_prev = carry[hd]
                    pv = jnp.dot(vt_s[hd, :, krows(jnp.maximum(kj - 1, 0))], p_prev, preferred_element_type=F32)
                    acc = _add_cols(acc, pv, c_prev)
                    st = jnp.dot(k_s[hd, krows(kj), :], qts[hd][:, c0:], preferred_element_type=F32)
                    st = jnp.where(_causal_t(tk, tq - c0, 0), st, NEG)
                    m_new = jnp.maximum(m[:, c0:], jnp.max(st, axis=0, keepdims=True))
                    p = jnp.exp(st - m_new).astype(vt_s.dtype)
                    acc = _set_cols(acc, jnp.exp(m[:, c0:] - m_new) * acc[:, c0:], c0)
                    m = _set_cols(m, m_new, c0)
                    if last:
                        pv = jnp.dot(vt_s[hd, :, krows(kj)], p, preferred_element_type=F32)
                        new.append((m, _add_cols(acc, pv, c0)))
                    else:
                        new.append((m, acc, p, c0))
                return tuple(new)

            init = tuple((jnp.full((1, tq), NEG, F32), jnp.zeros((LANES, tq), F32),
                          jnp.zeros((tk, tq), vt_s.dtype)) for hd in range(2))
            carry = lax.fori_loop(0, nfull, kv_step, init)
            carry = tuple(cr + (0,) for cr in carry)
            for t in range(n_diag):
                carry = diag_step(t, carry, t == n_diag - 1)
            o_parts, lse_parts = [], []
            for hd, (m, acc) in enumerate(carry):
                e0 = HEAD_DIM if hd == 0 else 0
                l = acc[e0:e0 + 1, :]
                o_parts.append((acc / l).T)
                lse_parts.append(m + jnp.log(l))
            o_ref[pl.ds(q0, tq), :] = jnp.where(_head_masks(), o_parts[0], o_parts[1]).astype(o_ref.dtype)
            lse_ref[0, :, qcols] = jnp.concatenate(lse_parts, axis=0)
            return 0
        lax.fori_loop(0, nq, q_block, 0)

    blk = lambda off: _big((s, LANES), lambda h: (0, off + h))
    vec = pl.BlockSpec((1, LANES), lambda h: (0, 0))
    return pl.pallas_call(
        body, name="attn_fwd", grid=(pairs,),
        in_specs=[blk(0), blk(pairs), blk(2 * pairs), blk(0), vec, vec],
        out_specs=[pl.BlockSpec((s, LANES), lambda h: (0, h)), pl.BlockSpec((1, 2, s), lambda h: (h, 0, 0))],
        out_shape=[jax.ShapeDtypeStruct((s, d_model), MXU_DTYPE), jax.ShapeDtypeStruct((pairs, 2, s), F32)],
        scratch_shapes=[pltpu.VMEM((2, LANES, s), MXU_DTYPE), pltpu.VMEM((2, s, LANES), MXU_DTYPE),
                        pltpu.VMEM((2, LANES, s), MXU_DTYPE)],
        compiler_params=_cparams(("arbitrary",)),
    )(proj, proj, proj, f_rep, qg, kg)


def _attn_bwd_t(proj, do, o, lse, f_rep, qg, kg, d_model):
    s = proj.shape[0]
    pairs = d_model // LANES
    tq, tk = min(ATT_TQ, s), min(ATT_TK, s)
    assert tq % tk == 0 and s % tq == 0
    nq = s // tq
    n_diag = tq // tk
    scale = HEAD_DIM ** -0.5
    ch = tk

    def norm_bwd(raw, g, dn, first):
        _, xn, r = _pair_norm(raw, g, first)
        dxn = dn * g
        t = dxn * xn
        mu0 = jnp.sum(jnp.where(first, t, 0.0), axis=-1, keepdims=True)
        mu1 = jnp.sum(jnp.where(first, 0.0, t), axis=-1, keepdims=True)
        mu = jnp.where(first, mu0, mu1) * (1.0 / HEAD_DIM)
        return r * (dxn - xn * mu), _colsum(dn * xn)

    def body(q_ref, k_ref, v_ref, do_ref, o_ref, lse_ref, frep_ref, qg_ref, kg_ref,
             dq_ref, dk_ref, dv_ref, df_ref, dqg_ref, dkg_ref,
             q_s, qt_s, k_s, kt_s, v_s, do_s, dot_s, dl_s, dk_s, dv_s):
        first = _head_masks()
        hp = pl.program_id(0)
        lane = lax.broadcasted_iota(jnp.int32, (1, LANES), 1)
        for ci in range(s // ch):
            rows = pl.ds(ci * ch, ch)
            ops = _attn_operands(q_ref[rows, :], k_ref[rows, :], frep_ref[rows, :], qg_ref[...], kg_ref[...], scale)
            v_s[rows, :] = v_ref[rows, :].astype(v_s.dtype)
            dov = do_ref[rows, :].astype(F32)
            ot = o_ref[rows, :].astype(F32).T
            for hd in range(2):
                own = _own_lanes(hd)
                q_s[hd, rows, :] = ops[hd][0].astype(q_s.dtype)
                qt_s[hd, :, rows] = ops[hd][0].T.astype(qt_s.dtype)
                k_s[hd, rows, :] = ops[hd][1].astype(k_s.dtype)
                kt_s[hd, :, rows] = ops[hd][1].T.astype(kt_s.dtype)
                doh = jnp.where(own, dov, 0.0)
                do_s[hd, rows, :] = doh.astype(do_s.dtype)
                doht = doh.T
                dot_s[hd, :, rows] = doht.astype(dot_s.dtype)
                dl_s[hd:hd + 1, rows] = jnp.sum(doht * ot, axis=0, keepdims=True)
        dk_s[...] = jnp.zeros_like(dk_s)
        dv_s[...] = jnp.zeros_like(dv_s)

        @pl.when(hp == 0)
        def _():
            df_ref[...] = jnp.zeros_like(df_ref)

        def q_block(qi, dqg):
            q0 = pl.multiple_of(qi * tq, tq)
            qcols = pl.ds(q0, tq)
            qrows = pl.ds(q0, tq)
            nfull = q0 // tk
            qts = [qt_s[hd, :, qcols] for hd in range(2)]
            dots = [dot_s[hd, :, qcols] for hd in range(2)]
            qns = [q_s[hd, qrows, :] for hd in range(2)]
            dons = [do_s[hd, qrows, :] for hd in range(2)]
            lse_r = [lse_ref[0, hd:hd + 1, qcols] for hd in range(2)]
            dl_r = [dl_s[hd:hd + 1, qcols] for hd in range(2)]
            bdt = qt_s.dtype

            def krows(kj):
                return pl.ds(pl.multiple_of(kj * tk, tk), tk)

            def scores(hd, kj):
                return (jnp.dot(k_s[hd, krows(kj), :], qts[hd], preferred_element_type=F32),
                        jnp.dot(v_s[krows(kj), :], dots[hd], preferred_element_type=F32))

            def products(hd, rows, ds, p, dqt, c0=0):
                dk_s[hd, rows, :] += jnp.dot(ds, qns[hd][c0:, :], preferred_element_type=F32)
                dv_s[rows, :] += jnp.dot(p, dons[hd][c0:, :], preferred_element_type=F32)
                return _add_cols(dqt, jnp.dot(kt_s[hd, :, rows], ds, preferred_element_type=F32), c0)

            def kv_step(kj, carry):
                new = []
                for hd in range(2):
                    dqt, ds_prev, p_prev = carry[hd]
                    st, dp = scores(hd, kj)
                    dqt = products(hd, krows(jnp.maximum(kj - 1, 0)), ds_prev, p_prev, dqt)
                    p = jnp.exp(st - lse_r[hd])
                    new.append((dqt, (p * (dp - dl_r[hd])).astype(bdt), p.astype(bdt)))
                return tuple(new)

            def diag_step(t, carry, last):
                c0, kj = t * tk, nfull + t
                new = []
                for hd in range(2):
                    dqt, ds_prev, p_prev, c_prev = carry[hd]
                    st = jnp.dot(k_s[hd, krows(kj), :], qts[hd][:, c0:], preferred_element_type=F32)
                    dp = jnp.dot(v_s[krows(kj), :], dots[hd][:, c0:], preferred_element_type=F32)
                    dqt = products(hd, krows(jnp.maximum(kj - 1, 0)), ds_prev, p_prev, dqt, c_prev)
                    st = jnp.where(_causal_t(tk, tq - c0, 0), st, NEG)
                    p = jnp.exp(st - lse_r[hd][:, c0:])
                    ds = (p * (dp - dl_r[hd][:, c0:])).astype(bdt)
                    if last:
                        new.append(products(hd, krows(kj), ds, p.astype(bdt), dqt, c0))
                    else:
                        new.append((dqt, ds, p.astype(bdt), c0))
                return tuple(new)

            init = tuple((jnp.zeros((LANES, tq), F32), jnp.zeros((tk, tq), bdt), jnp.zeros((tk, tq), bdt))
                         for hd in range(2))
            carry = lax.fori_loop(0, nfull, kv_step, init)
            carry = tuple(cr + (0,) for cr in carry)
            for t in range(n_diag):
                carry = diag_step(t, carry, t == n_diag - 1)
            dq_parts = [dqt.T for dqt in carry]
            rs0 = dq_parts[0][:, HEAD_DIM + AUG_F:HEAD_DIM + AUG_F + 1]
            rs1 = dq_parts[1][:, AUG_F:AUG_F + 1]
            df_ref[qrows, :] += jnp.where(lane == 2 * hp, rs0, 0.0) + jnp.where(lane == 2 * hp + 1, rs1, 0.0)
            dqn = jnp.where(first, dq_parts[0], dq_parts[1]) * scale
            dq_raw, dg = norm_bwd(q_ref[qrows, :], qg_ref[...], dqn, first)
            dq_ref[qrows, :] = dq_raw.astype(dq_ref.dtype)
            return dqg + dg
        dqg_ref[0] = lax.fori_loop(0, nq, q_block, jnp.zeros((1, LANES), F32))

        dkg = jnp.zeros((1, LANES), F32)
        for ci in range(s // ch):
            rows = pl.ds(ci * ch, ch)
            dk0, dk1 = dk_s[0, rows, :], dk_s[1, rows, :]
            cs0 = dk0[:, HEAD_DIM + AUG_ONE:HEAD_DIM + AUG_ONE + 1]
            cs1 = dk1[:, AUG_ONE:AUG_ONE + 1]
            df_ref[rows, :] -= jnp.where(lane == 2 * hp, cs0, 0.0) + jnp.where(lane == 2 * hp + 1, cs1, 0.0)
            dk_raw, dg = norm_bwd(k_ref[rows, :], kg_ref[...], jnp.where(first, dk0, dk1), first)
            dk_ref[rows, :] = dk_raw.astype(dk_ref.dtype)
            dkg = dkg + dg
            dv_ref[rows, :] = dv_s[rows, :].astype(dv_ref.dtype)
        dkg_ref[0] = dkg

    blk = lambda off: _big((s, LANES), lambda h: (0, off + h))
    outb = pl.BlockSpec((s, LANES), lambda h: (0, h))
    vec = pl.BlockSpec((1, LANES), lambda h: (0, 0))
    gout = pl.BlockSpec((1, 1, LANES), lambda h: (h, 0, 0))
    act = jax.ShapeDtypeStruct((s, d_model), MXU_DTYPE)
    gsh = jax.ShapeDtypeStruct((pairs, 1, LANES), F32)
    pair_rows = pltpu.VMEM((2, s, LANES), MXU_DTYPE)
    pair_cols = pltpu.VMEM((2, LANES, s), MXU_DTYPE)
    return pl.pallas_call(
        body, name="attn_bwd", grid=(pairs,),
        in_specs=[blk(0), blk(pairs), blk(2 * pairs), blk(0), blk(0),
                  pl.BlockSpec((1, 2, s), lambda h: (h, 0, 0)), blk(0), vec, vec],
        out_specs=[outb, outb, outb, pl.BlockSpec((s, LANES), lambda h: (0, 0)), gout, gout],
        out_shape=[act, act, act, jax.ShapeDtypeStruct((s, LANES), F32), gsh, gsh],
        scratch_shapes=[pair_rows, pair_cols, pair_rows, pair_cols, pltpu.VMEM((s, LANES), MXU_DTYPE),
                        pair_rows, pair_cols, pltpu.VMEM((8, s), F32),
                        pltpu.VMEM((2, s, LANES), F32), pltpu.VMEM((s, LANES), F32)],
        compiler_params=_cparams(("arbitrary",), ATT_BWD_VMEM_LIMIT),
    )(proj, proj, proj, do, o, lse, f_rep, qg, kg)


def _pad_cols(a, n):
    return jnp.pad(a, ((0, 0), (0, n - a.shape[1])))


def _fox_fwd(h, w):
    d = h.shape[1]
    pairs = d // LANES
    (proj,) = _mm("fox_in", h, w["w_in"], tm=1024, tn=640)
    f_cum = _gate_fwd(proj, w["b_f"], 3 * pairs)
    f16 = f_cum[:, :d // HEAD_DIM]
    f_rep = jnp.repeat(f16, HEAD_DIM, axis=1)
    o, lse = _attn_fwd_t(proj, f_rep, w["qg"], w["kg"], d)
    (y,) = _mm("fox_out", o, w["w_out"])
    return y, (proj, f_rep, o, lse)


def _fox_bwd(dy, h, w, saved):
    proj, f_rep, o, lse = saved
    s, d = h.shape
    pairs = d // LANES
    (do,) = _mm("fox_do", dy, w["w_out"], tb=True, out_dtypes=(MXU_DTYPE,))
    (dw_out,) = _mm("fox_dwout", o, dy, ta=True, out_dtypes=(MXU_DTYPE,))
    dq, dk, dv, d_f, dqg, dkg = _attn_bwd_t(proj, do, o, lse, f_rep, w["qg"], w["kg"], d)
    dfpre, db_f = _gate_bwd(proj, w["b_f"], d_f, 3 * pairs)
    dproj = jnp.concatenate([dq, dk, dv, dfpre], axis=1)
    (dw_in,) = _mm("fox_dwin", h, dproj, ta=True, out_dtypes=(MXU_DTYPE,), tn=640)
    (dh,) = _mm("fox_dh", dproj, w["w_in"], tb=True)
    fold = lambda g: jnp.sum(g, axis=(0, 1)).reshape(2, HEAD_DIM).sum(axis=0)
    return dh, dict(w_in=dw_in, w_out=dw_out, b_f=db_f[0, :d // HEAD_DIM], qg=fold(dqg), kg=fold(dkg))


_GELU_C = math.sqrt(2.0 / math.pi)


def _gelu(v):
    return 0.5 * v * (1.0 + jnp.tanh(_GELU_C * (v + 0.044715 * v * v * v)))


def _gelu_grad(v):
    t = jnp.tanh(_GELU_C * (v + 0.044715 * v * v * v))
    return 0.5 * (1.0 + t) + 0.5 * v * (1.0 - t * t) * (_GELU_C * (1.0 + 3.0 * 0.044715 * v * v))


def _ln_stats(v):
    mu = jnp.mean(v, axis=-1, keepdims=True)
    vc = v - mu
    r = lax.rsqrt(jnp.mean(vc * vc, axis=-1, keepdims=True) + EPS)
    return vc * r, r


def _sg_mask():
    t = lax.broadcasted_iota(jnp.int32, (SG_CHUNK, SG_CHUNK), 0) // SG_BLOCK
    sidx = lax.broadcasted_iota(jnp.int32, (SG_CHUNK, SG_CHUNK), 1) // SG_BLOCK
    return sidx <= t


def _sgu_fwd(uv_pre, ln_g, ln_b, w_s, b_st):
    s, w2 = uv_pre.shape
    wd = w2 // 2
    tr = min(256, s)

    def fn(uv_pre, ln_g, ln_b, w_s, b_st):
        uv = _gelu(uv_pre)
        u = uv[:, :wd]
        vh, _ = _ln_stats(uv[:, wd:])
        vl = (vh * ln_g + ln_b).astype(MXU_DTYPE)
        mask = _sg_mask()
        cols = []
        for g in range(SG_GROUPS):
            wg = jnp.where(mask, w_s[g * SG_CHUNK:(g + 1) * SG_CHUNK, :], 0.0).astype(MXU_DTYPE)
            parts = []
            for ci in range(tr // SG_CHUNK):
                vt = vl[ci * SG_CHUNK:(ci + 1) * SG_CHUNK, g * SG_CHUNK:(g + 1) * SG_CHUNK]
                parts.append(jnp.dot(wg, vt, preferred_element_type=F32) + b_st[:, g:g + 1])
            cols.append(jnp.concatenate(parts, axis=0) if len(parts) > 1 else parts[0])
        vout = jnp.concatenate(cols, axis=1)
        return (u * vout,), ()
    (m,), _ = _rowwise("sgu_fwd", fn, [uv_pre], [ln_g, ln_b, w_s, b_st], [(wd, MXU_DTYPE)], [], tr)
    return m


def _sgu_bwd(uv_pre, dm, ln_g, ln_b, w_s, b_st):
    s, w2 = uv_pre.shape
    wd = w2 // 2
    tr = min(256, s)

    def fn(uv_pre, dm, ln_g, ln_b, w_s, b_st):
        uv = _gelu(uv_pre)
        u = uv[:, :wd]
        vh, r = _ln_stats(uv[:, wd:])
        vl = (vh * ln_g + ln_b).astype(MXU_DTYPE)
        mask = _sg_mask()
        lane = lax.broadcasted_iota(jnp.int32, (1, LANES), 1)
        cols, dcols, dws, dbs = [], [], [], jnp.zeros((SG_CHUNK, LANES), F32)
        for g in range(SG_GROUPS):
            wg = jnp.where(mask, w_s[g * SG_CHUNK:(g + 1) * SG_CHUNK, :], 0.0).astype(MXU_DTYPE)
            parts, dparts = [], []
            dwg = jnp.zeros((SG_CHUNK, SG_CHUNK), F32)
            dbg = jnp.zeros((SG_CHUNK, 1), F32)
            for ci in range(tr // SG_CHUNK):
                rs = slice(ci * SG_CHUNK, (ci + 1) * SG_CHUNK)
                cs = slice(g * SG_CHUNK, (g + 1) * SG_CHUNK)
                vt = vl[rs, cs]
                parts.append(jnp.dot(wg, vt, preferred_element_type=F32) + b_st[:, g:g + 1])
                dvo = dm[rs, cs] * u[rs, cs]
                dvob = dvo.astype(MXU_DTYPE)
                dparts.append(lax.dot_general(wg, dvob, _TN, preferred_element_type=F32))
                dwg = dwg + lax.dot_general(dvob, vt, _NT, preferred_element_type=F32)
                dbg = dbg + jnp.sum(dvo, axis=-1, keepdims=True)
            cols.append(jnp.concatenate(parts, axis=0) if len(parts) > 1 else parts[0])
            dcols.append(jnp.concatenate(dparts, axis=0) if len(dparts) > 1 else dparts[0])
            dws.append(jnp.where(mask, dwg, 0.0))
            dbs = dbs + jnp.where(lane == g, dbg, 0.0)
        vout = jnp.concatenate(cols, axis=1)
        dvl = jnp.concatenate(dcols, axis=1)
        du = dm * vout
        dlg = _colsum(dvl * vh)
        dlb = _colsum(dvl)
        dvh = dvl * ln_g
        dv = r * (dvh - jnp.mean(dvh, axis=-1, keepdims=True) - vh * jnp.mean(dvh * vh, axis=-1, keepdims=True))
        dpre = jnp.concatenate([du, dv], axis=1) * _gelu_grad(uv_pre)
        return (dpre,), (dlg, dlb, jnp.concatenate(dws, axis=0), dbs)
    (dpre,), reds = _rowwise("sgu_bwd", fn, [uv_pre, dm], [ln_g, ln_b, w_s, b_st], [(w2, MXU_DTYPE)],
                             [(1, wd), (1, wd), (SG_GROUPS * SG_CHUNK, SG_CHUNK), (SG_CHUNK, LANES)], tr)
    return dpre, reds


def _sg_fwd(h, w):
    (uv_pre,) = _mm("sg_in", h, w["w_in"], tm=1024, tn=512)
    m = _sgu_fwd(uv_pre, w["ln_g"], w["ln_b"], w["w_s"], w["b_st"])
    (y,) = _mm("sg_out", m, w["w_out"])
    return y, (uv_pre, m)


def _sg_bwd(dy, h, w, saved):
    uv_pre, m = saved
    (dm,) = _mm("sg_dm", dy, w["w_out"], tb=True)
    (dw_out,) = _mm("sg_dwout", m, dy, ta=True, out_dtypes=(MXU_DTYPE,))
    dpre, (dlg, dlb, dws, dbs) = _sgu_bwd(uv_pre, dm, w["ln_g"], w["ln_b"], w["w_s"], w["b_st"])
    (dw_in,) = _mm("sg_dwin", h, dpre, ta=True, out_dtypes=(MXU_DTYPE,), tn=dpre.shape[1] // N_DEV, col_slots=N_DEV)
    (dh,) = _mm("sg_dh", dpre, w["w_in"], tb=True)
    return dh, dict(w_in=dw_in, w_out=dw_out, ln_g=dlg, ln_b=dlb, w_s=dws, b_s=dbs[:, :SG_GROUPS].T)


def _conv_fwd_kernel(ypad, w_dw, b_dw):
    s = ypad.shape[0] - CONV_PAD
    d = ypad.shape[1]
    tt = min(256, s)
    ext = tt + CONV_PAD

    def body(y_ref, w_ref, b_ref, o_ref):
        def chunk(ci, _):
            base = pl.multiple_of(ci * tt, tt)
            e = y_ref[pl.ds(base, ext), :]
            acc = jnp.zeros((tt, LANES), F32) + b_ref[...]
            for j in range(CONV_WIDTH):
                sh = pltpu.roll(e, ext - (CONV_PAD - CONV_WIDTH + 1 + j), 0)[:tt, :]
                acc = acc + w_ref[j:j + 1, :] * sh
            o_ref[pl.ds(base, tt), :] = acc
            return 0
        lax.fori_loop(0, s // tt, chunk, 0)

    return pl.pallas_call(
        body, name="conv_fwd", grid=(d // LANES,),
        in_specs=[pl.BlockSpec((s + CONV_PAD, LANES), lambda i: (0, i)),
                  pl.BlockSpec((CONV_PAD, LANES), lambda i: (0, i)), pl.BlockSpec((1, LANES), lambda i: (0, i))],
        out_specs=pl.BlockSpec((s, LANES), lambda i: (0, i)),
        out_shape=jax.ShapeDtypeStruct((s, d), F32),
        compiler_params=_cparams(("parallel",)),
    )(ypad, w_dw, b_dw)


def _conv_bwd_kernel(ypad, dpad, w_dw):
    s = ypad.shape[0] - CONV_PAD
    d = ypad.shape[1]
    tt = min(256, s)
    ext = tt + CONV_PAD

    def body(y_ref, d_ref, w_ref, o_ref, dw_ref):
        dw_ref[...] = jnp.zeros_like(dw_ref)

        def chunk(ci, _):
            base = pl.multiple_of(ci * tt, tt)
            ye = y_ref[pl.ds(base, ext), :]
            de = d_ref[pl.ds(base, ext), :]
            dcur = de[:tt, :]
            acc = jnp.zeros((tt, LANES), F32)
            for j in range(CONV_WIDTH):
                back = CONV_WIDTH - 1 - j
                dsh = dcur if back == 0 else pltpu.roll(de, ext - back, 0)[:tt, :]
                acc = acc + w_ref[j:j + 1, :] * dsh
                ysh = pltpu.roll(ye, ext - (CONV_PAD - CONV_WIDTH + 1 + j), 0)[:tt, :]
                dw_ref[j:j + 1, :] += _colsum(dcur * ysh)
            o_ref[pl.ds(base, tt), :] = acc
            return 0
        lax.fori_loop(0, s // tt, chunk, 0)

    return pl.pallas_call(
        body, name="conv_bwd", grid=(d // LANES,),
        in_specs=[pl.BlockSpec((s + CONV_PAD, LANES), lambda i: (0, i)),
                  pl.BlockSpec((s + CONV_PAD, LANES), lambda i: (0, i)),
                  pl.BlockSpec((CONV_PAD, LANES), lambda i: (0, i))],
        out_specs=[pl.BlockSpec((s, LANES), lambda i: (0, i)), pl.BlockSpec((CONV_PAD, LANES), lambda i: (0, i))],
        out_shape=[jax.ShapeDtypeStruct((s, d), F32), jax.ShapeDtypeStruct((CONV_PAD, d), F32)],
        compiler_params=_cparams(("parallel",)),
    )(ypad, dpad, w_dw)


def _cv_fwd(h, w):
    d = h.shape[1]
    (y1,) = _mm("cv_pw1", h, w["w_pw1"], tm=1024, tn=512)

    def glu(y1, b1):
        t = y1 + b1
        return (t[:, :d] * _sigmoid(t[:, d:]),), ()
    (y2,), _ = _rowwise("cv_glu", glu, [y1], [w["b_pw1"]], [(d, F32)], [], 256)
    y3 = _conv_fwd_kernel(jnp.pad(y2, ((CONV_PAD, 0), (0, 0))), w["w_dw"], w["b_dw"])

    def lnsilu(y3, g, b):
        vh, _ = _ln_stats(y3)
        y4 = vh * g + b
        return (y4 * _sigmoid(y4),), ()
    (y5,), _ = _rowwise("cv_lnsilu", lnsilu, [y3], [w["ln_g"], w["ln_b"]], [(d, MXU_DTYPE)], [], 256)
    (y,) = _mm("cv_pw2", y5, w["w_pw2"], epi=lambda acc, b: (acc + b,), vecs=(w["b_pw2"],))
    return y, (y1, y2, y3, y5)


def _cv_bwd(dy, h, w, saved):
    y1, y2, y3, y5 = saved
    d = h.shape[1]
    (dy5,) = _mm("cv_dy5", dy, w["w_pw2"], tb=True)
    (dw_pw2,) = _mm("cv_dwpw2", y5, dy, ta=True, out_dtypes=(MXU_DTYPE,))

    def ln_bwd(dy5, y3, dyb, g, b):
        vh, r = _ln_stats(y3)
        y4 = vh * g + b
        sg = _sigmoid(y4)
        dy4 = dy5 * (sg * (1.0 + y4 * (1.0 - sg)))
        dvh = dy4 * g
        dy3 = r * (dvh - jnp.mean(dvh, axis=-1, keepdims=True) - vh * jnp.mean(dvh * vh, axis=-1, keepdims=True))
        return (dy3,), (_colsum(dy4 * vh), _colsum(dy4), _colsum(dy3), _colsum(dyb.astype(F32)))
    (dy3,), (dlg, dlb, db_dw, db_pw2) = _rowwise("cv_ln_bwd", ln_bwd, [dy5, y3, dy], [w["ln_g"], w["ln_b"]],
                                                 [(d, F32)], [(1, d)] * 4, 256)
    dy2, dw_dw = _conv_bwd_kernel(jnp.pad(y2, ((CONV_PAD, 0), (0, 0))), jnp.pad(dy3, ((0, CONV_PAD), (0, 0))),
                                  w["w_dw"])

    def glu_bwd(y1, dy2, b1):
        t = y1 + b1
        a, sg = t[:, :d], _sigmoid(t[:, d:])
        dy1 = jnp.concatenate([dy2 * sg, dy2 * a * sg * (1.0 - sg)], axis=1)
        return (dy1,), (_colsum(dy1),)
    (dy1,), (db_pw1,) = _rowwise("cv_glu_bwd", glu_bwd, [y1, dy2], [w["b_pw1"]], [(2 * d, MXU_DTYPE)],
                                 [(1, 2 * d)], 256)
    (dw_pw1,) = _mm("cv_dwpw1", h, dy1, ta=True, out_dtypes=(MXU_DTYPE,), tn=dy1.shape[1] // N_DEV, col_slots=N_DEV)
    (dh,) = _mm("cv_dh", dy1, w["w_pw1"], tb=True)
    return dh, dict(w_pw1=dw_pw1, w_pw2=dw_pw2, b_pw1=db_pw1, b_pw2=db_pw2, w_dw=dw_dw[:CONV_WIDTH],
                    b_dw=db_dw, ln_g=dlg, ln_b=dlb)


def _ada_outer(c_t, dmod):
    def fn(c_t, dmod):
        acc = c_t[:, 0:1] * dmod[0:1, :]
        for b in range(1, N_DEV):
            acc = acc + c_t[:, b:b + 1] * dmod[b:b + 1, :]
        return (acc,), ()
    (g,), _ = _rowwise("ada_outer", fn, [c_t], [dmod], [(dmod.shape[1], F32)], [], 256)
    return g


def _adamw(name, parts, w, m, v, tr, layer=0, prev=None):
    npart, rows = parts.shape[0], parts.shape[1]
    cols = w.shape[1]

    def fn(parts, w, m, v):
        g = parts[0].astype(F32)
        for q in range(1, npart):
            g = g + parts[q].astype(F32)
        m_new = ADAM_B1 * m + (1.0 - ADAM_B1) * g
        v_new = ADAM_B2 * v + (1.0 - ADAM_B2) * (g * g)
        m_hat = m_new / (1.0 - ADAM_B1 ** ADAM_STEP)
        v_hat = v_new / (1.0 - ADAM_B2 ** ADAM_STEP)
        delta = -ADAM_LR * (m_hat / (jnp.sqrt(v_hat) + ADAM_EPS) + ADAM_WD * w)
        return (g, delta, m_new, v_new), ()
    tr = min(tr, rows)
    nblk = rows // tr
    n_prev = 0 if prev is None else 4

    def body(p_ref, w_ref, m_ref, v_ref, *rest):
        outs, _ = fn(p_ref[...], w_ref[...], m_ref[...], v_ref[...])
        for o_ref, o in zip(rest[n_prev:], outs):
            o_ref[...] = o

    spec = pl.BlockSpec((tr, cols), lambda i: (layer * nblk + i, 0))
    return pl.pallas_call(
        body, name=name, grid=(nblk,),
        in_specs=[pl.BlockSpec((npart, tr, cols), lambda i: (0, i, 0)), spec, spec, spec]
        + [pl.BlockSpec(memory_space=pl.ANY)] * n_prev,
        out_specs=[spec] * 4, out_shape=[jax.ShapeDtypeStruct(w.shape, F32)] * 4,
        input_output_aliases={4 + t: t for t in range(n_prev)},
        compiler_params=_cparams(("parallel",)),
    )(parts, w, m, v, *(prev or ()))


def _pack(arrays):
    flat = jnp.concatenate([a.reshape(-1).astype(F32) for a in arrays])
    n = flat.shape[0]
    rows = -(-n // (8 * LANES)) * 8
    return jnp.pad(flat, (0, rows * LANES - n)).reshape(rows, LANES)


def _unpack(buf, shapes, lead=()):
    flat = buf.reshape(lead + (-1,))
    out, off = [], 0
    for shp in shapes:
        n = math.prod(shp)
        out.append(flat[..., off:off + n].reshape(lead + tuple(shp)))
        off += n
    return out


ADAM_TILE_ELEMS = 1 << 17


def _row_tile(rows, cols):
    want = max(8, ADAM_TILE_ELEMS // max(cols, LANES))
    if rows <= want:
        return rows
    best = None
    for t in range(8, want + 1, 8):
        if rows % t == 0:
            best = t
    assert best is not None, (rows, cols)
    return best


def _local_step(xs, tgt, mods, norm_mix, norm_mlp, fetch, send):
    depth = len(mods)
    mixer_fwd = (_fox_fwd, _sg_fwd, _cv_fwd)
    mixer_bwd = (_fox_bwd, _sg_bwd, _cv_bwd)
    mw, w1, w2 = [None] * depth, [None] * depth, [None] * depth

    nsub = 2 * depth
    sub = []
    x_in = xs
    y_prev = gate_prev = None
    for k in range(nsub):
        i, is_mlp = k // 2, k % 2
        sh, sc = mods[i][3 * is_mlp], mods[i][3 * is_mlp + 1]
        g = (norm_mlp if is_mlp else norm_mix)[i:i + 1]
        wts, token = fetch(k, xs if k == 0 else y_prev)
        g = g + token
        if is_mlp:
            w1[i], w2[i] = wts
        else:
            mw[i] = wts
        if k == 0:
            h = _first_norm(x_in, g, sc, sh)
        else:
            x_in, h = _res_norm(x_in, y_prev, gate_prev, g, sc, sh)
        if is_mlp:
            y, saved = _mlp_fwd(h, w1[i], w2[i])
        else:
            y, saved = mixer_fwd[i % 3](h, mw[i])
        sub.append((x_in, h, y, saved))
        y_prev, gate_prev = y, mods[i][3 * is_mlp + 2]

    loss_part, dxo, dy, dgate = _final_loss(x_in, y_prev, gate_prev, tgt)

    dmods = [[None] * 6 for _ in range(depth)]
    g_norm = {'norm_mix': [None] * depth, 'norm_mlp': [None] * depth}
    g_mix = [None] * depth
    g_w1, g_w2 = [None] * depth, [None] * depth
    for k in reversed(range(nsub)):
        i, is_mlp = k // 2, k % 2
        x_k, h_k, _, saved = sub[k]
        dmods[i][3 * is_mlp + 2] = dgate
        if is_mlp:
            dh, g_w1[i], g_w2[i] = _mlp_bwd(dy, h_k, w1[i], w2[i], saved)
        else:
            dh, g_mix[i] = mixer_bwd[i % 3](dy, h_k, mw[i], saved)
        sc = mods[i][3 * is_mlp + 1]
        g = (norm_mlp if is_mlp else norm_mix)[i:i + 1]
        g = g + send(k, (g_w1[i], g_w2[i]) if is_mlp else g_mix[i])
        if k > 0:
            ip, mp = (k - 1) // 2, (k - 1) % 2
            dxo, dy, (dsh, dsc, dg, dgate) = _bwd_norm_gate(dxo, dh, x_k, sub[k - 1][2], g, sc, mods[ip][3 * mp + 2])
        else:
            dxo, (dsh, dsc, dg) = _bwd_norm_first(dxo, dh, x_k, g, sc)
        dmods[i][3 * is_mlp], dmods[i][3 * is_mlp + 1] = dsh, dsc
        g_norm['norm_mlp' if is_mlp else 'norm_mix'][i] = dg
    return loss_part, dxo, dmods, g_norm, g_mix, g_w1, g_w2


def kernel(x, c, norm_mix, norm_mlp, w_ada, b_ada, w_mlp_in, w_mlp_out, fox_w_in, fox_b_f, fox_q_norm, fox_k_norm, fox_w_out, sg_w_in, sg_ln_g, sg_ln_b, sg_w_s, sg_b_s, sg_w_out, cv_w_pw1, cv_b_pw1, cv_w_dw, cv_b_dw, cv_ln_g, cv_ln_b, cv_w_pw2, cv_b_pw2, loss_target, m_norm_mix, m_norm_mlp, m_w_ada, m_b_ada, m_w_mlp_in, m_w_mlp_out, m_fox_w_in, m_fox_b_f, m_fox_q_norm, m_fox_k_norm, m_fox_w_out, m_sg_w_in, m_sg_ln_g, m_sg_ln_b, m_sg_w_s, m_sg_b_s, m_sg_w_out, m_cv_w_pw1, m_cv_b_pw1, m_cv_w_dw, m_cv_b_dw, m_cv_ln_g, m_cv_ln_b, m_cv_w_pw2, m_cv_b_pw2, v_norm_mix, v_norm_mlp, v_w_ada, v_b_ada, v_w_mlp_in, v_w_mlp_out, v_fox_w_in, v_fox_b_f, v_fox_q_norm, v_fox_k_norm, v_fox_w_out, v_sg_w_in, v_sg_ln_g, v_sg_ln_b, v_sg_w_s, v_sg_b_s, v_sg_w_out, v_cv_w_pw1, v_cv_b_pw1, v_cv_w_dw, v_cv_b_dw, v_cv_ln_g, v_cv_ln_b, v_cv_w_pw2, v_cv_b_pw2):
    P = dict(zip(_ARGS, (x, c, norm_mix, norm_mlp, w_ada, b_ada, w_mlp_in, w_mlp_out, fox_w_in, fox_b_f, fox_q_norm, fox_k_norm, fox_w_out, sg_w_in, sg_ln_g, sg_ln_b, sg_w_s, sg_b_s, sg_w_out, cv_w_pw1, cv_b_pw1, cv_w_dw, cv_b_dw, cv_ln_g, cv_ln_b, cv_w_pw2, cv_b_pw2, loss_target, m_norm_mix, m_norm_mlp, m_w_ada, m_b_ada, m_w_mlp_in, m_w_mlp_out, m_fox_w_in, m_fox_b_f, m_fox_q_norm, m_fox_k_norm, m_fox_w_out, m_sg_w_in, m_sg_ln_g, m_sg_ln_b, m_sg_w_s, m_sg_b_s, m_sg_w_out, m_cv_w_pw1, m_cv_b_pw1, m_cv_w_dw, m_cv_b_dw, m_cv_ln_g, m_cv_ln_b, m_cv_w_pw2, m_cv_b_pw2, v_norm_mix, v_norm_mlp, v_w_ada, v_b_ada, v_w_mlp_in, v_w_mlp_out, v_fox_w_in, v_fox_b_f, v_fox_q_norm, v_fox_k_norm, v_fox_w_out, v_sg_w_in, v_sg_ln_g, v_sg_ln_b, v_sg_w_s, v_sg_b_s, v_sg_w_out, v_cv_w_pw1, v_cv_b_pw1, v_cv_w_dw, v_cv_b_dw, v_cv_ln_g, v_cv_ln_b, v_cv_w_pw2, v_cv_b_pw2)))
    me = 4 * lax.axis_index("x") + 2 * lax.axis_index("y") + lax.axis_index("c")
    xs = x[0]
    tgt = loss_target[0]
    s_len, d = xs.shape
    depth = norm_mix.shape[0]
    bf = lambda a: a.astype(MXU_DTYPE)

    cv_small = ['cv_b_pw1', 'cv_w_dw', 'cv_b_dw', 'cv_ln_g', 'cv_ln_b', 'cv_b_pw2']
    small_shapes = [c.shape] + [P[n].shape for n in cv_small]
    (small_all,) = _exchange("gather_small", [_pack([c] + [P[n] for n in cv_small])], scatter=False)
    sm = dict(zip(['c'] + cv_small, _unpack(small_all, small_shapes, lead=(N_DEV,))))
    c_all = sm['c'][:, 0, :]
    cat_last = lambda a: jnp.moveaxis(a, 0, -2).reshape(a.shape[1:-1] + (-1,))
    cvf = {n: cat_last(sm[n]) for n in cv_small}

    big = ['w_mlp_in', 'w_mlp_out', 'fox_w_in', 'fox_w_out', 'sg_w_in', 'sg_w_out', 'cv_w_pw1', 'cv_w_pw2']
    col_sharded = {'w_mlp_in', 'fox_w_in', 'sg_w_in', 'cv_w_pw1'}
    mixer_names = (('fox_w_in', 'fox_w_out'), ('sg_w_in', 'sg_w_out'), ('cv_w_pw1', 'cv_w_pw2'))
    nsub = 2 * depth
    groups = [[('w_mlp_in', k // 2), ('w_mlp_out', k // 2)] if k % 2 else
              [(nm, k // 6) for nm in mixer_names[(k // 2) % 3]] for k in range(nsub)]
    gather_handles = [None] * nsub
    scatter_handles = [None] * nsub

    c_act = c_all * _sigmoid(c_all)
    c_pad = bf(jnp.pad(c_act, ((0, 16 - N_DEV), (0, 0))))
    n_ada = w_ada.shape[2]
    (mod_part,) = _mm("ada_mod", c_pad, bf(jnp.transpose(w_ada, (1, 0, 2)).reshape(d, depth * n_ada)),
                      epi=lambda acc, b: (acc + b,),
                      vecs=(lax.dynamic_slice_in_dim(b_ada, me * n_ada, n_ada, axis=1).reshape(1, depth * n_ada),),
                      tn=n_ada)
    (mod_all,) = _exchange("gather_mod", [mod_part], scatter=False)
    mod_me = lax.dynamic_index_in_dim(mod_all, me, axis=1, keepdims=False)
    mod = jnp.transpose(mod_me.reshape(N_DEV, depth, n_ada), (1, 0, 2)).reshape(depth, 6 * d)
    mods = [[mod[i:i + 1, k * d:(k + 1) * d] for k in range(6)] for i in range(depth)]

    def mixer_weights(i, full_weight):
        kind, j = i % 3, i // 3
        if kind == 0:
            w_in = full_weight('fox_w_in')
            n_pad = -(-w_in.shape[1] // (5 * LANES)) * (5 * LANES)
            return dict(w_in=_pad_cols(w_in, n_pad), w_out=full_weight('fox_w_out'),
                        b_f=_pad_cols(fox_b_f[j:j + 1], LANES),
                        qg=jnp.tile(fox_q_norm[j:j + 1], (1, 2)), kg=jnp.tile(fox_k_norm[j:j + 1], (1, 2)))
        if kind == 1:
            return dict(w_in=full_weight('sg_w_in'), w_out=full_weight('sg_w_out'),
                        ln_g=sg_ln_g[j:j + 1], ln_b=sg_ln_b[j:j + 1],
                        w_s=sg_w_s[j].reshape(SG_GROUPS * SG_CHUNK, SG_CHUNK), b_st=_pad_cols(sg_b_s[j].T, LANES))
        return dict(w_pw1=full_weight('cv_w_pw1'), w_pw2=full_weight('cv_w_pw2'),
                    b_pw1=cvf['cv_b_pw1'][j:j + 1], b_pw2=cvf['cv_b_pw2'][j:j + 1],
                    w_dw=jnp.pad(cvf['cv_w_dw'][j], ((0, CONV_PAD - CONV_WIDTH), (0, 0))),
                    b_dw=cvf['cv_b_dw'][j:j + 1], ln_g=cvf['cv_ln_g'][j:j + 1], ln_b=cvf['cv_ln_b'][j:j + 1])

    dep = mod_all
    for k in range(nsub):
        gather_handles[k], dep = _exchange_start(f"gather_start_{k}", [bf(P[nm][j]) for nm, j in groups[k]],
                                                 False, dep)
    first_token = dep

    def fetch(k, dep):
        got = _exchange_wait(f"gather_wait_{k}", gather_handles[k], False, dep)
        by_name = {nm: g for (nm, _), g in zip(groups[k], got)}

        def full_weight(name):
            g = by_name[name]
            if name in col_sharded:
                return jnp.transpose(g, (1, 0, 2)).reshape(g.shape[1], -1)
            return g.reshape(-1, g.shape[2])
        wts = (full_weight('w_mlp_in'), full_weight('w_mlp_out')) if k % 2 else mixer_weights(k // 2, full_weight)
        return wts, (first_token if k == 0 else jnp.zeros((1, 1), F32))

    def to_slots(name, g2d):
        if name in col_sharded:
            r = g2d.shape[0]
            return jnp.transpose(g2d.reshape(r, N_DEV, -1), (1, 0, 2))
        return g2d.reshape(N_DEV, -1, g2d.shape[1])

    def send(k, grads, dep=None):
        key = {'fox_w_in': 'w_in', 'fox_w_out': 'w_out', 'sg_w_in': 'w_in', 'sg_w_out': 'w_out',
               'cv_w_pw1': 'w_pw1', 'cv_w_pw2': 'w_pw2'}
        slots = []
        for nm, _ in groups[k]:
            g = grads[0] if nm == 'w_mlp_in' else grads[1] if nm == 'w_mlp_out' else grads[key[nm]]
            if g.ndim == 2:
                g = to_slots(nm, g[:, :P[nm].shape[-1] * N_DEV] if nm in col_sharded else g)
            slots.append(g)
        scatter_handles[k], token = _exchange_start(f"scatter_start_{k}", slots, True, slots[0] if dep is None else dep)
        return token

    send_later = lambda k, grads: jnp.zeros((1, 1), F32) if k == 0 else send(k, grads)
    loss_part, dxo, dmods, g_norm, g_mix, _, _ = _local_step(xs, tgt, mods, norm_mix, norm_mlp, fetch, send_later)
    loss = lax.psum(loss_part, ("x", "y", "c"))
    grad_x = dxo[None]

    stack = lambda key, kind: jnp.stack([g_mix[i][key].reshape(P[name_of[(kind, key)]].shape[1:])
                                         for i in range(depth) if i % 3 == kind])
    name_of = {(0, 'b_f'): 'fox_b_f', (0, 'qg'): 'fox_q_norm', (0, 'kg'): 'fox_k_norm',
               (1, 'ln_g'): 'sg_ln_g', (1, 'ln_b'): 'sg_ln_b', (1, 'w_s'): 'sg_w_s', (1, 'b_s'): 'sg_b_s'}
    dmod_me = jnp.concatenate([jnp.concatenate(r, axis=1) for r in dmods], axis=0)
    small_g = {'dmod': dmod_me,
               'norm_mix': jnp.concatenate(g_norm['norm_mix'], axis=0),
               'norm_mlp': jnp.concatenate(g_norm['norm_mlp'], axis=0)}
    for (kind, key), nm in name_of.items():
        small_g[nm] = stack(key, kind)
    cv_keys = {'cv_b_pw1': 'b_pw1', 'cv_w_dw': 'w_dw', 'cv_b_dw': 'b_dw', 'cv_ln_g': 'ln_g', 'cv_ln_b': 'ln_b',
               'cv_b_pw2': 'b_pw2'}
    for nm, key in cv_keys.items():
        small_g[nm] = jnp.stack([g_mix[i][key].reshape(cvf[nm].shape[1:]) for i in range(depth) if i % 3 == 2])
    sm_names = [n for n in small_g if n != 'dmod']
    dmod_all, sm_parts = _exchange("gather_small_grads", [dmod_me, _pack([small_g[n] for n in sm_names])],
                                   scatter=False)
    send(0, g_mix[0], sm_parts)

    out = {}

    def finish(name, parts, shard_of=None):
        w, m, v = P[name], P['m_' + name], P['v_' + name]
        cols = w.shape[-1]
        r2 = lambda a: a.reshape(-1, cols)
        rows = r2(w).shape[0]
        res = _adamw("adamw_" + name, parts.reshape(parts.shape[0], rows, cols), r2(w), r2(m), r2(v),
                     _row_tile(rows, cols))
        out[name] = tuple(r.reshape(w.shape) for r in res)

    c_t = c_act.T
    ada_g = []
    for i in range(depth):
        blk = lax.dynamic_slice_in_dim(dmod_all[:, i, :], me * n_ada, n_ada, axis=1)
        ada_g.append(_ada_outer(c_t, blk))
    finish('w_ada', jnp.stack(ada_g)[None])
    finish('b_ada', dmod_all)


    def local_block(nm, a):
        if nm in cv_keys:
            n_loc = P[nm].shape[-1]
            return lax.dynamic_slice_in_dim(a, me * n_loc, n_loc, axis=a.ndim - 1)
        return a
    full_shapes = [small_g[n].shape for n in sm_names]

    def pack_full(prefix):
        arrs = []
        for nm in sm_names:
            a = P[prefix + nm]
            if nm in cv_keys:
                full = jnp.zeros(small_g[nm].shape, F32)
                a = lax.dynamic_update_slice_in_dim(full, a, me * a.shape[-1], axis=a.ndim - 1)
            arrs.append(a)
        return _pack(arrs)
    res = _adamw("adamw_small", sm_parts, pack_full(''), pack_full('m_'), pack_full('v_'),
                 _row_tile(sm_parts.shape[1], LANES))
    unp = [_unpack(r, full_shapes) for r in res]
    for idx, nm in enumerate(sm_names):
        out[nm] = tuple(local_block(nm, unp[t][idx]) for t in range(4))

    stacked = {}
    last = out['w_ada'][0]
    for k in reversed(range(nsub)):
        recv = _exchange_wait(f"scatter_wait_{k}", scatter_handles[k], True, last)
        for (nm, j), parts in zip(groups[k], recv):
            nl, r, cc = P[nm].shape
            flat = lambda a: a.reshape(nl * r, cc)
            stacked[nm] = _adamw(f"adamw_{nm}_{j}", parts, flat(P[nm]), flat(P['m_' + nm]), flat(P['v_' + nm]),
                                 _row_tile(r, cc), layer=j, prev=stacked.get(nm))
            last = stacked[nm][0]
    for nm in big:
        out[nm] = tuple(a.reshape(P[nm].shape) for a in stacked[nm])

    outs = [loss, grad_x]
    for t in range(4):
        outs += [out[n][t] for n in _WEIGHTS]
    return tuple(outs)
```

```python
import functools
import math

import jax
import jax.numpy as jnp
from jax import lax
from jax.experimental import pallas as pl
from jax.experimental.pallas import tpu as pltpu

F32 = jnp.float32
MXU_DTYPE = jnp.bfloat16
EPS = 1e-6
N_DEV = 8
HEAD_DIM = 64
LANES = 128
CONV_WIDTH = 31
CONV_PAD = 32
SG_CHUNK = 128
SG_BLOCK = 64
SG_GROUPS = 8
SCAN_BLOCK = 256
VMEM_LIMIT = 48 * 1024 * 1024
ATT_BWD_VMEM_LIMIT = 56 * 1024 * 1024
ADAM_LR, ADAM_B1, ADAM_B2, ADAM_EPS, ADAM_WD, ADAM_STEP = 0.001, 0.9, 0.999, 1e-08, 0.01, 10
NEG = -1e30

_WEIGHTS = ['norm_mix', 'norm_mlp', 'w_ada', 'b_ada', 'w_mlp_in', 'w_mlp_out', 'fox_w_in', 'fox_b_f',
            'fox_q_norm', 'fox_k_norm', 'fox_w_out', 'sg_w_in', 'sg_ln_g', 'sg_ln_b', 'sg_w_s', 'sg_b_s',
            'sg_w_out', 'cv_w_pw1', 'cv_b_pw1', 'cv_w_dw', 'cv_b_dw', 'cv_ln_g', 'cv_ln_b', 'cv_w_pw2',
            'cv_b_pw2']
_ARGS = ['x', 'c'] + _WEIGHTS + ['loss_target'] + ['m_' + n for n in _WEIGHTS] + ['v_' + n for n in _WEIGHTS]


def _cparams(sem=None, vmem=VMEM_LIMIT):
    return pltpu.CompilerParams(dimension_semantics=sem, vmem_limit_bytes=vmem)


def _colsum(v):
    return jnp.sum(v, axis=0, keepdims=True)


def _sigmoid(v):
    return 1.0 / (1.0 + jnp.exp(-v))


def _rowwise(name, fn, rows, consts, row_out, red_out, tr):
    n_rows = rows[0].shape[0]
    tr = min(tr, n_rows)
    assert n_rows % tr == 0
    nr, nc, no = len(rows), len(consts), len(row_out)

    def body(*refs):
        ins = [r[...] for r in refs[:nr + nc]]
        outs, reds = fn(*ins)
        out_refs = refs[nr + nc:nr + nc + no]
        red_refs = refs[nr + nc + no:]
        for o_ref, o in zip(out_refs, outs):
            o_ref[...] = o.astype(o_ref.dtype)
        if red_refs:
            @pl.when(pl.program_id(0) == 0)
            def _():
                for r_ref in red_refs:
                    r_ref[...] = jnp.zeros_like(r_ref)
            for r_ref, r in zip(red_refs, reds):
                r_ref[...] += r

    def rspec(a):
        return pl.BlockSpec((tr,) + a.shape[1:], lambda i: (i,) + (0,) * (a.ndim - 1))

    def cspec(shape):
        return pl.BlockSpec(shape, lambda i: (0,) * len(shape))

    out_shape = [jax.ShapeDtypeStruct((n_rows, w), dt) for w, dt in row_out]
    out_shape += [jax.ShapeDtypeStruct(s, F32) for s in red_out]
    out_specs = [pl.BlockSpec((tr, w), lambda i: (i, 0)) for w, _ in row_out] + [cspec(s) for s in red_out]
    res = pl.pallas_call(
        body, name=name, grid=(n_rows // tr,),
        in_specs=[rspec(a) for a in rows] + [cspec(a.shape) for a in consts],
        out_specs=out_specs, out_shape=out_shape,
        compiler_params=_cparams(("arbitrary",)),
    )(*rows, *consts)
    return res[:no], res[no:]


def _mm(name, a, b, *, ta=False, tb=False, out_dtypes=(F32,), epi=None, tiles=(), vecs=(), tm=512, tn=512,
        col_slots=0):
    m_dim, k_dim = (a.shape[1], a.shape[0]) if ta else a.shape
    n_dim = b.shape[0] if tb else b.shape[1]
    assert (b.shape[1] if tb else b.shape[0]) == k_dim
    tm, tn = min(tm, m_dim), min(tn, n_dim)
    assert m_dim % tm == 0 and n_dim % tn == 0, (name, m_dim, n_dim, tm, tn)
    dims = (((0 if ta else 1,), (1 if tb else 0,)), ((), ()))
    nx = len(tiles) + len(vecs)

    def body(a_ref, b_ref, *rest):
        acc = lax.dot_general(a_ref[...], b_ref[...], dims, preferred_element_type=F32)
        outs = epi(acc, *[r[...] for r in rest[:nx]]) if epi is not None else (acc,)
        for o_ref, o in zip(rest[nx:], outs):
            o_ref[...] = o.astype(o_ref.dtype)

    a_spec = pl.BlockSpec((k_dim, tm), lambda i, j: (0, i)) if ta else pl.BlockSpec((tm, k_dim), lambda i, j: (i, 0))
    b_spec = pl.BlockSpec((tn, k_dim), lambda i, j: (j, 0)) if tb else pl.BlockSpec((k_dim, tn), lambda i, j: (0, j))
    t_spec = pl.BlockSpec((tm, tn), lambda i, j: (i, j))
    v_spec = pl.BlockSpec((1, tn), lambda i, j: (0, j))
    if col_slots:
        assert n_dim == col_slots * tn
        o_spec = pl.BlockSpec((None, tm, tn), lambda i, j: (j, i, 0))
        o_shape = (col_slots, m_dim, tn)
    else:
        o_spec, o_shape = t_spec, (m_dim, n_dim)
    res = pl.pallas_call(
        body, name=name, grid=(m_dim // tm, n_dim // tn),
        in_specs=[a_spec, b_spec] + [t_spec] * len(tiles) + [v_spec] * len(vecs),
        out_specs=[o_spec] * len(out_dtypes),
        out_shape=[jax.ShapeDtypeStruct(o_shape, dt) for dt in out_dtypes],
        compiler_params=_cparams(("parallel", "parallel")),
    )(a, b, *tiles, *vecs)
    return res


def _exchange_copies(scatter, in_refs, land_refs, send_sems, recv_sems, local_sems):
    n = len(in_refs)
    x, y, c = lax.axis_index("x"), lax.axis_index("y"), lax.axis_index("c")
    me = 4 * x + 2 * y + c
    local = [pltpu.make_async_copy(in_refs[a].at[me] if scatter else in_refs[a], land_refs[a].at[me],
                                   local_sems.at[a]) for a in range(n)]
    send, arrive = [], []
    for k in range(1, N_DEV):
        px, py, pc = x ^ ((k >> 2) & 1), y ^ ((k >> 1) & 1), c ^ (k & 1)
        peer = 4 * px + 2 * py + pc
        for a in range(n):
            src = in_refs[a].at[peer] if scatter else in_refs[a]
            sems = dict(send_sem=send_sems.at[a * (N_DEV - 1) + k - 1], recv_sem=recv_sems.at[a * (N_DEV - 1) + k - 1],
                        device_id=(px, py, pc), device_id_type=pl.DeviceIdType.MESH)
            send.append(pltpu.make_async_remote_copy(src_ref=src, dst_ref=land_refs[a].at[me], **sems))
            arrive.append(pltpu.make_async_remote_copy(src_ref=src, dst_ref=land_refs[a].at[peer], **sems))
    return local, send, arrive


def _land_shape(a, scatter):
    return ((N_DEV,) + a.shape[1:]) if scatter else ((N_DEV,) + a.shape)


def _exchange(name, arrays, scatter):
    n = len(arrays)

    def body(*refs):
        local, send, arrive = _exchange_copies(scatter, refs[:n], refs[n:2 * n], *refs[2 * n:])
        for cp in local + send:
            cp.start()
        for cp, arr in zip(send, arrive):
            cp.wait_send()
            arr.wait_recv()
        for cp in local:
            cp.wait()

    any_spec = pl.BlockSpec(memory_space=pl.ANY)
    return pl.pallas_call(
        body, name=name,
        in_specs=[any_spec] * n, out_specs=[any_spec] * n,
        out_shape=[jax.ShapeDtypeStruct(_land_shape(a, scatter), a.dtype) for a in arrays],
        scratch_shapes=[pltpu.SemaphoreType.DMA((n * (N_DEV - 1),)),
                        pltpu.SemaphoreType.DMA((n * (N_DEV - 1),)),
                        pltpu.SemaphoreType.DMA((n,))],
        compiler_params=pltpu.CompilerParams(has_side_effects=True),
    )(*arrays)


_HBM = pl.BlockSpec(memory_space=pltpu.HBM)
_SEM = pl.BlockSpec(memory_space=pltpu.SEMAPHORE)
_EFFECT = pltpu.SideEffectType.DATAFLOW_SIDE_EFFECTING


def _exchange_start(name, arrays, scatter, dep):
    n = len(arrays)
    nsem = n * (N_DEV - 1)
    srcs = [pltpu.with_memory_space_constraint(a, pltpu.HBM) for a in arrays]
    lands = [pltpu.with_memory_space_constraint(lax.empty(_land_shape(a, scatter), a.dtype), pltpu.HBM) for a in arrays]

    def body(*refs):
        sems = refs[2 * n + 1:2 * n + 4]
        local, send, _ = _exchange_copies(scatter, refs[:n], refs[n:2 * n], *sems)
        for cp in local + send:
            cp.start()
        token = refs[-1]
        token[...] = jnp.zeros_like(token)

    res = pl.pallas_call(
        body, name=name,
        in_specs=[_HBM] * (2 * n) + [pl.BlockSpec(memory_space=pl.ANY)],
        out_specs=[_SEM] * 3 + [_HBM] * (2 * n) + [pl.BlockSpec(memory_space=pltpu.VMEM)],
        out_shape=[pltpu.SemaphoreType.DMA((nsem,)), pltpu.SemaphoreType.DMA((nsem,)), pltpu.SemaphoreType.DMA((n,))]
        + [pltpu.HBM(a.shape, a.dtype) for a in arrays]
        + [pltpu.HBM(_land_shape(a, scatter), a.dtype) for a in arrays]
        + [jax.ShapeDtypeStruct((8, LANES), F32)],
        input_output_aliases={i: 3 + i for i in range(2 * n)},
        compiler_params=pltpu.CompilerParams(has_side_effects=_EFFECT),
    )(*srcs, *lands, dep)
    return res[:-1], res[-1][0:1, 0:1]


def _exchange_wait(name, handles, scatter, after):
    n = (len(handles) - 3) // 2
    sems, thru = handles[:3], handles[3:]

    def body(*refs):
        local, send, arrive = _exchange_copies(scatter, refs[:n], refs[n:2 * n], *refs[2 * n:2 * n + 3])
        for cp, arr in zip(send, arrive):
            cp.wait_send()
            arr.wait_recv()
        for cp in local:
            cp.wait()

    res = pl.pallas_call(
        body, name=name,
        in_specs=[_HBM] * (2 * n) + [_SEM] * 3 + [pl.BlockSpec(memory_space=pl.ANY)],
        out_specs=[_HBM] * (2 * n),
        out_shape=[pltpu.HBM(t.shape, t.dtype) for t in thru],
        input_output_aliases={i: i for i in range(2 * n)},
        compiler_params=pltpu.CompilerParams(has_side_effects=_EFFECT),
    )(*thru, *sems, after)
    return res[n:]


def _norm_mod(x, g, sc, sh):
    r = lax.rsqrt(jnp.mean(x * x, axis=-1, keepdims=True) + EPS)
    return (x * r * g) * (1.0 + sc) + sh


def _first_norm(x, g, sc, sh):
    (h,), _ = _rowwise("first_norm", lambda x, g, sc, sh: ((_norm_mod(x, g, sc, sh),), ()),
                       [x], [g, sc, sh], [(x.shape[1], MXU_DTYPE)], [], 256)
    return h


def _res_norm(x, y, gate, g, sc, sh):
    def fn(x, y, gate, g, sc, sh):
        xn = x + gate * y
        return (xn, _norm_mod(xn, g, sc, sh)), ()
    (xn, h), _ = _rowwise("res_norm", fn, [x, y], [gate, g, sc, sh],
                          [(x.shape[1], F32), (x.shape[1], MXU_DTYPE)], [], 256)
    return xn, h


def _final_loss(x, y, gate, target):
    d = x.shape[1]

    def fn(x, y, target, gate):
        err = (x + gate * y) - target
        part = jnp.sum(jnp.sum(err * err, axis=-1, keepdims=True), axis=0, keepdims=True) * (0.5 / d)
        dx = err * (1.0 / d)
        return (dx, dx * gate), (jnp.broadcast_to(part, (1, LANES)), _colsum(dx * y))
    (dx, dy), (loss, dgate) = _rowwise("final_loss", fn, [x, y, target], [gate],
                                       [(d, F32), (d, MXU_DTYPE)], [(1, LANES), (1, d)], 256)
    return loss[0, 0], dx, dy, dgate


def _norm_bwd_core(dxo, dh, x, g, sc):
    r = lax.rsqrt(jnp.mean(x * x, axis=-1, keepdims=True) + EPS)
    xn = x * r
    dsh = _colsum(dh)
    dsc = _colsum(dh * (xn * g))
    dyy = dh * (1.0 + sc)
    dg = _colsum(dyy * xn)
    dxn = dyy * g
    dxi = dxo + r * (dxn - xn * jnp.mean(dxn * xn, axis=-1, keepdims=True))
    return dxi, dsh, dsc, dg


def _bwd_norm_gate(dxo, dh, x, y_prev, g, sc, gate_prev):
    d = x.shape[1]

    def fn(dxo, dh, x, y_prev, g, sc, gate_prev):
        dxi, dsh, dsc, dg = _norm_bwd_core(dxo, dh, x, g, sc)
        return (dxi, dxi * gate_prev), (dsh, dsc, dg, _colsum(dxi * y_prev))
    (dxi, dy), reds = _rowwise("bwd_norm_gate", fn, [dxo, dh, x, y_prev], [g, sc, gate_prev],
                               [(d, F32), (d, MXU_DTYPE)], [(1, d)] * 4, 256)
    return dxi, dy, reds


def _bwd_norm_first(dxo, dh, x, g, sc):
    d = x.shape[1]

    def fn(dxo, dh, x, g, sc):
        dxi, dsh, dsc, dg = _norm_bwd_core(dxo, dh, x, g, sc)
        return (dxi,), (dsh, dsc, dg)
    (dxi,), reds = _rowwise("bwd_norm_first", fn, [dxo, dh, x], [g, sc], [(d, F32)], [(1, d)] * 3, 256)
    return dxi, reds


def _mlp_fwd(h, w1, w2):
    def epi(acc):
        r = jnp.maximum(acc, 0.0)
        return acc, r * r
    a, z = _mm("mlp_in", h, w1, out_dtypes=(MXU_DTYPE, MXU_DTYPE), epi=epi, tm=1024, tn=512)
    (out,) = _mm("mlp_out", z, w2, tm=512, tn=512)
    return out, (a, z)


def _mlp_bwd(dy, h, w1, w2, saved):
    a, z = saved

    def epi(acc, a):
        return (acc * (2.0 * jnp.maximum(a.astype(F32), 0.0)),)
    (da,) = _mm("mlp_dz", dy, w2, tb=True, out_dtypes=(MXU_DTYPE,), epi=epi, tiles=(a,), tm=1024, tn=512)
    (dw2,) = _mm("mlp_dw2", z, dy, ta=True, out_dtypes=(MXU_DTYPE,))
    (dw1,) = _mm("mlp_dw1", h, da, ta=True, out_dtypes=(MXU_DTYPE,), tn=da.shape[1] // N_DEV, col_slots=N_DEV)
    (dh,) = _mm("mlp_dh", da, w1, tb=True)
    return dh, dw1, dw2


def _split3(v):
    hi = v.astype(jnp.bfloat16)
    r1 = v - hi.astype(F32)
    mid = r1.astype(jnp.bfloat16)
    lo = (r1 - mid.astype(F32)).astype(jnp.bfloat16)
    return hi, mid, lo


def _tri_matmul(tri, v):
    hi, mid, lo = _split3(v)
    dot = functools.partial(jnp.dot, preferred_element_type=F32)
    return dot(tri, hi) + dot(tri, mid) + dot(tri, lo)


def _log_sigmoid(v):
    return jnp.minimum(v, 0.0) - jnp.log(1.0 + jnp.exp(-jnp.abs(v)))


def _gate_fwd(proj, b_pad, col_block):
    s = proj.shape[0]
    tb = min(SCAN_BLOCK, s)
    nblk = s // tb

    def body(f_ref, b_ref, o_ref):
        row = lax.broadcasted_iota(jnp.int32, (tb, tb), 0)
        col = lax.broadcasted_iota(jnp.int32, (tb, tb), 1)
        tri = (col <= row).astype(jnp.bfloat16)

        def step(i, carry):
            rows = pl.ds(pl.multiple_of(i * tb, tb), tb)
            lf = _log_sigmoid(f_ref[rows, :] + b_ref[...])
            f = _tri_matmul(tri, lf) + carry
            o_ref[rows, :] = f
            return f[tb - 1:tb, :]
        lax.fori_loop(0, nblk, step, jnp.zeros((1, LANES), F32))

    return pl.pallas_call(
        body, name="gate_fwd", grid=(1,),
        in_specs=[pl.BlockSpec((s, LANES), lambda i: (0, col_block)), pl.BlockSpec((1, LANES), lambda i: (0, 0))],
        out_specs=pl.BlockSpec((s, LANES), lambda i: (0, 0)),
        out_shape=jax.ShapeDtypeStruct((s, LANES), F32),
        compiler_params=_cparams(("arbitrary",)),
    )(proj, b_pad)


def _gate_bwd(proj, b_pad, d_f, col_block):
    s = proj.shape[0]
    tb = min(SCAN_BLOCK, s)
    nblk = s // tb

    def body(f_ref, b_ref, d_ref, o_ref, db_ref):
        row = lax.broadcasted_iota(jnp.int32, (tb, tb), 0)
        col = lax.broadcasted_iota(jnp.int32, (tb, tb), 1)
        tri = (col >= row).astype(jnp.bfloat16)

        def step(j, carry):
            acc, db = carry
            i = nblk - 1 - j
            rows = pl.ds(pl.multiple_of(i * tb, tb), tb)
            dlf = _tri_matmul(tri, d_ref[rows, :]) + acc
            dpre = dlf * _sigmoid(-(f_ref[rows, :] + b_ref[...]))
            o_ref[rows, :] = dpre.astype(o_ref.dtype)
            return dlf[0:1, :], db + _colsum(dpre)
        _, db = lax.fori_loop(0, nblk, step, (jnp.zeros((1, LANES), F32), jnp.zeros((1, LANES), F32)))
        db_ref[...] = db

    return pl.pallas_call(
        body, name="gate_bwd", grid=(1,),
        in_specs=[pl.BlockSpec((s, LANES), lambda i: (0, col_block)), pl.BlockSpec((1, LANES), lambda i: (0, 0)),
                  pl.BlockSpec((s, LANES), lambda i: (0, 0))],
        out_specs=[pl.BlockSpec((s, LANES), lambda i: (0, 0)), pl.BlockSpec((1, LANES), lambda i: (0, 0))],
        out_shape=[jax.ShapeDtypeStruct((s, LANES), MXU_DTYPE), jax.ShapeDtypeStruct((1, LANES), F32)],
        compiler_params=_cparams(("arbitrary",)),
    )(proj, b_pad, d_f)


def _head_masks():
    lane = lax.broadcasted_iota(jnp.int32, (1, LANES), 1)
    return lane < HEAD_DIM


def _pair_norm(v, g, first):
    v2 = v * v
    ss0 = jnp.sum(jnp.where(first, v2, 0.0), axis=-1, keepdims=True)
    ss1 = jnp.sum(jnp.where(first, 0.0, v2), axis=-1, keepdims=True)
    r = jnp.where(first, lax.rsqrt(ss0 * (1.0 / HEAD_DIM) + EPS), lax.rsqrt(ss1 * (1.0 / HEAD_DIM) + EPS))
    vn = v * r
    return vn * g, vn, r


_NT = (((1,), (1,)), ((), ()))
_TN = (((0,), (0,)), ((), ()))

ATT_TQ = 512
ATT_FWD_TQ = 1024
ATT_TK = 256
AUG_F, AUG_ONE = 0, 3


def _own_lanes(hd):
    lane = lax.broadcasted_iota(jnp.int32, (1, LANES), 1)
    return (lane < HEAD_DIM) if hd == 0 else (lane >= HEAD_DIM)


def _aug_lanes(hd, f_other):
    lane = lax.broadcasted_iota(jnp.int32, (1, LANES), 1) - (HEAD_DIM if hd == 0 else 0)
    hi, mid, lo = [t.astype(F32) for t in _split3(f_other)]
    zero = jnp.zeros_like(f_other)
    f_terms = jnp.where(lane == 0, hi, jnp.where(lane == 1, mid, jnp.where(lane == 2, lo, zero)))
    f_shift = jnp.where(lane == 3, hi, jnp.where(lane == 4, mid, jnp.where(lane == 5, lo, zero)))
    ones_lo = jnp.where(lane < 3, 1.0, 0.0) * jnp.where(lane >= 0, 1.0, 0.0)
    ones_hi = jnp.where(lane < 6, 1.0, 0.0) * jnp.where(lane >= 3, 1.0, 0.0)
    return f_terms + ones_hi, ones_lo - f_shift


def _attn_operands(q_raw, k_raw, f_rep, qg, kg, scale):
    first = _head_masks()
    qn, _, _ = _pair_norm(q_raw, qg, first)
    kn, _, _ = _pair_norm(k_raw, kg, first)
    f_other = pltpu.roll(f_rep, HEAD_DIM, 1)
    out = []
    for hd in range(2):
        own = _own_lanes(hd)
        q_x, k_x = _aug_lanes(hd, f_other)
        out.append((jnp.where(own, qn * scale, q_x), jnp.where(own, kn, k_x)))
    return out


def _set_cols(a, b, c0):
    return b if c0 == 0 else jnp.concatenate([a[:, :c0], b], axis=1)


def _add_cols(a, b, c0):
    return _set_cols(a, a[:, c0:] + b, c0)


def _causal_t(tk, tq, off):
    r = lax.broadcasted_iota(jnp.int32, (tk, tq), 0)
    c = lax.broadcasted_iota(jnp.int32, (tk, tq), 1)
    return (r - c) <= off


def _big(shape, index_map):
    return pl.BlockSpec(shape, index_map, pipeline_mode=pl.Buffered(1))


def _attn_fwd_t(proj, f_rep, qg, kg, d_model):
    s = proj.shape[0]
    pairs = d_model // LANES
    tq, tk = min(ATT_FWD_TQ, s), min(ATT_TK, s)
    assert tq % tk == 0 and s % tq == 0
    nq = s // tq
    n_diag = tq // tk
    scale = HEAD_DIM ** -0.5
    ch = tk

    def body(q_ref, k_ref, v_ref, frep_ref, qg_ref, kg_ref, o_ref, lse_ref, qt_s, k_s, vt_s):
        for ci in range(s // ch):
            rows = pl.ds(ci * ch, ch)
            ops = _attn_operands(q_ref[rows, :], k_ref[rows, :], frep_ref[rows, :], qg_ref[...], kg_ref[...], scale)
            vv = v_ref[rows, :]
            for hd in range(2):
                own = _own_lanes(hd)
                lane = lax.broadcasted_iota(jnp.int32, (1, LANES), 1)
                one_lane = lane == (HEAD_DIM if hd == 0 else 0)
                qt_s[hd, :, rows] = ops[hd][0].T.astype(qt_s.dtype)
                k_s[hd, rows, :] = ops[hd][1].astype(k_s.dtype)
                vt_s[hd, :, rows] = jnp.where(own, vv, jnp.where(one_lane, 1.0, 0.0)).T.astype(vt_s.dtype)

        def q_block(qi, _):
            q0 = pl.multiple_of(qi * tq, tq)
            qcols = pl.ds(q0, tq)
            nfull = q0 // tk
            qts = [qt_s[hd, :, qcols] for hd in range(2)]

            def krows(kj):
                return pl.ds(pl.multiple_of(kj * tk, tk), tk)

            def scores(hd, kj):
                return jnp.dot(k_s[hd, krows(kj), :], qts[hd], preferred_element_type=F32)

            def kv_step(kj, carry):
                new = []
                for hd in range(2):
                    m, acc, p_prev = carry[hd]
                    st = scores(hd, kj)
                    pv = jnp.dot(vt_s[hd, :, krows(jnp.maximum(kj - 1, 0))], p_prev, preferred_element_type=F32)
                    m_new = jnp.maximum(m, jnp.max(st, axis=0, keepdims=True))
                    p = jnp.exp(st - m_new).astype(vt_s.dtype)
                    new.append((m_new, jnp.exp(m - m_new) * (acc + pv), p))
                return tuple(new)

            def diag_step(t, carry, last):
                c0, kj = t * tk, nfull + t
                new = []
                for hd in range(2):
                    m, acc, p_prev, c_prev = carry[hd]
                    pv = jnp.dot(vt_s[hd, :, krows(jnp.maximum(kj - 1, 0))], p_prev, preferred_element_type=F32)
                    acc = _add_cols(acc, pv, c_prev)
                    st = jnp.dot(k_s[hd, krows(kj), :], qts[hd][:, c0:], preferred_element_type=F32)
                    st = jnp.where(_causal_t(tk, tq - c0, 0), st, NEG)
                    m_new = jnp.maximum(m[:, c0:], jnp.max(st, axis=0, keepdims=True))
                    p = jnp.exp(st - m_new).astype(vt_s.dtype)
                    acc = _set_cols(acc, jnp.exp(m[:, c0:] - m_new) * acc[:, c0:], c0)
                    m = _set_cols(m, m_new, c0)
                    if last:
                        pv = jnp.dot(vt_s[hd, :, krows(kj)], p, preferred_element_type=F32)
                        new.append((m, _add_cols(acc, pv, c0)))
                    else:
                        new.append((m, acc, p, c0))
                return tuple(new)

            init = tuple((jnp.full((1, tq), NEG, F32), jnp.zeros((LANES, tq), F32),
                          jnp.zeros((tk, tq), vt_s.dtype)) for hd in range(2))
            carry = lax.fori_loop(0, nfull, kv_step, init)
            carry = tuple(cr + (0,) for cr in carry)
            for t in range(n_diag):
                carry = diag_step(t, carry, t == n_diag - 1)
            o_parts, lse_parts = [], []
            for hd, (m, acc) in enumerate(carry):
                e0 = HEAD_DIM if hd == 0 else 0
                l = acc[e0:e0 + 1, :]
                o_parts.append((acc / l).T)
                lse_parts.append(m + jnp.log(l))
            o_ref[pl.ds(q0, tq), :] = jnp.where(_head_masks(), o_parts[0], o_parts[1]).astype(o_ref.dtype)
            lse_ref[0, :, qcols] = jnp.concatenate(lse_parts, axis=0)
            return 0
        lax.fori_loop(0, nq, q_block, 0)

    blk = lambda off: _big((s, LANES), lambda h: (0, off + h))
    vec = pl.BlockSpec((1, LANES), lambda h: (0, 0))
    return pl.pallas_call(
        body, name="attn_fwd", grid=(pairs,),
        in_specs=[blk(0), blk(pairs), blk(2 * pairs), blk(0), vec, vec],
        out_specs=[pl.BlockSpec((s, LANES), lambda h: (0, h)), pl.BlockSpec((1, 2, s), lambda h: (h, 0, 0))],
        out_shape=[jax.ShapeDtypeStruct((s, d_model), MXU_DTYPE), jax.ShapeDtypeStruct((pairs, 2, s), F32)],
        scratch_shapes=[pltpu.VMEM((2, LANES, s), MXU_DTYPE), pltpu.VMEM((2, s, LANES), MXU_DTYPE),
                        pltpu.VMEM((2, LANES, s), MXU_DTYPE)],
        compiler_params=_cparams(("arbitrary",)),
    )(proj, proj, proj, f_rep, qg, kg)


def _attn_bwd_t(proj, do, o, lse, f_rep, qg, kg, d_model):
    s = proj.shape[0]
    pairs = d_model // LANES
    tq, tk = min(ATT_TQ, s), min(ATT_TK, s)
    assert tq % tk == 0 and s % tq == 0
    nq = s // tq
    n_diag = tq // tk
    scale = HEAD_DIM ** -0.5
    ch = tk

    def norm_bwd(raw, g, dn, first):
        _, xn, r = _pair_norm(raw, g, first)
        dxn = dn * g
        t = dxn * xn
        mu0 = jnp.sum(jnp.where(first, t, 0.0), axis=-1, keepdims=True)
        mu1 = jnp.sum(jnp.where(first, 0.0, t), axis=-1, keepdims=True)
        mu = jnp.where(first, mu0, mu1) * (1.0 / HEAD_DIM)
        return r * (dxn - xn * mu), _colsum(dn * xn)

    def body(q_ref, k_ref, v_ref, do_ref, o_ref, lse_ref, frep_ref, qg_ref, kg_ref,
             dq_ref, dk_ref, dv_ref, df_ref, dqg_ref, dkg_ref,
             q_s, qt_s, k_s, kt_s, v_s, do_s, dot_s, dl_s, dk_s, dv_s):
        first = _head_masks()
        hp = pl.program_id(0)
        lane = lax.broadcasted_iota(jnp.int32, (1, LANES), 1)
        for ci in range(s // ch):
            rows = pl.ds(ci * ch, ch)
            ops = _attn_operands(q_ref[rows, :], k_ref[rows, :], frep_ref[rows, :], qg_ref[...], kg_ref[...], scale)
            v_s[rows, :] = v_ref[rows, :].astype(v_s.dtype)
            dov = do_ref[rows, :].astype(F32)
            ot = o_ref[rows, :].astype(F32).T
            for hd in range(2):
                own = _own_lanes(hd)
                q_s[hd, rows, :] = ops[hd][0].astype(q_s.dtype)
                qt_s[hd, :, rows] = ops[hd][0].T.astype(qt_s.dtype)
                k_s[hd, rows, :] = ops[hd][1].astype(k_s.dtype)
                kt_s[hd, :, rows] = ops[hd][1].T.astype(kt_s.dtype)
                doh = jnp.where(own, dov, 0.0)
                do_s[hd, rows, :] = doh.astype(do_s.dtype)
                doht = doh.T
                dot_s[hd, :, rows] = doht.astype(dot_s.dtype)
                dl_s[hd:hd + 1, rows] = jnp.sum(doht * ot, axis=0, keepdims=True)
        dk_s[...] = jnp.zeros_like(dk_s)
        dv_s[...] = jnp.zeros_like(dv_s)

        @pl.when(hp == 0)
        def _():
            df_ref[...] = jnp.zeros_like(df_ref)

        def q_block(qi, dqg):
            q0 = pl.multiple_of(qi * tq, tq)
            qcols = pl.ds(q0, tq)
            qrows = pl.ds(q0, tq)
            nfull = q0 // tk
            qts = [qt_s[hd, :, qcols] for hd in range(2)]
            dots = [dot_s[hd, :, qcols] for hd in range(2)]
            qns = [q_s[hd, qrows, :] for hd in range(2)]
            dons = [do_s[hd, qrows, :] for hd in range(2)]
            lse_r = [lse_ref[0, hd:hd + 1, qcols] for hd in range(2)]
            dl_r = [dl_s[hd:hd + 1, qcols] for hd in range(2)]
            bdt = qt_s.dtype

            def krows(kj):
                return pl.ds(pl.multiple_of(kj * tk, tk), tk)

            def scores(hd, kj):
                return (jnp.dot(k_s[hd, krows(kj), :], qts[hd], preferred_element_type=F32),
                        jnp.dot(v_s[krows(kj), :], dots[hd], preferred_element_type=F32))

            def products(hd, rows, ds, p, dqt, c0=0):
                dk_s[hd, rows, :] += jnp.dot(ds, qns[hd][c0:, :], preferred_element_type=F32)
                dv_s[rows, :] += jnp.dot(p, dons[hd][c0:, :], preferred_element_type=F32)
                return _add_cols(dqt, jnp.dot(kt_s[hd, :, rows], ds, preferred_element_type=F32), c0)

            def kv_step(kj, carry):
                new = []
                for hd in range(2):
                    dqt, ds_prev, p_prev = carry[hd]
                    st, dp = scores(hd, kj)
                    dqt = products(hd, krows(jnp.maximum(kj - 1, 0)), ds_prev, p_prev, dqt)
                    p = jnp.exp(st - lse_r[hd])
                    new.append((dqt, (p * (dp - dl_r[hd])).astype(bdt), p.astype(bdt)))
                return tuple(new)

            def diag_step(t, carry, last):
                c0, kj = t * tk, nfull + t
                new = []
                for hd in range(2):
                    dqt, ds_prev, p_prev, c_prev = carry[hd]
                    st = jnp.dot(k_s[hd, krows(kj), :], qts[hd][:, c0:], preferred_element_type=F32)
                    dp = jnp.dot(v_s[krows(kj), :], dots[hd][:, c0:], preferred_element_type=F32)
                    dqt = products(hd, krows(jnp.maximum(kj - 1, 0)), ds_prev, p_prev, dqt, c_prev)
                    st = jnp.where(_causal_t(tk, tq - c0, 0), st, NEG)
                    p = jnp.exp(st - lse_r[hd][:, c0:])
                    ds = (p * (dp - dl_r[hd][:, c0:])).astype(bdt)
                    if last:
                        new.append(products(hd, krows(kj), ds, p.astype(bdt), dqt, c0))
                    else:
                        new.append((dqt, ds, p.astype(bdt), c0))
                return tuple(new)

            init = tuple((jnp.zeros((LANES, tq), F32), jnp.zeros((tk, tq), bdt), jnp.zeros((tk, tq), bdt))
                         for hd in range(2))
            carry = lax.fori_loop(0, nfull, kv_step, init)
            carry = tuple(cr + (0,) for cr in carry)
            for t in range(n_diag):
                carry = diag_step(t, carry, t == n_diag - 1)
            dq_parts = [dqt.T for dqt in carry]
            rs0 = dq_parts[0][:, HEAD_DIM + AUG_F:HEAD_DIM + AUG_F + 1]
            rs1 = dq_parts[1][:, AUG_F:AUG_F + 1]
            df_ref[qrows, :] += jnp.where(lane == 2 * hp, rs0, 0.0) + jnp.where(lane == 2 * hp + 1, rs1, 0.0)
            dqn = jnp.where(first, dq_parts[0], dq_parts[1]) * scale
            dq_raw, dg = norm_bwd(q_ref[qrows, :], qg_ref[...], dqn, first)
            dq_ref[qrows, :] = dq_raw.astype(dq_ref.dtype)
            return dqg + dg
        dqg_ref[0] = lax.fori_loop(0, nq, q_block, jnp.zeros((1, LANES), F32))

        dkg = jnp.zeros((1, LANES), F32)
        for ci in range(s // ch):
            rows = pl.ds(ci * ch, ch)
            dk0, dk1 = dk_s[0, rows, :], dk_s[1, rows, :]
            cs0 = dk0[:, HEAD_DIM + AUG_ONE:HEAD_DIM + AUG_ONE + 1]
            cs1 = dk1[:, AUG_ONE:AUG_ONE + 1]
            df_ref[rows, :] -= jnp.where(lane == 2 * hp, cs0, 0.0) + jnp.where(lane == 2 * hp + 1, cs1, 0.0)
            dk_raw, dg = norm_bwd(k_ref[rows, :], kg_ref[...], jnp.where(first, dk0, dk1), first)
            dk_ref[rows, :] = dk_raw.astype(dk_ref.dtype)
            dkg = dkg + dg
            dv_ref[rows, :] = dv_s[rows, :].astype(dv_ref.dtype)
        dkg_ref[0] = dkg

    blk = lambda off: _big((s, LANES), lambda h: (0, off + h))
    outb = pl.BlockSpec((s, LANES), lambda h: (0, h))
    vec = pl.BlockSpec((1, LANES), lambda h: (0, 0))
    gout = pl.BlockSpec((1, 1, LANES), lambda h: (h, 0, 0))
    act = jax.ShapeDtypeStruct((s, d_model), MXU_DTYPE)
    gsh = jax.ShapeDtypeStruct((pairs, 1, LANES), F32)
    pair_rows = pltpu.VMEM((2, s, LANES), MXU_DTYPE)
    pair_cols = pltpu.VMEM((2, LANES, s), MXU_DTYPE)
    return pl.pallas_call(
        body, name="attn_bwd", grid=(pairs,),
        in_specs=[blk(0), blk(pairs), blk(2 * pairs), blk(0), blk(0),
                  pl.BlockSpec((1, 2, s), lambda h: (h, 0, 0)), blk(0), vec, vec],
        out_specs=[outb, outb, outb, pl.BlockSpec((s, LANES), lambda h: (0, 0)), gout, gout],
        out_shape=[act, act, act, jax.ShapeDtypeStruct((s, LANES), F32), gsh, gsh],
        scratch_shapes=[pair_rows, pair_cols, pair_rows, pair_cols, pltpu.VMEM((s, LANES), MXU_DTYPE),
                        pair_rows, pair_cols, pltpu.VMEM((8, s), F32),
                        pltpu.VMEM((2, s, LANES), F32), pltpu.VMEM((s, LANES), F32)],
        compiler_params=_cparams(("arbitrary",), ATT_BWD_VMEM_LIMIT),
    )(proj, proj, proj, do, o, lse, f_rep, qg, kg)


def _pad_cols(a, n):
    return jnp.pad(a, ((0, 0), (0, n - a.shape[1])))


def _fox_fwd(h, w):
    d = h.shape[1]
    pairs = d // LANES
    (proj,) = _mm("fox_in", h, w["w_in"], tm=1024, tn=640)
    f_cum = _gate_fwd(proj, w["b_f"], 3 * pairs)
    f16 = f_cum[:, :d // HEAD_DIM]
    f_rep = jnp.repeat(f16, HEAD_DIM, axis=1)
    o, lse = _attn_fwd_t(proj, f_rep, w["qg"], w["kg"], d)
    (y,) = _mm("fox_out", o, w["w_out"])
    return y, (proj, f_rep, o, lse)


def _fox_bwd(dy, h, w, saved):
    proj, f_rep, o, lse = saved
    s, d = h.shape
    pairs = d // LANES
    (do,) = _mm("fox_do", dy, w["w_out"], tb=True, out_dtypes=(MXU_DTYPE,))
    (dw_out,) = _mm("fox_dwout", o, dy, ta=True, out_dtypes=(MXU_DTYPE,))
    dq, dk, dv, d_f, dqg, dkg = _attn_bwd_t(proj, do, o, lse, f_rep, w["qg"], w["kg"], d)
    dfpre, db_f = _gate_bwd(proj, w["b_f"], d_f, 3 * pairs)
    dproj = jnp.concatenate([dq, dk, dv, dfpre], axis=1)
    (dw_in,) = _mm("fox_dwin", h, dproj, ta=True, out_dtypes=(MXU_DTYPE,), tn=640)
    (dh,) = _mm("fox_dh", dproj, w["w_in"], tb=True)
    fold = lambda g: jnp.sum(g, axis=(0, 1)).reshape(2, HEAD_DIM).sum(axis=0)
    return dh, dict(w_in=dw_in, w_out=dw_out, b_f=db_f[0, :d // HEAD_DIM], qg=fold(dqg), kg=fold(dkg))


_GELU_C = math.sqrt(2.0 / math.pi)


def _gelu(v):
    return 0.5 * v * (1.0 + jnp.tanh(_GELU_C * (v + 0.044715 * v * v * v)))


def _gelu_grad(v):
    t = jnp.tanh(_GELU_C * (v + 0.044715 * v * v * v))
    return 0.5 * (1.0 + t) + 0.5 * v * (1.0 - t * t) * (_GELU_C * (1.0 + 3.0 * 0.044715 * v * v))


def _ln_stats(v):
    mu = jnp.mean(v, axis=-1, keepdims=True)
    vc = v - mu
    r = lax.rsqrt(jnp.mean(vc * vc, axis=-1, keepdims=True) + EPS)
    return vc * r, r


def _sg_mask():
    t = lax.broadcasted_iota(jnp.int32, (SG_CHUNK, SG_CHUNK), 0) // SG_BLOCK
    sidx = lax.broadcasted_iota(jnp.int32, (SG_CHUNK, SG_CHUNK), 1) // SG_BLOCK
    return sidx <= t


def _sgu_fwd(uv_pre, ln_g, ln_b, w_s, b_st):
    s, w2 = uv_pre.shape
    wd = w2 // 2
    tr = min(256, s)

    def fn(uv_pre, ln_g, ln_b, w_s, b_st):
        uv = _gelu(uv_pre)
        u = uv[:, :wd]
        vh, _ = _ln_stats(uv[:, wd:])
        vl = (vh * ln_g + ln_b).astype(MXU_DTYPE)
        mask = _sg_mask()
        cols = []
        for g in range(SG_GROUPS):
            wg = jnp.where(mask, w_s[g * SG_CHUNK:(g + 1) * SG_CHUNK, :], 0.0).astype(MXU_DTYPE)
            parts = []
            for ci in range(tr // SG_CHUNK):
                vt = vl[ci * SG_CHUNK:(ci + 1) * SG_CHUNK, g * SG_CHUNK:(g + 1) * SG_CHUNK]
                parts.append(jnp.dot(wg, vt, preferred_element_type=F32) + b_st[:, g:g + 1])
            cols.append(jnp.concatenate(parts, axis=0) if len(parts) > 1 else parts[0])
        vout = jnp.concatenate(cols, axis=1)
        return (u * vout,), ()
    (m,), _ = _rowwise("sgu_fwd", fn, [uv_pre], [ln_g, ln_b, w_s, b_st], [(wd, MXU_DTYPE)], [], tr)
    return m


def _sgu_bwd(uv_pre, dm, ln_g, ln_b, w_s, b_st):
    s, w2 = uv_pre.shape
    wd = w2 // 2
    tr = min(256, s)

    def fn(uv_pre, dm, ln_g, ln_b, w_s, b_st):
        uv = _gelu(uv_pre)
        u = uv[:, :wd]
        vh, r = _ln_stats(uv[:, wd:])
        vl = (vh * ln_g + ln_b).astype(MXU_DTYPE)
        mask = _sg_mask()
        lane = lax.broadcasted_iota(jnp.int32, (1, LANES), 1)
        cols, dcols, dws, dbs = [], [], [], jnp.zeros((SG_CHUNK, LANES), F32)
        for g in range(SG_GROUPS):
            wg = jnp.where(mask, w_s[g * SG_CHUNK:(g + 1) * SG_CHUNK, :], 0.0).astype(MXU_DTYPE)
            parts, dparts = [], []
            dwg = jnp.zeros((SG_CHUNK, SG_CHUNK), F32)
            dbg = jnp.zeros((SG_CHUNK, 1), F32)
            for ci in range(tr // SG_CHUNK):
                rs = slice(ci * SG_CHUNK, (ci + 1) * SG_CHUNK)
                cs = slice(g * SG_CHUNK, (g + 1) * SG_CHUNK)
                vt = vl[rs, cs]
                parts.append(jnp.dot(wg, vt, preferred_element_type=F32) + b_st[:, g:g + 1])
                dvo = dm[rs, cs] * u[rs, cs]
                dvob = dvo.astype(MXU_DTYPE)
                dparts.append(lax.dot_general(wg, dvob, _TN, preferred_element_type=F32))
                dwg = dwg + lax.dot_general(dvob, vt, _NT, preferred_element_type=F32)
                dbg = dbg + jnp.sum(dvo, axis=-1, keepdims=True)
            cols.append(jnp.concatenate(parts, axis=0) if len(parts) > 1 else parts[0])
            dcols.append(jnp.concatenate(dparts, axis=0) if len(dparts) > 1 else dparts[0])
            dws.append(jnp.where(mask, dwg, 0.0))
            dbs = dbs + jnp.where(lane == g, dbg, 0.0)
        vout = jnp.concatenate(cols, axis=1)
        dvl = jnp.concatenate(dcols, axis=1)
        du = dm * vout
        dlg = _colsum(dvl * vh)
        dlb = _colsum(dvl)
        dvh = dvl * ln_g
        dv = r * (dvh - jnp.mean(dvh, axis=-1, keepdims=True) - vh * jnp.mean(dvh * vh, axis=-1, keepdims=True))
        dpre = jnp.concatenate([du, dv], axis=1) * _gelu_grad(uv_pre)
        return (dpre,), (dlg, dlb, jnp.concatenate(dws, axis=0), dbs)
    (dpre,), reds = _rowwise("sgu_bwd", fn, [uv_pre, dm], [ln_g, ln_b, w_s, b_st], [(w2, MXU_DTYPE)],
                             [(1, wd), (1, wd), (SG_GROUPS * SG_CHUNK, SG_CHUNK), (SG_CHUNK, LANES)], tr)
    return dpre, reds


def _sg_fwd(h, w):
    (uv_pre,) = _mm("sg_in", h, w["w_in"], tm=1024, tn=512)
    m = _sgu_fwd(uv_pre, w["ln_g"], w["ln_b"], w["w_s"], w["b_st"])
    (y,) = _mm("sg_out", m, w["w_out"])
    return y, (uv_pre, m)


def _sg_bwd(dy, h, w, saved):
    uv_pre, m = saved
    (dm,) = _mm("sg_dm", dy, w["w_out"], tb=True)
    (dw_out,) = _mm("sg_dwout", m, dy, ta=True, out_dtypes=(MXU_DTYPE,))
    dpre, (dlg, dlb, dws, dbs) = _sgu_bwd(uv_pre, dm, w["ln_g"], w["ln_b"], w["w_s"], w["b_st"])
    (dw_in,) = _mm("sg_dwin", h, dpre, ta=True, out_dtypes=(MXU_DTYPE,), tn=dpre.shape[1] // N_DEV, col_slots=N_DEV)
    (dh,) = _mm("sg_dh", dpre, w["w_in"], tb=True)
    return dh, dict(w_in=dw_in, w_out=dw_out, ln_g=dlg, ln_b=dlb, w_s=dws, b_s=dbs[:, :SG_GROUPS].T)


def _conv_fwd_kernel(ypad, w_dw, b_dw):
    s = ypad.shape[0] - CONV_PAD
    d = ypad.shape[1]
    tt = min(256, s)
    ext = tt + CONV_PAD

    def body(y_ref, w_ref, b_ref, o_ref):
        def chunk(ci, _):
            base = pl.multiple_of(ci * tt, tt)
            e = y_ref[pl.ds(base, ext), :]
            acc = jnp.zeros((tt, LANES), F32) + b_ref[...]
            for j in range(CONV_WIDTH):
                sh = pltpu.roll(e, ext - (CONV_PAD - CONV_WIDTH + 1 + j), 0)[:tt, :]
                acc = acc + w_ref[j:j + 1, :] * sh
            o_ref[pl.ds(base, tt), :] = acc
            return 0
        lax.fori_loop(0, s // tt, chunk, 0)

    return pl.pallas_call(
        body, name="conv_fwd", grid=(d // LANES,),
        in_specs=[pl.BlockSpec((s + CONV_PAD, LANES), lambda i: (0, i)),
                  pl.BlockSpec((CONV_PAD, LANES), lambda i: (0, i)), pl.BlockSpec((1, LANES), lambda i: (0, i))],
        out_specs=pl.BlockSpec((s, LANES), lambda i: (0, i)),
        out_shape=jax.ShapeDtypeStruct((s, d), F32),
        compiler_params=_cparams(("parallel",)),
    )(ypad, w_dw, b_dw)


def _conv_bwd_kernel(ypad, dpad, w_dw):
    s = ypad.shape[0] - CONV_PAD
    d = ypad.shape[1]
    tt = min(256, s)
    ext = tt + CONV_PAD

    def body(y_ref, d_ref, w_ref, o_ref, dw_ref):
        dw_ref[...] = jnp.zeros_like(dw_ref)

        def chunk(ci, _):
            base = pl.multiple_of(ci * tt, tt)
            ye = y_ref[pl.ds(base, ext), :]
            de = d_ref[pl.ds(base, ext), :]
            dcur = de[:tt, :]
            acc = jnp.zeros((tt, LANES), F32)
            for j in range(CONV_WIDTH):
                back = CONV_WIDTH - 1 - j
                dsh = dcur if back == 0 else pltpu.roll(de, ext - back, 0)[:tt, :]
                acc = acc + w_ref[j:j + 1, :] * dsh
                ysh = pltpu.roll(ye, ext - (CONV_PAD - CONV_WIDTH + 1 + j), 0)[:tt, :]
                dw_ref[j:j + 1, :] += _colsum(dcur * ysh)
            o_ref[pl.ds(base, tt), :] = acc
            return 0
        lax.fori_loop(0, s // tt, chunk, 0)

    return pl.pallas_call(
        body, name="conv_bwd", grid=(d // LANES,),
        in_specs=[pl.BlockSpec((s + CONV_PAD, LANES), lambda i: (0, i)),
                  pl.BlockSpec((s + CONV_PAD, LANES), lambda i: (0, i)),
                  pl.BlockSpec((CONV_PAD, LANES), lambda i: (0, i))],
        out_specs=[pl.BlockSpec((s, LANES), lambda i: (0, i)), pl.BlockSpec((CONV_PAD, LANES), lambda i: (0, i))],
        out_shape=[jax.ShapeDtypeStruct((s, d), F32), jax.ShapeDtypeStruct((CONV_PAD, d), F32)],
        compiler_params=_cparams(("parallel",)),
    )(ypad, dpad, w_dw)


def _cv_fwd(h, w):
    d = h.shape[1]
    (y1,) = _mm("cv_pw1", h, w["w_pw1"], tm=1024, tn=512)

    def glu(y1, b1):
        t = y1 + b1
        return (t[:, :d] * _sigmoid(t[:, d:]),), ()
    (y2,), _ = _rowwise("cv_glu", glu, [y1], [w["b_pw1"]], [(d, F32)], [], 256)
    y3 = _conv_fwd_kernel(jnp.pad(y2, ((CONV_PAD, 0), (0, 0))), w["w_dw"], w["b_dw"])

    def lnsilu(y3, g, b):
        vh, _ = _ln_stats(y3)
        y4 = vh * g + b
        return (y4 * _sigmoid(y4),), ()
    (y5,), _ = _rowwise("cv_lnsilu", lnsilu, [y3], [w["ln_g"], w["ln_b"]], [(d, MXU_DTYPE)], [], 256)
    (y,) = _mm("cv_pw2", y5, w["w_pw2"], epi=lambda acc, b: (acc + b,), vecs=(w["b_pw2"],))
    return y, (y1, y2, y3, y5)


def _cv_bwd(dy, h, w, saved):
    y1, y2, y3, y5 = saved
    d = h.shape[1]
    (dy5,) = _mm("cv_dy5", dy, w["w_pw2"], tb=True)
    (dw_pw2,) = _mm("cv_dwpw2", y5, dy, ta=True, out_dtypes=(MXU_DTYPE,))

    def ln_bwd(dy5, y3, dyb, g, b):
        vh, r = _ln_stats(y3)
        y4 = vh * g + b
        sg = _sigmoid(y4)
        dy4 = dy5 * (sg * (1.0 + y4 * (1.0 - sg)))
        dvh = dy4 * g
        dy3 = r * (dvh - jnp.mean(dvh, axis=-1, keepdims=True) - vh * jnp.mean(dvh * vh, axis=-1, keepdims=True))
        return (dy3,), (_colsum(dy4 * vh), _colsum(dy4), _colsum(dy3), _colsum(dyb.astype(F32)))
    (dy3,), (dlg, dlb, db_dw, db_pw2) = _rowwise("cv_ln_bwd", ln_bwd, [dy5, y3, dy], [w["ln_g"], w["ln_b"]],
                                                 [(d, F32)], [(1, d)] * 4, 256)
    dy2, dw_dw = _conv_bwd_kernel(jnp.pad(y2, ((CONV_PAD, 0), (0, 0))), jnp.pad(dy3, ((0, CONV_PAD), (0, 0))),
                                  w["w_dw"])

    def glu_bwd(y1, dy2, b1):
        t = y1 + b1
        a, sg = t[:, :d], _sigmoid(t[:, d:])
        dy1 = jnp.concatenate([dy2 * sg, dy2 * a * sg * (1.0 - sg)], axis=1)
        return (dy1,), (_colsum(dy1),)
    (dy1,), (db_pw1,) = _rowwise("cv_glu_bwd", glu_bwd, [y1, dy2], [w["b_pw1"]], [(2 * d, MXU_DTYPE)],
                                 [(1, 2 * d)], 256)
    (dw_pw1,) = _mm("cv_dwpw1", h, dy1, ta=True, out_dtypes=(MXU_DTYPE,), tn=dy1.shape[1] // N_DEV, col_slots=N_DEV)
    (dh,) = _mm("cv_dh", dy1, w["w_pw1"], tb=True)
    return dh, dict(w_pw1=dw_pw1, w_pw2=dw_pw2, b_pw1=db_pw1, b_pw2=db_pw2, w_dw=dw_dw[:CONV_WIDTH],
                    b_dw=db_dw, ln_g=dlg, ln_b=dlb)


def _ada_outer(c_t, dmod):
    def fn(c_t, dmod):
        acc = c_t[:, 0:1] * dmod[0:1, :]
        for b in range(1, N_DEV):
            acc = acc + c_t[:, b:b + 1] * dmod[b:b + 1, :]
        return (acc,), ()
    (g,), _ = _rowwise("ada_outer", fn, [c_t], [dmod], [(dmod.shape[1], F32)], [], 256)
    return g


def _adamw(name, parts, w, m, v, tr, layer=0, prev=None):
    npart, rows = parts.shape[0], parts.shape[1]
    cols = w.shape[1]

    def fn(parts, w, m, v):
        g = parts[0].astype(F32)
        for q in range(1, npart):
            g = g + parts[q].astype(F32)
        m_new = ADAM_B1 * m + (1.0 - ADAM_B1) * g
        v_new = ADAM_B2 * v + (1.0 - ADAM_B2) * (g * g)
        m_hat = m_new / (1.0 - ADAM_B1 ** ADAM_STEP)
        v_hat = v_new / (1.0 - ADAM_B2 ** ADAM_STEP)
        delta = -ADAM_LR * (m_hat / (jnp.sqrt(v_hat) + ADAM_EPS) + ADAM_WD * w)
        return (g, delta, m_new, v_new), ()
    tr = min(tr, rows)
    nblk = rows // tr
    n_prev = 0 if prev is None else 4

    def body(p_ref, w_ref, m_ref, v_ref, *rest):
        outs, _ = fn(p_ref[...], w_ref[...], m_ref[...], v_ref[...])
        for o_ref, o in zip(rest[n_prev:], outs):
            o_ref[...] = o

    spec = pl.BlockSpec((tr, cols), lambda i: (layer * nblk + i, 0))
    return pl.pallas_call(
        body, name=name, grid=(nblk,),
        in_specs=[pl.BlockSpec((npart, tr, cols), lambda i: (0, i, 0)), spec, spec, spec]
        + [pl.BlockSpec(memory_space=pl.ANY)] * n_prev,
        out_specs=[spec] * 4, out_shape=[jax.ShapeDtypeStruct(w.shape, F32)] * 4,
        input_output_aliases={4 + t: t for t in range(n_prev)},
        compiler_params=_cparams(("parallel",)),
    )(parts, w, m, v, *(prev or ()))


def _pack(arrays):
    flat = jnp.concatenate([a.reshape(-1).astype(F32) for a in arrays])
    n = flat.shape[0]
    rows = -(-n // (8 * LANES)) * 8
    return jnp.pad(flat, (0, rows * LANES - n)).reshape(rows, LANES)


def _unpack(buf, shapes, lead=()):
    flat = buf.reshape(lead + (-1,))
    out, off = [], 0
    for shp in shapes:
        n = math.prod(shp)
        out.append(flat[..., off:off + n].reshape(lead + tuple(shp)))
        off += n
    return out


ADAM_TILE_ELEMS = 1 << 17


def _row_tile(rows, cols):
    want = max(8, ADAM_TILE_ELEMS // max(cols, LANES))
    if rows <= want:
        return rows
    best = None
    for t in range(8, want + 1, 8):
        if rows % t == 0:
            best = t
    assert best is not None, (rows, cols)
    return best


def _local_step(xs, tgt, mods, norm_mix, norm_mlp, fetch, send):
    depth = len(mods)
    mixer_fwd = (_fox_fwd, _sg_fwd, _cv_fwd)
    mixer_bwd = (_fox_bwd, _sg_bwd, _cv_bwd)
    mw, w1, w2 = [None] * depth, [None] * depth, [None] * depth

    nsub = 2 * depth
    sub = []
    x_in = xs
    y_prev = gate_prev = None
    for k in range(nsub):
        i, is_mlp = k // 2, k % 2
        sh, sc = mods[i][3 * is_mlp], mods[i][3 * is_mlp + 1]
        g = (norm_mlp if is_mlp else norm_mix)[i:i + 1]
        wts, token = fetch(k, xs if k == 0 else y_prev)
        g = g + token
        if is_mlp:
            w1[i], w2[i] = wts
        else:
            mw[i] = wts
        if k == 0:
            h = _first_norm(x_in, g, sc, sh)
        else:
            x_in, h = _res_norm(x_in, y_prev, gate_prev, g, sc, sh)
        if is_mlp:
            y, saved = _mlp_fwd(h, w1[i], w2[i])
        else:
            y, saved = mixer_fwd[i % 3](h, mw[i])
        sub.append((x_in, h, y, saved))
        y_prev, gate_prev = y, mods[i][3 * is_mlp + 2]

    loss_part, dxo, dy, dgate = _final_loss(x_in, y_prev, gate_prev, tgt)

    dmods = [[None] * 6 for _ in range(depth)]
    g_norm = {'norm_mix': [None] * depth, 'norm_mlp': [None] * depth}
    g_mix = [None] * depth
    g_w1, g_w2 = [None] * depth, [None] * depth
    for k in reversed(range(nsub)):
        i, is_mlp = k // 2, k % 2
        x_k, h_k, _, saved = sub[k]
        dmods[i][3 * is_mlp + 2] = dgate
        if is_mlp:
            dh, g_w1[i], g_w2[i] = _mlp_bwd(dy, h_k, w1[i], w2[i], saved)
        else:
            dh, g_mix[i] = mixer_bwd[i % 3](dy, h_k, mw[i], saved)
        sc = mods[i][3 * is_mlp + 1]
        g = (norm_mlp if is_mlp else norm_mix)[i:i + 1]
        g = g + send(k, (g_w1[i], g_w2[i]) if is_mlp else g_mix[i])
        if k > 0:
            ip, mp = (k - 1) // 2, (k - 1) % 2
            dxo, dy, (dsh, dsc, dg, dgate) = _bwd_norm_gate(dxo, dh, x_k, sub[k - 1][2], g, sc, mods[ip][3 * mp + 2])
        else:
            dxo, (dsh, dsc, dg) = _bwd_norm_first(dxo, dh, x_k, g, sc)
        dmods[i][3 * is_mlp], dmods[i][3 * is_mlp + 1] = dsh, dsc
        g_norm['norm_mlp' if is_mlp else 'norm_mix'][i] = dg
    return loss_part, dxo, dmods, g_norm, g_mix, g_w1, g_w2


def kernel(x, c, norm_mix, norm_mlp, w_ada, b_ada, w_mlp_in, w_mlp_out, fox_w_in, fox_b_f, fox_q_norm, fox_k_norm, fox_w_out, sg_w_in, sg_ln_g, sg_ln_b, sg_w_s, sg_b_s, sg_w_out, cv_w_pw1, cv_b_pw1, cv_w_dw, cv_b_dw, cv_ln_g, cv_ln_b, cv_w_pw2, cv_b_pw2, loss_target, m_norm_mix, m_norm_mlp, m_w_ada, m_b_ada, m_w_mlp_in, m_w_mlp_out, m_fox_w_in, m_fox_b_f, m_fox_q_norm, m_fox_k_norm, m_fox_w_out, m_sg_w_in, m_sg_ln_g, m_sg_ln_b, m_sg_w_s, m_sg_b_s, m_sg_w_out, m_cv_w_pw1, m_cv_b_pw1, m_cv_w_dw, m_cv_b_dw, m_cv_ln_g, m_cv_ln_b, m_cv_w_pw2, m_cv_b_pw2, v_norm_mix, v_norm_mlp, v_w_ada, v_b_ada, v_w_mlp_in, v_w_mlp_out, v_fox_w_in, v_fox_b_f, v_fox_q_norm, v_fox_k_norm, v_fox_w_out, v_sg_w_in, v_sg_ln_g, v_sg_ln_b, v_sg_w_s, v_sg_b_s, v_sg_w_out, v_cv_w_pw1, v_cv_b_pw1, v_cv_w_dw, v_cv_b_dw, v_cv_ln_g, v_cv_ln_b, v_cv_w_pw2, v_cv_b_pw2):
    P = dict(zip(_ARGS, (x, c, norm_mix, norm_mlp, w_ada, b_ada, w_mlp_in, w_mlp_out, fox_w_in, fox_b_f, fox_q_norm, fox_k_norm, fox_w_out, sg_w_in, sg_ln_g, sg_ln_b, sg_w_s, sg_b_s, sg_w_out, cv_w_pw1, cv_b_pw1, cv_w_dw, cv_b_dw, cv_ln_g, cv_ln_b, cv_w_pw2, cv_b_pw2, loss_target, m_norm_mix, m_norm_mlp, m_w_ada, m_b_ada, m_w_mlp_in, m_w_mlp_out, m_fox_w_in, m_fox_b_f, m_fox_q_norm, m_fox_k_norm, m_fox_w_out, m_sg_w_in, m_sg_ln_g, m_sg_ln_b, m_sg_w_s, m_sg_b_s, m_sg_w_out, m_cv_w_pw1, m_cv_b_pw1, m_cv_w_dw, m_cv_b_dw, m_cv_ln_g, m_cv_ln_b, m_cv_w_pw2, m_cv_b_pw2, v_norm_mix, v_norm_mlp, v_w_ada, v_b_ada, v_w_mlp_in, v_w_mlp_out, v_fox_w_in, v_fox_b_f, v_fox_q_norm, v_fox_k_norm, v_fox_w_out, v_sg_w_in, v_sg_ln_g, v_sg_ln_b, v_sg_w_s, v_sg_b_s, v_sg_w_out, v_cv_w_pw1, v_cv_b_pw1, v_cv_w_dw, v_cv_b_dw, v_cv_ln_g, v_cv_ln_b, v_cv_w_pw2, v_cv_b_pw2)))
    me = 4 * lax.axis_index("x") + 2 * lax.axis_index("y") + lax.axis_index("c")
    xs = x[0]
    tgt = loss_target[0]
    s_len, d = xs.shape
    depth = norm_mix.shape[0]
    bf = lambda a: a.astype(MXU_DTYPE)

    cv_small = ['cv_b_pw1', 'cv_w_dw', 'cv_b_dw', 'cv_ln_g', 'cv_ln_b', 'cv_b_pw2']
    small_shapes = [c.shape] + [P[n].shape for n in cv_small]
    (small_all,) = _exchange("gather_small", [_pack([c] + [P[n] for n in cv_small])], scatter=False)
    sm = dict(zip(['c'] + cv_small, _unpack(small_all, small_shapes, lead=(N_DEV,))))
    c_all = sm['c'][:, 0, :]
    cat_last = lambda a: jnp.moveaxis(a, 0, -2).reshape(a.shape[1:-1] + (-1,))
    cvf = {n: cat_last(sm[n]) for n in cv_small}

    big = ['w_mlp_in', 'w_mlp_out', 'fox_w_in', 'fox_w_out', 'sg_w_in', 'sg_w_out', 'cv_w_pw1', 'cv_w_pw2']
    col_sharded = {'w_mlp_in', 'fox_w_in', 'sg_w_in', 'cv_w_pw1'}
    mixer_names = (('fox_w_in', 'fox_w_out'), ('sg_w_in', 'sg_w_out'), ('cv_w_pw1', 'cv_w_pw2'))
    nsub = 2 * depth
    groups = [[('w_mlp_in', k // 2), ('w_mlp_out', k // 2)] if k % 2 else
              [(nm, k // 6) for nm in mixer_names[(k // 2) % 3]] for k in range(nsub)]
    gather_handles = [None] * nsub
    scatter_handles = [None] * nsub

    c_act = c_all * _sigmoid(c_all)
    c_pad = bf(jnp.pad(c_act, ((0, 16 - N_DEV), (0, 0))))
    n_ada = w_ada.shape[2]
    (mod_part,) = _mm("ada_mod", c_pad, bf(jnp.transpose(w_ada, (1, 0, 2)).reshape(d, depth * n_ada)),
                      epi=lambda acc, b: (acc + b,),
                      vecs=(lax.dynamic_slice_in_dim(b_ada, me * n_ada, n_ada, axis=1).reshape(1, depth * n_ada),),
                      tn=n_ada)
    (mod_all,) = _exchange("gather_mod", [mod_part], scatter=False)
    mod_me = lax.dynamic_index_in_dim(mod_all, me, axis=1, keepdims=False)
    mod = jnp.transpose(mod_me.reshape(N_DEV, depth, n_ada), (1, 0, 2)).reshape(depth, 6 * d)
    mods = [[mod[i:i + 1, k * d:(k + 1) * d] for k in range(6)] for i in range(depth)]

    def mixer_weights(i, full_weight):
        kind, j = i % 3, i // 3
        if kind == 0:
            w_in = full_weight('fox_w_in')
            n_pad = -(-w_in.shape[1] // (5 * LANES)) * (5 * LANES)
            return dict(w_in=_pad_cols(w_in, n_pad), w_out=full_weight('fox_w_out'),
                        b_f=_pad_cols(fox_b_f[j:j + 1], LANES),
                        qg=jnp.tile(fox_q_norm[j:j + 1], (1, 2)), kg=jnp.tile(fox_k_norm[j:j + 1], (1, 2)))
        if kind == 1:
            return dict(w_in=full_weight('sg_w_in'), w_out=full_weight('sg_w_out'),
                        ln_g=sg_ln_g[j:j + 1], ln_b=sg_ln_b[j:j + 1],
                        w_s=sg_w_s[j].reshape(SG_GROUPS * SG_CHUNK, SG_CHUNK), b_st=_pad_cols(sg_b_s[j].T, LANES))
        return dict(w_pw1=full_weight('cv_w_pw1'), w_pw2=full_weight('cv_w_pw2'),
                    b_pw1=cvf['cv_b_pw1'][j:j + 1], b_pw2=cvf['cv_b_pw2'][j:j + 1],
                    w_dw=jnp.pad(cvf['cv_w_dw'][j], ((0, CONV_PAD - CONV_WIDTH), (0, 0))),
                    b_dw=cvf['cv_b_dw'][j:j + 1], ln_g=cvf['cv_ln_g'][j:j + 1], ln_b=cvf['cv_ln_b'][j:j + 1])

    dep = mod_all
    for k in range(nsub):
        gather_handles[k], dep = _exchange_start(f"gather_start_{k}", [bf(P[nm][j]) for nm, j in groups[k]],
                                                 False, dep)
    first_token = dep

    def fetch(k, dep):
        got = _exchange_wait(f"gather_wait_{k}", gather_handles[k], False, dep)
        by_name = {nm: g for (nm, _), g in zip(groups[k], got)}

        def full_weight(name):
            g = by_name[name]
            if name in col_sharded:
                return jnp.transpose(g, (1, 0, 2)).reshape(g.shape[1], -1)
            return g.reshape(-1, g.shape[2])
        wts = (full_weight('w_mlp_in'), full_weight('w_mlp_out')) if k % 2 else mixer_weights(k // 2, full_weight)
        return wts, (first_token if k == 0 else jnp.zeros((1, 1), F32))

    def to_slots(name, g2d):
        if name in col_sharded:
            r = g2d.shape[0]
            return jnp.transpose(g2d.reshape(r, N_DEV, -1), (1, 0, 2))
        return g2d.reshape(N_DEV, -1, g2d.shape[1])

    def send(k, grads, dep=None):
        key = {'fox_w_in': 'w_in', 'fox_w_out': 'w_out', 'sg_w_in': 'w_in', 'sg_w_out': 'w_out',
               'cv_w_pw1': 'w_pw1', 'cv_w_pw2': 'w_pw2'}
        slots = []
        for nm, _ in groups[k]:
            g = grads[0] if nm == 'w_mlp_in' else grads[1] if nm == 'w_mlp_out' else grads[key[nm]]
            if g.ndim == 2:
                g = to_slots(nm, g[:, :P[nm].shape[-1] * N_DEV] if nm in col_sharded else g)
            slots.append(g)
        scatter_handles[k], token = _exchange_start(f"scatter_start_{k}", slots, True, slots[0] if dep is None else dep)
        return token

    send_later = lambda k, grads: jnp.zeros((1, 1), F32) if k == 0 else send(k, grads)
    loss_part, dxo, dmods, g_norm, g_mix, _, _ = _local_step(xs, tgt, mods, norm_mix, norm_mlp, fetch, send_later)
    loss = lax.psum(loss_part, ("x", "y", "c"))
    grad_x = dxo[None]

    stack = lambda key, kind: jnp.stack([g_mix[i][key].reshape(P[name_of[(kind, key)]].shape[1:])
                                         for i in range(depth) if i % 3 == kind])
    name_of = {(0, 'b_f'): 'fox_b_f', (0, 'qg'): 'fox_q_norm', (0, 'kg'): 'fox_k_norm',
               (1, 'ln_g'): 'sg_ln_g', (1, 'ln_b'): 'sg_ln_b', (1, 'w_s'): 'sg_w_s', (1, 'b_s'): 'sg_b_s'}
    dmod_me = jnp.concatenate([jnp.concatenate(r, axis=1) for r in dmods], axis=0)
    small_g = {'dmod': dmod_me,
               'norm_mix': jnp.concatenate(g_norm['norm_mix'], axis=0),
               'norm_mlp': jnp.concatenate(g_norm['norm_mlp'], axis=0)}
    for (kind, key), nm in name_of.items():
        small_g[nm] = stack(key, kind)
    cv_keys = {'cv_b_pw1': 'b_pw1', 'cv_w_dw': 'w_dw', 'cv_b_dw': 'b_dw', 'cv_ln_g': 'ln_g', 'cv_ln_b': 'ln_b',
               'cv_b_pw2': 'b_pw2'}
    for nm, key in cv_keys.items():
        small_g[nm] = jnp.stack([g_mix[i][key].reshape(cvf[nm].shape[1:]) for i in range(depth) if i % 3 == 2])
    sm_names = [n for n in small_g if n != 'dmod']
    dmod_all, sm_parts = _exchange("gather_small_grads", [dmod_me, _pack([small_g[n] for n in sm_names])],
                                   scatter=False)
    last_token = send(0, g_mix[0], sm_parts)

    out = {}

    def finish(name, parts, shard_of=None):
        w, m, v = P[name], P['m_' + name], P['v_' + name]
        cols = w.shape[-1]
        r2 = lambda a: a.reshape(-1, cols)
        rows = r2(w).shape[0]
        res = _adamw("adamw_" + name, parts.reshape(parts.shape[0], rows, cols), r2(w), r2(m), r2(v),
                     _row_tile(rows, cols))
        out[name] = tuple(r.reshape(w.shape) for r in res)

    c_t = c_act.T
    ada_g = []
    for i in range(depth):
        blk = lax.dynamic_slice_in_dim(dmod_all[:, i, :], me * n_ada, n_ada, axis=1)
        ada_g.append(_ada_outer(c_t, blk))
    finish('w_ada', jnp.stack(ada_g)[None])
    finish('b_ada', dmod_all)


    def local_block(nm, a):
        if nm in cv_keys:
            n_loc = P[nm].shape[-1]
            return lax.dynamic_slice_in_dim(a, me * n_loc, n_loc, axis=a.ndim - 1)
        return a
    full_shapes = [small_g[n].shape for n in sm_names]

    def pack_full(prefix):
        arrs = []
        for nm in sm_names:
            a = P[prefix + nm]
            if nm in cv_keys:
                full = jnp.zeros(small_g[nm].shape, F32)
                a = lax.dynamic_update_slice_in_dim(full, a, me * a.shape[-1], axis=a.ndim - 1)
            arrs.append(a)
        return _pack(arrs)
    res = _adamw("adamw_small", sm_parts, pack_full(''), pack_full('m_'), pack_full('v_'),
                 _row_tile(sm_parts.shape[1], LANES))
    unp = [_unpack(r, full_shapes) for r in res]
    for idx, nm in enumerate(sm_names):
        out[nm] = tuple(local_block(nm, unp[t][idx]) for t in range(4))

    stacked = {}
    last = sm_parts[0, :8, :] + last_token
    for k in reversed(range(nsub)):
        recv = _exchange_wait(f"scatter_wait_{k}", scatter_handles[k], True, last)
        for (nm, j), parts in zip(groups[k], recv):
            nl, r, cc = P[nm].shape
            flat = lambda a: a.reshape(nl * r, cc)
            stacked[nm] = _adamw(f"adamw_{nm}_{j}", parts, flat(P[nm]), flat(P['m_' + nm]), flat(P['v_' + nm]),
                                 _row_tile(r, cc), layer=j, prev=stacked.get(nm))
            last = stacked[nm][0]
    for nm in big:
        out[nm] = tuple(a.reshape(P[nm].shape) for a in stacked[nm])

    outs = [loss, grad_x]
    for t in range(4):
        outs += [out[n][t] for n in _WEIGHTS]
    return tuple(outs)
```

```python
import functools
import math

import jax
import jax.numpy as jnp
from jax import lax
from jax.experimental import pallas as pl
from jax.experimental.pallas import tpu as pltpu

F32 = jnp.float32
MXU_DTYPE = jnp.bfloat16
EPS = 1e-6
N_DEV = 8
HEAD_DIM = 64
LANES = 128
CONV_WIDTH = 31
CONV_PAD = 32
SG_CHUNK = 128
SG_BLOCK = 64
SG_GROUPS = 8
SCAN_BLOCK = 256
VMEM_LIMIT = 48 * 1024 * 1024
ATT_BWD_VMEM_LIMIT = 56 * 1024 * 1024
ADAM_LR, ADAM_B1, ADAM_B2, ADAM_EPS, ADAM_WD, ADAM_STEP = 0.001, 0.9, 0.999, 1e-08, 0.01, 10
NEG = -1e30

_WEIGHTS = ['norm_mix', 'norm_mlp', 'w_ada', 'b_ada', 'w_mlp_in', 'w_mlp_out', 'fox_w_in', 'fox_b_f',
            'fox_q_norm', 'fox_k_norm', 'fox_w_out', 'sg_w_in', 'sg_ln_g', 'sg_ln_b', 'sg_w_s', 'sg_b_s',
            'sg_w_out', 'cv_w_pw1', 'cv_b_pw1', 'cv_w_dw', 'cv_b_dw', 'cv_ln_g', 'cv_ln_b', 'cv_w_pw2',
            'cv_b_pw2']
_ARGS = ['x', 'c'] + _WEIGHTS + ['loss_target'] + ['m_' + n for n in _WEIGHTS] + ['v_' + n for n in _WEIGHTS]


def _cparams(sem=None, vmem=VMEM_LIMIT):
    return pltpu.CompilerParams(dimension_semantics=sem, vmem_limit_bytes=vmem)


def _colsum(v):
    return jnp.sum(v, axis=0, keepdims=True)


def _sigmoid(v):
    return 1.0 / (1.0 + jnp.exp(-v))


def _rowwise(name, fn, rows, consts, row_out, red_out, tr):
    n_rows = rows[0].shape[0]
    tr = min(tr, n_rows)
    assert n_rows % tr == 0
    nr, nc, no = len(rows), len(consts), len(row_out)

    def body(*refs):
        ins = [r[...] for r in refs[:nr + nc]]
        outs, reds = fn(*ins)
        out_refs = refs[nr + nc:nr + nc + no]
        red_refs = refs[nr + nc + no:]
        for o_ref, o in zip(out_refs, outs):
            o_ref[...] = o.astype(o_ref.dtype)
        if red_refs:
            @pl.when(pl.program_id(0) == 0)
            def _():
                for r_ref in red_refs:
                    r_ref[...] = jnp.zeros_like(r_ref)
            for r_ref, r in zip(red_refs, reds):
                r_ref[...] += r

    def rspec(a):
        return pl.BlockSpec((tr,) + a.shape[1:], lambda i: (i,) + (0,) * (a.ndim - 1))

    def cspec(shape):
        return pl.BlockSpec(shape, lambda i: (0,) * len(shape))

    out_shape = [jax.ShapeDtypeStruct((n_rows, w), dt) for w, dt in row_out]
    out_shape += [jax.ShapeDtypeStruct(s, F32) for s in red_out]
    out_specs = [pl.BlockSpec((tr, w), lambda i: (i, 0)) for w, _ in row_out] + [cspec(s) for s in red_out]
    res = pl.pallas_call(
        body, name=name, grid=(n_rows // tr,),
        in_specs=[rspec(a) for a in rows] + [cspec(a.shape) for a in consts],
        out_specs=out_specs, out_shape=out_shape,
        compiler_params=_cparams(("arbitrary",)),
    )(*rows, *consts)
    return res[:no], res[no:]


def _mm(name, a, b, *, ta=False, tb=False, out_dtypes=(F32,), epi=None, tiles=(), vecs=(), tm=512, tn=512,
        col_slots=0):
    m_dim, k_dim = (a.shape[1], a.shape[0]) if ta else a.shape
    n_dim = b.shape[0] if tb else b.shape[1]
    assert (b.shape[1] if tb else b.shape[0]) == k_dim
    tm, tn = min(tm, m_dim), min(tn, n_dim)
    assert m_dim % tm == 0 and n_dim % tn == 0, (name, m_dim, n_dim, tm, tn)
    dims = (((0 if ta else 1,), (1 if tb else 0,)), ((), ()))
    nx = len(tiles) + len(vecs)

    def body(a_ref, b_ref, *rest):
        acc = lax.dot_general(a_ref[...], b_ref[...], dims, preferred_element_type=F32)
        outs = epi(acc, *[r[...] for r in rest[:nx]]) if epi is not None else (acc,)
        for o_ref, o in zip(rest[nx:], outs):
            o_ref[...] = o.astype(o_ref.dtype)

    a_spec = pl.BlockSpec((k_dim, tm), lambda i, j: (0, i)) if ta else pl.BlockSpec((tm, k_dim), lambda i, j: (i, 0))
    b_spec = pl.BlockSpec((tn, k_dim), lambda i, j: (j, 0)) if tb else pl.BlockSpec((k_dim, tn), lambda i, j: (0, j))
    t_spec = pl.BlockSpec((tm, tn), lambda i, j: (i, j))
    v_spec = pl.BlockSpec((1, tn), lambda i, j: (0, j))
    if col_slots:
        assert n_dim == col_slots * tn
        o_spec = pl.BlockSpec((None, tm, tn), lambda i, j: (j, i, 0))
        o_shape = (col_slots, m_dim, tn)
    else:
        o_spec, o_shape = t_spec, (m_dim, n_dim)
    res = pl.pallas_call(
        body, name=name, grid=(m_dim // tm, n_dim // tn),
        in_specs=[a_spec, b_spec] + [t_spec] * len(tiles) + [v_spec] * len(vecs),
        out_specs=[o_spec] * len(out_dtypes),
        out_shape=[jax.ShapeDtypeStruct(o_shape, dt) for dt in out_dtypes],
        compiler_params=_cparams(("parallel", "parallel")),
    )(a, b, *tiles, *vecs)
    return res


def _exchange_copies(scatter, in_refs, land_refs, send_sems, recv_sems, local_sems):
    n = len(in_refs)
    x, y, c = lax.axis_index("x"), lax.axis_index("y"), lax.axis_index("c")
    me = 4 * x + 2 * y + c
    local = [pltpu.make_async_copy(in_refs[a].at[me] if scatter else in_refs[a], land_refs[a].at[me],
                                   local_sems.at[a]) for a in range(n)]
    send, arrive = [], []
    for k in range(1, N_DEV):
        px, py, pc = x ^ ((k >> 2) & 1), y ^ ((k >> 1) & 1), c ^ (k & 1)
        peer = 4 * px + 2 * py + pc
        for a in range(n):
            src = in_refs[a].at[peer] if scatter else in_refs[a]
            sems = dict(send_sem=send_sems.at[a * (N_DEV - 1) + k - 1], recv_sem=recv_sems.at[a * (N_DEV - 1) + k - 1],
                        device_id=(px, py, pc), device_id_type=pl.DeviceIdType.MESH)
            send.append(pltpu.make_async_remote_copy(src_ref=src, dst_ref=land_refs[a].at[me], **sems))
            arrive.append(pltpu.make_async_remote_copy(src_ref=src, dst_ref=land_refs[a].at[peer], **sems))
    return local, send, arrive


def _land_shape(a, scatter):
    return ((N_DEV,) + a.shape[1:]) if scatter else ((N_DEV,) + a.shape)


def _exchange(name, arrays, scatter):
    n = len(arrays)

    def body(*refs):
        local, send, arrive = _exchange_copies(scatter, refs[:n], refs[n:2 * n], *refs[2 * n:])
        for cp in local + send:
            cp.start()
        for cp, arr in zip(send, arrive):
            cp.wait_send()
            arr.wait_recv()
        for cp in local:
            cp.wait()

    any_spec = pl.BlockSpec(memory_space=pl.ANY)
    return pl.pallas_call(
        body, name=name,
        in_specs=[any_spec] * n, out_specs=[any_spec] * n,
        out_shape=[jax.ShapeDtypeStruct(_land_shape(a, scatter), a.dtype) for a in arrays],
        scratch_shapes=[pltpu.SemaphoreType.DMA((n * (N_DEV - 1),)),
                        pltpu.SemaphoreType.DMA((n * (N_DEV - 1),)),
                        pltpu.SemaphoreType.DMA((n,))],
        compiler_params=pltpu.CompilerParams(has_side_effects=True),
    )(*arrays)


_HBM = pl.BlockSpec(memory_space=pltpu.HBM)
_SEM = pl.BlockSpec(memory_space=pltpu.SEMAPHORE)
_EFFECT = pltpu.SideEffectType.DATAFLOW_SIDE_EFFECTING


def _exchange_start(name, arrays, scatter, dep):
    n = len(arrays)
    nsem = n * (N_DEV - 1)
    srcs = [pltpu.with_memory_space_constraint(a, pltpu.HBM) for a in arrays]
    lands = [pltpu.with_memory_space_constraint(lax.empty(_land_shape(a, scatter), a.dtype), pltpu.HBM) for a in arrays]

    def body(*refs):
        sems = refs[2 * n + 1:2 * n + 4]
        local, send, _ = _exchange_copies(scatter, refs[:n], refs[n:2 * n], *sems)
        for cp in local + send:
            cp.start()
        token = refs[-1]
        token[...] = jnp.zeros_like(token)

    res = pl.pallas_call(
        body, name=name,
        in_specs=[_HBM] * (2 * n) + [pl.BlockSpec(memory_space=pl.ANY)],
        out_specs=[_SEM] * 3 + [_HBM] * (2 * n) + [pl.BlockSpec(memory_space=pltpu.VMEM)],
        out_shape=[pltpu.SemaphoreType.DMA((nsem,)), pltpu.SemaphoreType.DMA((nsem,)), pltpu.SemaphoreType.DMA((n,))]
        + [pltpu.HBM(a.shape, a.dtype) for a in arrays]
        + [pltpu.HBM(_land_shape(a, scatter), a.dtype) for a in arrays]
        + [jax.ShapeDtypeStruct((8, LANES), F32)],
        input_output_aliases={i: 3 + i for i in range(2 * n)},
        compiler_params=pltpu.CompilerParams(has_side_effects=_EFFECT),
    )(*srcs, *lands, dep)
    return res[:-1], res[-1][0:1, 0:1]


def _exchange_wait(name, handles, scatter, after):
    n = (len(handles) - 3) // 2
    sems, thru = handles[:3], handles[3:]

    def body(*refs):
        local, send, arrive = _exchange_copies(scatter, refs[:n], refs[n:2 * n], *refs[2 * n:2 * n + 3])
        for cp, arr in zip(send, arrive):
            cp.wait_send()
            arr.wait_recv()
        for cp in local:
            cp.wait()

    res = pl.pallas_call(
        body, name=name,
        in_specs=[_HBM] * (2 * n) + [_SEM] * 3 + [pl.BlockSpec(memory_space=pl.ANY)],
        out_specs=[_HBM] * (2 * n),
        out_shape=[pltpu.HBM(t.shape, t.dtype) for t in thru],
        input_output_aliases={i: i for i in range(2 * n)},
        compiler_params=pltpu.CompilerParams(has_side_effects=_EFFECT),
    )(*thru, *sems, after)
    return res[n:]


def _norm_mod(x, g, sc, sh):
    r = lax.rsqrt(jnp.mean(x * x, axis=-1, keepdims=True) + EPS)
    return (x * r * g) * (1.0 + sc) + sh


def _first_norm(x, g, sc, sh):
    (h,), _ = _rowwise("first_norm", lambda x, g, sc, sh: ((_norm_mod(x, g, sc, sh),), ()),
                       [x], [g, sc, sh], [(x.shape[1], MXU_DTYPE)], [], 256)
    return h


def _res_norm(x, y, gate, g, sc, sh):
    def fn(x, y, gate, g, sc, sh):
        xn = x + gate * y
        return (xn, _norm_mod(xn, g, sc, sh)), ()
    (xn, h), _ = _rowwise("res_norm", fn, [x, y], [gate, g, sc, sh],
                          [(x.shape[1], F32), (x.shape[1], MXU_DTYPE)], [], 256)
    return xn, h


def _final_loss(x, y, gate, target):
    d = x.shape[1]

    def fn(x, y, target, gate):
        err = (x + gate * y) - target
        part = jnp.sum(jnp.sum(err * err, axis=-1, keepdims=True), axis=0, keepdims=True) * (0.5 / d)
        dx = err * (1.0 / d)
        return (dx, dx * gate), (jnp.broadcast_to(part, (1, LANES)), _colsum(dx * y))
    (dx, dy), (loss, dgate) = _rowwise("final_loss", fn, [x, y, target], [gate],
                                       [(d, F32), (d, MXU_DTYPE)], [(1, LANES), (1, d)], 256)
    return loss[0, 0], dx, dy, dgate


def _norm_bwd_core(dxo, dh, x, g, sc):
    r = lax.rsqrt(jnp.mean(x * x, axis=-1, keepdims=True) + EPS)
    xn = x * r
    dsh = _colsum(dh)
    dsc = _colsum(dh * (xn * g))
    dyy = dh * (1.0 + sc)
    dg = _colsum(dyy * xn)
    dxn = dyy * g
    dxi = dxo + r * (dxn - xn * jnp.mean(dxn * xn, axis=-1, keepdims=True))
    return dxi, dsh, dsc, dg


def _bwd_norm_gate(dxo, dh, x, y_prev, g, sc, gate_prev):
    d = x.shape[1]

    def fn(dxo, dh, x, y_prev, g, sc, gate_prev):
        dxi, dsh, dsc, dg = _norm_bwd_core(dxo, dh, x, g, sc)
        return (dxi, dxi * gate_prev), (dsh, dsc, dg, _colsum(dxi * y_prev))
    (dxi, dy), reds = _rowwise("bwd_norm_gate", fn, [dxo, dh, x, y_prev], [g, sc, gate_prev],
                               [(d, F32), (d, MXU_DTYPE)], [(1, d)] * 4, 256)
    return dxi, dy, reds


def _bwd_norm_first(dxo, dh, x, g, sc):
    d = x.shape[1]

    def fn(dxo, dh, x, g, sc):
        dxi, dsh, dsc, dg = _norm_bwd_core(dxo, dh, x, g, sc)
        return (dxi,), (dsh, dsc, dg)
    (dxi,), reds = _rowwise("bwd_norm_first", fn, [dxo, dh, x], [g, sc], [(d, F32)], [(1, d)] * 3, 256)
    return dxi, reds


def _mlp_fwd(h, w1, w2):
    def epi(acc):
        r = jnp.maximum(acc, 0.0)
        return acc, r * r
    a, z = _mm("mlp_in", h, w1, out_dtypes=(MXU_DTYPE, MXU_DTYPE), epi=epi, tm=1024, tn=512)
    (out,) = _mm("mlp_out", z, w2, tm=512, tn=512)
    return out, (a, z)


def _mlp_bwd(dy, h, w1, w2, saved):
    a, z = saved

    def epi(acc, a):
        return (acc * (2.0 * jnp.maximum(a.astype(F32), 0.0)),)
    (da,) = _mm("mlp_dz", dy, w2, tb=True, out_dtypes=(MXU_DTYPE,), epi=epi, tiles=(a,), tm=1024, tn=512)
    (dw2,) = _mm("mlp_dw2", z, dy, ta=True, out_dtypes=(MXU_DTYPE,))
    (dw1,) = _mm("mlp_dw1", h, da, ta=True, out_dtypes=(MXU_DTYPE,), tn=da.shape[1] // N_DEV, col_slots=N_DEV)
    (dh,) = _mm("mlp_dh", da, w1, tb=True)
    return dh, dw1, dw2


def _split3(v):
    hi = v.astype(jnp.bfloat16)
    r1 = v - hi.astype(F32)
    mid = r1.astype(jnp.bfloat16)
    lo = (r1 - mid.astype(F32)).astype(jnp.bfloat16)
    return hi, mid, lo


def _tri_matmul(a, b, split_lhs=False):
    dot = functools.partial(jnp.dot, preferred_element_type=F32)
    if split_lhs:
        hi, mid, lo = _split3(a)
        return dot(hi, b) + dot(mid, b) + dot(lo, b)
    hi, mid, lo = _split3(b)
    return dot(a, hi) + dot(a, mid) + dot(a, lo)


def _log_sigmoid(v):
    return jnp.minimum(v, 0.0) - jnp.log(1.0 + jnp.exp(-jnp.abs(v)))


def _gate_fwd(proj, b_pad, col_block, d_model):
    s = proj.shape[0]
    tb = min(SCAN_BLOCK, s)

    def body(f_ref, b_ref, o_ref, carry_s):
        @pl.when(pl.program_id(0) == 0)
        def _():
            carry_s[...] = jnp.zeros_like(carry_s)
        row = lax.broadcasted_iota(jnp.int32, (tb, tb), 0)
        col = lax.broadcasted_iota(jnp.int32, (tb, tb), 1)
        tri = (col <= row).astype(jnp.bfloat16)
        head = lax.broadcasted_iota(jnp.int32, (LANES, d_model), 0)
        lane = lax.broadcasted_iota(jnp.int32, (LANES, d_model), 1)
        spread = jnp.where(lane // HEAD_DIM == head, 1.0, 0.0).astype(jnp.bfloat16)
        lf = _log_sigmoid(f_ref[...] + b_ref[...])
        f = _tri_matmul(tri, lf) + carry_s[...]
        carry_s[...] = f[tb - 1:tb, :]
        o_ref[...] = _tri_matmul(f, spread, split_lhs=True)

    return pl.pallas_call(
        body, name="gate_fwd", grid=(s // tb,),
        in_specs=[pl.BlockSpec((tb, LANES), lambda i: (i, col_block)), pl.BlockSpec((1, LANES), lambda i: (0, 0))],
        out_specs=pl.BlockSpec((tb, d_model), lambda i: (i, 0)),
        out_shape=jax.ShapeDtypeStruct((s, d_model), F32),
        scratch_shapes=[pltpu.VMEM((1, LANES), F32)],
        compiler_params=_cparams(("arbitrary",)),
    )(proj, b_pad)


def _gate_bwd(proj, b_pad, d_f, col_block):
    s = proj.shape[0]
    tb = min(SCAN_BLOCK, s)
    nblk = s // tb

    def body(f_ref, b_ref, d_ref, o_ref, db_ref):
        row = lax.broadcasted_iota(jnp.int32, (tb, tb), 0)
        col = lax.broadcasted_iota(jnp.int32, (tb, tb), 1)
        tri = (col >= row).astype(jnp.bfloat16)

        def step(j, carry):
            acc, db = carry
            i = nblk - 1 - j
            rows = pl.ds(pl.multiple_of(i * tb, tb), tb)
            dlf = _tri_matmul(tri, d_ref[rows, :]) + acc
            dpre = dlf * _sigmoid(-(f_ref[rows, :] + b_ref[...]))
            o_ref[rows, :] = dpre.astype(o_ref.dtype)
            return dlf[0:1, :], db + _colsum(dpre)
        _, db = lax.fori_loop(0, nblk, step, (jnp.zeros((1, LANES), F32), jnp.zeros((1, LANES), F32)))
        db_ref[...] = db

    return pl.pallas_call(
        body, name="gate_bwd", grid=(1,),
        in_specs=[pl.BlockSpec((s, LANES), lambda i: (0, col_block)), pl.BlockSpec((1, LANES), lambda i: (0, 0)),
                  pl.BlockSpec((s, LANES), lambda i: (0, 0))],
        out_specs=[pl.BlockSpec((s, LANES), lambda i: (0, 0)), pl.BlockSpec((1, LANES), lambda i: (0, 0))],
        out_shape=[jax.ShapeDtypeStruct((s, LANES), MXU_DTYPE), jax.ShapeDtypeStruct((1, LANES), F32)],
        compiler_params=_cparams(("arbitrary",)),
    )(proj, b_pad, d_f)


def _head_masks():
    lane = lax.broadcasted_iota(jnp.int32, (1, LANES), 1)
    return lane < HEAD_DIM


def _pair_norm(v, g, first):
    v2 = v * v
    ss0 = jnp.sum(jnp.where(first, v2, 0.0), axis=-1, keepdims=True)
    ss1 = jnp.sum(jnp.where(first, 0.0, v2), axis=-1, keepdims=True)
    r = jnp.where(first, lax.rsqrt(ss0 * (1.0 / HEAD_DIM) + EPS), lax.rsqrt(ss1 * (1.0 / HEAD_DIM) + EPS))
    vn = v * r
    return vn * g, vn, r


_NT = (((1,), (1,)), ((), ()))
_TN = (((0,), (0,)), ((), ()))

ATT_TQ = 512
ATT_FWD_TQ = 1024
ATT_TK = 256
AUG_F, AUG_ONE = 0, 3


def _own_lanes(hd):
    lane = lax.broadcasted_iota(jnp.int32, (1, LANES), 1)
    return (lane < HEAD_DIM) if hd == 0 else (lane >= HEAD_DIM)


def _aug_lanes(hd, f_other):
    lane = lax.broadcasted_iota(jnp.int32, (1, LANES), 1) - (HEAD_DIM if hd == 0 else 0)
    hi, mid, lo = [t.astype(F32) for t in _split3(f_other)]
    zero = jnp.zeros_like(f_other)
    f_terms = jnp.where(lane == 0, hi, jnp.where(lane == 1, mid, jnp.where(lane == 2, lo, zero)))
    f_shift = jnp.where(lane == 3, hi, jnp.where(lane == 4, mid, jnp.where(lane == 5, lo, zero)))
    ones_lo = jnp.where(lane < 3, 1.0, 0.0) * jnp.where(lane >= 0, 1.0, 0.0)
    ones_hi = jnp.where(lane < 6, 1.0, 0.0) * jnp.where(lane >= 3, 1.0, 0.0)
    return f_terms + ones_hi, ones_lo - f_shift


def _attn_operands(q_raw, k_raw, f_rep, qg, kg, scale):
    first = _head_masks()
    qn, _, _ = _pair_norm(q_raw, qg, first)
    kn, _, _ = _pair_norm(k_raw, kg, first)
    f_other = pltpu.roll(f_rep, HEAD_DIM, 1)
    out = []
    for hd in range(2):
        own = _own_lanes(hd)
        q_x, k_x = _aug_lanes(hd, f_other)
        out.append((jnp.where(own, qn * scale, q_x), jnp.where(own, kn, k_x)))
    return out


def _set_cols(a, b, c0):
    return b if c0 == 0 else jnp.concatenate([a[:, :c0], b], axis=1)


def _add_cols(a, b, c0):
    return _set_cols(a, a[:, c0:] + b, c0)


def _causal_t(tk, tq, off):
    r = lax.broadcasted_iota(jnp.int32, (tk, tq), 0)
    c = lax.broadcasted_iota(jnp.int32, (tk, tq), 1)
    return (r - c) <= off


def _big(shape, index_map):
    return pl.BlockSpec(shape, index_map, pipeline_mode=pl.Buffered(1))


def _attn_fwd_t(proj, f_rep, qg, kg, d_model):
    s = proj.shape[0]
    pairs = d_model // LANES
    tq, tk = min(ATT_FWD_TQ, s), min(ATT_TK, s)
    assert tq % tk == 0 and s % tq == 0
    nq = s // tq
    n_diag = tq // tk
    scale = HEAD_DIM ** -0.5
    ch = tk

    def body(q_ref, k_ref, v_ref, frep_ref, qg_ref, kg_ref, o_ref, lse_ref, qt_s, k_s, vt_s):
        for ci in range(s // ch):
            rows = pl.ds(ci * ch, ch)
            ops = _attn_operands(q_ref[rows, :], k_ref[rows, :], frep_ref[rows, :], qg_ref[...], kg_ref[...], scale)
            vv = v_ref[rows, :]
            for hd in range(2):
                own = _own_lanes(hd)
                lane = lax.broadcasted_iota(jnp.int32, (1, LANES), 1)
                one_lane = lane == (HEAD_DIM if hd == 0 else 0)
                qt_s[hd, :, rows] = ops[hd][0].T.astype(qt_s.dtype)
                k_s[hd, rows, :] = ops[hd][1].astype(k_s.dtype)
                vt_s[hd, :, rows] = jnp.where(own, vv, jnp.where(one_lane, 1.0, 0.0)).T.astype(vt_s.dtype)

        def q_block(qi, _):
            q0 = pl.multiple_of(qi * tq, tq)
            qcols = pl.ds(q0, tq)
            nfull = q0 // tk
            qts = [qt_s[hd, :, qcols] for hd in range(2)]

            def krows(kj):
                return pl.ds(pl.multiple_of(kj * tk, tk), tk)

            def scores(hd, kj):
                return jnp.dot(k_s[hd, krows(kj), :], qts[hd], preferred_element_type=F32)

            def kv_step(kj, carry):
                new = []
                for hd in range(2):
                    m, acc, p_prev = carry[hd]
                    st = scores(hd, kj)
                    pv = jnp.dot(vt_s[hd, :, krows(jnp.maximum(kj - 1, 0))], p_prev, preferred_element_type=F32)
                    m_new = jnp.maximum(m, jnp.max(st, axis=0, keepdims=True))
                    p = jnp.exp(st - m_new).astype(vt_s.dtype)
                    new.append((m_new, jnp.exp(m - m_new) * (acc + pv), p))
                return tuple(new)

            def diag_step(t, carry, last):
                c0, kj = t * tk, nfull + t
                new = []
                for hd in range(2):
                    m, acc, p_prev, c_prev = carry[hd]
                    pv = jnp.dot(vt_s[hd, :, krows(jnp.maximum(kj - 1, 0))], p_prev, preferred_element_type=F32)
                    acc = _add_cols(acc, pv, c_prev)
                    st = jnp.dot(k_s[hd, krows(kj), :], qts[hd][:, c0:], preferred_element_type=F32)
                    st = jnp.where(_causal_t(tk, tq - c0, 0), st, NEG)
                    m_new = jnp.maximum(m[:, c0:], jnp.max(st, axis=0, keepdims=True))
                    p = jnp.exp(st - m_new).astype(vt_s.dtype)
                    acc = _set_cols(acc, jnp.exp(m[:, c0:] - m_new) * acc[:, c0:], c0)
                    m = _set_cols(m, m_new, c0)
                    if last:
                        pv = jnp.dot(vt_s[hd, :, krows(kj)], p, preferred_element_type=F32)
                        new.append((m, _add_cols(acc, pv, c0)))
                    else:
                        new.append((m, acc, p, c0))
                return tuple(new)

            init = tuple((jnp.full((1, tq), NEG, F32), jnp.zeros((LANES, tq), F32),
                          jnp.zeros((tk, tq), vt_s.dtype)) for hd in range(2))
            carry = lax.fori_loop(0, nfull, kv_step, init)
            carry = tuple(cr + (0,) for cr in carry)
            for t in range(n_diag):
                carry = diag_step(t, carry, t == n_diag - 1)
            o_parts, lse_parts = [], []
            for hd, (m, acc) in enumerate(carry):
                e0 = HEAD_DIM if hd == 0 else 0
                l = acc[e0:e0 + 1, :]
                o_parts.append((acc / l).T)
                lse_parts.append(m + jnp.log(l))
            o_ref[pl.ds(q0, tq), :] = jnp.where(_head_masks(), o_parts[0], o_parts[1]).astype(o_ref.dtype)
            lse_ref[0, :, qcols] = jnp.concatenate(lse_parts, axis=0)
            return 0
        lax.fori_loop(0, nq, q_block, 0)

    blk = lambda off: _big((s, LANES), lambda h: (0, off + h))
    vec = pl.BlockSpec((1, LANES), lambda h: (0, 0))
    return pl.pallas_call(
        body, name="attn_fwd", grid=(pairs,),
        in_specs=[blk(0), blk(pairs), blk(2 * pairs), blk(0), vec, vec],
        out_specs=[pl.BlockSpec((s, LANES), lambda h: (0, h)), pl.BlockSpec((1, 2, s), lambda h: (h, 0, 0))],
        out_shape=[jax.ShapeDtypeStruct((s, d_model), MXU_DTYPE), jax.ShapeDtypeStruct((pairs, 2, s), F32)],
        scratch_shapes=[pltpu.VMEM((2, LANES, s), MXU_DTYPE), pltpu.VMEM((2, s, LANES), MXU_DTYPE),
                        pltpu.VMEM((2, LANES, s), MXU_DTYPE)],
        compiler_params=_cparams(("arbitrary",)),
    )(proj, proj, proj, f_rep, qg, kg)


def _attn_bwd_t(proj, do, o, lse, f_rep, qg, kg, d_model):
    s = proj.shape[0]
    pairs = d_model // LANES
    tq, tk = min(ATT_TQ, s), min(ATT_TK, s)
    assert tq % tk == 0 and s % tq == 0
    nq = s // tq
    n_diag = tq // tk
    scale = HEAD_DIM ** -0.5
    ch = tk

    def norm_bwd(raw, g, dn, first):
        _, xn, r = _pair_norm(raw, g, first)
        dxn = dn * g
        t = dxn * xn
        mu0 = jnp.sum(jnp.where(first, t, 0.0), axis=-1, keepdims=True)
        mu1 = jnp.sum(jnp.where(first, 0.0, t), axis=-1, keepdims=True)
        mu = jnp.where(first, mu0, mu1) * (1.0 / HEAD_DIM)
        return r * (dxn - xn * mu), _colsum(dn * xn)

    def body(q_ref, k_ref, v_ref, do_ref, o_ref, lse_ref, frep_ref, qg_ref, kg_ref,
             dq_ref, dk_ref, dv_ref, df_ref, dqg_ref, dkg_ref,
             q_s, qt_s, k_s, kt_s, v_s, do_s, dot_s, dl_s, dk_s, dv_s):
        first = _head_masks()
        hp = pl.program_id(0)
        lane = lax.broadcasted_iota(jnp.int32, (1, LANES), 1)
        for ci in range(s // ch):
            rows = pl.ds(ci * ch, ch)
            ops = _attn_operands(q_ref[rows, :], k_ref[rows, :], frep_ref[rows, :], qg_ref[...], kg_ref[...], scale)
            v_s[rows, :] = v_ref[rows, :].astype(v_s.dtype)
            dov = do_ref[rows, :].astype(F32)
            ot = o_ref[rows, :].astype(F32).T
            for hd in range(2):
                own = _own_lanes(hd)
                q_s[hd, rows, :] = ops[hd][0].astype(q_s.dtype)
                qt_s[hd, :, rows] = ops[hd][0].T.astype(qt_s.dtype)
                k_s[hd, rows, :] = ops[hd][1].astype(k_s.dtype)
                kt_s[hd, :, rows] = ops[hd][1].T.astype(kt_s.dtype)
                doh = jnp.where(own, dov, 0.0)
                do_s[hd, rows, :] = doh.astype(do_s.dtype)
                doht = doh.T
                dot_s[hd, :, rows] = doht.astype(dot_s.dtype)
                dl_s[hd:hd + 1, rows] = jnp.sum(doht * ot, axis=0, keepdims=True)
        dk_s[...] = jnp.zeros_like(dk_s)
        dv_s[...] = jnp.zeros_like(dv_s)

        @pl.when(hp == 0)
        def _():
            df_ref[...] = jnp.zeros_like(df_ref)

        def q_block(qi, dqg):
            q0 = pl.multiple_of(qi * tq, tq)
            qcols = pl.ds(q0, tq)
            qrows = pl.ds(q0, tq)
            nfull = q0 // tk
            qts = [qt_s[hd, :, qcols] for hd in range(2)]
            dots = [dot_s[hd, :, qcols] for hd in range(2)]
            qns = [q_s[hd, qrows, :] for hd in range(2)]
            dons = [do_s[hd, qrows, :] for hd in range(2)]
            lse_r = [lse_ref[0, hd:hd + 1, qcols] for hd in range(2)]
            dl_r = [dl_s[hd:hd + 1, qcols] for hd in range(2)]
            bdt = qt_s.dtype

            def krows(kj):
                return pl.ds(pl.multiple_of(kj * tk, tk), tk)

            def scores(hd, kj):
                return (jnp.dot(k_s[hd, krows(kj), :], qts[hd], preferred_element_type=F32),
                        jnp.dot(v_s[krows(kj), :], dots[hd], preferred_element_type=F32))

            def products(hd, rows, ds, p, dqt, c0=0):
                dk_s[hd, rows, :] += jnp.dot(ds, qns[hd][c0:, :], preferred_element_type=F32)
                dv_s[rows, :] += jnp.dot(p, dons[hd][c0:, :], preferred_element_type=F32)
                return _add_cols(dqt, jnp.dot(kt_s[hd, :, rows], ds, preferred_element_type=F32), c0)

            def kv_step(kj, carry):
                new = []
                for hd in range(2):
                    dqt, ds_prev, p_prev = carry[hd]
                    st, dp = scores(hd, kj)
                    dqt = products(hd, krows(jnp.maximum(kj - 1, 0)), ds_prev, p_prev, dqt)
                    p = jnp.exp(st - lse_r[hd])
                    new.append((dqt, (p * (dp - dl_r[hd])).astype(bdt), p.astype(bdt)))
                return tuple(new)

            def diag_step(t, carry, last):
                c0, kj = t * tk, nfull + t
                new = []
                for hd in range(2):
                    dqt, ds_prev, p_prev, c_prev = carry[hd]
                    st = jnp.dot(k_s[hd, krows(kj), :], qts[hd][:, c0:], preferred_element_type=F32)
                    dp = jnp.dot(v_s[krows(kj), :], dots[hd][:, c0:], preferred_element_type=F32)
                    dqt = products(hd, krows(jnp.maximum(kj - 1, 0)), ds_prev, p_prev, dqt, c_prev)
                    st = jnp.where(_causal_t(tk, tq - c0, 0), st, NEG)
                    p = jnp.exp(st - lse_r[hd][:, c0:])
                    ds = (p * (dp - dl_r[hd][:, c0:])).astype(bdt)
                    if last:
                        new.append(products(hd, krows(kj), ds, p.astype(bdt), dqt, c0))
                    else:
                        new.append((dqt, ds, p.astype(bdt), c0))
                return tuple(new)

            init = tuple((jnp.zeros((LANES, tq), F32), jnp.zeros((tk, tq), bdt), jnp.zeros((tk, tq), bdt))
                         for hd in range(2))
            carry = lax.fori_loop(0, nfull, kv_step, init)
            carry = tuple(cr + (0,) for cr in carry)
            for t in range(n_diag):
                carry = diag_step(t, carry, t == n_diag - 1)
            dq_parts = [dqt.T for dqt in carry]
            rs0 = dq_parts[0][:, HEAD_DIM + AUG_F:HEAD_DIM + AUG_F + 1]
            rs1 = dq_parts[1][:, AUG_F:AUG_F + 1]
            df_ref[qrows, :] += jnp.where(lane == 2 * hp, rs0, 0.0) + jnp.where(lane == 2 * hp + 1, rs1, 0.0)
            dqn = jnp.where(first, dq_parts[0], dq_parts[1]) * scale
            dq_raw, dg = norm_bwd(q_ref[qrows, :], qg_ref[...], dqn, first)
            dq_ref[qrows, :] = dq_raw.astype(dq_ref.dtype)
            return dqg + dg
        dqg_ref[0] = lax.fori_loop(0, nq, q_block, jnp.zeros((1, LANES), F32))

        dkg = jnp.zeros((1, LANES), F32)
        for ci in range(s // ch):
            rows = pl.ds(ci * ch, ch)
            dk0, dk1 = dk_s[0, rows, :], dk_s[1, rows, :]
            cs0 = dk0[:, HEAD_DIM + AUG_ONE:HEAD_DIM + AUG_ONE + 1]
            cs1 = dk1[:, AUG_ONE:AUG_ONE + 1]
            df_ref[rows, :] -= jnp.where(lane == 2 * hp, cs0, 0.0) + jnp.where(lane == 2 * hp + 1, cs1, 0.0)
            dk_raw, dg = norm_bwd(k_ref[rows, :], kg_ref[...], jnp.where(first, dk0, dk1), first)
            dk_ref[rows, :] = dk_raw.astype(dk_ref.dtype)
            dkg = dkg + dg
            dv_ref[rows, :] = dv_s[rows, :].astype(dv_ref.dtype)
        dkg_ref[0] = dkg

    blk = lambda off: _big((s, LANES), lambda h: (0, off + h))
    outb = pl.BlockSpec((s, LANES), lambda h: (0, h))
    vec = pl.BlockSpec((1, LANES), lambda h: (0, 0))
    gout = pl.BlockSpec((1, 1, LANES), lambda h: (h, 0, 0))
    act = jax.ShapeDtypeStruct((s, d_model), MXU_DTYPE)
    gsh = jax.ShapeDtypeStruct((pairs, 1, LANES), F32)
    pair_rows = pltpu.VMEM((2, s, LANES), MXU_DTYPE)
    pair_cols = pltpu.VMEM((2, LANES, s), MXU_DTYPE)
    return pl.pallas_call(
        body, name="attn_bwd", grid=(pairs,),
        in_specs=[blk(0), blk(pairs), blk(2 * pairs), blk(0), blk(0),
                  pl.BlockSpec((1, 2, s), lambda h: (h, 0, 0)), blk(0), vec, vec],
        out_specs=[outb, outb, outb, pl.BlockSpec((s, LANES), lambda h: (0, 0)), gout, gout],
        out_shape=[act, act, act, jax.ShapeDtypeStruct((s, LANES), F32), gsh, gsh],
        scratch_shapes=[pair_rows, pair_cols, pair_rows, pair_cols, pltpu.VMEM((s, LANES), MXU_DTYPE),
                        pair_rows, pair_cols, pltpu.VMEM((8, s), F32),
                        pltpu.VMEM((2, s, LANES), F32), pltpu.VMEM((s, LANES), F32)],
        compiler_params=_cparams(("arbitrary",), ATT_BWD_VMEM_LIMIT),
    )(proj, proj, proj, do, o, lse, f_rep, qg, kg)


def _pad_cols(a, n):
    return jnp.pad(a, ((0, 0), (0, n - a.shape[1])))


def _fox_fwd(h, w):
    d = h.shape[1]
    pairs = d // LANES
    (proj,) = _mm("fox_in", h, w["w_in"], tm=1024, tn=640)
    f_rep = _gate_fwd(proj, w["b_f"], 3 * pairs, d)
    o, lse = _attn_fwd_t(proj, f_rep, w["qg"], w["kg"], d)
    (y,) = _mm("fox_out", o, w["w_out"])
    return y, (proj, f_rep, o, lse)


def _fox_bwd(dy, h, w, saved):
    proj, f_rep, o, lse = saved
    s, d = h.shape
    pairs = d // LANES
    (do,) = _mm("fox_do", dy, w["w_out"], tb=True, out_dtypes=(MXU_DTYPE,))
    (dw_out,) = _mm("fox_dwout", o, dy, ta=True, out_dtypes=(MXU_DTYPE,))
    dq, dk, dv, d_f, dqg, dkg = _attn_bwd_t(proj, do, o, lse, f_rep, w["qg"], w["kg"], d)
    dfpre, db_f = _gate_bwd(proj, w["b_f"], d_f, 3 * pairs)
    dproj = jnp.concatenate([dq, dk, dv, dfpre], axis=1)
    (dw_in,) = _mm("fox_dwin", h, dproj, ta=True, out_dtypes=(MXU_DTYPE,), tn=640)
    (dh,) = _mm("fox_dh", dproj, w["w_in"], tb=True)
    fold = lambda g: jnp.sum(g, axis=(0, 1)).reshape(2, HEAD_DIM).sum(axis=0)
    return dh, dict(w_in=dw_in, w_out=dw_out, b_f=db_f[0, :d // HEAD_DIM], qg=fold(dqg), kg=fold(dkg))


_GELU_C = math.sqrt(2.0 / math.pi)


def _gelu(v):
    return 0.5 * v * (1.0 + jnp.tanh(_GELU_C * (v + 0.044715 * v * v * v)))


def _gelu_grad(v):
    t = jnp.tanh(_GELU_C * (v + 0.044715 * v * v * v))
    return 0.5 * (1.0 + t) + 0.5 * v * (1.0 - t * t) * (_GELU_C * (1.0 + 3.0 * 0.044715 * v * v))


def _ln_stats(v):
    mu = jnp.mean(v, axis=-1, keepdims=True)
    vc = v - mu
    r = lax.rsqrt(jnp.mean(vc * vc, axis=-1, keepdims=True) + EPS)
    return vc * r, r


def _sg_mask():
    t = lax.broadcasted_iota(jnp.int32, (SG_CHUNK, SG_CHUNK), 0) // SG_BLOCK
    sidx = lax.broadcasted_iota(jnp.int32, (SG_CHUNK, SG_CHUNK), 1) // SG_BLOCK
    return sidx <= t


def _sgu_fwd(uv_pre, ln_g, ln_b, w_s, b_st):
    s, w2 = uv_pre.shape
    wd = w2 // 2
    tr = min(256, s)

    def fn(uv_pre, ln_g, ln_b, w_s, b_st):
        uv = _gelu(uv_pre)
        u = uv[:, :wd]
        vh, _ = _ln_stats(uv[:, wd:])
        vl = (vh * ln_g + ln_b).astype(MXU_DTYPE)
        mask = _sg_mask()
        cols = []
        for g in range(SG_GROUPS):
            wg = jnp.where(mask, w_s[g * SG_CHUNK:(g + 1) * SG_CHUNK, :], 0.0).astype(MXU_DTYPE)
            parts = []
            for ci in range(tr // SG_CHUNK):
                vt = vl[ci * SG_CHUNK:(ci + 1) * SG_CHUNK, g * SG_CHUNK:(g + 1) * SG_CHUNK]
                parts.append(jnp.dot(wg, vt, preferred_element_type=F32) + b_st[:, g:g + 1])
            cols.append(jnp.concatenate(parts, axis=0) if len(parts) > 1 else parts[0])
        vout = jnp.concatenate(cols, axis=1)
        return (u * vout,), ()
    (m,), _ = _rowwise("sgu_fwd", fn, [uv_pre], [ln_g, ln_b, w_s, b_st], [(wd, MXU_DTYPE)], [], tr)
    return m


def _sgu_bwd(uv_pre, dm, ln_g, ln_b, w_s, b_st):
    s, w2 = uv_pre.shape
    wd = w2 // 2
    tr = min(256, s)

    def fn(uv_pre, dm, ln_g, ln_b, w_s, b_st):
        uv = _gelu(uv_pre)
        u = uv[:, :wd]
        vh, r = _ln_stats(uv[:, wd:])
        vl = (vh * ln_g + ln_b).astype(MXU_DTYPE)
        mask = _sg_mask()
        lane = lax.broadcasted_iota(jnp.int32, (1, LANES), 1)
        cols, dcols, dws, dbs = [], [], [], jnp.zeros((SG_CHUNK, LANES), F32)
        for g in range(SG_GROUPS):
            wg = jnp.where(mask, w_s[g * SG_CHUNK:(g + 1) * SG_CHUNK, :], 0.0).astype(MXU_DTYPE)
            parts, dparts = [], []
            dwg = jnp.zeros((SG_CHUNK, SG_CHUNK), F32)
            dbg = jnp.zeros((SG_CHUNK, 1), F32)
            for ci in range(tr // SG_CHUNK):
                rs = slice(ci * SG_CHUNK, (ci + 1) * SG_CHUNK)
                cs = slice(g * SG_CHUNK, (g + 1) * SG_CHUNK)
                vt = vl[rs, cs]
                parts.append(jnp.dot(wg, vt, preferred_element_type=F32) + b_st[:, g:g + 1])
                dvo = dm[rs, cs] * u[rs, cs]
                dvob = dvo.astype(MXU_DTYPE)
                dparts.append(lax.dot_general(wg, dvob, _TN, preferred_element_type=F32))
                dwg = dwg + lax.dot_general(dvob, vt, _NT, preferred_element_type=F32)
                dbg = dbg + jnp.sum(dvo, axis=-1, keepdims=True)
            cols.append(jnp.concatenate(parts, axis=0) if len(parts) > 1 else parts[0])
            dcols.append(jnp.concatenate(dparts, axis=0) if len(dparts) > 1 else dparts[0])
            dws.append(jnp.where(mask, dwg, 0.0))
            dbs = dbs + jnp.where(lane == g, dbg, 0.0)
        vout = jnp.concatenate(cols, axis=1)
        dvl = jnp.concatenate(dcols, axis=1)
        du = dm * vout
        dlg = _colsum(dvl * vh)
        dlb = _colsum(dvl)
        dvh = dvl * ln_g
        dv = r * (dvh - jnp.mean(dvh, axis=-1, keepdims=True) - vh * jnp.mean(dvh * vh, axis=-1, keepdims=True))
        dpre = jnp.concatenate([du, dv], axis=1) * _gelu_grad(uv_pre)
        return (dpre,), (dlg, dlb, jnp.concatenate(dws, axis=0), dbs)
    (dpre,), reds = _rowwise("sgu_bwd", fn, [uv_pre, dm], [ln_g, ln_b, w_s, b_st], [(w2, MXU_DTYPE)],
                             [(1, wd), (1, wd), (SG_GROUPS * SG_CHUNK, SG_CHUNK), (SG_CHUNK, LANES)], tr)
    return dpre, reds


def _sg_fwd(h, w):
    (uv_pre,) = _mm("sg_in", h, w["w_in"], tm=1024, tn=512)
    m = _sgu_fwd(uv_pre, w["ln_g"], w["ln_b"], w["w_s"], w["b_st"])
    (y,) = _mm("sg_out", m, w["w_out"])
    return y, (uv_pre, m)


def _sg_bwd(dy, h, w, saved):
    uv_pre, m = saved
    (dm,) = _mm("sg_dm", dy, w["w_out"], tb=True)
    (dw_out,) = _mm("sg_dwout", m, dy, ta=True, out_dtypes=(MXU_DTYPE,))
    dpre, (dlg, dlb, dws, dbs) = _sgu_bwd(uv_pre, dm, w["ln_g"], w["ln_b"], w["w_s"], w["b_st"])
    (dw_in,) = _mm("sg_dwin", h, dpre, ta=True, out_dtypes=(MXU_DTYPE,), tn=dpre.shape[1] // N_DEV, col_slots=N_DEV)
    (dh,) = _mm("sg_dh", dpre, w["w_in"], tb=True)
    return dh, dict(w_in=dw_in, w_out=dw_out, ln_g=dlg, ln_b=dlb, w_s=dws, b_s=dbs[:, :SG_GROUPS].T)


def _conv_fwd_kernel(ypad, w_dw, b_dw):
    s = ypad.shape[0] - CONV_PAD
    d = ypad.shape[1]
    tt = min(256, s)
    ext = tt + CONV_PAD

    def body(y_ref, w_ref, b_ref, o_ref):
        def chunk(ci, _):
            base = pl.multiple_of(ci * tt, tt)
            e = y_ref[pl.ds(base, ext), :]
            acc = jnp.zeros((tt, LANES), F32) + b_ref[...]
            for j in range(CONV_WIDTH):
                sh = pltpu.roll(e, ext - (CONV_PAD - CONV_WIDTH + 1 + j), 0)[:tt, :]
                acc = acc + w_ref[j:j + 1, :] * sh
            o_ref[pl.ds(base, tt), :] = acc
            return 0
        lax.fori_loop(0, s // tt, chunk, 0)

    return pl.pallas_call(
        body, name="conv_fwd", grid=(d // LANES,),
        in_specs=[pl.BlockSpec((s + CONV_PAD, LANES), lambda i: (0, i)),
                  pl.BlockSpec((CONV_PAD, LANES), lambda i: (0, i)), pl.BlockSpec((1, LANES), lambda i: (0, i))],
        out_specs=pl.BlockSpec((s, LANES), lambda i: (0, i)),
        out_shape=jax.ShapeDtypeStruct((s, d), F32),
        compiler_params=_cparams(("parallel",)),
    )(ypad, w_dw, b_dw)


def _conv_bwd_kernel(ypad, dpad, w_dw):
    s = ypad.shape[0] - CONV_PAD
    d = ypad.shape[1]
    tt = min(256, s)
    ext = tt + CONV_PAD

    def body(y_ref, d_ref, w_ref, o_ref, dw_ref):
        dw_ref[...] = jnp.zeros_like(dw_ref)

        def chunk(ci, _):
            base = pl.multiple_of(ci * tt, tt)
            ye = y_ref[pl.ds(base, ext), :]
            de = d_ref[pl.ds(base, ext), :]
            dcur = de[:tt, :]
            acc = jnp.zeros((tt, LANES), F32)
            for j in range(CONV_WIDTH):
                back = CONV_WIDTH - 1 - j
                dsh = dcur if back == 0 else pltpu.roll(de, ext - back, 0)[:tt, :]
                acc = acc + w_ref[j:j + 1, :] * dsh
                ysh = pltpu.roll(ye, ext - (CONV_PAD - CONV_WIDTH + 1 + j), 0)[:tt, :]
                dw_ref[j:j + 1, :] += _colsum(dcur * ysh)
            o_ref[pl.ds(base, tt), :] = acc
            return 0
        lax.fori_loop(0, s // tt, chunk, 0)

    return pl.pallas_call(
        body, name="conv_bwd", grid=(d // LANES,),
        in_specs=[pl.BlockSpec((s + CONV_PAD, LANES), lambda i: (0, i)),
                  pl.BlockSpec((s + CONV_PAD, LANES), lambda i: (0, i)),
                  pl.BlockSpec((CONV_PAD, LANES), lambda i: (0, i))],
        out_specs=[pl.BlockSpec((s, LANES), lambda i: (0, i)), pl.BlockSpec((CONV_PAD, LANES), lambda i: (0, i))],
        out_shape=[jax.ShapeDtypeStruct((s, d), F32), jax.ShapeDtypeStruct((CONV_PAD, d), F32)],
        compiler_params=_cparams(("parallel",)),
    )(ypad, dpad, w_dw)


def _cv_fwd(h, w):
    d = h.shape[1]
    (y1,) = _mm("cv_pw1", h, w["w_pw1"], tm=1024, tn=512)

    def glu(y1, b1):
        t = y1 + b1
        return (t[:, :d] * _sigmoid(t[:, d:]),), ()
    (y2,), _ = _rowwise("cv_glu", glu, [y1], [w["b_pw1"]], [(d, F32)], [], 256)
    y3 = _conv_fwd_kernel(jnp.pad(y2, ((CONV_PAD, 0), (0, 0))), w["w_dw"], w["b_dw"])

    def lnsilu(y3, g, b):
        vh, _ = _ln_stats(y3)
        y4 = vh * g + b
        return (y4 * _sigmoid(y4),), ()
    (y5,), _ = _rowwise("cv_lnsilu", lnsilu, [y3], [w["ln_g"], w["ln_b"]], [(d, MXU_DTYPE)], [], 256)
    (y,) = _mm("cv_pw2", y5, w["w_pw2"], epi=lambda acc, b: (acc + b,), vecs=(w["b_pw2"],))
    return y, (y1, y2, y3, y5)


def _cv_bwd(dy, h, w, saved):
    y1, y2, y3, y5 = saved
    d = h.shape[1]
    (dy5,) = _mm("cv_dy5", dy, w["w_pw2"], tb=True)
    (dw_pw2,) = _mm("cv_dwpw2", y5, dy, ta=True, out_dtypes=(MXU_DTYPE,))

    def ln_bwd(dy5, y3, dyb, g, b):
        vh, r = _ln_stats(y3)
        y4 = vh * g + b
        sg = _sigmoid(y4)
        dy4 = dy5 * (sg * (1.0 + y4 * (1.0 - sg)))
        dvh = dy4 * g
        dy3 = r * (dvh - jnp.mean(dvh, axis=-1, keepdims=True) - vh * jnp.mean(dvh * vh, axis=-1, keepdims=True))
        return (dy3,), (_colsum(dy4 * vh), _colsum(dy4), _colsum(dy3), _colsum(dyb.astype(F32)))
    (dy3,), (dlg, dlb, db_dw, db_pw2) = _rowwise("cv_ln_bwd", ln_bwd, [dy5, y3, dy], [w["ln_g"], w["ln_b"]],
                                                 [(d, F32)], [(1, d)] * 4, 256)
    dy2, dw_dw = _conv_bwd_kernel(jnp.pad(y2, ((CONV_PAD, 0), (0, 0))), jnp.pad(dy3, ((0, CONV_PAD), (0, 0))),
                                  w["w_dw"])

    def glu_bwd(y1, dy2, b1):
        t = y1 + b1
        a, sg = t[:, :d], _sigmoid(t[:, d:])
        dy1 = jnp.concatenate([dy2 * sg, dy2 * a * sg * (1.0 - sg)], axis=1)
        return (dy1,), (_colsum(dy1),)
    (dy1,), (db_pw1,) = _rowwise("cv_glu_bwd", glu_bwd, [y1, dy2], [w["b_pw1"]], [(2 * d, MXU_DTYPE)],
                                 [(1, 2 * d)], 256)
    (dw_pw1,) = _mm("cv_dwpw1", h, dy1, ta=True, out_dtypes=(MXU_DTYPE,), tn=dy1.shape[1] // N_DEV, col_slots=N_DEV)
    (dh,) = _mm("cv_dh", dy1, w["w_pw1"], tb=True)
    return dh, dict(w_pw1=dw_pw1, w_pw2=dw_pw2, b_pw1=db_pw1, b_pw2=db_pw2, w_dw=dw_dw[:CONV_WIDTH],
                    b_dw=db_dw, ln_g=dlg, ln_b=dlb)


def _ada_outer(c_t, dmod):
    def fn(c_t, dmod):
        acc = c_t[:, 0:1] * dmod[0:1, :]
        for b in range(1, N_DEV):
            acc = acc + c_t[:, b:b + 1] * dmod[b:b + 1, :]
        return (acc,), ()
    (g,), _ = _rowwise("ada_outer", fn, [c_t], [dmod], [(dmod.shape[1], F32)], [], 256)
    return g


def _adamw(name, parts, w, m, v, tr, layer=0, prev=None):
    npart, rows = parts.shape[0], parts.shape[1]
    cols = w.shape[1]

    def fn(parts, w, m, v):
        g = parts[0].astype(F32)
        for q in range(1, npart):
            g = g + parts[q].astype(F32)
        m_new = ADAM_B1 * m + (1.0 - ADAM_B1) * g
        v_new = ADAM_B2 * v + (1.0 - ADAM_B2) * (g * g)
        m_hat = m_new / (1.0 - ADAM_B1 ** ADAM_STEP)
        v_hat = v_new / (1.0 - ADAM_B2 ** ADAM_STEP)
        delta = -ADAM_LR * (m_hat / (jnp.sqrt(v_hat) + ADAM_EPS) + ADAM_WD * w)
        return (g, delta, m_new, v_new), ()
    tr = min(tr, rows)
    nblk = rows // tr
    n_prev = 0 if prev is None else 4

    def body(p_ref, w_ref, m_ref, v_ref, *rest):
        outs, _ = fn(p_ref[...], w_ref[...], m_ref[...], v_ref[...])
        for o_ref, o in zip(rest[n_prev:], outs):
            o_ref[...] = o

    spec = pl.BlockSpec((tr, cols), lambda i: (layer * nblk + i, 0))
    return pl.pallas_call(
        body, name=name, grid=(nblk,),
        in_specs=[pl.BlockSpec((npart, tr, cols), lambda i: (0, i, 0)), spec, spec, spec]
        + [pl.BlockSpec(memory_space=pl.ANY)] * n_prev,
        out_specs=[spec] * 4, out_shape=[jax.ShapeDtypeStruct(w.shape, F32)] * 4,
        input_output_aliases={4 + t: t for t in range(n_prev)},
        compiler_params=_cparams(("parallel",)),
    )(parts, w, m, v, *(prev or ()))


def _pack(arrays):
    flat = jnp.concatenate([a.reshape(-1).astype(F32) for a in arrays])
    n = flat.shape[0]
    rows = -(-n // (8 * LANES)) * 8
    return jnp.pad(flat, (0, rows * LANES - n)).reshape(rows, LANES)


def _unpack(buf, shapes, lead=()):
    flat = buf.reshape(lead + (-1,))
    out, off = [], 0
    for shp in shapes:
        n = math.prod(shp)
        out.append(flat[..., off:off + n].reshape(lead + tuple(shp)))
        off += n
    return out


ADAM_TILE_ELEMS = 1 << 17


def _row_tile(rows, cols):
    want = max(8, ADAM_TILE_ELEMS // max(cols, LANES))
    if rows <= want:
        return rows
    best = None
    for t in range(8, want + 1, 8):
        if rows % t == 0:
            best = t
    assert best is not None, (rows, cols)
    return best


def _local_step(xs, tgt, mods, norm_mix, norm_mlp, fetch, send):
    depth = len(mods)
    mixer_fwd = (_fox_fwd, _sg_fwd, _cv_fwd)
    mixer_bwd = (_fox_bwd, _sg_bwd, _cv_bwd)
    mw, w1, w2 = [None] * depth, [None] * depth, [None] * depth

    nsub = 2 * depth
    sub = []
    x_in = xs
    y_prev = gate_prev = None
    for k in range(nsub):
        i, is_mlp = k // 2, k % 2
        sh, sc = mods[i][3 * is_mlp], mods[i][3 * is_mlp + 1]
        g = (norm_mlp if is_mlp else norm_mix)[i:i + 1]
        wts, token = fetch(k, xs if k == 0 else y_prev)
        g = g + token
        if is_mlp:
            w1[i], w2[i] = wts
        else:
            mw[i] = wts
        if k == 0:
            h = _first_norm(x_in, g, sc, sh)
        else:
            x_in, h = _res_norm(x_in, y_prev, gate_prev, g, sc, sh)
        if is_mlp:
            y, saved = _mlp_fwd(h, w1[i], w2[i])
        else:
            y, saved = mixer_fwd[i % 3](h, mw[i])
        sub.append((x_in, h, y, saved))
        y_prev, gate_prev = y, mods[i][3 * is_mlp + 2]

    loss_part, dxo, dy, dgate = _final_loss(x_in, y_prev, gate_prev, tgt)

    dmods = [[None] * 6 for _ in range(depth)]
    g_norm = {'norm_mix': [None] * depth, 'norm_mlp': [None] * depth}
    g_mix = [None] * depth
    g_w1, g_w2 = [None] * depth, [None] * depth
    for k in reversed(range(nsub)):
        i, is_mlp = k // 2, k % 2
        x_k, h_k, _, saved = sub[k]
        dmods[i][3 * is_mlp + 2] = dgate
        if is_mlp:
            dh, g_w1[i], g_w2[i] = _mlp_bwd(dy, h_k, w1[i], w2[i], saved)
        else:
            dh, g_mix[i] = mixer_bwd[i % 3](dy, h_k, mw[i], saved)
        sc = mods[i][3 * is_mlp + 1]
        g = (norm_mlp if is_mlp else norm_mix)[i:i + 1]
        g = g + send(k, (g_w1[i], g_w2[i]) if is_mlp else g_mix[i])
        if k > 0:
            ip, mp = (k - 1) // 2, (k - 1) % 2
            dxo, dy, (dsh, dsc, dg, dgate) = _bwd_norm_gate(dxo, dh, x_k, sub[k - 1][2], g, sc, mods[ip][3 * mp + 2])
        else:
            dxo, (dsh, dsc, dg) = _bwd_norm_first(dxo, dh, x_k, g, sc)
        dmods[i][3 * is_mlp], dmods[i][3 * is_mlp + 1] = dsh, dsc
        g_norm['norm_mlp' if is_mlp else 'norm_mix'][i] = dg
    return loss_part, dxo, dmods, g_norm, g_mix, g_w1, g_w2


def kernel(x, c, norm_mix, norm_mlp, w_ada, b_ada, w_mlp_in, w_mlp_out, fox_w_in, fox_b_f, fox_q_norm, fox_k_norm, fox_w_out, sg_w_in, sg_ln_g, sg_ln_b, sg_w_s, sg_b_s, sg_w_out, cv_w_pw1, cv_b_pw1, cv_w_dw, cv_b_dw, cv_ln_g, cv_ln_b, cv_w_pw2, cv_b_pw2, loss_target, m_norm_mix, m_norm_mlp, m_w_ada, m_b_ada, m_w_mlp_in, m_w_mlp_out, m_fox_w_in, m_fox_b_f, m_fox_q_norm, m_fox_k_norm, m_fox_w_out, m_sg_w_in, m_sg_ln_g, m_sg_ln_b, m_sg_w_s, m_sg_b_s, m_sg_w_out, m_cv_w_pw1, m_cv_b_pw1, m_cv_w_dw, m_cv_b_dw, m_cv_ln_g, m_cv_ln_b, m_cv_w_pw2, m_cv_b_pw2, v_norm_mix, v_norm_mlp, v_w_ada, v_b_ada, v_w_mlp_in, v_w_mlp_out, v_fox_w_in, v_fox_b_f, v_fox_q_norm, v_fox_k_norm, v_fox_w_out, v_sg_w_in, v_sg_ln_g, v_sg_ln_b, v_sg_w_s, v_sg_b_s, v_sg_w_out, v_cv_w_pw1, v_cv_b_pw1, v_cv_w_dw, v_cv_b_dw, v_cv_ln_g, v_cv_ln_b, v_cv_w_pw2, v_cv_b_pw2):
    P = dict(zip(_ARGS, (x, c, norm_mix, norm_mlp, w_ada, b_ada, w_mlp_in, w_mlp_out, fox_w_in, fox_b_f, fox_q_norm, fox_k_norm, fox_w_out, sg_w_in, sg_ln_g, sg_ln_b, sg_w_s, sg_b_s, sg_w_out, cv_w_pw1, cv_b_pw1, cv_w_dw, cv_b_dw, cv_ln_g, cv_ln_b, cv_w_pw2, cv_b_pw2, loss_target, m_norm_mix, m_norm_mlp, m_w_ada, m_b_ada, m_w_mlp_in, m_w_mlp_out, m_fox_w_in, m_fox_b_f, m_fox_q_norm, m_fox_k_norm, m_fox_w_out, m_sg_w_in, m_sg_ln_g, m_sg_ln_b, m_sg_w_s, m_sg_b_s, m_sg_w_out, m_cv_w_pw1, m_cv_b_pw1, m_cv_w_dw, m_cv_b_dw, m_cv_ln_g, m_cv_ln_b, m_cv_w_pw2, m_cv_b_pw2, v_norm_mix, v_norm_mlp, v_w_ada, v_b_ada, v_w_mlp_in, v_w_mlp_out, v_fox_w_in, v_fox_b_f, v_fox_q_norm, v_fox_k_norm, v_fox_w_out, v_sg_w_in, v_sg_ln_g, v_sg_ln_b, v_sg_w_s, v_sg_b_s, v_sg_w_out, v_cv_w_pw1, v_cv_b_pw1, v_cv_w_dw, v_cv_b_dw, v_cv_ln_g, v_cv_ln_b, v_cv_w_pw2, v_cv_b_pw2)))
    me = 4 * lax.axis_index("x") + 2 * lax.axis_index("y") + lax.axis_index("c")
    xs = x[0]
    tgt = loss_target[0]
    s_len, d = xs.shape
    depth = norm_mix.shape[0]
    bf = lambda a: a.astype(MXU_DTYPE)

    cv_small = ['cv_b_pw1', 'cv_w_dw', 'cv_b_dw', 'cv_ln_g', 'cv_ln_b', 'cv_b_pw2']
    small_shapes = [c.shape] + [P[n].shape for n in cv_small]
    (small_all,) = _exchange("gather_small", [_pack([c] + [P[n] for n in cv_small])], scatter=False)
    sm = dict(zip(['c'] + cv_small, _unpack(small_all, small_shapes, lead=(N_DEV,))))
    c_all = sm['c'][:, 0, :]
    cat_last = lambda a: jnp.moveaxis(a, 0, -2).reshape(a.shape[1:-1] + (-1,))
    cvf = {n: cat_last(sm[n]) for n in cv_small}

    big = ['w_mlp_in', 'w_mlp_out', 'fox_w_in', 'fox_w_out', 'sg_w_in', 'sg_w_out', 'cv_w_pw1', 'cv_w_pw2']
    col_sharded = {'w_mlp_in', 'fox_w_in', 'sg_w_in', 'cv_w_pw1'}
    mixer_names = (('fox_w_in', 'fox_w_out'), ('sg_w_in', 'sg_w_out'), ('cv_w_pw1', 'cv_w_pw2'))
    nsub = 2 * depth
    groups = [[('w_mlp_in', k // 2), ('w_mlp_out', k // 2)] if k % 2 else
              [(nm, k // 6) for nm in mixer_names[(k // 2) % 3]] for k in range(nsub)]
    gather_handles = [None] * nsub
    scatter_handles = [None] * nsub

    c_act = c_all * _sigmoid(c_all)
    c_pad = bf(jnp.pad(c_act, ((0, 16 - N_DEV), (0, 0))))
    n_ada = w_ada.shape[2]
    (mod_part,) = _mm("ada_mod", c_pad, bf(jnp.transpose(w_ada, (1, 0, 2)).reshape(d, depth * n_ada)),
                      epi=lambda acc, b: (acc + b,),
                      vecs=(lax.dynamic_slice_in_dim(b_ada, me * n_ada, n_ada, axis=1).reshape(1, depth * n_ada),),
                      tn=n_ada)
    (mod_all,) = _exchange("gather_mod", [mod_part], scatter=False)
    mod_me = lax.dynamic_index_in_dim(mod_all, me, axis=1, keepdims=False)
    mod = jnp.transpose(mod_me.reshape(N_DEV, depth, n_ada), (1, 0, 2)).reshape(depth, 6 * d)
    mods = [[mod[i:i + 1, k * d:(k + 1) * d] for k in range(6)] for i in range(depth)]

    def mixer_weights(i, full_weight):
        kind, j = i % 3, i // 3
        if kind == 0:
            w_in = full_weight('fox_w_in')
            n_pad = -(-w_in.shape[1] // (5 * LANES)) * (5 * LANES)
            return dict(w_in=_pad_cols(w_in, n_pad), w_out=full_weight('fox_w_out'),
                        b_f=_pad_cols(fox_b_f[j:j + 1], LANES),
                        qg=jnp.tile(fox_q_norm[j:j + 1], (1, 2)), kg=jnp.tile(fox_k_norm[j:j + 1], (1, 2)))
        if kind == 1:
            return dict(w_in=full_weight('sg_w_in'), w_out=full_weight('sg_w_out'),
                        ln_g=sg_ln_g[j:j + 1], ln_b=sg_ln_b[j:j + 1],
                        w_s=sg_w_s[j].reshape(SG_GROUPS * SG_CHUNK, SG_CHUNK), b_st=_pad_cols(sg_b_s[j].T, LANES))
        return dict(w_pw1=full_weight('cv_w_pw1'), w_pw2=full_weight('cv_w_pw2'),
                    b_pw1=cvf['cv_b_pw1'][j:j + 1], b_pw2=cvf['cv_b_pw2'][j:j + 1],
                    w_dw=jnp.pad(cvf['cv_w_dw'][j], ((0, CONV_PAD - CONV_WIDTH), (0, 0))),
                    b_dw=cvf['cv_b_dw'][j:j + 1], ln_g=cvf['cv_ln_g'][j:j + 1], ln_b=cvf['cv_ln_b'][j:j + 1])

    dep = mod_all
    for k in range(nsub):
        gather_handles[k], dep = _exchange_start(f"gather_start_{k}", [bf(P[nm][j]) for nm, j in groups[k]],
                                                 False, dep)
    first_token = dep

    def fetch(k, dep):
        got = _exchange_wait(f"gather_wait_{k}", gather_handles[k], False, dep)
        by_name = {nm: g for (nm, _), g in zip(groups[k], got)}

        def full_weight(name):
            g = by_name[name]
            if name in col_sharded:
                return jnp.transpose(g, (1, 0, 2)).reshape(g.shape[1], -1)
            return g.reshape(-1, g.shape[2])
        wts = (full_weight('w_mlp_in'), full_weight('w_mlp_out')) if k % 2 else mixer_weights(k // 2, full_weight)
        return wts, (first_token if k == 0 else jnp.zeros((1, 1), F32))

    def to_slots(name, g2d):
        if name in col_sharded:
            r = g2d.shape[0]
            return jnp.transpose(g2d.reshape(r, N_DEV, -1), (1, 0, 2))
        return g2d.reshape(N_DEV, -1, g2d.shape[1])

    def send(k, grads, dep=None):
        key = {'fox_w_in': 'w_in', 'fox_w_out': 'w_out', 'sg_w_in': 'w_in', 'sg_w_out': 'w_out',
               'cv_w_pw1': 'w_pw1', 'cv_w_pw2': 'w_pw2'}
        slots = []
        for nm, _ in groups[k]:
            g = grads[0] if nm == 'w_mlp_in' else grads[1] if nm == 'w_mlp_out' else grads[key[nm]]
            if g.ndim == 2:
                g = to_slots(nm, g[:, :P[nm].shape[-1] * N_DEV] if nm in col_sharded else g)
            slots.append(g)
        scatter_handles[k], token = _exchange_start(f"scatter_start_{k}", slots, True, slots[0] if dep is None else dep)
        return token

    send_later = lambda k, grads: jnp.zeros((1, 1), F32) if k == 0 else send(k, grads)
    loss_part, dxo, dmods, g_norm, g_mix, _, _ = _local_step(xs, tgt, mods, norm_mix, norm_mlp, fetch, send_later)
    loss = lax.psum(loss_part, ("x", "y", "c"))
    grad_x = dxo[None]

    stack = lambda key, kind: jnp.stack([g_mix[i][key].reshape(P[name_of[(kind, key)]].shape[1:])
                                         for i in range(depth) if i % 3 == kind])
    name_of = {(0, 'b_f'): 'fox_b_f', (0, 'qg'): 'fox_q_norm', (0, 'kg'): 'fox_k_norm',
               (1, 'ln_g'): 'sg_ln_g', (1, 'ln_b'): 'sg_ln_b', (1, 'w_s'): 'sg_w_s', (1, 'b_s'): 'sg_b_s'}
    dmod_me = jnp.concatenate([jnp.concatenate(r, axis=1) for r in dmods], axis=0)
    small_g = {'dmod': dmod_me,
               'norm_mix': jnp.concatenate(g_norm['norm_mix'], axis=0),
               'norm_mlp': jnp.concatenate(g_norm['norm_mlp'], axis=0)}
    for (kind, key), nm in name_of.items():
        small_g[nm] = stack(key, kind)
    cv_keys = {'cv_b_pw1': 'b_pw1', 'cv_w_dw': 'w_dw', 'cv_b_dw': 'b_dw', 'cv_ln_g': 'ln_g', 'cv_ln_b': 'ln_b',
               'cv_b_pw2': 'b_pw2'}
    for nm, key in cv_keys.items():
        small_g[nm] = jnp.stack([g_mix[i][key].reshape(cvf[nm].shape[1:]) for i in range(depth) if i % 3 == 2])
    sg_names = list(small_g)
    sg_shapes = [small_g[n].shape for n in sg_names]
    (sg_all,) = _exchange("gather_small_grads", [_pack([small_g[n] for n in sg_names])], scatter=False)
    send(0, g_mix[0], sg_all)

    dmod_all = _unpack(sg_all, sg_shapes, lead=(N_DEV,))[0]
    out = {}

    def finish(name, parts, shard_of=None):
        w, m, v = P[name], P['m_' + name], P['v_' + name]
        cols = w.shape[-1]
        r2 = lambda a: a.reshape(-1, cols)
        rows = r2(w).shape[0]
        res = _adamw("adamw_" + name, parts.reshape(parts.shape[0], rows, cols), r2(w), r2(m), r2(v),
                     _row_tile(rows, cols))
        out[name] = tuple(r.reshape(w.shape) for r in res)

    c_t = c_act.T
    ada_g = []
    for i in range(depth):
        blk = lax.dynamic_slice_in_dim(dmod_all[:, i, :], me * n_ada, n_ada, axis=1)
        ada_g.append(_ada_outer(c_t, blk))
    finish('w_ada', jnp.stack(ada_g)[None])
    finish('b_ada', dmod_all)

    sm_names = [n for n in sg_names if n != 'dmod']
    sm_parts = jnp.stack([_pack([_unpack(sg_all[q], sg_shapes)[sg_names.index(n)] for n in sm_names])
                          for q in range(N_DEV)])

    def local_block(nm, a):
        if nm in cv_keys:
            n_loc = P[nm].shape[-1]
            return lax.dynamic_slice_in_dim(a, me * n_loc, n_loc, axis=a.ndim - 1)
        return a
    full_shapes = [small_g[n].shape for n in sm_names]

    def pack_full(prefix):
        arrs = []
        for nm in sm_names:
            a = P[prefix + nm]
            if nm in cv_keys:
                full = jnp.zeros(small_g[nm].shape, F32)
                a = lax.dynamic_update_slice_in_dim(full, a, me * a.shape[-1], axis=a.ndim - 1)
            arrs.append(a)
        return _pack(arrs)
    res = _adamw("adamw_small", sm_parts, pack_full(''), pack_full('m_'), pack_full('v_'),
                 _row_tile(sm_parts.shape[1], LANES))
    unp = [_unpack(r, full_shapes) for r in res]
    for idx, nm in enumerate(sm_names):
        out[nm] = tuple(local_block(nm, unp[t][idx]) for t in range(4))

    stacked = {}
    last = out['w_ada'][0]
    for k in reversed(range(nsub)):
        recv = _exchange_wait(f"scatter_wait_{k}", scatter_handles[k], True, last)
        for (nm, j), parts in zip(groups[k], recv):
            nl, r, cc = P[nm].shape
            flat = lambda a: a.reshape(nl * r, cc)
            stacked[nm] = _adamw(f"adamw_{nm}_{j}", parts, flat(P[nm]), flat(P['m_' + nm]), flat(P['v_' + nm]),
                                 _row_tile(r, cc), layer=j, prev=stacked.get(nm))
            last = stacked[nm][0]
    for nm in big:
        out[nm] = tuple(a.reshape(P[nm].shape) for a in stacked[nm])

    outs = [loss, grad_x]
    for t in range(4):
        outs += [out[n][t] for n in _WEIGHTS]
    return tuple(outs)
```
